```python
import jax, jax.numpy as jnp
from jax import lax
import numpy as np

D_MODEL = 1024
BATCH = 8
SEQ = 2048
DEPTH = 4
DEC_BATCH = 128
DEC_SEQ = 4
PAST_LEN = 16384
PAGE_SIZE = 128

N_MIXERS = 2
CONV_WIDTH = 3
E_CONV = D_MODEL
HG_EXPAND = 128
HG_HEADS = D_MODEL // HG_EXPAND
HG_DK = HG_EXPAND
HG_DV = D_MODEL // HG_HEADS
HG_F = HG_HEADS * HG_DK
HG_I = HG_HEADS * HG_DV
CHUNK = 32
EPS = 1e-6
N_CONV_LAYERS = (DEPTH + 1) // 2
N_HGRN_LAYERS = DEPTH // 2

kernel_name = "hybrid_shortconv_hgrn2_adaln_step"


def rmsnorm(x, g):
    xf = x.astype(jnp.float32)
    y = xf * lax.rsqrt(jnp.mean(xf * xf, axis=-1, keepdims=True) + EPS)
    return (y * g.astype(jnp.float32)).astype(x.dtype)


def ada_mod(c, w, b):
    m = jnp.einsum('bd,de->be', jax.nn.silu(c), w) + b
    shift, scale, gate = jnp.split(m, 3, axis=-1)
    return shift[:, None], scale[:, None], gate[:, None]


def short_conv_mixer(h, conv_state, w_in, conv_w, w_out):
    T = h.shape[1]
    proj = jnp.einsum('btd,de->bte', h, w_in)
    b_gate, c_gate, v, z = jnp.split(proj, 4, axis=-1)
    u = c_gate * v
    u_ext = jnp.concatenate([conv_state.astype(u.dtype), u], axis=1)
    conv = sum(u_ext[:, k:k + T] * conv_w[k] for k in range(CONV_WIDTH))
    y = b_gate * conv * jax.nn.silu(z)
    out = jnp.einsum('bte,ed->btd', y, w_out)
    return out, u_ext[:, -(CONV_WIDTH - 1):]


def _to_chunks(a, n, L):
    B = a.shape[0]
    return jnp.moveaxis(a.reshape((B, n, L) + a.shape[2:]), 1, 0)


def hgrn2_recurrence(q, k, v, logf, s0):
    B, T = q.shape[0], q.shape[1]
    L = min(CHUNK, T)
    n = -(-T // L)
    pad = n * L - T
    if pad:
        pw = ((0, 0), (0, pad), (0, 0), (0, 0))
        q, k, v, logf = (jnp.pad(a, pw) for a in (q, k, v, logf))
    qs, ks, vs, gs = (_to_chunks(a, n, L) for a in (q, k, v, logf))
    causal = jnp.tril(jnp.ones((L, L), dtype=bool))[None, :, :, None, None]

    def step(S, inp):
        qc, kc, vc, gc = inp
        G = jnp.cumsum(gc, axis=1)
        o_inter = jnp.einsum('blhk,bhkv->blhv', qc * jnp.exp(G), S)
        diff = jnp.where(causal, G[:, :, None] - G[:, None, :], -jnp.inf)
        A = jnp.einsum('bthk,btshk,bshk->bhts', qc, jnp.exp(diff), kc)
        o_intra = jnp.einsum('bhts,bshv->bthv', A, vc)
        G_last = G[:, -1]
        S_new = S * jnp.exp(G_last)[..., None] + jnp.einsum(
            'bshk,bshv->bhkv', kc * jnp.exp(G_last[:, None] - G), vc)
        return S_new, o_inter + o_intra

    S_fin, o = lax.scan(step, s0, (qs, ks, vs, gs))
    o = jnp.moveaxis(o, 0, 1).reshape(B, n * L, q.shape[2], v.shape[-1])[:, :T]
    return o, S_fin


def hgrn2_mixer(h, state, w_in, lb, onorm_g, w_out):
    B, T = h.shape[0], h.shape[1]
    proj = jnp.einsum('btd,de->bte', h, w_in)
    q, fpre, i, z = jnp.split(proj, [HG_F, 2 * HG_F, 2 * HG_F + HG_I], axis=-1)
    fpre = fpre.astype(jnp.float32)
    lbf = lb.astype(jnp.float32)
    logf = jnp.logaddexp(jnp.log(lbf), jnp.log1p(-lbf) + jax.nn.log_sigmoid(fpre))
    k = (1.0 - lbf) * jax.nn.sigmoid(-fpre)
    qf = jax.nn.silu(q.astype(jnp.float32))
    hs = lambda a, d: a.reshape(B, T, HG_HEADS, d)
    o, S_fin = hgrn2_recurrence(hs(qf, HG_DK), hs(k, HG_DK), hs(i.astype(jnp.float32), HG_DV),
                                hs(logf, HG_DK), state.astype(jnp.float32))
    o = o * lax.rsqrt(jnp.mean(o * o, axis=-1, keepdims=True) + EPS)
    o = (o * onorm_g.astype(jnp.float32).reshape(HG_HEADS, HG_DV)).reshape(B, T, HG_I)
    y = o.astype(h.dtype) * jax.nn.silu(z)
    return jnp.einsum('bte,ed->btd', y, w_out), S_fin


def trunk(x, c, conv_states, hgrn_states, norm_g, w_ada, b_ada, conv_w_in, conv_w, conv_w_out,
          hgrn_w_in, hgrn_lower_bounds, hgrn_onorm_g, hgrn_w_out, final_norm_g):
    p = jax.nn.softmax(hgrn_lower_bounds.astype(jnp.float32), axis=0)
    lbs = jnp.concatenate([jnp.zeros_like(p[:1]), jnp.cumsum(p[1:], axis=0)], axis=0)
    new_conv, new_hgrn = [], []
    for layer in range(DEPTH):
        shift, scale, gate = ada_mod(c, w_ada[layer], b_ada[layer])
        h = rmsnorm(x, norm_g[layer]) * (1.0 + scale) + shift
        j = layer // N_MIXERS
        if layer % N_MIXERS == 0:
            out, st = short_conv_mixer(h, conv_states[j], conv_w_in[j], conv_w[j], conv_w_out[j])
            new_conv.append(st)
        else:
            out, st = hgrn2_mixer(h, hgrn_states[j], hgrn_w_in[j], lbs[layer], hgrn_onorm_g[j], hgrn_w_out[j])
            new_hgrn.append(st)
        x = x + gate * out.astype(x.dtype)
    return rmsnorm(x, final_norm_g), jnp.stack(new_conv), jnp.stack(new_hgrn)


def setup_inputs(seed: int = 0) -> dict:
    key = jax.random.key(seed)
    ks = jax.random.split(key, 20)
    f32 = jnp.float32
    nrm = lambda k, shape, s: jax.random.normal(k, shape, f32) * s
    return {
        "x_prompt": nrm(ks[0], (BATCH, SEQ, D_MODEL), 1.0),
        "x_sample": nrm(ks[1], (DEC_BATCH, DEC_SEQ, D_MODEL), 1.0),
        "state_conv": nrm(ks[2], (N_CONV_LAYERS, DEC_BATCH, CONV_WIDTH - 1, E_CONV), 1.0),
        "state_hgrn": nrm(ks[3], (N_HGRN_LAYERS, DEC_BATCH, HG_HEADS, HG_DK, HG_DV), 0.5),
        "c_prompt": nrm(ks[4], (BATCH, D_MODEL), 1.0),
        "c_sample": nrm(ks[5], (DEC_BATCH, D_MODEL), 1.0),
        "norm_g": 1.0 + nrm(ks[6], (DEPTH, D_MODEL), 0.02),
        "w_ada": nrm(ks[7], (DEPTH, D_MODEL, 3 * D_MODEL), 0.3 * D_MODEL ** -0.5),
        "b_ada": nrm(ks[8], (DEPTH, 3 * D_MODEL), 0.02),
        "conv_w_in": nrm(ks[9], (N_CONV_LAYERS, D_MODEL, 4 * E_CONV), D_MODEL ** -0.5),
        "conv_w": nrm(ks[10], (N_CONV_LAYERS, CONV_WIDTH, E_CONV), CONV_WIDTH ** -0.5),
        "conv_w_out": nrm(ks[11], (N_CONV_LAYERS, E_CONV, D_MODEL), E_CONV ** -0.5),
        "hgrn_w_in": nrm(ks[12], (N_HGRN_LAYERS, D_MODEL, 2 * HG_F + 2 * HG_I), D_MODEL ** -0.5),
        "hgrn_lower_bounds": 1.0 + nrm(ks[13], (DEPTH, HG_F), 0.1),
        "hgrn_onorm_g": 1.0 + nrm(ks[14], (N_HGRN_LAYERS, HG_I), 0.02),
        "hgrn_w_out": nrm(ks[15], (N_HGRN_LAYERS, HG_I, D_MODEL), HG_I ** -0.5),
        "final_norm_g": 1.0 + nrm(ks[16], (D_MODEL,), 0.02),
    }


def reference(x_prompt, x_sample, state_conv, state_hgrn, c_prompt, c_sample, norm_g, w_ada, b_ada,
              conv_w_in, conv_w, conv_w_out, hgrn_w_in, hgrn_lower_bounds, hgrn_onorm_g, hgrn_w_out,
              final_norm_g):
    weights = (norm_g, w_ada, b_ada, conv_w_in, conv_w, conv_w_out, hgrn_w_in,
               hgrn_lower_bounds, hgrn_onorm_g, hgrn_w_out, final_norm_g)
    zero_conv = jnp.zeros((N_CONV_LAYERS, x_prompt.shape[0], CONV_WIDTH - 1, E_CONV), x_prompt.dtype)
    zero_hgrn = jnp.zeros((N_HGRN_LAYERS, x_prompt.shape[0], HG_HEADS, HG_DK, HG_DV), jnp.float32)
    y_prompt, conv_p, hgrn_p = trunk(x_prompt, c_prompt, zero_conv, zero_hgrn, *weights)
    y_sample, conv_s, hgrn_s = trunk(x_sample, c_sample, state_conv, state_hgrn, *weights)
    return (y_prompt, y_sample, conv_p, hgrn_p, conv_s, hgrn_s)
```

```python
import functools

import jax
import jax.numpy as jnp
from jax import lax
from jax.experimental import pallas as pl
from jax.experimental.pallas import tpu as pltpu

F32 = jnp.float32
BF16 = jnp.bfloat16

D_MODEL = 1024
DEPTH = 4
CONV_WIDTH = 3
E_CONV = D_MODEL
HG_HEADS = 8
HG_DK = 128
HG_DV = 128
HG_F = HG_HEADS * HG_DK
HG_I = HG_HEADS * HG_DV
EPS = 1e-6

LANE = 128
SUBLANE = 8
CHUNK = 128
SUB = 32
N_SUB = CHUNK // SUB
CONV_COL_TILE = 256
CONV_TB = 512
HGRN_TB = 256
SAMPLE_SEQ_TILE = 8
SAMPLE_ROW_TILE = 128
VMEM_LIMIT = 56 * 1024 * 1024


def _dot(a, b):
    return jnp.dot(a, b, preferred_element_type=F32)


def _dot_nt(a, b):
    return lax.dot_general(a, b, (((1,), (1,)), ((), ())), preferred_element_type=F32)


def _dot_tn(a, b):
    return lax.dot_general(a, b, (((0,), (0,)), ((), ())), preferred_element_type=F32)


def _silu(x):
    return x * jax.nn.sigmoid(x)


def _rmsnorm(x, g):
    ms = jnp.mean(x * x, axis=-1, keepdims=True)
    return x * lax.rsqrt(ms + EPS) * g


def _split_mod(mod):
    return mod[:, :D_MODEL], mod[:, D_MODEL:2 * D_MODEL], mod[:, 2 * D_MODEL:]


def _lower_bound(raw, layer):
    rows = [raw[i:i + 1, :] for i in range(DEPTH)]
    m = functools.reduce(jnp.maximum, rows)
    es = [jnp.exp(r - m) for r in rows]
    tot = functools.reduce(lambda a, b: a + b, es)
    acc = es[1]
    for i in range(2, layer + 1):
        acc = acc + es[i]
    return acc / tot


def _gate_math(fpre, lb):
    t = jnp.exp(-jnp.abs(fpre))
    log_sig = jnp.minimum(fpre, 0.0) - jnp.log1p(t)
    a = jnp.log(lb)
    b = jnp.log1p(-lb) + log_sig
    logf = jnp.maximum(a, b) + jnp.log1p(jnp.exp(-jnp.abs(a - b)))
    sig_neg = jnp.where(fpre >= 0.0, t, 1.0) / (1.0 + t)
    return logf, (1.0 - lb) * sig_neg


def _head_norm_gate(o, z, og):
    parts = []
    for hd in range(HG_HEADS):
        hs = slice(hd * HG_DV, (hd + 1) * HG_DV)
        oh = o[:, hs]
        ms = jnp.mean(oh * oh, axis=-1, keepdims=True)
        parts.append(oh * lax.rsqrt(ms + EPS))
    on = jnp.concatenate(parts, axis=1) * og
    return (on * _silu(z)).astype(BF16)


def _ada_kernel(c_ref, w_ref, b_ref, o_ref):
    s = _silu(c_ref[...]).astype(BF16)
    o_ref[...] = _dot(s, w_ref[...].astype(BF16)) + b_ref[...]


def _ada_call(c_all, w_ada, b_ada):
    rows = c_all.shape[0]
    n_col = 3
    col = 3 * D_MODEL // n_col
    return pl.pallas_call(
        _ada_kernel,
        grid=(DEPTH, n_col),
        in_specs=[
            pl.BlockSpec((rows, D_MODEL), lambda l, n: (0, 0)),
            pl.BlockSpec((None, D_MODEL, col), lambda l, n: (l, 0, n)),
            pl.BlockSpec((None, 1, col), lambda l, n: (l, 0, n)),
        ],
        out_specs=pl.BlockSpec((None, rows, col), lambda l, n: (l, 0, n)),
        out_shape=jax.ShapeDtypeStruct((DEPTH, rows, 3 * D_MODEL), F32),
        compiler_params=pltpu.CompilerParams(
            dimension_semantics=("arbitrary", "arbitrary"), vmem_limit_bytes=VMEM_LIMIT),
        name="ada_mod",
    )(c_all, w_ada, b_ada.reshape(DEPTH, 1, 3 * D_MODEL))


def _conv_columns(h, win_ref, cw_ref, y_scr, prev_fn, tail_fn):
    ct = CONV_COL_TILE
    for n in range(E_CONV // ct):
        def col(j):
            return slice(j * E_CONV + n * ct, j * E_CONV + (n + 1) * ct)
        cs = slice(n * ct, (n + 1) * ct)
        b_gate = _dot(h, win_ref[:, col(0)])
        c_gate = _dot(h, win_ref[:, col(1)])
        v = _dot(h, win_ref[:, col(2)])
        z = _dot(h, win_ref[:, col(3)])
        u = c_gate * v
        p1, p2 = prev_fn(n, u)
        conv = cw_ref[0:1, cs] * p2 + cw_ref[1:2, cs] * p1 + cw_ref[2:3, cs] * u
        y_scr[:, cs] = (b_gate * conv * _silu(z)).astype(BF16)
        tail_fn(n, u)


def _conv_prompt_kernel(x_ref, mod_ref, g_ref, win_ref, cw_ref, wout_ref, xo_ref, st_ref, y_scr, *, tb):
    t = pl.program_id(1)
    ct = CONV_COL_TILE

    @pl.when(t == 0)
    def _():
        st_ref[...] = jnp.zeros_like(st_ref)

    x = x_ref[...]
    shift, scale, gate = _split_mod(mod_ref[...])
    h = (_rmsnorm(x, g_ref[...]) * (1.0 + scale) + shift).astype(BF16)
    row = lax.broadcasted_iota(jnp.int32, (tb, ct), 0)

    def prev_fn(n, u):
        cs = slice(n * ct, (n + 1) * ct)
        c0 = st_ref[SUBLANE - 2:SUBLANE - 1, cs]
        c1 = st_ref[SUBLANE - 1:SUBLANE, cs]
        p1 = jnp.where(row == 0, c1, pltpu.roll(u, 1, 0))
        p2 = jnp.where(row == 0, c0, jnp.where(row == 1, c1, pltpu.roll(u, 2, 0)))
        return p1, p2

    def tail_fn(n, u):
        st_ref[:, n * ct:(n + 1) * ct] = u[tb - SUBLANE:tb, :]

    _conv_columns(h, win_ref, cw_ref, y_scr, prev_fn, tail_fn)
    out = _dot(y_scr[...], wout_ref[...])
    xo_ref[...] = x + gate * out


def _conv_prompt_call(x, mod, g, w_in, cw, w_out):
    bsz, seq, _ = x.shape
    tb = CONV_TB
    const2 = lambda b, t: (0, 0)
    return pl.pallas_call(
        functools.partial(_conv_prompt_kernel, tb=tb),
        grid=(bsz, seq // tb),
        in_specs=[
            pl.BlockSpec((None, tb, D_MODEL), lambda b, t: (b, t, 0)),
            pl.BlockSpec((None, 1, 3 * D_MODEL), lambda b, t: (b, 0, 0)),
            pl.BlockSpec((1, D_MODEL), const2),
            pl.BlockSpec((D_MODEL, 4 * E_CONV), const2, pipeline_mode=pl.Buffered(1)),
            pl.BlockSpec((CONV_WIDTH, E_CONV), const2),
            pl.BlockSpec((E_CONV, D_MODEL), const2, pipeline_mode=pl.Buffered(1)),
        ],
        out_specs=[
            pl.BlockSpec((None, tb, D_MODEL), lambda b, t: (b, t, 0)),
            pl.BlockSpec((None, SUBLANE, E_CONV), lambda b, t: (b, 0, 0)),
        ],
        out_shape=[
            jax.ShapeDtypeStruct((bsz, seq, D_MODEL), F32),
            jax.ShapeDtypeStruct((bsz, SUBLANE, E_CONV), F32),
        ],
        scratch_shapes=[pltpu.VMEM((tb, E_CONV), BF16)],
        compiler_params=pltpu.CompilerParams(
            dimension_semantics=("arbitrary", "arbitrary"), vmem_limit_bytes=VMEM_LIMIT),
        name="conv_prompt",
    )(x, mod, g, w_in, cw, w_out)


def _conv_sample_kernel(x_ref, mod_ref, g_ref, win_ref, cw_ref, wout_ref, p1_ref, p2_ref,
                        xo_ref, u_ref, y_scr, *, rows, seq):
    ct = CONV_COL_TILE
    x = x_ref[...]
    shift, scale, gate = _split_mod(mod_ref[...])
    h = (_rmsnorm(x, g_ref[...]) * (1.0 + scale) + shift).astype(BF16)
    pos = lax.broadcasted_iota(jnp.int32, (rows, ct), 0) % seq

    def prev_fn(n, u):
        cs = slice(n * ct, (n + 1) * ct)
        p1 = jnp.where(pos >= 1, pltpu.roll(u, 1, 0), p1_ref[:, cs])
        p2 = jnp.where(pos >= 2, pltpu.roll(u, 2, 0), p2_ref[:, cs])
        return p1, p2

    def tail_fn(n, u):
        u_ref[:, n * ct:(n + 1) * ct] = u

    _conv_columns(h, win_ref, cw_ref, y_scr, prev_fn, tail_fn)
    out = _dot(y_scr[...], wout_ref[...])
    xo_ref[...] = x + gate * out


def _row_spec(width):
    return pl.BlockSpec((SAMPLE_ROW_TILE, width), lambda i: (i, 0))


def _resident_spec(shape):
    return pl.BlockSpec(shape, lambda i: (0,) * len(shape), pipeline_mode=pl.Buffered(1))


def _conv_sample_call(x, mod, g, w_in, cw, w_out, p1, p2, seq):
    rows = x.shape[0]
    rt = SAMPLE_ROW_TILE
    return pl.pallas_call(
        functools.partial(_conv_sample_kernel, rows=rt, seq=seq),
        grid=(rows // rt,),
        in_specs=[
            _row_spec(D_MODEL), _row_spec(3 * D_MODEL), _resident_spec((1, D_MODEL)),
            _resident_spec((D_MODEL, 4 * E_CONV)), _resident_spec((CONV_WIDTH, E_CONV)),
            _resident_spec((E_CONV, D_MODEL)), _row_spec(E_CONV), _row_spec(E_CONV),
        ],
        out_specs=[_row_spec(D_MODEL), _row_spec(E_CONV)],
        out_shape=[
            jax.ShapeDtypeStruct((rows, D_MODEL), F32),
            jax.ShapeDtypeStruct((rows, E_CONV), F32),
        ],
        scratch_shapes=[pltpu.VMEM((rt, E_CONV), BF16)],
        compiler_params=pltpu.CompilerParams(
            dimension_semantics=("arbitrary",), vmem_limit_bytes=VMEM_LIMIT),
        name="conv_sample",
    )(x, mod, g, w_in, cw, w_out, p1, p2)


def _rows_from_blocks(vecs):
    return jnp.concatenate([jnp.broadcast_to(v, (SUB, v.shape[1])) for v in vecs], axis=0)


def _chunk_masks():
    r = lax.broadcasted_iota(jnp.int32, (CHUNK, CHUNK), 0)
    c = lax.broadcasted_iota(jnp.int32, (CHUNK, CHUNK), 1)
    delta = r // SUB - c // SUB
    tri = jnp.where((delta == 0) & (c <= r), 1.0, 0.0).astype(BF16)
    return r, c, delta, tri


def _hgrn_chunk(qf, lf, kk, vv, s_ref, tri, r_idx, c_idx, delta):
    hi = lf.astype(BF16)
    lo = (lf - hi.astype(F32)).astype(BF16)
    gl = _dot(tri, hi) + _dot(tri, lo)
    gend = [gl[SUB * j + SUB - 1:SUB * j + SUB, :] for j in range(N_SUB)]
    fb = [jnp.exp(g) for g in gend]
    tot = _rows_from_blocks(gend)

    q_loc = qf * jnp.exp(gl)
    k_end = kk * jnp.exp(tot - gl)
    q_dia = qf * jnp.exp(gl - tot)

    one = jnp.ones_like(fb[0])
    s2 = [one, fb[0], fb[1], fb[2]]
    s3 = [one, one, fb[1] * fb[0], fb[2] * fb[1]]
    pre = [one, fb[0], fb[0] * fb[1], fb[0] * fb[1] * fb[2]]
    suf = [fb[1] * fb[2] * fb[3], fb[2] * fb[3], fb[3], one]
    ptot = pre[3] * fb[3]

    q0 = q_dia.astype(BF16)
    q1 = q_loc.astype(BF16)
    q2 = (q_loc * _rows_from_blocks(s2)).astype(BF16)
    q3 = (q_loc * _rows_from_blocks(s3)).astype(BF16)
    qe = (q_loc * _rows_from_blocks(pre)).astype(BF16)
    kb = k_end.astype(BF16)
    kr = (k_end * _rows_from_blocks(suf)).astype(BF16)
    vb = vv.astype(BF16)

    outs = []
    for hd in range(HG_HEADS):
        hs = slice(hd * HG_DK, (hd + 1) * HG_DK)
        lhs = jnp.concatenate([q0[:, hs], q1[:, hs], q2[:, hs], q3[:, hs]], axis=0)
        ast = _dot_nt(lhs, kb[:, hs])
        a = jnp.where((delta == 0) & (c_idx <= r_idx), ast[0:CHUNK],
            jnp.where(delta == 1, ast[CHUNK:2 * CHUNK],
            jnp.where(delta == 2, ast[2 * CHUNK:3 * CHUNK],
            jnp.where(delta == 3, ast[3 * CHUNK:4 * CHUNK], 0.0))))
        s_old = s_ref[hd]
        o_h = _dot(qe[:, hs], s_old.astype(BF16)) + _dot(a.astype(BF16), vb[:, hs])
        dec = jnp.broadcast_to(ptot[:, hs], (HG_DK, HG_DK)).T
        s_ref[hd] = s_old * dec + _dot_tn(kr[:, hs], vb[:, hs])
        outs.append(o_h)
    return jnp.concatenate(outs, axis=1)


def _hgrn_prompt_kernel(x_ref, mod_ref, g_ref, win_ref, lbraw_ref, og_ref, wout_ref, fg_ref,
                        xo_ref, s_ref, q_scr, lf_scr, k_scr, v_scr, o_scr, *, tb, layer, final):
    t = pl.program_id(1)

    @pl.when(t == 0)
    def _():
        s_ref[...] = jnp.zeros_like(s_ref)

    x = x_ref[...]
    shift, scale, gate = _split_mod(mod_ref[...])
    h = (_rmsnorm(x, g_ref[...]) * (1.0 + scale) + shift).astype(BF16)
    lb = _lower_bound(lbraw_ref[...], layer)

    q_scr[...] = _silu(_dot(h, win_ref[:, 0:HG_F]))
    logf, kk = _gate_math(_dot(h, win_ref[:, HG_F:2 * HG_F]), lb)
    lf_scr[...] = logf
    k_scr[...] = kk
    v_scr[...] = _dot(h, win_ref[:, 2 * HG_F:2 * HG_F + HG_I])

    r_idx, c_idx, delta, tri = _chunk_masks()

    def chunk(c, carry):
        rows = pl.ds(pl.multiple_of(c * CHUNK, CHUNK), CHUNK)
        o_scr[rows, :] = _hgrn_chunk(q_scr[rows, :], lf_scr[rows, :], k_scr[rows, :], v_scr[rows, :],
                                     s_ref, tri, r_idx, c_idx, delta)
        return carry

    lax.fori_loop(0, tb // CHUNK, chunk, 0)

    z = _dot(h, win_ref[:, 2 * HG_F + HG_I:])
    y = _head_norm_gate(o_scr[...], z, og_ref[...])
    xn = x + gate * _dot(y, wout_ref[...])
    if final:
        xn = _rmsnorm(xn, fg_ref[...])
    xo_ref[...] = xn


def _hgrn_prompt_call(x, mod, g, w_in, lb_raw, og, w_out, fg, layer, final):
    bsz, seq, _ = x.shape
    tb = HGRN_TB
    const2 = lambda b, t: (0, 0)
    return pl.pallas_call(
        functools.partial(_hgrn_prompt_kernel, tb=tb, layer=layer, final=final),
        grid=(bsz, seq // tb),
        in_specs=[
            pl.BlockSpec((None, tb, D_MODEL), lambda b, t: (b, t, 0)),
            pl.BlockSpec((None, 1, 3 * D_MODEL), lambda b, t: (b, 0, 0)),
            pl.BlockSpec((1, D_MODEL), const2),
            pl.BlockSpec((D_MODEL, 2 * HG_F + 2 * HG_I), const2, pipeline_mode=pl.Buffered(1)),
            pl.BlockSpec((DEPTH, HG_F), const2),
            pl.BlockSpec((1, HG_I), const2),
            pl.BlockSpec((HG_I, D_MODEL), const2, pipeline_mode=pl.Buffered(1)),
            pl.BlockSpec((1, D_MODEL), const2),
        ],
        out_specs=[
            pl.BlockSpec((None, tb, D_MODEL), lambda b, t: (b, t, 0)),
            pl.BlockSpec((None, HG_HEADS, HG_DK, HG_DV), lambda b, t: (b, 0, 0, 0)),
        ],
        out_shape=[
            jax.ShapeDtypeStruct((bsz, seq, D_MODEL), F32),
            jax.ShapeDtypeStruct((bsz, HG_HEADS, HG_DK, HG_DV), F32),
        ],
        scratch_shapes=[pltpu.VMEM((tb, HG_F), F32)] * 4 + [pltpu.VMEM((tb, HG_I), F32)],
        compiler_params=pltpu.CompilerParams(
            dimension_semantics=("arbitrary", "arbitrary"), vmem_limit_bytes=VMEM_LIMIT),
        name="hgrn_prompt",
    )(x, mod, g, w_in, lb_raw, og, w_out, fg)


def _shift_rows(a, d, rows):
    return a if d == 0 else pltpu.roll(a, d % rows, 0)


def _hgrn_sample_pre_kernel(x_ref, mod_ref, g_ref, win_ref, lbraw_ref,
                            qe_ref, kr_ref, v_ref, eg_ref, oi_ref, z_ref, *, rows, seq, layer):
    x = x_ref[...]
    shift, scale, _ = _split_mod(mod_ref[...])
    h = (_rmsnorm(x, g_ref[...]) * (1.0 + scale) + shift).astype(BF16)
    lb = _lower_bound(lbraw_ref[...], layer)
    qf = _silu(_dot(h, win_ref[:, 0:HG_F]))
    lf, kk = _gate_math(_dot(h, win_ref[:, HG_F:2 * HG_F]), lb)
    vv = _dot(h, win_ref[:, 2 * HG_F:2 * HG_F + HG_I])
    z_ref[...] = _dot(h, win_ref[:, 2 * HG_F + HG_I:])

    pos = lax.broadcasted_iota(jnp.int32, (rows, HG_F), 0) % seq
    g_inc = lf
    g_rest = jnp.zeros_like(lf)
    for d in range(1, seq):
        g_inc = g_inc + jnp.where(pos >= d, _shift_rows(lf, d, rows), 0.0)
        g_rest = g_rest + jnp.where(pos + d < seq, _shift_rows(lf, -d, rows), 0.0)
    eg = jnp.exp(g_inc)
    eg_ref[...] = eg
    qe_ref[...] = (qf * eg).astype(BF16)
    kr_ref[...] = kk * jnp.exp(g_rest)
    v_ref[...] = vv

    o_intra = jnp.zeros((rows, HG_I), F32)
    for d in range(seq):
        kd = _shift_rows(kk, d, rows)
        vd = _shift_rows(vv, d, rows)
        e = qf * kd
        if d > 0:
            e = jnp.where(pos >= d, e * jnp.exp(g_inc - _shift_rows(g_inc, d, rows)), 0.0)
        parts = []
        for hd in range(HG_HEADS):
            hs = slice(hd * HG_DK, (hd + 1) * HG_DK)
            a = jnp.sum(e[:, hs], axis=-1, keepdims=True)
            parts.append(a * vd[:, hs])
        o_intra = o_intra + jnp.concatenate(parts, axis=1)
    oi_ref[...] = o_intra


def _hgrn_sample_pre_call(x, mod, g, w_in, lb_raw, seq, layer):
    rows = x.shape[0]
    rt = SAMPLE_ROW_TILE
    act = jax.ShapeDtypeStruct((rows, HG_F), F32)
    return pl.pallas_call(
        functools.partial(_hgrn_sample_pre_kernel, rows=rt, seq=seq, layer=layer),
        grid=(rows // rt,),
        in_specs=[
            _row_spec(D_MODEL), _row_spec(3 * D_MODEL), _resident_spec((1, D_MODEL)),
            _resident_spec((D_MODEL, 2 * HG_F + 2 * HG_I)), _resident_spec((DEPTH, HG_F)),
        ],
        out_specs=[_row_spec(HG_F)] * 6,
        out_shape=[jax.ShapeDtypeStruct((rows, HG_F), BF16), act, act, act, act, act],
        compiler_params=pltpu.CompilerParams(
            dimension_semantics=("arbitrary",), vmem_limit_bytes=VMEM_LIMIT),
        name="hgrn_sample_pre",
    )(x, mod, g, w_in, lb_raw)


def _hgrn_sample_state_kernel(qe_ref, kr_ref, v_ref, eg_ref, s_ref, oo_ref, so_ref, *, seq):
    per_tile = SUBLANE // seq
    row = lax.broadcasted_iota(jnp.int32, (SUBLANE, HG_DK), 0)
    for p in range(SAMPLE_SEQ_TILE // per_tile):
        rs = slice(p * SUBLANE, (p + 1) * SUBLANE)
        for hd in range(HG_HEADS):
            hs = slice(hd * HG_DK, (hd + 1) * HG_DK)
            q8 = qe_ref[rs, hs]
            k8 = kr_ref[rs, hs]
            v8 = v_ref[rs, hs].astype(BF16)
            e8 = eg_ref[rs, hs]
            o8 = jnp.zeros((SUBLANE, HG_DV), F32)
            for j in range(per_tile):
                b = p * per_tile + j
                mine = (row >= j * seq) & (row < (j + 1) * seq)
                s_old = s_ref[b, hd]
                o8 = jnp.where(mine, _dot(q8, s_old.astype(BF16)), o8)
                kj = jnp.where(mine, k8, 0.0).astype(BF16)
                last = e8[(j + 1) * seq - 1:(j + 1) * seq, :]
                dec = jnp.broadcast_to(last, (HG_DK, HG_DK)).T
                so_ref[b, hd] = s_old * dec + _dot_tn(kj, v8)
            oo_ref[rs, hs] = o8


def _hgrn_sample_state_call(qe, kr, v, eg, state, seq):
    rows = qe.shape[0]
    n_seq = state.shape[0]
    tile_rows = SAMPLE_SEQ_TILE * seq
    row_spec = pl.BlockSpec((tile_rows, HG_F), lambda i: (i, 0))
    st_spec = pl.BlockSpec((SAMPLE_SEQ_TILE, HG_HEADS, HG_DK, HG_DV), lambda i: (i, 0, 0, 0))
    return pl.pallas_call(
        functools.partial(_hgrn_sample_state_kernel, seq=seq),
        grid=(n_seq // SAMPLE_SEQ_TILE,),
        in_specs=[row_spec, row_spec, row_spec, row_spec, st_spec],
        out_specs=[row_spec, st_spec],
        out_shape=[
            jax.ShapeDtypeStruct((rows, HG_I), F32),
            jax.ShapeDtypeStruct(state.shape, F32),
        ],
        compiler_params=pltpu.CompilerParams(
            dimension_semantics=("arbitrary",), vmem_limit_bytes=VMEM_LIMIT),
        name="hgrn_sample_state",
    )(qe, kr, v, eg, state)


def _hgrn_sample_post_kernel(x_ref, mod_ref, oa_ref, ob_ref, z_ref, og_ref, wout_ref, fg_ref, xo_ref, *, final):
    x = x_ref[...]
    _, _, gate = _split_mod(mod_ref[...])
    y = _head_norm_gate(oa_ref[...] + ob_ref[...], z_ref[...], og_ref[...])
    xn = x + gate * _dot(y, wout_ref[...])
    if final:
        xn = _rmsnorm(xn, fg_ref[...])
    xo_ref[...] = xn


def _hgrn_sample_post_call(x, mod, o_inter, o_intra, z, og, w_out, fg, final):
    rows = x.shape[0]
    return pl.pallas_call(
        functools.partial(_hgrn_sample_post_kernel, final=final),
        grid=(rows // SAMPLE_ROW_TILE,),
        in_specs=[
            _row_spec(D_MODEL), _row_spec(3 * D_MODEL), _row_spec(HG_I), _row_spec(HG_I),
            _row_spec(HG_I), _resident_spec((1, HG_I)), _resident_spec((HG_I, D_MODEL)),
            _resident_spec((1, D_MODEL)),
        ],
        out_specs=_row_spec(D_MODEL),
        out_shape=jax.ShapeDtypeStruct((rows, D_MODEL), F32),
        compiler_params=pltpu.CompilerParams(
            dimension_semantics=("arbitrary",), vmem_limit_bytes=VMEM_LIMIT),
        name="hgrn_sample_post",
    )(x, mod, o_inter, o_intra, z, og, w_out, fg)


def kernel(x_prompt, x_sample, state_conv, state_hgrn, c_prompt, c_sample, norm_g, w_ada, b_ada, conv_w_in, conv_w, conv_w_out, hgrn_w_in, hgrn_lower_bounds, hgrn_onorm_g, hgrn_w_out, final_norm_g):
    n_p = x_prompt.shape[0]
    n_s, t_s, _ = x_sample.shape
    rows_s = n_s * t_s
    assert SUBLANE % t_s == 0 and t_s >= CONV_WIDTH - 1

    mods = _ada_call(jnp.concatenate([c_prompt, c_sample], axis=0), w_ada, b_ada)
    mods_p = mods[:, :n_p].reshape(DEPTH, n_p, 1, 3 * D_MODEL)
    mods_s = jnp.repeat(mods[:, n_p:], t_s, axis=1)

    conv_w_in_b = conv_w_in.astype(BF16)
    conv_w_out_b = conv_w_out.astype(BF16)
    hgrn_w_in_b = hgrn_w_in.astype(BF16)
    hgrn_w_out_b = hgrn_w_out.astype(BF16)
    fg = final_norm_g.reshape(1, D_MODEL)

    xp = x_prompt
    xs = x_sample.reshape(rows_s, D_MODEL)
    conv_p, hgrn_p, conv_s, hgrn_s = [], [], [], []
    for layer in range(DEPTH):
        j = layer // 2
        g = norm_g[layer].reshape(1, D_MODEL)
        final = layer == DEPTH - 1
        if layer % 2 == 0:
            xp, tail = _conv_prompt_call(xp, mods_p[layer], g, conv_w_in_b[j], conv_w[j], conv_w_out_b[j])
            conv_p.append(tail[:, SUBLANE - (CONV_WIDTH - 1):])
            st = state_conv[j]
            zero = jnp.zeros((n_s, t_s - 1, E_CONV), F32)
            p1 = jnp.concatenate([st[:, 1:2], zero], axis=1).reshape(rows_s, E_CONV)
            p2 = jnp.concatenate([st, zero[:, 1:]], axis=1).reshape(rows_s, E_CONV)
            xs, u = _conv_sample_call(xs, mods_s[layer], g, conv_w_in_b[j], conv_w[j], conv_w_out_b[j],
                                      p1, p2, t_s)
            conv_s.append(u.reshape(n_s, t_s, E_CONV)[:, t_s - (CONV_WIDTH - 1):])
        else:
            og = hgrn_onorm_g[j].reshape(1, HG_I)
            xp, s_new = _hgrn_prompt_call(xp, mods_p[layer], g, hgrn_w_in_b[j], hgrn_lower_bounds, og,
                                          hgrn_w_out_b[j], fg, layer, final)
            hgrn_p.append(s_new)
            qe, kr, v, eg, o_intra, z = _hgrn_sample_pre_call(xs, mods_s[layer], g, hgrn_w_in_b[j],
                                                              hgrn_lower_bounds, t_s, layer)
            o_inter, s_new = _hgrn_sample_state_call(qe, kr, v, eg, state_hgrn[j], t_s)
            hgrn_s.append(s_new)
            xs = _hgrn_sample_post_call(xs, mods_s[layer], o_inter, o_intra, z, og, hgrn_w_out_b[j], fg, final)
    return (xp, xs.reshape(n_s, t_s, D_MODEL), jnp.stack(conv_p), jnp.stack(hgrn_p),
            jnp.stack(conv_s), jnp.stack(hgrn_s))
```

```python
import functools

import jax
import jax.numpy as jnp
from jax import lax
from jax.experimental import pallas as pl
from jax.experimental.pallas import tpu as pltpu

F32 = jnp.float32
BF16 = jnp.bfloat16

D_MODEL = 1024
DEPTH = 4
CONV_WIDTH = 3
E_CONV = D_MODEL
HG_HEADS = 8
HG_DK = 128
HG_DV = 128
HG_F = HG_HEADS * HG_DK
HG_I = HG_HEADS * HG_DV
EPS = 1e-6

SUBLANE = 8
CHUNK = 128
SUB = 32
N_SUB = CHUNK // SUB
CONV_COL_TILE = 256
CONV_TB = 512
HGRN_TB = 256
SAMPLE_SEQ_BLOCK = 32
SAMPLE_STATE_BLOCK = 8
VMEM_LIMIT = 56 * 1024 * 1024


def _dot(a, b):
    return jnp.dot(a, b, preferred_element_type=F32)


def _dot_nt(a, b):
    return lax.dot_general(a, b, (((1,), (1,)), ((), ())), preferred_element_type=F32)


def _dot_tn(a, b):
    return lax.dot_general(a, b, (((0,), (0,)), ((), ())), preferred_element_type=F32)


def _silu(x):
    return x * jax.nn.sigmoid(x)


def _rmsnorm(x, g):
    ms = jnp.mean(x * x, axis=-1, keepdims=True)
    return x * lax.rsqrt(ms + EPS) * g


def _split_mod(mod):
    return mod[:, :D_MODEL], mod[:, D_MODEL:2 * D_MODEL], mod[:, 2 * D_MODEL:]


def _mod_norm(x, g, shift, scale):
    return (_rmsnorm(x, g) * (1.0 + scale) + shift).astype(BF16)


def _lower_bound(raw, layer):
    rows = [raw[i:i + 1, :] for i in range(DEPTH)]
    m = functools.reduce(jnp.maximum, rows)
    es = [jnp.exp(r - m) for r in rows]
    tot = functools.reduce(lambda a, b: a + b, es)
    acc = es[1]
    for i in range(2, layer + 1):
        acc = acc + es[i]
    return acc / tot


def _gate_math(fpre, lb):
    t = jnp.exp(-jnp.abs(fpre))
    log_sig = jnp.minimum(fpre, 0.0) - jnp.log(1.0 + t)
    a = jnp.log(lb)
    b = jnp.log1p(-lb) + log_sig
    logf = jnp.maximum(a, b) + jnp.log(1.0 + jnp.exp(-jnp.abs(a - b)))
    sig_neg = jnp.where(fpre >= 0.0, t, 1.0) / (1.0 + t)
    return logf, (1.0 - lb) * sig_neg


def _head_norm_gate(o, z, og):
    parts = []
    for hd in range(HG_HEADS):
        hs = slice(hd * HG_DV, (hd + 1) * HG_DV)
        oh = o[:, hs]
        ms = jnp.mean(oh * oh, axis=-1, keepdims=True)
        parts.append(oh * lax.rsqrt(ms + EPS))
    on = jnp.concatenate(parts, axis=1) * og
    return (on * _silu(z)).astype(BF16)


def _params(n_grid):
    return pltpu.CompilerParams(dimension_semantics=("arbitrary",) * n_grid, vmem_limit_bytes=VMEM_LIMIT)


def _resident(shape, index, n_grid):
    if n_grid == 1:
        return pl.BlockSpec(shape, lambda i: index, pipeline_mode=pl.Buffered(1))
    return pl.BlockSpec(shape, lambda b, t: index, pipeline_mode=pl.Buffered(1))


def _ada_kernel(c_ref, w_ref, b_ref, o_ref):
    s = _silu(c_ref[...]).astype(BF16)
    o_ref[...] = _dot(s, w_ref[...].astype(BF16)) + b_ref[...]


def _ada_call(c_all, w_ada, b_ada):
    rows = c_all.shape[0]
    n_col = 3
    col = 3 * D_MODEL // n_col
    return pl.pallas_call(
        _ada_kernel,
        grid=(DEPTH, n_col),
        in_specs=[
            pl.BlockSpec((rows, D_MODEL), lambda l, n: (0, 0)),
            pl.BlockSpec((None, D_MODEL, col), lambda l, n: (l, 0, n)),
            pl.BlockSpec((None, 1, col), lambda l, n: (l, 0, n)),
        ],
        out_specs=pl.BlockSpec((None, rows, col), lambda l, n: (l, 0, n)),
        out_shape=jax.ShapeDtypeStruct((DEPTH, rows, 3 * D_MODEL), F32),
        compiler_params=_params(2),
        name="ada_mod",
    )(c_all, w_ada, b_ada.reshape(DEPTH, 1, 3 * D_MODEL))


def _conv_columns(h, win_ref, cw_ref, y_scr, prev_fn, tail_fn):
    ct = CONV_COL_TILE
    for n in range(E_CONV // ct):
        def col(j):
            return slice(j * E_CONV + n * ct, j * E_CONV + (n + 1) * ct)
        cs = slice(n * ct, (n + 1) * ct)
        b_gate = _dot(h, win_ref[:, col(0)])
        c_gate = _dot(h, win_ref[:, col(1)])
        v = _dot(h, win_ref[:, col(2)])
        z = _dot(h, win_ref[:, col(3)])
        u = c_gate * v
        p1, p2 = prev_fn(cs, u)
        conv = cw_ref[0:1, cs] * p2 + cw_ref[1:2, cs] * p1 + cw_ref[2:3, cs] * u
        y_scr[:, cs] = (b_gate * conv * _silu(z)).astype(BF16)
        tail_fn(cs, u)


def _conv_prompt_kernel(x_ref, mod_ref, g_ref, win_ref, cw_ref, wout_ref, xo_ref, st_ref, y_scr, *, tb):
    b = pl.program_id(0)
    t = pl.program_id(1)
    ct = CONV_COL_TILE

    @pl.when(t == 0)
    def _():
        st_ref[...] = jnp.zeros_like(st_ref)

    x = x_ref[...]
    shift, scale, gate = _split_mod(mod_ref[pl.ds(b, 1), :])
    h = _mod_norm(x, g_ref[...], shift, scale)
    row = lax.broadcasted_iota(jnp.int32, (tb, ct), 0)

    def prev_fn(cs, u):
        c0 = st_ref[SUBLANE - 2:SUBLANE - 1, cs]
        c1 = st_ref[SUBLANE - 1:SUBLANE, cs]
        p1 = jnp.where(row == 0, c1, pltpu.roll(u, 1, 0))
        p2 = jnp.where(row == 0, c0, jnp.where(row == 1, c1, pltpu.roll(u, 2, 0)))
        return p1, p2

    def tail_fn(cs, u):
        st_ref[:, cs] = u[tb - SUBLANE:tb, :]

    _conv_columns(h, win_ref, cw_ref, y_scr, prev_fn, tail_fn)
    xo_ref[...] = x + gate * _dot(y_scr[...], wout_ref[...])


def _conv_prompt_call(x, mods, norm_g, w_in, cw, w_out, layer, mod_block):
    bsz, seq, _ = x.shape
    tb = CONV_TB
    j = layer // 2
    return pl.pallas_call(
        functools.partial(_conv_prompt_kernel, tb=tb),
        grid=(bsz, seq // tb),
        in_specs=[
            pl.BlockSpec((None, tb, D_MODEL), lambda b, t: (b, t, 0)),
            _resident((None, bsz, 3 * D_MODEL), (layer, mod_block, 0), 2),
            _resident((None, 1, D_MODEL), (layer, 0, 0), 2),
            _resident((None, D_MODEL, 4 * E_CONV), (j, 0, 0), 2),
            _resident((None, CONV_WIDTH, E_CONV), (j, 0, 0), 2),
            _resident((None, E_CONV, D_MODEL), (j, 0, 0), 2),
        ],
        out_specs=[
            pl.BlockSpec((None, tb, D_MODEL), lambda b, t: (b, t, 0)),
            pl.BlockSpec((None, SUBLANE, E_CONV), lambda b, t: (b, 0, 0)),
        ],
        out_shape=[
            jax.ShapeDtypeStruct((bsz, seq, D_MODEL), F32),
            jax.ShapeDtypeStruct((bsz, SUBLANE, E_CONV), F32),
        ],
        scratch_shapes=[pltpu.VMEM((tb, E_CONV), BF16)],
        compiler_params=_params(2),
        name="conv_prompt",
    )(x, mods, norm_g, w_in, cw, w_out)


def _tok(s, width):
    return slice(s * width, (s + 1) * width)


def _stack_tokens(ref, n_tok, width, cols=None):
    parts = []
    for s in range(n_tok):
        lo = s * width + (0 if cols is None else cols.start)
        hi = s * width + (width if cols is None else cols.stop)
        parts.append(ref[:, lo:hi])
    return jnp.concatenate(parts, axis=0)


def _sample_hidden(x_ref, mod_ref, g_ref, n_tok):
    shift, scale, _ = _split_mod(mod_ref[...])
    g = g_ref[...]
    return jnp.concatenate(
        [_mod_norm(x_ref[:, _tok(s, D_MODEL)], g, shift, scale) for s in range(n_tok)], axis=0)


def _conv_sample_kernel(x_ref, mod_ref, g_ref, win_ref, cw_ref, wout_ref, st_ref,
                        xo_ref, so_ref, y_scr, *, n_tok):
    ns = x_ref.shape[0]
    h = _sample_hidden(x_ref, mod_ref, g_ref, n_tok)

    def prev_fn(cs, u):
        st0 = st_ref[:, cs]
        st1 = st_ref[:, E_CONV + cs.start:E_CONV + cs.stop]
        p1 = jnp.concatenate([st1, u[:(n_tok - 1) * ns]], axis=0)
        p2 = jnp.concatenate([st0, st1, u[:(n_tok - 2) * ns]], axis=0)
        return p1, p2

    def tail_fn(cs, u):
        so_ref[:, cs] = u[(n_tok - 2) * ns:(n_tok - 1) * ns]
        so_ref[:, E_CONV + cs.start:E_CONV + cs.stop] = u[(n_tok - 1) * ns:]

    _conv_columns(h, win_ref, cw_ref, y_scr, prev_fn, tail_fn)
    out = _dot(y_scr[...], wout_ref[...])
    gate = mod_ref[:, 2 * D_MODEL:]
    for s in range(n_tok):
        xo_ref[:, _tok(s, D_MODEL)] = x_ref[:, _tok(s, D_MODEL)] + gate * out[s * ns:(s + 1) * ns]


def _seq_rows(width, layer=None):
    if layer is None:
        return pl.BlockSpec((SAMPLE_SEQ_BLOCK, width), lambda i: (i, 0))
    return pl.BlockSpec((None, SAMPLE_SEQ_BLOCK, width), lambda i: (layer, i, 0))


def _conv_sample_call(x, mods, norm_g, w_in, cw, w_out, state, layer, n_tok):
    n_seq = x.shape[0]
    j = layer // 2
    return pl.pallas_call(
        functools.partial(_conv_sample_kernel, n_tok=n_tok),
        grid=(n_seq // SAMPLE_SEQ_BLOCK,),
        in_specs=[
            _seq_rows(n_tok * D_MODEL),
            _seq_rows(3 * D_MODEL, layer),
            _resident((None, 1, D_MODEL), (layer, 0, 0), 1),
            _resident((None, D_MODEL, 4 * E_CONV), (j, 0, 0), 1),
            _resident((None, CONV_WIDTH, E_CONV), (j, 0, 0), 1),
            _resident((None, E_CONV, D_MODEL), (j, 0, 0), 1),
            _seq_rows((CONV_WIDTH - 1) * E_CONV, j),
        ],
        out_specs=[_seq_rows(n_tok * D_MODEL), _seq_rows((CONV_WIDTH - 1) * E_CONV)],
        out_shape=[
            jax.ShapeDtypeStruct((n_seq, n_tok * D_MODEL), F32),
            jax.ShapeDtypeStruct((n_seq, (CONV_WIDTH - 1) * E_CONV), F32),
        ],
        scratch_shapes=[pltpu.VMEM((n_tok * SAMPLE_SEQ_BLOCK, E_CONV), BF16)],
        compiler_params=_params(1),
        name="conv_sample",
    )(x, mods, norm_g, w_in, cw, w_out, state)


def _rows_from_blocks(vecs):
    return jnp.concatenate([jnp.broadcast_to(v, (SUB, v.shape[1])) for v in vecs], axis=0)


def _chunk_masks():
    r = lax.broadcasted_iota(jnp.int32, (CHUNK, CHUNK), 0)
    c = lax.broadcasted_iota(jnp.int32, (CHUNK, CHUNK), 1)
    delta = r // SUB - c // SUB
    tri = jnp.where((delta == 0) & (c <= r), 1.0, 0.0).astype(BF16)
    return r, c, delta, tri


def _hgrn_chunk(qf, lf, kk, vv, s_ref, tri, r_idx, c_idx, delta):
    hi = lf.astype(BF16)
    lo = (lf - hi.astype(F32)).astype(BF16)
    gl = _dot(tri, hi) + _dot(tri, lo)
    gend = [gl[SUB * j + SUB - 1:SUB * j + SUB, :] for j in range(N_SUB)]
    fb = [jnp.exp(g) for g in gend]
    tot = _rows_from_blocks(gend)

    q_loc = qf * jnp.exp(gl)
    k_end = kk * jnp.exp(tot - gl)
    q_dia = qf * jnp.exp(gl - tot)

    one = jnp.ones_like(fb[0])
    s2 = [one, fb[0], fb[1], fb[2]]
    s3 = [one, one, fb[1] * fb[0], fb[2] * fb[1]]
    pre = [one, fb[0], fb[0] * fb[1], fb[0] * fb[1] * fb[2]]
    suf = [fb[1] * fb[2] * fb[3], fb[2] * fb[3], fb[3], one]
    ptot = pre[3] * fb[3]

    q0 = q_dia.astype(BF16)
    q1 = q_loc.astype(BF16)
    q2 = (q_loc * _rows_from_blocks(s2)).astype(BF16)
    q3 = (q_loc * _rows_from_blocks(s3)).astype(BF16)
    qe = (q_loc * _rows_from_blocks(pre)).astype(BF16)
    kb = k_end.astype(BF16)
    kr = (k_end * _rows_from_blocks(suf)).astype(BF16)
    vb = vv.astype(BF16)

    outs = []
    for hd in range(HG_HEADS):
        hs = slice(hd * HG_DK, (hd + 1) * HG_DK)
        lhs = jnp.concatenate([q0[:, hs], q1[:, hs], q2[:, hs], q3[:, hs]], axis=0)
        ast = _dot_nt(lhs, kb[:, hs])
        a = jnp.where((delta == 0) & (c_idx <= r_idx), ast[0:CHUNK],
            jnp.where(delta == 1, ast[CHUNK:2 * CHUNK],
            jnp.where(delta == 2, ast[2 * CHUNK:3 * CHUNK],
            jnp.where(delta == 3, ast[3 * CHUNK:4 * CHUNK], 0.0))))
        s_old = s_ref[hd]
        o_h = _dot(qe[:, hs], s_old.astype(BF16)) + _dot(a.astype(BF16), vb[:, hs])
        dec = jnp.broadcast_to(ptot[:, hs], (HG_DK, HG_DK)).T
        s_ref[hd] = s_old * dec + _dot_tn(kr[:, hs], vb[:, hs])
        outs.append(o_h)
    return jnp.concatenate(outs, axis=1)


def _hgrn_prompt_kernel(*refs, tb, layer, final, has_prev):
    n_in = 8 + int(has_prev)
    x_ref, mod_ref, g_ref, win_ref, lbraw_ref, og_ref, wout_ref, fg_ref = refs[:8]
    xo_ref, s_ref = refs[n_in:n_in + 2]
    q_scr, lf_scr, k_scr, v_scr, o_scr = refs[n_in + 2:]
    b = pl.program_id(0)
    t = pl.program_id(1)

    @pl.when(t == 0)
    def _():
        s_ref[...] = jnp.zeros_like(s_ref)

    x = x_ref[...]
    shift, scale, gate = _split_mod(mod_ref[pl.ds(b, 1), :])
    h = _mod_norm(x, g_ref[...], shift, scale)
    lb = _lower_bound(lbraw_ref[...], layer)

    q_scr[...] = _silu(_dot(h, win_ref[:, 0:HG_F]))
    logf, kk = _gate_math(_dot(h, win_ref[:, HG_F:2 * HG_F]), lb)
    lf_scr[...] = logf
    k_scr[...] = kk
    v_scr[...] = _dot(h, win_ref[:, 2 * HG_F:2 * HG_F + HG_I])

    r_idx, c_idx, delta, tri = _chunk_masks()

    def chunk(c, carry):
        rows = pl.ds(pl.multiple_of(c * CHUNK, CHUNK), CHUNK)
        o_scr[rows, :] = _hgrn_chunk(q_scr[rows, :], lf_scr[rows, :], k_scr[rows, :], v_scr[rows, :],
                                     s_ref, tri, r_idx, c_idx, delta)
        return carry

    lax.fori_loop(0, tb // CHUNK, chunk, 0)

    z = _dot(h, win_ref[:, 2 * HG_F + HG_I:])
    y = _head_norm_gate(o_scr[...], z, og_ref[...])
    xn = x + gate * _dot(y, wout_ref[...])
    if final:
        xn = _rmsnorm(xn, fg_ref[...])
    xo_ref[...] = xn


def _hgrn_prompt_call(x, mods, norm_g, w_in, lb_raw, og, w_out, fg, prev_states, layer, mod_block):
    bsz, seq, _ = x.shape
    tb = HGRN_TB
    j = layer // 2
    n_hgrn = DEPTH // 2
    final = layer == DEPTH - 1
    in_specs = [
        pl.BlockSpec((None, tb, D_MODEL), lambda b, t: (b, t, 0)),
        _resident((None, bsz, 3 * D_MODEL), (layer, mod_block, 0), 2),
        _resident((None, 1, D_MODEL), (layer, 0, 0), 2),
        _resident((None, D_MODEL, 2 * HG_F + 2 * HG_I), (j, 0, 0), 2),
        _resident((DEPTH, HG_F), (0, 0), 2),
        _resident((None, 1, HG_I), (j, 0, 0), 2),
        _resident((None, HG_I, D_MODEL), (j, 0, 0), 2),
        _resident((1, D_MODEL), (0, 0), 2),
    ]
    args = [x, mods, norm_g, w_in, lb_raw, og, w_out, fg]
    aliases = {}
    if prev_states is not None:
        in_specs.append(pl.BlockSpec(memory_space=pl.ANY))
        args.append(prev_states)
        aliases = {len(args) - 1: 1}
    return pl.pallas_call(
        functools.partial(_hgrn_prompt_kernel, tb=tb, layer=layer, final=final,
                          has_prev=prev_states is not None),
        grid=(bsz, seq // tb),
        in_specs=in_specs,
        out_specs=[
            pl.BlockSpec((None, tb, D_MODEL), lambda b, t: (b, t, 0)),
            pl.BlockSpec((None, None, HG_HEADS, HG_DK, HG_DV), lambda b, t: (j, b, 0, 0, 0)),
        ],
        out_shape=[
            jax.ShapeDtypeStruct((bsz, seq, D_MODEL), F32),
            jax.ShapeDtypeStruct((n_hgrn, bsz, HG_HEADS, HG_DK, HG_DV), F32),
        ],
        scratch_shapes=[pltpu.VMEM((tb, HG_F), F32)] * 4 + [pltpu.VMEM((tb, HG_I), F32)],
        input_output_aliases=aliases,
        compiler_params=_params(2),
        name="hgrn_prompt",
    )(*args)


def _hgrn_sample_pre_kernel(x_ref, mod_ref, g_ref, win_ref, lbraw_ref,
                            qe_ref, kr_ref, v_ref, el_ref, oi_ref, z_ref, *, n_tok, layer):
    ns = x_ref.shape[0]
    h = _sample_hidden(x_ref, mod_ref, g_ref, n_tok)
    lb = _lower_bound(lbraw_ref[...], layer)
    qf = _silu(_dot(h, win_ref[:, 0:HG_F]))
    lf, kk = _gate_math(_dot(h, win_ref[:, HG_F:2 * HG_F]), lb)
    vv = _dot(h, win_ref[:, 2 * HG_F:2 * HG_F + HG_I])
    z = _dot(h, win_ref[:, 2 * HG_F + HG_I:])

    def tok(a, s):
        return a[s * ns:(s + 1) * ns]

    g = [tok(lf, 0)]
    for s in range(1, n_tok):
        g.append(g[-1] + tok(lf, s))
    el_ref[...] = jnp.exp(g[-1])
    for s in range(n_tok):
        qe_ref[:, _tok(s, HG_F)] = tok(qf, s) * jnp.exp(g[s])
        kr_ref[:, _tok(s, HG_F)] = tok(kk, s) * jnp.exp(g[-1] - g[s])
        v_ref[:, _tok(s, HG_I)] = tok(vv, s)
        z_ref[:, _tok(s, HG_I)] = tok(z, s)

    for t in range(n_tok):
        acc = jnp.zeros((ns, HG_I), F32)
        for s in range(t + 1):
            e = tok(qf, t) * tok(kk, s)
            if s < t:
                e = e * jnp.exp(g[t] - g[s])
            vs = tok(vv, s)
            parts = []
            for hd in range(HG_HEADS):
                hs = slice(hd * HG_DK, (hd + 1) * HG_DK)
                parts.append(jnp.sum(e[:, hs], axis=-1, keepdims=True) * vs[:, hs])
            acc = acc + jnp.concatenate(parts, axis=1)
        oi_ref[:, _tok(t, HG_I)] = acc


def _hgrn_sample_pre_call(x, mods, norm_g, w_in, lb_raw, layer, n_tok):
    n_seq = x.shape[0]
    j = layer // 2
    act = jax.ShapeDtypeStruct((n_seq, n_tok * HG_F), F32)
    return pl.pallas_call(
        functools.partial(_hgrn_sample_pre_kernel, n_tok=n_tok, layer=layer),
        grid=(n_seq // SAMPLE_SEQ_BLOCK,),
        in_specs=[
            _seq_rows(n_tok * D_MODEL),
            _seq_rows(3 * D_MODEL, layer),
            _resident((None, 1, D_MODEL), (layer, 0, 0), 1),
            _resident((None, D_MODEL, 2 * HG_F + 2 * HG_I), (j, 0, 0), 1),
            _resident((DEPTH, HG_F), (0, 0), 1),
        ],
        out_specs=[_seq_rows(n_tok * HG_F)] * 3 + [_seq_rows(HG_F)] + [_seq_rows(n_tok * HG_I)] * 2,
        out_shape=[act, act, act, jax.ShapeDtypeStruct((n_seq, HG_F), F32), act, act],
        compiler_params=_params(1),
        name="hgrn_sample_pre",
    )(x, mods, norm_g, w_in, lb_raw)


def _hgrn_sample_state_kernel(*refs, n_tok, has_prev):
    n_in = 5 + int(has_prev)
    qe_ref, kr_ref, v_ref, el_ref, s_ref = refs[:5]
    oo_ref, so_ref = refs[n_in:n_in + 2]
    ns = SAMPLE_STATE_BLOCK
    seq_of_row = lax.broadcasted_iota(jnp.int32, (n_tok * ns, HG_DK), 0) % ns
    for hd in range(HG_HEADS):
        hs = slice(hd * HG_DK, (hd + 1) * HG_DK)
        q = _stack_tokens(qe_ref, n_tok, HG_F, hs).astype(BF16)
        k = _stack_tokens(kr_ref, n_tok, HG_F, hs)
        v = _stack_tokens(v_ref, n_tok, HG_I, hs).astype(BF16)
        el = el_ref[:, hs]
        o = jnp.zeros((n_tok * ns, HG_DV), F32)
        for b in range(ns):
            mine = seq_of_row == b
            s_old = s_ref[b, hd]
            o = jnp.where(mine, _dot(q, s_old.astype(BF16)), o)
            kb = jnp.where(mine, k, 0.0).astype(BF16)
            dec = jnp.broadcast_to(el[b:b + 1, :], (HG_DK, HG_DK)).T
            so_ref[b, hd] = s_old * dec + _dot_tn(kb, v)
        for t in range(n_tok):
            oo_ref[:, t * HG_I + hs.start:t * HG_I + hs.stop] = o[t * ns:(t + 1) * ns]


def _hgrn_sample_state_call(qe, kr, v, el, states, prev_states, layer, n_tok):
    n_seq = qe.shape[0]
    j = layer // 2
    nb = SAMPLE_STATE_BLOCK
    rows = lambda width: pl.BlockSpec((nb, width), lambda i: (i, 0))
    st_spec = pl.BlockSpec((None, nb, HG_HEADS, HG_DK, HG_DV), lambda i: (j, i, 0, 0, 0))
    in_specs = [rows(n_tok * HG_F), rows(n_tok * HG_F), rows(n_tok * HG_I), rows(HG_F), st_spec]
    args = [qe, kr, v, el, states]
    aliases = {}
    if prev_states is not None:
        in_specs.append(pl.BlockSpec(memory_space=pl.ANY))
        args.append(prev_states)
        aliases = {len(args) - 1: 1}
    return pl.pallas_call(
        functools.partial(_hgrn_sample_state_kernel, n_tok=n_tok, has_prev=prev_states is not None),
        grid=(n_seq // nb,),
        in_specs=in_specs,
        out_specs=[rows(n_tok * HG_I), st_spec],
        out_shape=[
            jax.ShapeDtypeStruct((n_seq, n_tok * HG_I), F32),
            jax.ShapeDtypeStruct(states.shape, F32),
        ],
        input_output_aliases=aliases,
        compiler_params=_params(1),
        name="hgrn_sample_state",
    )(*args)


def _hgrn_sample_post_kernel(x_ref, mod_ref, oa_ref, ob_ref, z_ref, og_ref, wout_ref, fg_ref, xo_ref,
                             *, n_tok, final):
    ns = x_ref.shape[0]
    o = _stack_tokens(oa_ref, n_tok, HG_I) + _stack_tokens(ob_ref, n_tok, HG_I)
    y = _head_norm_gate(o, _stack_tokens(z_ref, n_tok, HG_I), og_ref[...])
    out = _dot(y, wout_ref[...])
    gate = mod_ref[:, 2 * D_MODEL:]
    for s in range(n_tok):
        xn = x_ref[:, _tok(s, D_MODEL)] + gate * out[s * ns:(s + 1) * ns]
        if final:
            xn = _rmsnorm(xn, fg_ref[...])
        xo_ref[:, _tok(s, D_MODEL)] = xn


def _hgrn_sample_post_call(x, mods, o_inter, o_intra, z, og, w_out, fg, layer, n_tok):
    n_seq = x.shape[0]
    j = layer // 2
    return pl.pallas_call(
        functools.partial(_hgrn_sample_post_kernel, n_tok=n_tok, final=layer == DEPTH - 1),
        grid=(n_seq // SAMPLE_SEQ_BLOCK,),
        in_specs=[
            _seq_rows(n_tok * D_MODEL),
            _seq_rows(3 * D_MODEL, layer),
            _seq_rows(n_tok * HG_I), _seq_rows(n_tok * HG_I), _seq_rows(n_tok * HG_I),
            _resident((None, 1, HG_I), (j, 0, 0), 1),
            _resident((None, HG_I, D_MODEL), (j, 0, 0), 1),
            _resident((1, D_MODEL), (0, 0), 1),
        ],
        out_specs=_seq_rows(n_tok * D_MODEL),
        out_shape=jax.ShapeDtypeStruct((n_seq, n_tok * D_MODEL), F32),
        compiler_params=_params(1),
        name="hgrn_sample_post",
    )(x, mods, o_inter, o_intra, z, og, w_out, fg)


def kernel(x_prompt, x_sample, state_conv, state_hgrn, c_prompt, c_sample, norm_g, w_ada, b_ada, conv_w_in, conv_w, conv_w_out, hgrn_w_in, hgrn_lower_bounds, hgrn_onorm_g, hgrn_w_out, final_norm_g):
    n_p = x_prompt.shape[0]
    n_s, t_s, _ = x_sample.shape
    assert t_s >= CONV_WIDTH - 1 and n_s % SAMPLE_SEQ_BLOCK == 0 and n_s % n_p == 0

    mods = _ada_call(jnp.concatenate([c_sample, c_prompt], axis=0), w_ada, b_ada)
    mod_block_p = n_s // n_p

    conv_w_in_b = conv_w_in.astype(BF16)
    conv_w_out_b = conv_w_out.astype(BF16)
    hgrn_w_in_b = hgrn_w_in.astype(BF16)
    hgrn_w_out_b = hgrn_w_out.astype(BF16)
    norm_g3 = norm_g.reshape(DEPTH, 1, D_MODEL)
    og3 = hgrn_onorm_g.reshape(DEPTH // 2, 1, HG_I)
    fg = final_norm_g.reshape(1, D_MODEL)
    state_conv2 = state_conv.reshape(state_conv.shape[0], n_s, (CONV_WIDTH - 1) * E_CONV)

    xp = x_prompt
    xs = x_sample.reshape(n_s, t_s * D_MODEL)
    conv_p, conv_s = [], []
    hgrn_p = hgrn_s = None
    for layer in range(DEPTH):
        if layer % 2 == 0:
            xp, tail = _conv_prompt_call(xp, mods, norm_g3, conv_w_in_b, conv_w, conv_w_out_b, layer, mod_block_p)
            conv_p.append(tail[:, SUBLANE - (CONV_WIDTH - 1):])
            xs, st = _conv_sample_call(xs, mods, norm_g3, conv_w_in_b, conv_w, conv_w_out_b, state_conv2,
                                       layer, t_s)
            conv_s.append(st.reshape(n_s, CONV_WIDTH - 1, E_CONV))
        else:
            xp, hgrn_p = _hgrn_prompt_call(xp, mods, norm_g3, hgrn_w_in_b, hgrn_lower_bounds, og3,
                                           hgrn_w_out_b, fg, hgrn_p, layer, mod_block_p)
            qe, kr, v, el, o_intra, z = _hgrn_sample_pre_call(xs, mods, norm_g3, hgrn_w_in_b,
                                                              hgrn_lower_bounds, layer, t_s)
            o_inter, hgrn_s = _hgrn_sample_state_call(qe, kr, v, el, state_hgrn, hgrn_s, layer, t_s)
            xs = _hgrn_sample_post_call(xs, mods, o_inter, o_intra, z, og3, hgrn_w_out_b, fg, layer, t_s)
    return (xp, xs.reshape(n_s, t_s, D_MODEL), jnp.stack(conv_p), hgrn_p, jnp.stack(conv_s), hgrn_s)
```

```python
import functools

import jax
import jax.numpy as jnp
from jax import lax
from jax.experimental import pallas as pl
from jax.experimental.pallas import tpu as pltpu

F32 = jnp.float32
BF16 = jnp.bfloat16

D_MODEL = 1024
DEPTH = 4
CONV_WIDTH = 3
E_CONV = D_MODEL
HG_HEADS = 8
HG_DK = 128
HG_DV = 128
HG_F = HG_HEADS * HG_DK
HG_I = HG_HEADS * HG_DV
EPS = 1e-6

SUBLANE = 8
CHUNK = 128
SUB = 32
N_SUB = CHUNK // SUB
CONV_COL_TILE = 256
CONV_TB = 512
HGRN_TB = 256
HGRN_GB = 2
HGRN_PIPE_SKEW = 3
SAMPLE_SEQ_BLOCK = 32
SAMPLE_STATE_BLOCK = 8
VMEM_LIMIT = 56 * 1024 * 1024


def _dot(a, b):
    return jnp.dot(a, b, preferred_element_type=F32)


def _dot_nt(a, b):
    return lax.dot_general(a, b, (((1,), (1,)), ((), ())), preferred_element_type=F32)


def _dot_tn(a, b):
    return lax.dot_general(a, b, (((0,), (0,)), ((), ())), preferred_element_type=F32)


def _silu(x):
    return x * jax.nn.sigmoid(x)


def _rmsnorm(x, g):
    ms = jnp.mean(x * x, axis=-1, keepdims=True)
    return x * lax.rsqrt(ms + EPS) * g


def _split_mod(mod):
    return mod[:, :D_MODEL], mod[:, D_MODEL:2 * D_MODEL], mod[:, 2 * D_MODEL:]


def _mod_norm(x, g, shift, scale):
    return (_rmsnorm(x, g) * (1.0 + scale) + shift).astype(BF16)


def _lower_bound(raw, layer):
    rows = [raw[i:i + 1, :] for i in range(DEPTH)]
    m = functools.reduce(jnp.maximum, rows)
    es = [jnp.exp(r - m) for r in rows]
    tot = functools.reduce(lambda a, b: a + b, es)
    acc = es[1]
    for i in range(2, layer + 1):
        acc = acc + es[i]
    return acc / tot


def _gate_math(fpre, lb):
    t = jnp.exp(-jnp.abs(fpre))
    log_sig = jnp.minimum(fpre, 0.0) - jnp.log(1.0 + t)
    a = jnp.log(lb)
    b = jnp.log1p(-lb) + log_sig
    logf = jnp.maximum(a, b) + jnp.log(1.0 + jnp.exp(-jnp.abs(a - b)))
    sig_neg = jnp.where(fpre >= 0.0, t, 1.0) / (1.0 + t)
    return logf, (1.0 - lb) * sig_neg


def _head_norm_gate(o, z, og):
    parts = []
    for hd in range(HG_HEADS):
        hs = slice(hd * HG_DV, (hd + 1) * HG_DV)
        oh = o[:, hs]
        ms = jnp.mean(oh * oh, axis=-1, keepdims=True)
        parts.append(oh * lax.rsqrt(ms + EPS))
    on = jnp.concatenate(parts, axis=1) * og
    return (on * _silu(z)).astype(BF16)


def _params(n_grid):
    return pltpu.CompilerParams(dimension_semantics=("arbitrary",) * n_grid, vmem_limit_bytes=VMEM_LIMIT)


def _resident(shape, index, n_grid):
    if n_grid == 1:
        return pl.BlockSpec(shape, lambda i: index, pipeline_mode=pl.Buffered(1))
    return pl.BlockSpec(shape, lambda b, t: index, pipeline_mode=pl.Buffered(1))


def _ada_kernel(c_ref, w_ref, b_ref, o_ref):
    s = _silu(c_ref[...]).astype(BF16)
    o_ref[...] = _dot(s, w_ref[...].astype(BF16)) + b_ref[...]


def _ada_call(c_all, w_ada, b_ada):
    rows = c_all.shape[0]
    n_col = 3
    col = 3 * D_MODEL // n_col
    return pl.pallas_call(
        _ada_kernel,
        grid=(DEPTH, n_col),
        in_specs=[
            pl.BlockSpec((rows, D_MODEL), lambda l, n: (0, 0)),
            pl.BlockSpec((None, D_MODEL, col), lambda l, n: (l, 0, n)),
            pl.BlockSpec((None, 1, col), lambda l, n: (l, 0, n)),
        ],
        out_specs=pl.BlockSpec((None, rows, col), lambda l, n: (l, 0, n)),
        out_shape=jax.ShapeDtypeStruct((DEPTH, rows, 3 * D_MODEL), F32),
        compiler_params=_params(2),
        name="ada_mod",
    )(c_all, w_ada, b_ada.reshape(DEPTH, 1, 3 * D_MODEL))


def _conv_columns(h, win_ref, cw_ref, y_scr, prev_fn, tail_fn):
    ct = CONV_COL_TILE
    for n in range(E_CONV // ct):
        def col(j):
            return slice(j * E_CONV + n * ct, j * E_CONV + (n + 1) * ct)
        cs = slice(n * ct, (n + 1) * ct)
        b_gate = _dot(h, win_ref[:, col(0)])
        c_gate = _dot(h, win_ref[:, col(1)])
        v = _dot(h, win_ref[:, col(2)])
        z = _dot(h, win_ref[:, col(3)])
        u = c_gate * v
        p1, p2 = prev_fn(cs, u)
        conv = cw_ref[0:1, cs] * p2 + cw_ref[1:2, cs] * p1 + cw_ref[2:3, cs] * u
        y_scr[:, cs] = (b_gate * conv * _silu(z)).astype(BF16)
        tail_fn(cs, u)


def _conv_prompt_kernel(x_ref, mod_ref, g_ref, win_ref, cw_ref, wout_ref, xo_ref, st_ref, y_scr, *, tb):
    b = pl.program_id(0)
    t = pl.program_id(1)
    ct = CONV_COL_TILE

    @pl.when(t == 0)
    def _():
        st_ref[...] = jnp.zeros_like(st_ref)

    x = x_ref[...]
    shift, scale, gate = _split_mod(mod_ref[pl.ds(b, 1), :])
    h = _mod_norm(x, g_ref[...], shift, scale)
    row = lax.broadcasted_iota(jnp.int32, (tb, ct), 0)

    def prev_fn(cs, u):
        c0 = st_ref[SUBLANE - 2:SUBLANE - 1, cs]
        c1 = st_ref[SUBLANE - 1:SUBLANE, cs]
        p1 = jnp.where(row == 0, c1, pltpu.roll(u, 1, 0))
        p2 = jnp.where(row == 0, c0, jnp.where(row == 1, c1, pltpu.roll(u, 2, 0)))
        return p1, p2

    def tail_fn(cs, u):
        st_ref[:, cs] = u[tb - SUBLANE:tb, :]

    _conv_columns(h, win_ref, cw_ref, y_scr, prev_fn, tail_fn)
    xo_ref[...] = x + gate * _dot(y_scr[...], wout_ref[...])


def _conv_prompt_call(x, mods, norm_g, w_in, cw, w_out, layer, mod_block):
    bsz, seq, _ = x.shape
    tb = CONV_TB
    j = layer // 2
    return pl.pallas_call(
        functools.partial(_conv_prompt_kernel, tb=tb),
        grid=(bsz, seq // tb),
        in_specs=[
            pl.BlockSpec((None, tb, D_MODEL), lambda b, t: (b, t, 0)),
            _resident((None, bsz, 3 * D_MODEL), (layer, mod_block, 0), 2),
            _resident((None, 1, D_MODEL), (layer, 0, 0), 2),
            _resident((None, D_MODEL, 4 * E_CONV), (j, 0, 0), 2),
            _resident((None, CONV_WIDTH, E_CONV), (j, 0, 0), 2),
            _resident((None, E_CONV, D_MODEL), (j, 0, 0), 2),
        ],
        out_specs=[
            pl.BlockSpec((None, tb, D_MODEL), lambda b, t: (b, t, 0)),
            pl.BlockSpec((None, SUBLANE, E_CONV), lambda b, t: (b, 0, 0)),
        ],
        out_shape=[
            jax.ShapeDtypeStruct((bsz, seq, D_MODEL), F32),
            jax.ShapeDtypeStruct((bsz, SUBLANE, E_CONV), F32),
        ],
        scratch_shapes=[pltpu.VMEM((tb, E_CONV), BF16)],
        compiler_params=_params(2),
        name="conv_prompt",
    )(x, mods, norm_g, w_in, cw, w_out)


def _tok(s, width):
    return slice(s * width, (s + 1) * width)


def _stack_tokens(ref, n_tok, width, cols=None):
    parts = []
    for s in range(n_tok):
        lo = s * width + (0 if cols is None else cols.start)
        hi = s * width + (width if cols is None else cols.stop)
        parts.append(ref[:, lo:hi])
    return jnp.concatenate(parts, axis=0)


def _sample_hidden(x_ref, mod_ref, g_ref, n_tok):
    shift, scale, _ = _split_mod(mod_ref[...])
    g = g_ref[...]
    return jnp.concatenate(
        [_mod_norm(x_ref[:, _tok(s, D_MODEL)], g, shift, scale) for s in range(n_tok)], axis=0)


def _conv_sample_kernel(x_ref, mod_ref, g_ref, win_ref, cw_ref, wout_ref, st_ref,
                        xo_ref, so_ref, y_scr, *, n_tok):
    ns = x_ref.shape[0]
    h = _sample_hidden(x_ref, mod_ref, g_ref, n_tok)

    def prev_fn(cs, u):
        st0 = st_ref[:, cs]
        st1 = st_ref[:, E_CONV + cs.start:E_CONV + cs.stop]
        p1 = jnp.concatenate([st1, u[:(n_tok - 1) * ns]], axis=0)
        p2 = jnp.concatenate([st0, st1, u[:(n_tok - 2) * ns]], axis=0)
        return p1, p2

    def tail_fn(cs, u):
        so_ref[:, cs] = u[(n_tok - 2) * ns:(n_tok - 1) * ns]
        so_ref[:, E_CONV + cs.start:E_CONV + cs.stop] = u[(n_tok - 1) * ns:]

    _conv_columns(h, win_ref, cw_ref, y_scr, prev_fn, tail_fn)
    out = _dot(y_scr[...], wout_ref[...])
    gate = mod_ref[:, 2 * D_MODEL:]
    for s in range(n_tok):
        xo_ref[:, _tok(s, D_MODEL)] = x_ref[:, _tok(s, D_MODEL)] + gate * out[s * ns:(s + 1) * ns]


def _seq_rows(width, layer=None):
    if layer is None:
        return pl.BlockSpec((SAMPLE_SEQ_BLOCK, width), lambda i: (i, 0))
    return pl.BlockSpec((None, SAMPLE_SEQ_BLOCK, width), lambda i: (layer, i, 0))


def _conv_sample_call(x, mods, norm_g, w_in, cw, w_out, state, layer, n_tok):
    n_seq = x.shape[0]
    j = layer // 2
    return pl.pallas_call(
        functools.partial(_conv_sample_kernel, n_tok=n_tok),
        grid=(n_seq // SAMPLE_SEQ_BLOCK,),
        in_specs=[
            _seq_rows(n_tok * D_MODEL),
            _seq_rows(3 * D_MODEL, layer),
            _resident((None, 1, D_MODEL), (layer, 0, 0), 1),
            _resident((None, D_MODEL, 4 * E_CONV), (j, 0, 0), 1),
            _resident((None, CONV_WIDTH, E_CONV), (j, 0, 0), 1),
            _resident((None, E_CONV, D_MODEL), (j, 0, 0), 1),
            _seq_rows((CONV_WIDTH - 1) * E_CONV, j),
        ],
        out_specs=[_seq_rows(n_tok * D_MODEL), _seq_rows((CONV_WIDTH - 1) * E_CONV)],
        out_shape=[
            jax.ShapeDtypeStruct((n_seq, n_tok * D_MODEL), F32),
            jax.ShapeDtypeStruct((n_seq, (CONV_WIDTH - 1) * E_CONV), F32),
        ],
        scratch_shapes=[pltpu.VMEM((n_tok * SAMPLE_SEQ_BLOCK, E_CONV), BF16)],
        compiler_params=_params(1),
        name="conv_sample",
    )(x, mods, norm_g, w_in, cw, w_out, state)


def _rows_from_blocks(vecs):
    return jnp.concatenate([jnp.broadcast_to(v, (SUB, v.shape[1])) for v in vecs], axis=0)


def _chunk_masks():
    r = lax.broadcasted_iota(jnp.int32, (CHUNK, CHUNK), 0)
    c = lax.broadcasted_iota(jnp.int32, (CHUNK, CHUNK), 1)
    delta = r // SUB - c // SUB
    tri = jnp.where((delta == 0) & (c <= r), 1.0, 0.0).astype(BF16)
    return r, c, delta, tri


def _chunk_operands(qf, lf, kk, vv, tri):
    hi = lf.astype(BF16)
    lo = (lf - hi.astype(F32)).astype(BF16)
    gl = _dot(tri, hi) + _dot(tri, lo)
    gend = [gl[SUB * j + SUB - 1:SUB * j + SUB, :] for j in range(N_SUB)]
    fb = [jnp.exp(g) for g in gend]
    tot = _rows_from_blocks(gend)

    q_loc = qf * jnp.exp(gl)
    k_end = kk * jnp.exp(tot - gl)
    q_dia = qf * jnp.exp(gl - tot)

    one = jnp.ones_like(fb[0])
    s2 = [one, fb[0], fb[1], fb[2]]
    s3 = [one, one, fb[1] * fb[0], fb[2] * fb[1]]
    pre = [one, fb[0], fb[0] * fb[1], fb[0] * fb[1] * fb[2]]
    suf = [fb[1] * fb[2] * fb[3], fb[2] * fb[3], fb[3], one]
    ptot = pre[3] * fb[3]

    q0 = q_dia.astype(BF16)
    q1 = q_loc.astype(BF16)
    q2 = (q_loc * _rows_from_blocks(s2)).astype(BF16)
    q3 = (q_loc * _rows_from_blocks(s3)).astype(BF16)
    qe = (q_loc * _rows_from_blocks(pre)).astype(BF16)
    kb = k_end.astype(BF16)
    kr = (k_end * _rows_from_blocks(suf)).astype(BF16)
    vb = vv.astype(BF16)

    return dict(lhs=jnp.concatenate([q0, q1, q2, q3], axis=0), kb=kb, qe=qe, kr=kr, vb=vb, ptot=ptot)


def _chunk_scores(ops):
    return _dot_nt(ops["lhs"], ops["kb"])


def _chunk_finish(ops, ast, s_old, r_idx, c_idx, delta):
    a = jnp.where((delta == 0) & (c_idx <= r_idx), ast[0:CHUNK],
        jnp.where(delta == 1, ast[CHUNK:2 * CHUNK],
        jnp.where(delta == 2, ast[2 * CHUNK:3 * CHUNK],
        jnp.where(delta == 3, ast[3 * CHUNK:4 * CHUNK], 0.0))))
    o = _dot(ops["qe"], s_old.astype(BF16)) + _dot(a.astype(BF16), ops["vb"])
    dec = jnp.broadcast_to(ops["ptot"], (HG_DK, HG_DK)).T
    return o, s_old * dec + _dot_tn(ops["kr"], ops["vb"])


def _hgrn_prompt_kernel(*refs, tb, gb, layer, final, has_prev):
    n_in = 8 + int(has_prev)
    x_ref, mod_ref, g_ref, win_ref, lbraw_ref, og_ref, wout_ref, fg_ref = refs[:8]
    xo_ref, s_ref = refs[n_in:n_in + 2]
    q_scr, lf_scr, k_scr, v_scr, z_scr, o_scr = refs[n_in + 2:]
    b = pl.program_id(0)
    t = pl.program_id(1)

    @pl.when(t == 0)
    def _():
        s_ref[...] = jnp.zeros_like(s_ref)

    seqs = range(gb)
    lb = _lower_bound(lbraw_ref[...], layer)
    x = [x_ref[g] for g in seqs]
    mod = [_split_mod(mod_ref[pl.ds(b * gb + g, 1), :]) for g in seqs]
    h = [_mod_norm(x[g], g_ref[...], mod[g][0], mod[g][1]) for g in seqs]
    for g in seqs:
        q_scr[g] = _silu(_dot(h[g], win_ref[:, 0:HG_F]))
    for g in seqs:
        logf, kk = _gate_math(_dot(h[g], win_ref[:, HG_F:2 * HG_F]), lb)
        lf_scr[g] = logf
        k_scr[g] = kk
    for g in seqs:
        v_scr[g] = _dot(h[g], win_ref[:, 2 * HG_F:2 * HG_F + HG_I])
    for g in seqs:
        z_scr[g] = _dot(h[g], win_ref[:, 2 * HG_F + HG_I:])

    r_idx, c_idx, delta, tri = _chunk_masks()
    chains = [(c, hd, g) for c in range(tb // CHUNK) for hd in range(HG_HEADS) for g in seqs]
    ops, scores = {}, {}

    def where(n):
        c, hd, g = chains[n]
        return g, hd, slice(c * CHUNK, (c + 1) * CHUNK), slice(hd * HG_DK, (hd + 1) * HG_DK)

    for step in range(len(chains) + 2 * HGRN_PIPE_SKEW):
        n = step
        if n < len(chains):
            g, hd, rows, hs = where(n)
            ops[n] = _chunk_operands(q_scr[g, rows, hs], lf_scr[g, rows, hs], k_scr[g, rows, hs],
                                     v_scr[g, rows, hs], tri)
        n = step - HGRN_PIPE_SKEW
        if 0 <= n < len(chains):
            scores[n] = _chunk_scores(ops[n])
        n = step - 2 * HGRN_PIPE_SKEW
        if 0 <= n < len(chains):
            g, hd, rows, hs = where(n)
            o, s_new = _chunk_finish(ops.pop(n), scores.pop(n), s_ref[g, hd], r_idx, c_idx, delta)
            o_scr[g, rows, hs] = o
            s_ref[g, hd] = s_new

    y = [_head_norm_gate(o_scr[g], z_scr[g], og_ref[...]) for g in seqs]
    for g in seqs:
        xn = x[g] + mod[g][2] * _dot(y[g], wout_ref[...])
        if final:
            xn = _rmsnorm(xn, fg_ref[...])
        xo_ref[g] = xn


def _hgrn_prompt_call(x, mods, norm_g, w_in, lb_raw, og, w_out, fg, prev_states, layer, mod_block):
    bsz, seq, _ = x.shape
    tb = HGRN_TB
    gb = HGRN_GB
    j = layer // 2
    n_hgrn = DEPTH // 2
    final = layer == DEPTH - 1
    in_specs = [
        pl.BlockSpec((gb, tb, D_MODEL), lambda b, t: (b, t, 0)),
        _resident((None, bsz, 3 * D_MODEL), (layer, mod_block, 0), 2),
        _resident((None, 1, D_MODEL), (layer, 0, 0), 2),
        _resident((None, D_MODEL, 2 * HG_F + 2 * HG_I), (j, 0, 0), 2),
        _resident((DEPTH, HG_F), (0, 0), 2),
        _resident((None, 1, HG_I), (j, 0, 0), 2),
        _resident((None, HG_I, D_MODEL), (j, 0, 0), 2),
        _resident((1, D_MODEL), (0, 0), 2),
    ]
    args = [x, mods, norm_g, w_in, lb_raw, og, w_out, fg]
    aliases = {}
    if prev_states is not None:
        in_specs.append(pl.BlockSpec(memory_space=pl.ANY))
        args.append(prev_states)
        aliases = {len(args) - 1: 1}
    return pl.pallas_call(
        functools.partial(_hgrn_prompt_kernel, tb=tb, gb=gb, layer=layer, final=final,
                          has_prev=prev_states is not None),
        grid=(bsz // gb, seq // tb),
        in_specs=in_specs,
        out_specs=[
            pl.BlockSpec((gb, tb, D_MODEL), lambda b, t: (b, t, 0)),
            pl.BlockSpec((None, gb, HG_HEADS, HG_DK, HG_DV), lambda b, t: (j, b, 0, 0, 0)),
        ],
        out_shape=[
            jax.ShapeDtypeStruct((bsz, seq, D_MODEL), F32),
            jax.ShapeDtypeStruct((n_hgrn, bsz, HG_HEADS, HG_DK, HG_DV), F32),
        ],
        scratch_shapes=[pltpu.VMEM((gb, tb, HG_F), F32)] * 4 + [pltpu.VMEM((gb, tb, HG_I), F32)] * 2,
        input_output_aliases=aliases,
        compiler_params=_params(2),
        name="hgrn_prompt",
    )(*args)


def _hgrn_sample_pre_kernel(x_ref, mod_ref, g_ref, win_ref, lbraw_ref,
                            qe_ref, kr_ref, v_ref, el_ref, oi_ref, z_ref, *, n_tok, layer):
    ns = x_ref.shape[0]
    h = _sample_hidden(x_ref, mod_ref, g_ref, n_tok)
    lb = _lower_bound(lbraw_ref[...], layer)
    qf = _silu(_dot(h, win_ref[:, 0:HG_F]))
    lf, kk = _gate_math(_dot(h, win_ref[:, HG_F:2 * HG_F]), lb)
    vv = _dot(h, win_ref[:, 2 * HG_F:2 * HG_F + HG_I])
    z = _dot(h, win_ref[:, 2 * HG_F + HG_I:])

    def tok(a, s):
        return a[s * ns:(s + 1) * ns]

    g = [tok(lf, 0)]
    for s in range(1, n_tok):
        g.append(g[-1] + tok(lf, s))
    el_ref[...] = jnp.exp(g[-1])
    for s in range(n_tok):
        qe_ref[:, _tok(s, HG_F)] = tok(qf, s) * jnp.exp(g[s])
        kr_ref[:, _tok(s, HG_F)] = tok(kk, s) * jnp.exp(g[-1] - g[s])
        v_ref[:, _tok(s, HG_I)] = tok(vv, s)
        z_ref[:, _tok(s, HG_I)] = tok(z, s)

    for t in range(n_tok):
        acc = jnp.zeros((ns, HG_I), F32)
        for s in range(t + 1):
            e = tok(qf, t) * tok(kk, s)
            if s < t:
                e = e * jnp.exp(g[t] - g[s])
            vs = tok(vv, s)
            parts = []
            for hd in range(HG_HEADS):
                hs = slice(hd * HG_DK, (hd + 1) * HG_DK)
                parts.append(jnp.sum(e[:, hs], axis=-1, keepdims=True) * vs[:, hs])
            acc = acc + jnp.concatenate(parts, axis=1)
        oi_ref[:, _tok(t, HG_I)] = acc


def _hgrn_sample_pre_call(x, mods, norm_g, w_in, lb_raw, layer, n_tok):
    n_seq = x.shape[0]
    j = layer // 2
    act = jax.ShapeDtypeStruct((n_seq, n_tok * HG_F), F32)
    return pl.pallas_call(
        functools.partial(_hgrn_sample_pre_kernel, n_tok=n_tok, layer=layer),
        grid=(n_seq // SAMPLE_SEQ_BLOCK,),
        in_specs=[
            _seq_rows(n_tok * D_MODEL),
            _seq_rows(3 * D_MODEL, layer),
            _resident((None, 1, D_MODEL), (layer, 0, 0), 1),
            _resident((None, D_MODEL, 2 * HG_F + 2 * HG_I), (j, 0, 0), 1),
            _resident((DEPTH, HG_F), (0, 0), 1),
        ],
        out_specs=[_seq_rows(n_tok * HG_F)] * 3 + [_seq_rows(HG_F)] + [_seq_rows(n_tok * HG_I)] * 2,
        out_shape=[act, act, act, jax.ShapeDtypeStruct((n_seq, HG_F), F32), act, act],
        compiler_params=_params(1),
        name="hgrn_sample_pre",
    )(x, mods, norm_g, w_in, lb_raw)


def _hgrn_sample_state_kernel(*refs, n_tok, has_prev):
    n_in = 5 + int(has_prev)
    qe_ref, kr_ref, v_ref, el_ref, s_ref = refs[:5]
    oo_ref, so_ref = refs[n_in:n_in + 2]
    ns = SAMPLE_STATE_BLOCK
    seq_of_row = lax.broadcasted_iota(jnp.int32, (n_tok * ns, HG_DK), 0) % ns
    for hd in range(HG_HEADS):
        hs = slice(hd * HG_DK, (hd + 1) * HG_DK)
        q = _stack_tokens(qe_ref, n_tok, HG_F, hs).astype(BF16)
        k = _stack_tokens(kr_ref, n_tok, HG_F, hs)
        v = _stack_tokens(v_ref, n_tok, HG_I, hs).astype(BF16)
        el = el_ref[:, hs]
        o = jnp.zeros((n_tok * ns, HG_DV), F32)
        for b in range(ns):
            mine = seq_of_row == b
            s_old = s_ref[b, hd]
            o = jnp.where(mine, _dot(q, s_old.astype(BF16)), o)
            kb = jnp.where(mine, k, 0.0).astype(BF16)
            dec = jnp.broadcast_to(el[b:b + 1, :], (HG_DK, HG_DK)).T
            so_ref[b, hd] = s_old * dec + _dot_tn(kb, v)
        for t in range(n_tok):
            oo_ref[:, t * HG_I + hs.start:t * HG_I + hs.stop] = o[t * ns:(t + 1) * ns]


def _hgrn_sample_state_call(qe, kr, v, el, states, prev_states, layer, n_tok):
    n_seq = qe.shape[0]
    j = layer // 2
    nb = SAMPLE_STATE_BLOCK
    rows = lambda width: pl.BlockSpec((nb, width), lambda i: (i, 0))
    st_spec = pl.BlockSpec((None, nb, HG_HEADS, HG_DK, HG_DV), lambda i: (j, i, 0, 0, 0))
    in_specs = [rows(n_tok * HG_F), rows(n_tok * HG_F), rows(n_tok * HG_I), rows(HG_F), st_spec]
    args = [qe, kr, v, el, states]
    aliases = {}
    if prev_states is not None:
        in_specs.append(pl.BlockSpec(memory_space=pl.ANY))
        args.append(prev_states)
        aliases = {len(args) - 1: 1}
    return pl.pallas_call(
        functools.partial(_hgrn_sample_state_kernel, n_tok=n_tok, has_prev=prev_states is not None),
        grid=(n_seq // nb,),
        in_specs=in_specs,
        out_specs=[rows(n_tok * HG_I), st_spec],
        out_shape=[
            jax.ShapeDtypeStruct((n_seq, n_tok * HG_I), F32),
            jax.ShapeDtypeStruct(states.shape, F32),
        ],
        input_output_aliases=aliases,
        compiler_params=_params(1),
        name="hgrn_sample_state",
    )(*args)


def _hgrn_sample_post_kernel(x_ref, mod_ref, oa_ref, ob_ref, z_ref, og_ref, wout_ref, fg_ref, xo_ref,
                             *, n_tok, final):
    ns = x_ref.shape[0]
    o = _stack_tokens(oa_ref, n_tok, HG_I) + _stack_tokens(ob_ref, n_tok, HG_I)
    y = _head_norm_gate(o, _stack_tokens(z_ref, n_tok, HG_I), og_ref[...])
    out = _dot(y, wout_ref[...])
    gate = mod_ref[:, 2 * D_MODEL:]
    for s in range(n_tok):
        xn = x_ref[:, _tok(s, D_MODEL)] + gate * out[s * ns:(s + 1) * ns]
        if final:
            xn = _rmsnorm(xn, fg_ref[...])
        xo_ref[:, _tok(s, D_MODEL)] = xn


def _hgrn_sample_post_call(x, mods, o_inter, o_intra, z, og, w_out, fg, layer, n_tok):
    n_seq = x.shape[0]
    j = layer // 2
    return pl.pallas_call(
        functools.partial(_hgrn_sample_post_kernel, n_tok=n_tok, final=layer == DEPTH - 1),
        grid=(n_seq // SAMPLE_SEQ_BLOCK,),
        in_specs=[
            _seq_rows(n_tok * D_MODEL),
            _seq_rows(3 * D_MODEL, layer),
            _seq_rows(n_tok * HG_I), _seq_rows(n_tok * HG_I), _seq_rows(n_tok * HG_I),
            _resident((None, 1, HG_I), (j, 0, 0), 1),
            _resident((None, HG_I, D_MODEL), (j, 0, 0), 1),
            _resident((1, D_MODEL), (0, 0), 1),
        ],
        out_specs=_seq_rows(n_tok * D_MODEL),
        out_shape=jax.ShapeDtypeStruct((n_seq, n_tok * D_MODEL), F32),
        compiler_params=_params(1),
        name="hgrn_sample_post",
    )(x, mods, o_inter, o_intra, z, og, w_out, fg)


def kernel(x_prompt, x_sample, state_conv, state_hgrn, c_prompt, c_sample, norm_g, w_ada, b_ada, conv_w_in, conv_w, conv_w_out, hgrn_w_in, hgrn_lower_bounds, hgrn_onorm_g, hgrn_w_out, final_norm_g):
    n_p = x_prompt.shape[0]
    n_s, t_s, _ = x_sample.shape
    assert t_s >= CONV_WIDTH - 1 and n_s % SAMPLE_SEQ_BLOCK == 0 and n_s % n_p == 0

    mods = _ada_call(jnp.concatenate([c_sample, c_prompt], axis=0), w_ada, b_ada)
    mod_block_p = n_s // n_p

    conv_w_in_b = conv_w_in.astype(BF16)
    conv_w_out_b = conv_w_out.astype(BF16)
    hgrn_w_in_b = hgrn_w_in.astype(BF16)
    hgrn_w_out_b = hgrn_w_out.astype(BF16)
    norm_g3 = norm_g.reshape(DEPTH, 1, D_MODEL)
    og3 = hgrn_onorm_g.reshape(DEPTH // 2, 1, HG_I)
    fg = final_norm_g.reshape(1, D_MODEL)
    state_conv2 = state_conv.reshape(state_conv.shape[0], n_s, (CONV_WIDTH - 1) * E_CONV)

    xp = x_prompt
    xs = x_sample.reshape(n_s, t_s * D_MODEL)
    conv_p, conv_s = [], []
    hgrn_p = hgrn_s = None
    for layer in range(DEPTH):
        if layer % 2 == 0:
            xp, tail = _conv_prompt_call(xp, mods, norm_g3, conv_w_in_b, conv_w, conv_w_out_b, layer, mod_block_p)
            conv_p.append(tail[:, SUBLANE - (CONV_WIDTH - 1):])
            xs, st = _conv_sample_call(xs, mods, norm_g3, conv_w_in_b, conv_w, conv_w_out_b, state_conv2,
                                       layer, t_s)
            conv_s.append(st.reshape(n_s, CONV_WIDTH - 1, E_CONV))
        else:
            xp, hgrn_p = _hgrn_prompt_call(xp, mods, norm_g3, hgrn_w_in_b, hgrn_lower_bounds, og3,
                                           hgrn_w_out_b, fg, hgrn_p, layer, mod_block_p)
            qe, kr, v, el, o_intra, z = _hgrn_sample_pre_call(xs, mods, norm_g3, hgrn_w_in_b,
                                                              hgrn_lower_bounds, layer, t_s)
            o_inter, hgrn_s = _hgrn_sample_state_call(qe, kr, v, el, state_hgrn, hgrn_s, layer, t_s)
            xs = _hgrn_sample_post_call(xs, mods, o_inter, o_intra, z, og3, hgrn_w_out_b, fg, layer, t_s)
    return (xp, xs.reshape(n_s, t_s, D_MODEL), jnp.stack(conv_p), hgrn_p, jnp.stack(conv_s), hgrn_s)
```

```python
import functools

import jax
import jax.numpy as jnp
from jax import lax
from jax.experimental import pallas as pl
from jax.experimental.pallas import tpu as pltpu

F32 = jnp.float32
BF16 = jnp.bfloat16

D_MODEL = 1024
DEPTH = 4
CONV_WIDTH = 3
E_CONV = D_MODEL
HG_HEADS = 8
HG_DK = 128
HG_DV = 128
HG_F = HG_HEADS * HG_DK
HG_I = HG_HEADS * HG_DV
EPS = 1e-6

SUBLANE = 8
CHUNK = 128
SUB = CHUNK // 2
CONV_COL_TILE = 256
CONV_TB = 512
CONV_GB = 2
HGRN_TB = 256
HGRN_GB = 2
HGRN_PIPE_SKEW = 3
SAMPLE_SEQ_BLOCK = 32
SAMPLE_STATE_BLOCK = 8
VMEM_LIMIT = 56 * 1024 * 1024


def _dot(a, b):
    return jnp.dot(a, b, preferred_element_type=F32)


def _dot_nt(a, b):
    return lax.dot_general(a, b, (((1,), (1,)), ((), ())), preferred_element_type=F32)


def _dot_tn(a, b):
    return lax.dot_general(a, b, (((0,), (0,)), ((), ())), preferred_element_type=F32)


def _silu(x):
    return x * jax.nn.sigmoid(x)


def _rmsnorm(x, g):
    ms = jnp.mean(x * x, axis=-1, keepdims=True)
    return x * lax.rsqrt(ms + EPS) * g


def _split_mod(mod):
    return mod[:, :D_MODEL], mod[:, D_MODEL:2 * D_MODEL], mod[:, 2 * D_MODEL:]


def _mod_norm(x, g, shift, scale):
    return (_rmsnorm(x, g) * (1.0 + scale) + shift).astype(BF16)


def _lower_bound(raw, layer):
    rows = [raw[i:i + 1, :] for i in range(DEPTH)]
    m = functools.reduce(jnp.maximum, rows)
    es = [jnp.exp(r - m) for r in rows]
    tot = functools.reduce(lambda a, b: a + b, es)
    acc = es[1]
    for i in range(2, layer + 1):
        acc = acc + es[i]
    return acc / tot


def _gate_math(fpre, lb):
    t = jnp.exp(-jnp.abs(fpre))
    r = 1.0 / (1.0 + t)
    pos = fpre >= 0.0
    sig = jnp.where(pos, 1.0, t) * r
    sig_neg = jnp.where(pos, t, 1.0) * r
    return jnp.log2(lb + (1.0 - lb) * sig), (1.0 - lb) * sig_neg


def _head_norm_gate(o, z, og):
    parts = []
    for hd in range(HG_HEADS):
        hs = slice(hd * HG_DV, (hd + 1) * HG_DV)
        oh = o[:, hs]
        ms = jnp.mean(oh * oh, axis=-1, keepdims=True)
        parts.append(oh * lax.rsqrt(ms + EPS))
    on = jnp.concatenate(parts, axis=1) * og
    return (on * _silu(z)).astype(BF16)


def _params(n_grid):
    return pltpu.CompilerParams(dimension_semantics=("arbitrary",) * n_grid, vmem_limit_bytes=VMEM_LIMIT)


def _resident(shape, index, n_grid):
    if n_grid == 1:
        return pl.BlockSpec(shape, lambda i: index, pipeline_mode=pl.Buffered(1))
    return pl.BlockSpec(shape, lambda b, t: index, pipeline_mode=pl.Buffered(1))


def _ada_kernel(c_ref, w_ref, b_ref, o_ref):
    s = _silu(c_ref[...]).astype(BF16)
    o_ref[...] = _dot(s, w_ref[...].astype(BF16)) + b_ref[...]


def _ada_call(c_all, w_ada, b_ada):
    rows = c_all.shape[0]
    n_col = 3
    col = 3 * D_MODEL // n_col
    return pl.pallas_call(
        _ada_kernel,
        grid=(DEPTH, n_col),
        in_specs=[
            pl.BlockSpec((rows, D_MODEL), lambda l, n: (0, 0)),
            pl.BlockSpec((None, D_MODEL, col), lambda l, n: (l, 0, n)),
            pl.BlockSpec((None, 1, col), lambda l, n: (l, 0, n)),
        ],
        out_specs=pl.BlockSpec((None, rows, col), lambda l, n: (l, 0, n)),
        out_shape=jax.ShapeDtypeStruct((DEPTH, rows, 3 * D_MODEL), F32),
        compiler_params=_params(2),
        name="ada_mod",
    )(c_all, w_ada, b_ada.reshape(DEPTH, 1, 3 * D_MODEL))


def _conv_columns(h, win_ref, cw_ref, y_scr, prev_fn, tail_fn):
    ct = CONV_COL_TILE
    for n in range(E_CONV // ct):
        def col(j):
            return slice(j * E_CONV + n * ct, j * E_CONV + (n + 1) * ct)
        cs = slice(n * ct, (n + 1) * ct)
        b_gate = _dot(h, win_ref[:, col(0)])
        c_gate = _dot(h, win_ref[:, col(1)])
        v = _dot(h, win_ref[:, col(2)])
        z = _dot(h, win_ref[:, col(3)])
        u = c_gate * v
        p1, p2 = prev_fn(cs, u)
        conv = cw_ref[0:1, cs] * p2 + cw_ref[1:2, cs] * p1 + cw_ref[2:3, cs] * u
        y_scr[:, cs] = (b_gate * conv * _silu(z)).astype(BF16)
        tail_fn(cs, u)


def _conv_prompt_kernel(x_ref, mod_ref, g_ref, win_ref, cw_ref, wout_ref, xo_ref, st_ref, y_scr, *, tb, gb):
    b = pl.program_id(0)
    t = pl.program_id(1)
    ct = CONV_COL_TILE

    @pl.when(t == 0)
    def _():
        st_ref[...] = jnp.zeros_like(st_ref)

    seqs = range(gb)
    x = [x_ref[g] for g in seqs]
    mod = [_split_mod(mod_ref[pl.ds(b * gb + g, 1), :]) for g in seqs]
    h = [_mod_norm(x[g], g_ref[...], mod[g][0], mod[g][1]) for g in seqs]
    row = lax.broadcasted_iota(jnp.int32, (tb, ct), 0)

    for g in seqs:
        def prev_fn(cs, u, g=g):
            c0 = st_ref[g, SUBLANE - 2:SUBLANE - 1, cs]
            c1 = st_ref[g, SUBLANE - 1:SUBLANE, cs]
            p1 = jnp.where(row == 0, c1, pltpu.roll(u, 1, 0))
            p2 = jnp.where(row == 0, c0, jnp.where(row == 1, c1, pltpu.roll(u, 2, 0)))
            return p1, p2

        def tail_fn(cs, u, g=g):
            st_ref[g, :, cs] = u[tb - SUBLANE:tb, :]

        _conv_columns(h[g], win_ref, cw_ref, y_scr.at[g], prev_fn, tail_fn)
    for g in seqs:
        xo_ref[g] = x[g] + mod[g][2] * _dot(y_scr[g], wout_ref[...])


def _conv_prompt_call(x, mods, norm_g, w_in, cw, w_out, layer, mod_block):
    bsz, seq, _ = x.shape
    tb = CONV_TB
    gb = CONV_GB
    j = layer // 2
    return pl.pallas_call(
        functools.partial(_conv_prompt_kernel, tb=tb, gb=gb),
        grid=(bsz // gb, seq // tb),
        in_specs=[
            pl.BlockSpec((gb, tb, D_MODEL), lambda b, t: (b, t, 0)),
            _resident((None, bsz, 3 * D_MODEL), (layer, mod_block, 0), 2),
            _resident((None, 1, D_MODEL), (layer, 0, 0), 2),
            _resident((None, D_MODEL, 4 * E_CONV), (j, 0, 0), 2),
            _resident((None, CONV_WIDTH, E_CONV), (j, 0, 0), 2),
            _resident((None, E_CONV, D_MODEL), (j, 0, 0), 2),
        ],
        out_specs=[
            pl.BlockSpec((gb, tb, D_MODEL), lambda b, t: (b, t, 0)),
            pl.BlockSpec((gb, SUBLANE, E_CONV), lambda b, t: (b, 0, 0)),
        ],
        out_shape=[
            jax.ShapeDtypeStruct((bsz, seq, D_MODEL), F32),
            jax.ShapeDtypeStruct((bsz, SUBLANE, E_CONV), F32),
        ],
        scratch_shapes=[pltpu.VMEM((gb, tb, E_CONV), BF16)],
        compiler_params=_params(2),
        name="conv_prompt",
    )(x, mods, norm_g, w_in, cw, w_out)


def _tok(s, width):
    return slice(s * width, (s + 1) * width)


def _stack_tokens(ref, n_tok, width, cols=None):
    parts = []
    for s in range(n_tok):
        lo = s * width + (0 if cols is None else cols.start)
        hi = s * width + (width if cols is None else cols.stop)
        parts.append(ref[:, lo:hi])
    return jnp.concatenate(parts, axis=0)


def _sample_hidden(x_ref, mod_ref, g_ref, n_tok):
    shift, scale, _ = _split_mod(mod_ref[...])
    g = g_ref[...]
    return jnp.concatenate(
        [_mod_norm(x_ref[:, _tok(s, D_MODEL)], g, shift, scale) for s in range(n_tok)], axis=0)


def _conv_sample_kernel(x_ref, mod_ref, g_ref, win_ref, cw_ref, wout_ref, st_ref,
                        xo_ref, so_ref, y_scr, *, n_tok):
    ns = x_ref.shape[0]
    h = _sample_hidden(x_ref, mod_ref, g_ref, n_tok)

    def prev_fn(cs, u):
        st0 = st_ref[:, cs]
        st1 = st_ref[:, E_CONV + cs.start:E_CONV + cs.stop]
        p1 = jnp.concatenate([st1, u[:(n_tok - 1) * ns]], axis=0)
        p2 = jnp.concatenate([st0, st1, u[:(n_tok - 2) * ns]], axis=0)
        return p1, p2

    def tail_fn(cs, u):
        so_ref[:, cs] = u[(n_tok - 2) * ns:(n_tok - 1) * ns]
        so_ref[:, E_CONV + cs.start:E_CONV + cs.stop] = u[(n_tok - 1) * ns:]

    _conv_columns(h, win_ref, cw_ref, y_scr, prev_fn, tail_fn)
    out = _dot(y_scr[...], wout_ref[...])
    gate = mod_ref[:, 2 * D_MODEL:]
    for s in range(n_tok):
        xo_ref[:, _tok(s, D_MODEL)] = x_ref[:, _tok(s, D_MODEL)] + gate * out[s * ns:(s + 1) * ns]


def _seq_rows(width, layer=None):
    if layer is None:
        return pl.BlockSpec((SAMPLE_SEQ_BLOCK, width), lambda i: (i, 0))
    return pl.BlockSpec((None, SAMPLE_SEQ_BLOCK, width), lambda i: (layer, i, 0))


def _conv_sample_call(x, mods, norm_g, w_in, cw, w_out, state, layer, n_tok):
    n_seq = x.shape[0]
    j = layer // 2
    return pl.pallas_call(
        functools.partial(_conv_sample_kernel, n_tok=n_tok),
        grid=(n_seq // SAMPLE_SEQ_BLOCK,),
        in_specs=[
            _seq_rows(n_tok * D_MODEL),
            _seq_rows(3 * D_MODEL, layer),
            _resident((None, 1, D_MODEL), (layer, 0, 0), 1),
            _resident((None, D_MODEL, 4 * E_CONV), (j, 0, 0), 1),
            _resident((None, CONV_WIDTH, E_CONV), (j, 0, 0), 1),
            _resident((None, E_CONV, D_MODEL), (j, 0, 0), 1),
            _seq_rows((CONV_WIDTH - 1) * E_CONV, j),
        ],
        out_specs=[_seq_rows(n_tok * D_MODEL), _seq_rows((CONV_WIDTH - 1) * E_CONV)],
        out_shape=[
            jax.ShapeDtypeStruct((n_seq, n_tok * D_MODEL), F32),
            jax.ShapeDtypeStruct((n_seq, (CONV_WIDTH - 1) * E_CONV), F32),
        ],
        scratch_shapes=[pltpu.VMEM((n_tok * SAMPLE_SEQ_BLOCK, E_CONV), BF16)],
        compiler_params=_params(1),
        name="conv_sample",
    )(x, mods, norm_g, w_in, cw, w_out, state)


def _rows_from_blocks(vecs):
    return jnp.concatenate([jnp.broadcast_to(v, (SUB, v.shape[1])) for v in vecs], axis=0)


def _chunk_masks():
    r = lax.broadcasted_iota(jnp.int32, (CHUNK, CHUNK), 0)
    c = lax.broadcasted_iota(jnp.int32, (CHUNK, CHUNK), 1)
    causal = (r // SUB == c // SUB) & (c <= r)
    tri = jnp.where(causal, 1.0, 0.0).astype(BF16)
    first_cols = lax.broadcasted_iota(jnp.int32, (SUB, CHUNK), 1) < SUB
    return causal, first_cols, jnp.concatenate([tri, tri], axis=1)


def _chunk_operands(qf, lf, kk, vv, tri2):
    hi = lf.astype(BF16)
    lo = (lf - hi.astype(F32)).astype(BF16)
    gl = _dot(tri2, jnp.concatenate([hi, lo], axis=0))
    t0 = gl[SUB - 1:SUB, :]
    t1 = gl[2 * SUB - 1:2 * SUB, :]
    mid = _rows_from_blocks([0.5 * t0, 0.5 * t1])
    q_mid = qf * jnp.exp2(gl - mid)
    k_mid = kk * jnp.exp2(mid - gl)

    q_next = q_mid[SUB:] * jnp.exp2(0.5 * (t0 + t1))
    q_state = q_mid * _rows_from_blocks([jnp.exp2(0.5 * t0), jnp.exp2(0.5 * t1 + t0)])
    k_state = k_mid * _rows_from_blocks([jnp.exp2(0.5 * t0 + t1), jnp.exp2(0.5 * t1)])
    return dict(lhs=jnp.concatenate([q_mid, q_next], axis=0).astype(BF16), kb=k_mid.astype(BF16),
                qe=q_state.astype(BF16), kr=k_state.astype(BF16), vb=vv.astype(BF16),
                ptot=jnp.exp2(t0 + t1))


def _chunk_scores(ops):
    return _dot_nt(ops["lhs"], ops["kb"])


def _chunk_finish(ops, ast, s_old, causal, first_cols):
    same = jnp.where(causal, ast[0:CHUNK], 0.0)
    a = jnp.concatenate([same[:SUB], jnp.where(first_cols, ast[CHUNK:], same[SUB:])], axis=0)
    o = _dot(jnp.concatenate([ops["qe"], a.astype(BF16)], axis=1),
             jnp.concatenate([s_old.astype(BF16), ops["vb"]], axis=0))
    dec = jnp.broadcast_to(ops["ptot"], (HG_DK, HG_DK)).T
    return o, s_old * dec + _dot_tn(ops["kr"], ops["vb"])


def _hgrn_prompt_kernel(x_ref, mod_ref, g_ref, win_ref, lbraw_ref, og_ref, wout_ref, fg_ref,
                        xo_ref, s_ref, q_scr, lf_scr, k_scr, v_scr, z_scr, o_scr, *, tb, gb, layer, final):
    b = pl.program_id(0)
    t = pl.program_id(1)

    @pl.when(t == 0)
    def _():
        s_ref[...] = jnp.zeros_like(s_ref)

    seqs = range(gb)
    lb = _lower_bound(lbraw_ref[...], layer)
    x = [x_ref[g] for g in seqs]
    mod = [_split_mod(mod_ref[pl.ds(b * gb + g, 1), :]) for g in seqs]
    h = [_mod_norm(x[g], g_ref[...], mod[g][0], mod[g][1]) for g in seqs]
    for g in seqs:
        q_scr[g] = _silu(_dot(h[g], win_ref[:, 0:HG_F]))
    for g in seqs:
        logf, kk = _gate_math(_dot(h[g], win_ref[:, HG_F:2 * HG_F]), lb)
        lf_scr[g] = logf
        k_scr[g] = kk
    for g in seqs:
        v_scr[g] = _dot(h[g], win_ref[:, 2 * HG_F:2 * HG_F + HG_I])
    for g in seqs:
        z_scr[g] = _dot(h[g], win_ref[:, 2 * HG_F + HG_I:])

    causal, first_cols, tri = _chunk_masks()
    chains = [(c, hd, g) for c in range(tb // CHUNK) for hd in range(HG_HEADS) for g in seqs]
    ops, scores = {}, {}

    def where(n):
        c, hd, g = chains[n]
        return g, hd, slice(c * CHUNK, (c + 1) * CHUNK), slice(hd * HG_DK, (hd + 1) * HG_DK)

    for step in range(len(chains) + 2 * HGRN_PIPE_SKEW):
        n = step
        if n < len(chains):
            g, hd, rows, hs = where(n)
            ops[n] = _chunk_operands(q_scr[g, rows, hs], lf_scr[g, rows, hs], k_scr[g, rows, hs],
                                     v_scr[g, rows, hs], tri)
        n = step - HGRN_PIPE_SKEW
        if 0 <= n < len(chains):
            scores[n] = _chunk_scores(ops[n])
        n = step - 2 * HGRN_PIPE_SKEW
        if 0 <= n < len(chains):
            g, hd, rows, hs = where(n)
            o, s_new = _chunk_finish(ops.pop(n), scores.pop(n), s_ref[g, hd], causal, first_cols)
            o_scr[g, rows, hs] = o
            s_ref[g, hd] = s_new

    y = [_head_norm_gate(o_scr[g], z_scr[g], og_ref[...]) for g in seqs]
    for g in seqs:
        xn = x[g] + mod[g][2] * _dot(y[g], wout_ref[...])
        if final:
            xn = _rmsnorm(xn, fg_ref[...])
        xo_ref[g] = xn


def _hgrn_prompt_call(x, mods, norm_g, w_in, lb_raw, og, w_out, fg, layer, mod_block):
    bsz, seq, _ = x.shape
    tb = HGRN_TB
    gb = HGRN_GB
    j = layer // 2
    final = layer == DEPTH - 1
    in_specs = [
        pl.BlockSpec((gb, tb, D_MODEL), lambda b, t: (b, t, 0)),
        _resident((None, bsz, 3 * D_MODEL), (layer, mod_block, 0), 2),
        _resident((None, 1, D_MODEL), (layer, 0, 0), 2),
        _resident((None, D_MODEL, 2 * HG_F + 2 * HG_I), (j, 0, 0), 2),
        _resident((DEPTH, HG_F), (0, 0), 2),
        _resident((None, 1, HG_I), (j, 0, 0), 2),
        _resident((None, HG_I, D_MODEL), (j, 0, 0), 2),
        _resident((1, D_MODEL), (0, 0), 2),
    ]
    return pl.pallas_call(
        functools.partial(_hgrn_prompt_kernel, tb=tb, gb=gb, layer=layer, final=final),
        grid=(bsz // gb, seq // tb),
        in_specs=in_specs,
        out_specs=[
            pl.BlockSpec((gb, tb, D_MODEL), lambda b, t: (b, t, 0)),
            pl.BlockSpec((gb, HG_HEADS, HG_DK, HG_DV), lambda b, t: (b, 0, 0, 0)),
        ],
        out_shape=[
            jax.ShapeDtypeStruct((bsz, seq, D_MODEL), F32),
            jax.ShapeDtypeStruct((bsz, HG_HEADS, HG_DK, HG_DV), F32),
        ],
        scratch_shapes=[pltpu.VMEM((gb, tb, HG_F), F32)] * 4 + [pltpu.VMEM((gb, tb, HG_I), F32)] * 2,
        compiler_params=_params(2),
        name="hgrn_prompt",
    )(x, mods, norm_g, w_in, lb_raw, og, w_out, fg)


def _hgrn_sample_pre_kernel(x_ref, mod_ref, g_ref, win_ref, lbraw_ref,
                            qe_ref, kr_ref, v_ref, el_ref, oi_ref, z_ref, *, n_tok, layer):
    ns = x_ref.shape[0]
    h = _sample_hidden(x_ref, mod_ref, g_ref, n_tok)
    lb = _lower_bound(lbraw_ref[...], layer)
    qf = _silu(_dot(h, win_ref[:, 0:HG_F]))
    lf, kk = _gate_math(_dot(h, win_ref[:, HG_F:2 * HG_F]), lb)
    vv = _dot(h, win_ref[:, 2 * HG_F:2 * HG_F + HG_I])
    z = _dot(h, win_ref[:, 2 * HG_F + HG_I:])

    def tok(a, s):
        return a[s * ns:(s + 1) * ns]

    g = [tok(lf, 0)]
    for s in range(1, n_tok):
        g.append(g[-1] + tok(lf, s))
    el_ref[...] = jnp.exp2(g[-1])
    for s in range(n_tok):
        qe_ref[:, _tok(s, HG_F)] = tok(qf, s) * jnp.exp2(g[s])
        kr_ref[:, _tok(s, HG_F)] = tok(kk, s) * jnp.exp2(g[-1] - g[s])
        v_ref[:, _tok(s, HG_I)] = tok(vv, s)
        z_ref[:, _tok(s, HG_I)] = tok(z, s)

    for t in range(n_tok):
        acc = jnp.zeros((ns, HG_I), F32)
        for s in range(t + 1):
            e = tok(qf, t) * tok(kk, s)
            if s < t:
                e = e * jnp.exp2(g[t] - g[s])
            vs = tok(vv, s)
            parts = []
            for hd in range(HG_HEADS):
                hs = slice(hd * HG_DK, (hd + 1) * HG_DK)
                parts.append(jnp.sum(e[:, hs], axis=-1, keepdims=True) * vs[:, hs])
            acc = acc + jnp.concatenate(parts, axis=1)
        oi_ref[:, _tok(t, HG_I)] = acc


def _hgrn_sample_pre_call(x, mods, norm_g, w_in, lb_raw, layer, n_tok):
    n_seq = x.shape[0]
    j = layer // 2
    act = jax.ShapeDtypeStruct((n_seq, n_tok * HG_F), F32)
    return pl.pallas_call(
        functools.partial(_hgrn_sample_pre_kernel, n_tok=n_tok, layer=layer),
        grid=(n_seq // SAMPLE_SEQ_BLOCK,),
        in_specs=[
            _seq_rows(n_tok * D_MODEL),
            _seq_rows(3 * D_MODEL, layer),
            _resident((None, 1, D_MODEL), (layer, 0, 0), 1),
            _resident((None, D_MODEL, 2 * HG_F + 2 * HG_I), (j, 0, 0), 1),
            _resident((DEPTH, HG_F), (0, 0), 1),
        ],
        out_specs=[_seq_rows(n_tok * HG_F)] * 3 + [_seq_rows(HG_F)] + [_seq_rows(n_tok * HG_I)] * 2,
        out_shape=[act, act, act, jax.ShapeDtypeStruct((n_seq, HG_F), F32), act, act],
        compiler_params=_params(1),
        name="hgrn_sample_pre",
    )(x, mods, norm_g, w_in, lb_raw)


def _hgrn_sample_state_kernel(*refs, n_tok, has_prev):
    n_in = 5 + int(has_prev)
    qe_ref, kr_ref, v_ref, el_ref, s_ref = refs[:5]
    oo_ref, so_ref = refs[n_in:n_in + 2]
    ns = SAMPLE_STATE_BLOCK
    seq_of_row = lax.broadcasted_iota(jnp.int32, (n_tok * ns, HG_DK), 0) % ns
    for hd in range(HG_HEADS):
        hs = slice(hd * HG_DK, (hd + 1) * HG_DK)
        q = _stack_tokens(qe_ref, n_tok, HG_F, hs).astype(BF16)
        k = _stack_tokens(kr_ref, n_tok, HG_F, hs)
        v = _stack_tokens(v_ref, n_tok, HG_I, hs).astype(BF16)
        el = el_ref[:, hs]
        o = jnp.zeros((n_tok * ns, HG_DV), F32)
        for b in range(ns):
            mine = seq_of_row == b
            s_old = s_ref[b, hd]
            o = jnp.where(mine, _dot(q, s_old.astype(BF16)), o)
            kb = jnp.where(mine, k, 0.0).astype(BF16)
            dec = jnp.broadcast_to(el[b:b + 1, :], (HG_DK, HG_DK)).T
            so_ref[b, hd] = s_old * dec + _dot_tn(kb, v)
        for t in range(n_tok):
            oo_ref[:, t * HG_I + hs.start:t * HG_I + hs.stop] = o[t * ns:(t + 1) * ns]


def _hgrn_sample_state_call(qe, kr, v, el, states, prev_states, layer, n_tok):
    n_seq = qe.shape[0]
    j = layer // 2
    nb = SAMPLE_STATE_BLOCK
    rows = lambda width: pl.BlockSpec((nb, width), lambda i: (i, 0))
    st_spec = pl.BlockSpec((None, nb, HG_HEADS, HG_DK, HG_DV), lambda i: (j, i, 0, 0, 0))
    in_specs = [rows(n_tok * HG_F), rows(n_tok * HG_F), rows(n_tok * HG_I), rows(HG_F), st_spec]
    args = [qe, kr, v, el, states]
    aliases = {}
    if prev_states is not None:
        in_specs.append(pl.BlockSpec(memory_space=pl.ANY))
        args.append(prev_states)
        aliases = {len(args) - 1: 1}
    return pl.pallas_call(
        functools.partial(_hgrn_sample_state_kernel, n_tok=n_tok, has_prev=prev_states is not None),
        grid=(n_seq // nb,),
        in_specs=in_specs,
        out_specs=[rows(n_tok * HG_I), st_spec],
        out_shape=[
            jax.ShapeDtypeStruct((n_seq, n_tok * HG_I), F32),
            jax.ShapeDtypeStruct(states.shape, F32),
        ],
        input_output_aliases=aliases,
        compiler_params=_params(1),
        name="hgrn_sample_state",
    )(*args)


def _hgrn_sample_post_kernel(x_ref, mod_ref, oa_ref, ob_ref, z_ref, og_ref, wout_ref, fg_ref, xo_ref,
                             *, n_tok, final):
    ns = x_ref.shape[0]
    o = _stack_tokens(oa_ref, n_tok, HG_I) + _stack_tokens(ob_ref, n_tok, HG_I)
    y = _head_norm_gate(o, _stack_tokens(z_ref, n_tok, HG_I), og_ref[...])
    out = _dot(y, wout_ref[...])
    gate = mod_ref[:, 2 * D_MODEL:]
    for s in range(n_tok):
        xn = x_ref[:, _tok(s, D_MODEL)] + gate * out[s * ns:(s + 1) * ns]
        if final:
            xn = _rmsnorm(xn, fg_ref[...])
        xo_ref[:, _tok(s, D_MODEL)] = xn


def _hgrn_sample_post_call(x, mods, o_inter, o_intra, z, og, w_out, fg, layer, n_tok):
    n_seq = x.shape[0]
    j = layer // 2
    return pl.pallas_call(
        functools.partial(_hgrn_sample_post_kernel, n_tok=n_tok, final=layer == DEPTH - 1),
        grid=(n_seq // SAMPLE_SEQ_BLOCK,),
        in_specs=[
            _seq_rows(n_tok * D_MODEL),
            _seq_rows(3 * D_MODEL, layer),
            _seq_rows(n_tok * HG_I), _seq_rows(n_tok * HG_I), _seq_rows(n_tok * HG_I),
            _resident((None, 1, HG_I), (j, 0, 0), 1),
            _resident((None, HG_I, D_MODEL), (j, 0, 0), 1),
            _resident((1, D_MODEL), (0, 0), 1),
        ],
        out_specs=_seq_rows(n_tok * D_MODEL),
        out_shape=jax.ShapeDtypeStruct((n_seq, n_tok * D_MODEL), F32),
        compiler_params=_params(1),
        name="hgrn_sample_post",
    )(x, mods, o_inter, o_intra, z, og, w_out, fg)


def kernel(x_prompt, x_sample, state_conv, state_hgrn, c_prompt, c_sample, norm_g, w_ada, b_ada, conv_w_in, conv_w, conv_w_out, hgrn_w_in, hgrn_lower_bounds, hgrn_onorm_g, hgrn_w_out, final_norm_g):
    n_p = x_prompt.shape[0]
    n_s, t_s, _ = x_sample.shape
    assert t_s >= CONV_WIDTH - 1 and n_s % SAMPLE_SEQ_BLOCK == 0 and n_s % n_p == 0

    mods = _ada_call(jnp.concatenate([c_sample, c_prompt], axis=0), w_ada, b_ada)
    mod_block_p = n_s // n_p

    conv_w_in_b = conv_w_in.astype(BF16)
    conv_w_out_b = conv_w_out.astype(BF16)
    hgrn_w_in_b = hgrn_w_in.astype(BF16)
    hgrn_w_out_b = hgrn_w_out.astype(BF16)
    norm_g3 = norm_g.reshape(DEPTH, 1, D_MODEL)
    og3 = hgrn_onorm_g.reshape(DEPTH // 2, 1, HG_I)
    fg = final_norm_g.reshape(1, D_MODEL)
    state_conv2 = state_conv.reshape(state_conv.shape[0], n_s, (CONV_WIDTH - 1) * E_CONV)

    xp = x_prompt
    xs = x_sample.reshape(n_s, t_s * D_MODEL)
    conv_p, conv_s, hgrn_p = [], [], []
    hgrn_s = None
    for layer in range(DEPTH):
        if layer % 2 == 0:
            xp, tail = _conv_prompt_call(xp, mods, norm_g3, conv_w_in_b, conv_w, conv_w_out_b, layer, mod_block_p)
            conv_p.append(tail[:, SUBLANE - (CONV_WIDTH - 1):])
            xs, st = _conv_sample_call(xs, mods, norm_g3, conv_w_in_b, conv_w, conv_w_out_b, state_conv2,
                                       layer, t_s)
            conv_s.append(st.reshape(n_s, CONV_WIDTH - 1, E_CONV))
        else:
            xp, s_new = _hgrn_prompt_call(xp, mods, norm_g3, hgrn_w_in_b, hgrn_lower_bounds, og3,
                                          hgrn_w_out_b, fg, layer, mod_block_p)
            hgrn_p.append(s_new)
            qe, kr, v, el, o_intra, z = _hgrn_sample_pre_call(xs, mods, norm_g3, hgrn_w_in_b,
                                                              hgrn_lower_bounds, layer, t_s)
            o_inter, hgrn_s = _hgrn_sample_state_call(qe, kr, v, el, state_hgrn, hgrn_s, layer, t_s)
            xs = _hgrn_sample_post_call(xs, mods, o_inter, o_intra, z, og3, hgrn_w_out_b, fg, layer, t_s)
    return (xp, xs.reshape(n_s, t_s, D_MODEL), jnp.stack(conv_p), jnp.stack(hgrn_p), jnp.stack(conv_s), hgrn_s)
```

```python
import functools

import jax
import jax.numpy as jnp
from jax import lax
from jax.experimental import pallas as pl
from jax.experimental.pallas import tpu as pltpu

F32 = jnp.float32
BF16 = jnp.bfloat16

D_MODEL = 1024
DEPTH = 4
CONV_WIDTH = 3
E_CONV = D_MODEL
HG_HEADS = 8
HG_DK = 128
HG_DV = 128
HG_F = HG_HEADS * HG_DK
HG_I = HG_HEADS * HG_DV
EPS = 1e-6

SUBLANE = 8
CHUNK = 128
SUB = CHUNK // 2
CONV_COL_TILE = 256
CONV_TB = 512
CONV_GB = 2
HGRN_TB = 256
HGRN_GB = 2
HGRN_PIPE_SKEW = 3
HGRN_SAFE_LOG2_SPAN = 200.0
SAMPLE_SEQ_BLOCK = 32
SAMPLE_STATE_BLOCK = 8
VMEM_LIMIT = 56 * 1024 * 1024


def _dot(a, b):
    return jnp.dot(a, b, preferred_element_type=F32)


def _dot_nt(a, b):
    return lax.dot_general(a, b, (((1,), (1,)), ((), ())), preferred_element_type=F32)


def _dot_tn(a, b):
    return lax.dot_general(a, b, (((0,), (0,)), ((), ())), preferred_element_type=F32)


def _silu(x):
    return x * jax.nn.sigmoid(x)


def _rmsnorm(x, g):
    ms = jnp.mean(x * x, axis=-1, keepdims=True)
    return x * lax.rsqrt(ms + EPS) * g


def _split_mod(mod):
    return mod[:, :D_MODEL], mod[:, D_MODEL:2 * D_MODEL], mod[:, 2 * D_MODEL:]


def _mod_norm(x, g, shift, scale):
    return (_rmsnorm(x, g) * (1.0 + scale) + shift).astype(BF16)


def _mod_norm_seq(x, g, shift, scale):
    ms = jnp.mean(x * x, axis=-1, keepdims=True)
    return (x * lax.rsqrt(ms + EPS) * (g * (1.0 + scale)) + shift).astype(BF16)


def _lower_bound(raw, layer):
    rows = [raw[i:i + 1, :] for i in range(DEPTH)]
    m = functools.reduce(jnp.maximum, rows)
    es = [jnp.exp(r - m) for r in rows]
    tot = functools.reduce(lambda a, b: a + b, es)
    acc = es[1]
    for i in range(2, layer + 1):
        acc = acc + es[i]
    return acc / tot


def _gate_math(fpre, lb):
    t = jnp.exp(-jnp.abs(fpre))
    r = 1.0 / (1.0 + t)
    pos = fpre >= 0.0
    sig = jnp.where(pos, 1.0, t) * r
    sig_neg = jnp.where(pos, t, 1.0) * r
    return jnp.log2(lb + (1.0 - lb) * sig), (1.0 - lb) * sig_neg


def _head_norm_gate(o, z, og):
    parts = []
    for hd in range(HG_HEADS):
        hs = slice(hd * HG_DV, (hd + 1) * HG_DV)
        oh = o[:, hs]
        ms = jnp.mean(oh * oh, axis=-1, keepdims=True)
        parts.append(oh * lax.rsqrt(ms + EPS))
    on = jnp.concatenate(parts, axis=1) * og
    return (on * _silu(z)).astype(BF16)


def _params(n_grid):
    return pltpu.CompilerParams(dimension_semantics=("arbitrary",) * n_grid, vmem_limit_bytes=VMEM_LIMIT)


def _resident(shape, index, n_grid):
    if n_grid == 1:
        return pl.BlockSpec(shape, lambda i: index, pipeline_mode=pl.Buffered(1))
    return pl.BlockSpec(shape, lambda b, t: index, pipeline_mode=pl.Buffered(1))


def _ada_kernel(c_ref, w_ref, b_ref, o_ref):
    s = _silu(c_ref[...]).astype(BF16)
    o_ref[...] = _dot(s, w_ref[...].astype(BF16)) + b_ref[...]


def _ada_call(c_all, w_ada, b_ada):
    rows = c_all.shape[0]
    n_col = 3
    col = 3 * D_MODEL // n_col
    return pl.pallas_call(
        _ada_kernel,
        grid=(DEPTH, n_col),
        in_specs=[
            pl.BlockSpec((rows, D_MODEL), lambda l, n: (0, 0)),
            pl.BlockSpec((None, D_MODEL, col), lambda l, n: (l, 0, n)),
            pl.BlockSpec((None, 1, col), lambda l, n: (l, 0, n)),
        ],
        out_specs=pl.BlockSpec((None, rows, col), lambda l, n: (l, 0, n)),
        out_shape=jax.ShapeDtypeStruct((DEPTH, rows, 3 * D_MODEL), F32),
        compiler_params=_params(2),
        name="ada_mod",
    )(c_all, w_ada, b_ada.reshape(DEPTH, 1, 3 * D_MODEL))


def _conv_columns(h, win_ref, cw_ref, y_scr, prev_fn, tail_fn):
    ct = CONV_COL_TILE
    for n in range(E_CONV // ct):
        def col(j):
            return slice(j * E_CONV + n * ct, j * E_CONV + (n + 1) * ct)
        cs = slice(n * ct, (n + 1) * ct)
        b_gate = _dot(h, win_ref[:, col(0)])
        c_gate = _dot(h, win_ref[:, col(1)])
        v = _dot(h, win_ref[:, col(2)])
        z = _dot(h, win_ref[:, col(3)])
        u = c_gate * v
        p1, p2 = prev_fn(cs, u)
        conv = cw_ref[0:1, cs] * p2 + cw_ref[1:2, cs] * p1 + cw_ref[2:3, cs] * u
        y_scr[:, cs] = (b_gate * conv * _silu(z)).astype(BF16)
        tail_fn(cs, u)


def _conv_prompt_kernel(x_ref, mod_ref, g_ref, win_ref, cw_ref, wout_ref, xo_ref, st_ref, y_scr, *, tb, gb):
    b = pl.program_id(0)
    t = pl.program_id(1)
    ct = CONV_COL_TILE

    @pl.when(t == 0)
    def _():
        st_ref[...] = jnp.zeros_like(st_ref)

    seqs = range(gb)
    x = [x_ref[g] for g in seqs]
    mod = [_split_mod(mod_ref[pl.ds(b * gb + g, 1), :]) for g in seqs]
    h = [_mod_norm_seq(x[g], g_ref[...], mod[g][0], mod[g][1]) for g in seqs]
    row = lax.broadcasted_iota(jnp.int32, (tb, ct), 0)

    for g in seqs:
        def prev_fn(cs, u, g=g):
            c0 = st_ref[g, SUBLANE - 2:SUBLANE - 1, cs]
            c1 = st_ref[g, SUBLANE - 1:SUBLANE, cs]
            p1 = jnp.where(row == 0, c1, pltpu.roll(u, 1, 0))
            p2 = jnp.where(row == 0, c0, jnp.where(row == 1, c1, pltpu.roll(u, 2, 0)))
            return p1, p2

        def tail_fn(cs, u, g=g):
            st_ref[g, :, cs] = u[tb - SUBLANE:tb, :]

        _conv_columns(h[g], win_ref, cw_ref, y_scr.at[g], prev_fn, tail_fn)
    for g in seqs:
        xo_ref[g] = x[g] + mod[g][2] * _dot(y_scr[g], wout_ref[...])


def _conv_prompt_call(x, mods, norm_g, w_in, cw, w_out, layer, mod_block):
    bsz, seq, _ = x.shape
    tb = CONV_TB
    gb = CONV_GB
    j = layer // 2
    return pl.pallas_call(
        functools.partial(_conv_prompt_kernel, tb=tb, gb=gb),
        grid=(bsz // gb, seq // tb),
        in_specs=[
            pl.BlockSpec((gb, tb, D_MODEL), lambda b, t: (b, t, 0)),
            _resident((None, bsz, 3 * D_MODEL), (layer, mod_block, 0), 2),
            _resident((None, 1, D_MODEL), (layer, 0, 0), 2),
            _resident((None, D_MODEL, 4 * E_CONV), (j, 0, 0), 2),
            _resident((None, CONV_WIDTH, E_CONV), (j, 0, 0), 2),
            _resident((None, E_CONV, D_MODEL), (j, 0, 0), 2),
        ],
        out_specs=[
            pl.BlockSpec((gb, tb, D_MODEL), lambda b, t: (b, t, 0)),
            pl.BlockSpec((gb, SUBLANE, E_CONV), lambda b, t: (b, 0, 0)),
        ],
        out_shape=[
            jax.ShapeDtypeStruct((bsz, seq, D_MODEL), F32),
            jax.ShapeDtypeStruct((bsz, SUBLANE, E_CONV), F32),
        ],
        scratch_shapes=[pltpu.VMEM((gb, tb, E_CONV), BF16)],
        compiler_params=_params(2),
        name="conv_prompt",
    )(x, mods, norm_g, w_in, cw, w_out)


def _tok(s, width):
    return slice(s * width, (s + 1) * width)


def _stack_tokens(ref, n_tok, width, cols=None):
    parts = []
    for s in range(n_tok):
        lo = s * width + (0 if cols is None else cols.start)
        hi = s * width + (width if cols is None else cols.stop)
        parts.append(ref[:, lo:hi])
    return jnp.concatenate(parts, axis=0)


def _sample_hidden(x_ref, mod_ref, g_ref, n_tok):
    shift, scale, _ = _split_mod(mod_ref[...])
    g = g_ref[...]
    return jnp.concatenate(
        [_mod_norm(x_ref[:, _tok(s, D_MODEL)], g, shift, scale) for s in range(n_tok)], axis=0)


def _conv_sample_kernel(x_ref, mod_ref, g_ref, win_ref, cw_ref, wout_ref, st_ref,
                        xo_ref, so_ref, y_scr, *, n_tok):
    ns = x_ref.shape[0]
    h = _sample_hidden(x_ref, mod_ref, g_ref, n_tok)

    def prev_fn(cs, u):
        st0 = st_ref[:, cs]
        st1 = st_ref[:, E_CONV + cs.start:E_CONV + cs.stop]
        p1 = jnp.concatenate([st1, u[:(n_tok - 1) * ns]], axis=0)
        p2 = jnp.concatenate([st0, st1, u[:(n_tok - 2) * ns]], axis=0)
        return p1, p2

    def tail_fn(cs, u):
        so_ref[:, cs] = u[(n_tok - 2) * ns:(n_tok - 1) * ns]
        so_ref[:, E_CONV + cs.start:E_CONV + cs.stop] = u[(n_tok - 1) * ns:]

    _conv_columns(h, win_ref, cw_ref, y_scr, prev_fn, tail_fn)
    out = _dot(y_scr[...], wout_ref[...])
    gate = mod_ref[:, 2 * D_MODEL:]
    for s in range(n_tok):
        xo_ref[:, _tok(s, D_MODEL)] = x_ref[:, _tok(s, D_MODEL)] + gate * out[s * ns:(s + 1) * ns]


def _seq_rows(width, layer=None):
    if layer is None:
        return pl.BlockSpec((SAMPLE_SEQ_BLOCK, width), lambda i: (i, 0))
    return pl.BlockSpec((None, SAMPLE_SEQ_BLOCK, width), lambda i: (layer, i, 0))


def _conv_sample_call(x, mods, norm_g, w_in, cw, w_out, state, layer, n_tok):
    n_seq = x.shape[0]
    j = layer // 2
    return pl.pallas_call(
        functools.partial(_conv_sample_kernel, n_tok=n_tok),
        grid=(n_seq // SAMPLE_SEQ_BLOCK,),
        in_specs=[
            _seq_rows(n_tok * D_MODEL),
            _seq_rows(3 * D_MODEL, layer),
            _resident((None, 1, D_MODEL), (layer, 0, 0), 1),
            _resident((None, D_MODEL, 4 * E_CONV), (j, 0, 0), 1),
            _resident((None, CONV_WIDTH, E_CONV), (j, 0, 0), 1),
            _resident((None, E_CONV, D_MODEL), (j, 0, 0), 1),
            _seq_rows((CONV_WIDTH - 1) * E_CONV, j),
        ],
        out_specs=[_seq_rows(n_tok * D_MODEL), _seq_rows((CONV_WIDTH - 1) * E_CONV)],
        out_shape=[
            jax.ShapeDtypeStruct((n_seq, n_tok * D_MODEL), F32),
            jax.ShapeDtypeStruct((n_seq, (CONV_WIDTH - 1) * E_CONV), F32),
        ],
        scratch_shapes=[pltpu.VMEM((n_tok * SAMPLE_SEQ_BLOCK, E_CONV), BF16)],
        compiler_params=_params(1),
        name="conv_sample",
    )(x, mods, norm_g, w_in, cw, w_out, state)


def _rows_from_blocks(vecs):
    return jnp.concatenate([jnp.broadcast_to(v, (SUB, v.shape[1])) for v in vecs], axis=0)


def _chunk_masks():
    r = lax.broadcasted_iota(jnp.int32, (CHUNK, CHUNK), 0)
    c = lax.broadcasted_iota(jnp.int32, (CHUNK, CHUNK), 1)
    causal = (r // SUB == c // SUB) & (c <= r)
    tri = jnp.where(causal, 1.0, 0.0).astype(BF16)
    first_cols = lax.broadcasted_iota(jnp.int32, (SUB, CHUNK), 1) < SUB
    return causal, first_cols, jnp.concatenate([tri, tri], axis=1)


def _chunk_decay(lf, tri2):
    hi = lf.astype(BF16)
    lo = (lf - hi.astype(F32)).astype(BF16)
    return _dot(tri2, jnp.concatenate([hi, lo], axis=0))


def _chunk_key_operands(kk, gl):
    t0 = gl[SUB - 1:SUB, :]
    t1 = gl[2 * SUB - 1:2 * SUB, :]
    k_mid = kk * jnp.exp2(_rows_from_blocks([0.5 * t0, 0.5 * t1]) - gl)
    k_state = k_mid * _rows_from_blocks([jnp.exp2(0.5 * t0 + t1), jnp.exp2(0.5 * t1)])
    return k_mid.astype(BF16), k_state.astype(BF16), jnp.exp2(t0 + t1), jnp.minimum(t0, t1)


def _chunk_query_operands(qf, gl):
    t0 = gl[SUB - 1:SUB, :]
    t1 = gl[2 * SUB - 1:2 * SUB, :]
    q_mid = qf * jnp.exp2(gl - _rows_from_blocks([0.5 * t0, 0.5 * t1]))
    q_next = q_mid[SUB:] * jnp.exp2(0.5 * (t0 + t1))
    q_state = q_mid * _rows_from_blocks([jnp.exp2(0.5 * t0), jnp.exp2(0.5 * t1 + t0)])
    return jnp.concatenate([q_mid, q_next], axis=0).astype(BF16), q_state.astype(BF16)


def _chunk_finish(ast, qe, kr, vb, ptot, s_old, causal, first_cols):
    same = jnp.where(causal, ast[0:CHUNK], 0.0)
    a = jnp.concatenate([same[:SUB], jnp.where(first_cols, ast[CHUNK:], same[SUB:])], axis=0)
    o = _dot(jnp.concatenate([qe, a.astype(BF16)], axis=1),
             jnp.concatenate([s_old.astype(BF16), vb], axis=0))
    dec = jnp.broadcast_to(ptot, (HG_DK, HG_DK)).T
    return o, s_old * dec + _dot_tn(kr, vb)


def _exact_recurrence(q_ref, k_ref, f_ref, v_ref, o_ref, hs, s0, n_rows):
    def col(tile, r):
        return jnp.broadcast_to(tile[r:r + 1, :], (HG_DK, HG_DK)).T

    def step(i, s):
        rows = pl.ds(pl.multiple_of(i * SUBLANE, SUBLANE), SUBLANE)
        q, k, f, v = q_ref[rows, hs], k_ref[rows, hs], f_ref[rows, hs], v_ref[rows, hs]
        out = []
        for r in range(SUBLANE):
            s = s * col(f, r) + col(k, r) * v[r:r + 1, :]
            out.append(jnp.sum(col(q, r) * s, axis=0, keepdims=True))
        o_ref[rows, hs] = jnp.concatenate(out, axis=0)
        return s
    return lax.fori_loop(0, n_rows // SUBLANE, step, s0)


def _hgrn_prompt_kernel(x_ref, mod_ref, g_ref, win_ref, lbraw_ref, og_ref, wout_ref, fg_ref,
                        xo_ref, s_ref, gl_scr, z_scr, o_scr, lhs_scr, km_scr, qe_scr, kr_scr, vb_scr, pt_scr,
                        s0_scr, *, tb, gb, layer, final):
    b = pl.program_id(0)
    t = pl.program_id(1)
    n_chunks = tb // CHUNK
    lhs_rows = CHUNK + SUB

    @pl.when(t == 0)
    def _():
        s_ref[...] = jnp.zeros_like(s_ref)

    s0_scr[...] = s_ref[...]

    seqs = range(gb)
    chunks = [slice(c * CHUNK, (c + 1) * CHUNK) for c in range(n_chunks)]
    causal, first_cols, tri2 = _chunk_masks()
    lb = _lower_bound(lbraw_ref[...], layer)
    mod = [_split_mod(mod_ref[pl.ds(b * gb + g, 1), :]) for g in seqs]
    h = [_mod_norm_seq(x_ref[g], g_ref[...], mod[g][0], mod[g][1]) for g in seqs]
    gates = [_gate_math(_dot(h[g], win_ref[:, HG_F:2 * HG_F]), lb) for g in seqs]
    qf = [_silu(_dot(h[g], win_ref[:, 0:HG_F])) for g in seqs]
    span = jnp.zeros((1, HG_F), F32)
    for g in seqs:
        logf, kk = gates[g]
        for c, rows in enumerate(chunks):
            gl = _chunk_decay(logf[rows], tri2)
            gl_scr[g, rows, :] = gl
            km_scr[g, rows, :], kr_scr[g, rows, :], ptot, tmin = _chunk_key_operands(kk[rows], gl)
            pt_scr[g, c * SUBLANE:(c + 1) * SUBLANE, :] = jnp.broadcast_to(ptot, (SUBLANE, HG_F))
            span = jnp.minimum(span, tmin)
    for g in seqs:
        for c, rows in enumerate(chunks):
            lhs, qe = _chunk_query_operands(qf[g][rows], gl_scr[g, rows, :])
            lhs_scr[g, c * lhs_rows:(c + 1) * lhs_rows, :] = lhs
            qe_scr[g, rows, :] = qe
    for g in seqs:
        vb_scr[g] = _dot(h[g], win_ref[:, 2 * HG_F:2 * HG_F + HG_I]).astype(BF16)
    for g in seqs:
        z_scr[g] = _dot(h[g], win_ref[:, 2 * HG_F + HG_I:])

    def finish_rows(g, rows):
        y = _head_norm_gate(o_scr[g, rows, :], z_scr[g, rows, :], og_ref[...])
        xn = x_ref[g, rows, :] + mod[g][2] * _dot(y, wout_ref[...])
        if final:
            xn = _rmsnorm(xn, fg_ref[...])
        xo_ref[g, rows, :] = xn

    chains = [(c, hd, g) for c in range(n_chunks) for hd in range(HG_HEADS) for g in seqs]
    scores = {}
    for step in range(len(chains) + HGRN_PIPE_SKEW):
        if step < len(chains):
            c, hd, g = chains[step]
            hs = slice(hd * HG_DK, (hd + 1) * HG_DK)
            scores[step] = _dot_nt(lhs_scr[g, c * lhs_rows:(c + 1) * lhs_rows, hs], km_scr[g, chunks[c], hs])
        n = step - HGRN_PIPE_SKEW
        if n >= 0:
            c, hd, g = chains[n]
            rows = chunks[c]
            hs = slice(hd * HG_DK, (hd + 1) * HG_DK)
            o, s_new = _chunk_finish(scores.pop(n), qe_scr[g, rows, hs], kr_scr[g, rows, hs],
                                     vb_scr[g, rows, hs], pt_scr[g, c * SUBLANE:c * SUBLANE + 1, hs],
                                     s_ref[g, hd], causal, first_cols)
            o_scr[g, rows, hs] = o
            s_ref[g, hd] = s_new
            if n + 1 == len(chains) or chains[n + 1][0] != c:
                for gg in seqs:
                    finish_rows(gg, rows)

    @pl.when(jnp.min(span) < -HGRN_SAFE_LOG2_SPAN)
    def _():
        def exact(q_tmp, k_tmp, f_tmp, v_tmp):
            for g in seqs:
                hg = _mod_norm_seq(x_ref[g], g_ref[...], mod[g][0], mod[g][1])
                q_tmp[...] = _silu(_dot(hg, win_ref[:, 0:HG_F]))
                logf, kk = _gate_math(_dot(hg, win_ref[:, HG_F:2 * HG_F]), lb)
                k_tmp[...] = kk
                f_tmp[...] = jnp.exp2(logf)
                v_tmp[...] = _dot(hg, win_ref[:, 2 * HG_F:2 * HG_F + HG_I])
                for hd in range(HG_HEADS):
                    hs = slice(hd * HG_DK, (hd + 1) * HG_DK)
                    s_ref[g, hd] = _exact_recurrence(q_tmp, k_tmp, f_tmp, v_tmp, o_scr.at[g], hs,
                                                     s0_scr[g, hd], tb)
                for rows in chunks:
                    finish_rows(g, rows)

        pl.run_scoped(exact, *[pltpu.VMEM((tb, HG_F), F32)] * 4)


def _hgrn_prompt_call(x, mods, norm_g, w_in, lb_raw, og, w_out, fg, layer, mod_block):
    bsz, seq, _ = x.shape
    tb = HGRN_TB
    gb = HGRN_GB
    j = layer // 2
    final = layer == DEPTH - 1
    in_specs = [
        pl.BlockSpec((gb, tb, D_MODEL), lambda b, t: (b, t, 0)),
        _resident((None, bsz, 3 * D_MODEL), (layer, mod_block, 0), 2),
        _resident((None, 1, D_MODEL), (layer, 0, 0), 2),
        _resident((None, D_MODEL, 2 * HG_F + 2 * HG_I), (j, 0, 0), 2),
        _resident((DEPTH, HG_F), (0, 0), 2),
        _resident((None, 1, HG_I), (j, 0, 0), 2),
        _resident((None, HG_I, D_MODEL), (j, 0, 0), 2),
        _resident((1, D_MODEL), (0, 0), 2),
    ]
    return pl.pallas_call(
        functools.partial(_hgrn_prompt_kernel, tb=tb, gb=gb, layer=layer, final=final),
        grid=(bsz // gb, seq // tb),
        in_specs=in_specs,
        out_specs=[
            pl.BlockSpec((gb, tb, D_MODEL), lambda b, t: (b, t, 0)),
            pl.BlockSpec((gb, HG_HEADS, HG_DK, HG_DV), lambda b, t: (b, 0, 0, 0)),
        ],
        out_shape=[
            jax.ShapeDtypeStruct((bsz, seq, D_MODEL), F32),
            jax.ShapeDtypeStruct((bsz, HG_HEADS, HG_DK, HG_DV), F32),
        ],
        scratch_shapes=[
            pltpu.VMEM((gb, tb, HG_F), F32),
            pltpu.VMEM((gb, tb, HG_I), F32),
            pltpu.VMEM((gb, tb, HG_I), F32),
            pltpu.VMEM((gb, tb + tb // 2, HG_F), BF16),
            pltpu.VMEM((gb, tb, HG_F), BF16),
            pltpu.VMEM((gb, tb, HG_F), BF16),
            pltpu.VMEM((gb, tb, HG_F), BF16),
            pltpu.VMEM((gb, tb, HG_I), BF16),
            pltpu.VMEM((gb, (tb // CHUNK) * SUBLANE, HG_F), F32),
            pltpu.VMEM((gb, HG_HEADS, HG_DK, HG_DV), F32),
        ],
        compiler_params=_params(2),
        name="hgrn_prompt",
    )(x, mods, norm_g, w_in, lb_raw, og, w_out, fg)


def _hgrn_sample_pre_kernel(x_ref, mod_ref, g_ref, win_ref, lbraw_ref,
                            qe_ref, kr_ref, v_ref, el_ref, oi_ref, z_ref, *, n_tok, layer):
    ns = x_ref.shape[0]
    h = _sample_hidden(x_ref, mod_ref, g_ref, n_tok)
    lb = _lower_bound(lbraw_ref[...], layer)
    qf = _silu(_dot(h, win_ref[:, 0:HG_F]))
    lf, kk = _gate_math(_dot(h, win_ref[:, HG_F:2 * HG_F]), lb)
    vv = _dot(h, win_ref[:, 2 * HG_F:2 * HG_F + HG_I])
    z = _dot(h, win_ref[:, 2 * HG_F + HG_I:])

    def tok(a, s):
        return a[s * ns:(s + 1) * ns]

    g = [tok(lf, 0)]
    for s in range(1, n_tok):
        g.append(g[-1] + tok(lf, s))
    el_ref[...] = jnp.exp2(g[-1])
    for s in range(n_tok):
        qe_ref[:, _tok(s, HG_F)] = tok(qf, s) * jnp.exp2(g[s])
        kr_ref[:, _tok(s, HG_F)] = tok(kk, s) * jnp.exp2(g[-1] - g[s])
        v_ref[:, _tok(s, HG_I)] = tok(vv, s)
        z_ref[:, _tok(s, HG_I)] = tok(z, s)

    for t in range(n_tok):
        acc = jnp.zeros((ns, HG_I), F32)
        for s in range(t + 1):
            e = tok(qf, t) * tok(kk, s)
            if s < t:
                e = e * jnp.exp2(g[t] - g[s])
            vs = tok(vv, s)
            parts = []
            for hd in range(HG_HEADS):
                hs = slice(hd * HG_DK, (hd + 1) * HG_DK)
                parts.append(jnp.sum(e[:, hs], axis=-1, keepdims=True) * vs[:, hs])
            acc = acc + jnp.concatenate(parts, axis=1)
        oi_ref[:, _tok(t, HG_I)] = acc


def _hgrn_sample_pre_call(x, mods, norm_g, w_in, lb_raw, layer, n_tok):
    n_seq = x.shape[0]
    j = layer // 2
    act = jax.ShapeDtypeStruct((n_seq, n_tok * HG_F), F32)
    return pl.pallas_call(
        functools.partial(_hgrn_sample_pre_kernel, n_tok=n_tok, layer=layer),
        grid=(n_seq // SAMPLE_SEQ_BLOCK,),
        in_specs=[
            _seq_rows(n_tok * D_MODEL),
            _seq_rows(3 * D_MODEL, layer),
            _resident((None, 1, D_MODEL), (layer, 0, 0), 1),
            _resident((None, D_MODEL, 2 * HG_F + 2 * HG_I), (j, 0, 0), 1),
            _resident((DEPTH, HG_F), (0, 0), 1),
        ],
        out_specs=[_seq_rows(n_tok * HG_F)] * 3 + [_seq_rows(HG_F)] + [_seq_rows(n_tok * HG_I)] * 2,
        out_shape=[act, act, act, jax.ShapeDtypeStruct((n_seq, HG_F), F32), act, act],
        compiler_params=_params(1),
        name="hgrn_sample_pre",
    )(x, mods, norm_g, w_in, lb_raw)


def _hgrn_sample_state_kernel(*refs, n_tok, has_prev):
    n_in = 5 + int(has_prev)
    qe_ref, kr_ref, v_ref, el_ref, s_ref = refs[:5]
    oo_ref, so_ref = refs[n_in:n_in + 2]
    ns = SAMPLE_STATE_BLOCK
    seq_of_row = lax.broadcasted_iota(jnp.int32, (n_tok * ns, HG_DK), 0) % ns
    for hd in range(HG_HEADS):
        hs = slice(hd * HG_DK, (hd + 1) * HG_DK)
        q = _stack_tokens(qe_ref, n_tok, HG_F, hs).astype(BF16)
        k = _stack_tokens(kr_ref, n_tok, HG_F, hs)
        v = _stack_tokens(v_ref, n_tok, HG_I, hs).astype(BF16)
        el = el_ref[:, hs]
        o = jnp.zeros((n_tok * ns, HG_DV), F32)
        for b in range(ns):
            mine = seq_of_row == b
            s_old = s_ref[b, hd]
            o = jnp.where(mine, _dot(q, s_old.astype(BF16)), o)
            kb = jnp.where(mine, k, 0.0).astype(BF16)
            dec = jnp.broadcast_to(el[b:b + 1, :], (HG_DK, HG_DK)).T
            so_ref[b, hd] = s_old * dec + _dot_tn(kb, v)
        for t in range(n_tok):
            oo_ref[:, t * HG_I + hs.start:t * HG_I + hs.stop] = o[t * ns:(t + 1) * ns]


def _hgrn_sample_state_call(qe, kr, v, el, states, prev_states, layer, n_tok):
    n_seq = qe.shape[0]
    j = layer // 2
    nb = SAMPLE_STATE_BLOCK
    rows = lambda width: pl.BlockSpec((nb, width), lambda i: (i, 0))
    st_spec = pl.BlockSpec((None, nb, HG_HEADS, HG_DK, HG_DV), lambda i: (j, i, 0, 0, 0))
    in_specs = [rows(n_tok * HG_F), rows(n_tok * HG_F), rows(n_tok * HG_I), rows(HG_F), st_spec]
    args = [qe, kr, v, el, states]
    aliases = {}
    if prev_states is not None:
        in_specs.append(pl.BlockSpec(memory_space=pl.ANY))
        args.append(prev_states)
        aliases = {len(args) - 1: 1}
    return pl.pallas_call(
        functools.partial(_hgrn_sample_state_kernel, n_tok=n_tok, has_prev=prev_states is not None),
        grid=(n_seq // nb,),
        in_specs=in_specs,
        out_specs=[rows(n_tok * HG_I), st_spec],
        out_shape=[
            jax.ShapeDtypeStruct((n_seq, n_tok * HG_I), F32),
            jax.ShapeDtypeStruct(states.shape, F32),
        ],
        input_output_aliases=aliases,
        compiler_params=_params(1),
        name="hgrn_sample_state",
    )(*args)


def _hgrn_sample_post_kernel(x_ref, mod_ref, oa_ref, ob_ref, z_ref, og_ref, wout_ref, fg_ref, xo_ref,
                             *, n_tok, final):
    ns = x_ref.shape[0]
    o = _stack_tokens(oa_ref, n_tok, HG_I) + _stack_tokens(ob_ref, n_tok, HG_I)
    y = _head_norm_gate(o, _stack_tokens(z_ref, n_tok, HG_I), og_ref[...])
    out = _dot(y, wout_ref[...])
    gate = mod_ref[:, 2 * D_MODEL:]
    for s in range(n_tok):
        xn = x_ref[:, _tok(s, D_MODEL)] + gate * out[s * ns:(s + 1) * ns]
        if final:
            xn = _rmsnorm(xn, fg_ref[...])
        xo_ref[:, _tok(s, D_MODEL)] = xn


def _hgrn_sample_post_call(x, mods, o_inter, o_intra, z, og, w_out, fg, layer, n_tok):
    n_seq = x.shape[0]
    j = layer // 2
    return pl.pallas_call(
        functools.partial(_hgrn_sample_post_kernel, n_tok=n_tok, final=layer == DEPTH - 1),
        grid=(n_seq // SAMPLE_SEQ_BLOCK,),
        in_specs=[
            _seq_rows(n_tok * D_MODEL),
            _seq_rows(3 * D_MODEL, layer),
            _seq_rows(n_tok * HG_I), _seq_rows(n_tok * HG_I), _seq_rows(n_tok * HG_I),
            _resident((None, 1, HG_I), (j, 0, 0), 1),
            _resident((None, HG_I, D_MODEL), (j, 0, 0), 1),
            _resident((1, D_MODEL), (0, 0), 1),
        ],
        out_specs=_seq_rows(n_tok * D_MODEL),
        out_shape=jax.ShapeDtypeStruct((n_seq, n_tok * D_MODEL), F32),
        compiler_params=_params(1),
        name="hgrn_sample_post",
    )(x, mods, o_inter, o_intra, z, og, w_out, fg)


def kernel(x_prompt, x_sample, state_conv, state_hgrn, c_prompt, c_sample, norm_g, w_ada, b_ada, conv_w_in, conv_w, conv_w_out, hgrn_w_in, hgrn_lower_bounds, hgrn_onorm_g, hgrn_w_out, final_norm_g):
    n_p = x_prompt.shape[0]
    n_s, t_s, _ = x_sample.shape
    assert t_s >= CONV_WIDTH - 1 and n_s % SAMPLE_SEQ_BLOCK == 0 and n_s % n_p == 0

    mods = _ada_call(jnp.concatenate([c_sample, c_prompt], axis=0), w_ada, b_ada)
    mod_block_p = n_s // n_p

    conv_w_in_b = conv_w_in.astype(BF16)
    conv_w_out_b = conv_w_out.astype(BF16)
    hgrn_w_in_b = hgrn_w_in.astype(BF16)
    hgrn_w_out_b = hgrn_w_out.astype(BF16)
    norm_g3 = norm_g.reshape(DEPTH, 1, D_MODEL)
    og3 = hgrn_onorm_g.reshape(DEPTH // 2, 1, HG_I)
    fg = final_norm_g.reshape(1, D_MODEL)
    state_conv2 = state_conv.reshape(state_conv.shape[0], n_s, (CONV_WIDTH - 1) * E_CONV)

    xp = x_prompt
    xs = x_sample.reshape(n_s, t_s * D_MODEL)
    conv_p, conv_s, hgrn_p = [], [], []
    hgrn_s = None
    for layer in range(DEPTH):
        if layer % 2 == 0:
            xp, tail = _conv_prompt_call(xp, mods, norm_g3, conv_w_in_b, conv_w, conv_w_out_b, layer, mod_block_p)
            conv_p.append(tail[:, SUBLANE - (CONV_WIDTH - 1):])
            xs, st = _conv_sample_call(xs, mods, norm_g3, conv_w_in_b, conv_w, conv_w_out_b, state_conv2,
                                       layer, t_s)
            conv_s.append(st.reshape(n_s, CONV_WIDTH - 1, E_CONV))
        else:
            xp, s_new = _hgrn_prompt_call(xp, mods, norm_g3, hgrn_w_in_b, hgrn_lower_bounds, og3,
                                          hgrn_w_out_b, fg, layer, mod_block_p)
            hgrn_p.append(s_new)
            qe, kr, v, el, o_intra, z = _hgrn_sample_pre_call(xs, mods, norm_g3, hgrn_w_in_b,
                                                              hgrn_lower_bounds, layer, t_s)
            o_inter, hgrn_s = _hgrn_sample_state_call(qe, kr, v, el, state_hgrn, hgrn_s, layer, t_s)
            xs = _hgrn_sample_post_call(xs, mods, o_inter, o_intra, z, og3, hgrn_w_out_b, fg, layer, t_s)
    return (xp, xs.reshape(n_s, t_s, D_MODEL), jnp.stack(conv_p), jnp.stack(hgrn_p), jnp.stack(conv_s), hgrn_s)
```

```python
import functools

import jax
import jax.numpy as jnp
from jax import lax
from jax.experimental import pallas as pl
from jax.experimental.pallas import tpu as pltpu

F32 = jnp.float32
BF16 = jnp.bfloat16

D_MODEL = 1024
DEPTH = 4
CONV_WIDTH = 3
E_CONV = D_MODEL
HG_HEADS = 8
HG_DK = 128
HG_DV = 128
HG_F = HG_HEADS * HG_DK
HG_I = HG_HEADS * HG_DV
EPS = 1e-6

SUBLANE = 8
CHUNK = 128
SUB = CHUNK // 2
CONV_COL_TILE = 256
CONV_TB = 512
CONV_GB = 2
HGRN_TB = 256
HGRN_GB = 2
HGRN_PIPE_SKEW = 3
HGRN_SAFE_LOG2_SPAN = 200.0
SAMPLE_SEQ_BLOCK = 32
SAMPLE_STATE_BLOCK = 8
VMEM_LIMIT = 56 * 1024 * 1024


def _dot(a, b):
    return jnp.dot(a, b, preferred_element_type=F32)


def _dot_nt(a, b):
    return lax.dot_general(a, b, (((1,), (1,)), ((), ())), preferred_element_type=F32)


def _dot_tn(a, b):
    return lax.dot_general(a, b, (((0,), (0,)), ((), ())), preferred_element_type=F32)


def _silu(x):
    return x * jax.nn.sigmoid(x)


def _rmsnorm(x, g):
    ms = jnp.mean(x * x, axis=-1, keepdims=True)
    return x * lax.rsqrt(ms + EPS) * g


def _split_mod(mod):
    return mod[:, :D_MODEL], mod[:, D_MODEL:2 * D_MODEL], mod[:, 2 * D_MODEL:]


def _mod_norm(x, g, shift, scale):
    return (_rmsnorm(x, g) * (1.0 + scale) + shift).astype(BF16)


def _mod_norm_seq(x, g, shift, scale):
    ms = jnp.mean(x * x, axis=-1, keepdims=True)
    return (x * lax.rsqrt(ms + EPS) * (g * (1.0 + scale)) + shift).astype(BF16)


def _lower_bound(raw, layer):
    rows = [raw[i:i + 1, :] for i in range(DEPTH)]
    m = functools.reduce(jnp.maximum, rows)
    es = [jnp.exp(r - m) for r in rows]
    tot = functools.reduce(lambda a, b: a + b, es)
    acc = es[1]
    for i in range(2, layer + 1):
        acc = acc + es[i]
    return acc / tot


def _gate_math(fpre, lb):
    t = jnp.exp(-jnp.abs(fpre))
    r = 1.0 / (1.0 + t)
    pos = fpre >= 0.0
    sig = jnp.where(pos, 1.0, t) * r
    sig_neg = jnp.where(pos, t, 1.0) * r
    return jnp.log2(lb + (1.0 - lb) * sig), (1.0 - lb) * sig_neg


def _head_norm_gate(o, z, og):
    parts = []
    for hd in range(HG_HEADS):
        hs = slice(hd * HG_DV, (hd + 1) * HG_DV)
        oh = o[:, hs]
        ms = jnp.mean(oh * oh, axis=-1, keepdims=True)
        parts.append(oh * lax.rsqrt(ms + EPS))
    on = jnp.concatenate(parts, axis=1) * og
    return (on * _silu(z)).astype(BF16)


def _params(n_grid):
    return pltpu.CompilerParams(dimension_semantics=("arbitrary",) * n_grid, vmem_limit_bytes=VMEM_LIMIT)


def _resident(shape, index, n_grid):
    if n_grid == 1:
        return pl.BlockSpec(shape, lambda i: index, pipeline_mode=pl.Buffered(1))
    return pl.BlockSpec(shape, lambda b, t: index, pipeline_mode=pl.Buffered(1))


def _ada_kernel(c_ref, w_ref, b_ref, o_ref):
    s = _silu(c_ref[...]).astype(BF16)
    o_ref[...] = _dot(s, w_ref[...].astype(BF16)) + b_ref[...]


def _ada_call(c_all, w_ada, b_ada):
    rows = c_all.shape[0]
    n_col = 3
    col = 3 * D_MODEL // n_col
    return pl.pallas_call(
        _ada_kernel,
        grid=(DEPTH, n_col),
        in_specs=[
            pl.BlockSpec((rows, D_MODEL), lambda l, n: (0, 0)),
            pl.BlockSpec((None, D_MODEL, col), lambda l, n: (l, 0, n)),
            pl.BlockSpec((None, 1, col), lambda l, n: (l, 0, n)),
        ],
        out_specs=pl.BlockSpec((None, rows, col), lambda l, n: (l, 0, n)),
        out_shape=jax.ShapeDtypeStruct((DEPTH, rows, 3 * D_MODEL), F32),
        compiler_params=_params(2),
        name="ada_mod",
    )(c_all, w_ada, b_ada.reshape(DEPTH, 1, 3 * D_MODEL))


def _conv_columns(h, win_ref, cw_ref, y_scr, prev_fn, tail_fn):
    ct = CONV_COL_TILE
    for n in range(E_CONV // ct):
        def col(j):
            return slice(j * E_CONV + n * ct, j * E_CONV + (n + 1) * ct)
        cs = slice(n * ct, (n + 1) * ct)
        b_gate = _dot(h, win_ref[:, col(0)])
        c_gate = _dot(h, win_ref[:, col(1)])
        v = _dot(h, win_ref[:, col(2)])
        z = _dot(h, win_ref[:, col(3)])
        u = c_gate * v
        p1, p2 = prev_fn(cs, u)
        conv = cw_ref[0:1, cs] * p2 + cw_ref[1:2, cs] * p1 + cw_ref[2:3, cs] * u
        y_scr[:, cs] = (b_gate * conv * _silu(z)).astype(BF16)
        tail_fn(cs, u)


def _conv_prompt_kernel(x_ref, mod_ref, g_ref, win_ref, cw_ref, wout_ref, xo_ref, st_ref, y_scr, *, tb, gb):
    b = pl.program_id(0)
    t = pl.program_id(1)
    ct = CONV_COL_TILE

    @pl.when(t == 0)
    def _():
        st_ref[...] = jnp.zeros_like(st_ref)

    seqs = range(gb)
    x = [x_ref[g] for g in seqs]
    mod = [_split_mod(mod_ref[pl.ds(b * gb + g, 1), :]) for g in seqs]
    h = [_mod_norm_seq(x[g], g_ref[...], mod[g][0], mod[g][1]) for g in seqs]
    row = lax.broadcasted_iota(jnp.int32, (tb, ct), 0)

    for g in seqs:
        def prev_fn(cs, u, g=g):
            c0 = st_ref[g, SUBLANE - 2:SUBLANE - 1, cs]
            c1 = st_ref[g, SUBLANE - 1:SUBLANE, cs]
            p1 = jnp.where(row == 0, c1, pltpu.roll(u, 1, 0))
            p2 = jnp.where(row == 0, c0, jnp.where(row == 1, c1, pltpu.roll(u, 2, 0)))
            return p1, p2

        def tail_fn(cs, u, g=g):
            st_ref[g, :, cs] = u[tb - SUBLANE:tb, :]

        _conv_columns(h[g], win_ref, cw_ref, y_scr.at[g], prev_fn, tail_fn)
    for g in seqs:
        xo_ref[g] = x[g] + mod[g][2] * _dot(y_scr[g], wout_ref[...])


def _conv_prompt_call(x, mods, norm_g, w_in, cw, w_out, layer, mod_block):
    bsz, seq, _ = x.shape
    tb = CONV_TB
    gb = CONV_GB
    j = layer // 2
    return pl.pallas_call(
        functools.partial(_conv_prompt_kernel, tb=tb, gb=gb),
        grid=(bsz // gb, seq // tb),
        in_specs=[
            pl.BlockSpec((gb, tb, D_MODEL), lambda b, t: (b, t, 0)),
            _resident((None, bsz, 3 * D_MODEL), (layer, mod_block, 0), 2),
            _resident((None, 1, D_MODEL), (layer, 0, 0), 2),
            _resident((None, D_MODEL, 4 * E_CONV), (j, 0, 0), 2),
            _resident((None, CONV_WIDTH, E_CONV), (j, 0, 0), 2),
            _resident((None, E_CONV, D_MODEL), (j, 0, 0), 2),
        ],
        out_specs=[
            pl.BlockSpec((gb, tb, D_MODEL), lambda b, t: (b, t, 0)),
            pl.BlockSpec((gb, SUBLANE, E_CONV), lambda b, t: (b, 0, 0)),
        ],
        out_shape=[
            jax.ShapeDtypeStruct((bsz, seq, D_MODEL), F32),
            jax.ShapeDtypeStruct((bsz, SUBLANE, E_CONV), F32),
        ],
        scratch_shapes=[pltpu.VMEM((gb, tb, E_CONV), BF16)],
        compiler_params=_params(2),
        name="conv_prompt",
    )(x, mods, norm_g, w_in, cw, w_out)


def _tok(s, width):
    return slice(s * width, (s + 1) * width)


def _stack_tokens(ref, n_tok, width, cols=None):
    parts = []
    for s in range(n_tok):
        lo = s * width + (0 if cols is None else cols.start)
        hi = s * width + (width if cols is None else cols.stop)
        parts.append(ref[:, lo:hi])
    return jnp.concatenate(parts, axis=0)


def _sample_hidden(x_ref, mod_ref, g_ref, n_tok):
    shift, scale, _ = _split_mod(mod_ref[...])
    g = g_ref[...]
    return jnp.concatenate(
        [_mod_norm(x_ref[:, _tok(s, D_MODEL)], g, shift, scale) for s in range(n_tok)], axis=0)


def _conv_sample_kernel(x_ref, mod_ref, g_ref, win_ref, cw_ref, wout_ref, st_ref,
                        xo_ref, so_ref, y_scr, *, n_tok):
    ns = x_ref.shape[0]
    h = _sample_hidden(x_ref, mod_ref, g_ref, n_tok)

    def prev_fn(cs, u):
        st0 = st_ref[:, cs]
        st1 = st_ref[:, E_CONV + cs.start:E_CONV + cs.stop]
        p1 = jnp.concatenate([st1, u[:(n_tok - 1) * ns]], axis=0)
        p2 = jnp.concatenate([st0, st1, u[:(n_tok - 2) * ns]], axis=0)
        return p1, p2

    def tail_fn(cs, u):
        so_ref[:, cs] = u[(n_tok - 2) * ns:(n_tok - 1) * ns]
        so_ref[:, E_CONV + cs.start:E_CONV + cs.stop] = u[(n_tok - 1) * ns:]

    _conv_columns(h, win_ref, cw_ref, y_scr, prev_fn, tail_fn)
    out = _dot(y_scr[...], wout_ref[...])
    gate = mod_ref[:, 2 * D_MODEL:]
    for s in range(n_tok):
        xo_ref[:, _tok(s, D_MODEL)] = x_ref[:, _tok(s, D_MODEL)] + gate * out[s * ns:(s + 1) * ns]


def _seq_rows(width, layer=None):
    if layer is None:
        return pl.BlockSpec((SAMPLE_SEQ_BLOCK, width), lambda i: (i, 0))
    return pl.BlockSpec((None, SAMPLE_SEQ_BLOCK, width), lambda i: (layer, i, 0))


def _conv_sample_call(x, mods, norm_g, w_in, cw, w_out, state, layer, n_tok):
    n_seq = x.shape[0]
    j = layer // 2
    return pl.pallas_call(
        functools.partial(_conv_sample_kernel, n_tok=n_tok),
        grid=(n_seq // SAMPLE_SEQ_BLOCK,),
        in_specs=[
            _seq_rows(n_tok * D_MODEL),
            _seq_rows(3 * D_MODEL, layer),
            _resident((None, 1, D_MODEL), (layer, 0, 0), 1),
            _resident((None, D_MODEL, 4 * E_CONV), (j, 0, 0), 1),
            _resident((None, CONV_WIDTH, E_CONV), (j, 0, 0), 1),
            _resident((None, E_CONV, D_MODEL), (j, 0, 0), 1),
            _seq_rows((CONV_WIDTH - 1) * E_CONV, j),
        ],
        out_specs=[_seq_rows(n_tok * D_MODEL), _seq_rows((CONV_WIDTH - 1) * E_CONV)],
        out_shape=[
            jax.ShapeDtypeStruct((n_seq, n_tok * D_MODEL), F32),
            jax.ShapeDtypeStruct((n_seq, (CONV_WIDTH - 1) * E_CONV), F32),
        ],
        scratch_shapes=[pltpu.VMEM((n_tok * SAMPLE_SEQ_BLOCK, E_CONV), BF16)],
        compiler_params=_params(1),
        name="conv_sample",
    )(x, mods, norm_g, w_in, cw, w_out, state)


def _rows_from_blocks(vecs):
    return jnp.concatenate([jnp.broadcast_to(v, (SUB, v.shape[1])) for v in vecs], axis=0)


def _chunk_masks():
    r = lax.broadcasted_iota(jnp.int32, (CHUNK, CHUNK), 0)
    c = lax.broadcasted_iota(jnp.int32, (CHUNK, CHUNK), 1)
    causal = (r // SUB == c // SUB) & (c <= r)
    tri = jnp.where(causal, 1.0, 0.0).astype(BF16)
    first_cols = lax.broadcasted_iota(jnp.int32, (SUB, CHUNK), 1) < SUB
    return causal, first_cols, jnp.concatenate([tri, tri], axis=1)


def _chunk_decay(lf, tri2):
    hi = lf.astype(BF16)
    lo = (lf - hi.astype(F32)).astype(BF16)
    return _dot(tri2, jnp.concatenate([hi, lo], axis=0))


def _chunk_key_operands(kk, gl):
    t0 = gl[SUB - 1:SUB, :]
    t1 = gl[2 * SUB - 1:2 * SUB, :]
    k_mid = kk * jnp.exp2(_rows_from_blocks([0.5 * t0, 0.5 * t1]) - gl)
    k_state = k_mid * _rows_from_blocks([jnp.exp2(0.5 * t0 + t1), jnp.exp2(0.5 * t1)])
    return k_mid.astype(BF16), k_state.astype(BF16), jnp.exp2(t0 + t1), jnp.minimum(t0, t1)


def _chunk_query_operands(qf, gl):
    t0 = gl[SUB - 1:SUB, :]
    t1 = gl[2 * SUB - 1:2 * SUB, :]
    q_mid = qf * jnp.exp2(gl - _rows_from_blocks([0.5 * t0, 0.5 * t1]))
    q_next = q_mid[SUB:] * jnp.exp2(0.5 * (t0 + t1))
    q_state = q_mid * _rows_from_blocks([jnp.exp2(0.5 * t0), jnp.exp2(0.5 * t1 + t0)])
    return jnp.concatenate([q_mid, q_next], axis=0).astype(BF16), q_state.astype(BF16)


def _chunk_finish(ast, qe, kr, vb, ptot, s_old, causal, first_cols):
    same = jnp.where(causal, ast[0:CHUNK], 0.0)
    a = jnp.concatenate([same[:SUB], jnp.where(first_cols, ast[CHUNK:], same[SUB:])], axis=0)
    o = _dot(jnp.concatenate([qe, a.astype(BF16)], axis=1),
             jnp.concatenate([s_old.astype(BF16), vb], axis=0))
    dec = jnp.broadcast_to(ptot, (HG_DK, HG_DK)).T
    return o, s_old * dec + _dot_tn(kr, vb)


def _exact_recurrence(q_ref, k_ref, f_ref, v_ref, o_ref, s0, n_rows):
    def col(tile, r):
        return jnp.broadcast_to(tile[r:r + 1, :], (HG_DK, HG_DK)).T

    def step(i, s):
        rows = pl.ds(pl.multiple_of(i * SUBLANE, SUBLANE), SUBLANE)
        q, k, f, v = q_ref[rows, :], k_ref[rows, :], f_ref[rows, :], v_ref[rows, :]
        out = []
        for r in range(SUBLANE):
            s = s * col(f, r) + col(k, r) * v[r:r + 1, :]
            out.append(jnp.sum(col(q, r) * s, axis=0, keepdims=True))
        o_ref[rows, :] = jnp.concatenate(out, axis=0)
        return s
    return lax.fori_loop(0, n_rows // SUBLANE, step, s0)


def _hgrn_prompt_kernel(x_ref, mod_ref, g_ref, win_ref, lbraw_ref, og_ref, wout_ref, fg_ref,
                        xo_ref, s_ref, gl_scr, z_scr, o_scr, lhs_scr, km_scr, qe_scr, kr_scr, vb_scr, pt_scr,
                        s0_scr, *, tb, gb, layer, final):
    b = pl.program_id(0)
    t = pl.program_id(1)
    n_chunks = tb // CHUNK
    lhs_rows = CHUNK + SUB

    @pl.when(t == 0)
    def _():
        s_ref[...] = jnp.zeros_like(s_ref)

    s0_scr[...] = s_ref[...]

    seqs = range(gb)
    chunks = [slice(c * CHUNK, (c + 1) * CHUNK) for c in range(n_chunks)]
    causal, first_cols, tri2 = _chunk_masks()
    lb = _lower_bound(lbraw_ref[...], layer)
    mod = [_split_mod(mod_ref[pl.ds(b * gb + g, 1), :]) for g in seqs]
    h = [_mod_norm_seq(x_ref[g], g_ref[...], mod[g][0], mod[g][1]) for g in seqs]
    gates = [_gate_math(_dot(h[g], win_ref[:, HG_F:2 * HG_F]), lb) for g in seqs]
    qf = [_silu(_dot(h[g], win_ref[:, 0:HG_F])) for g in seqs]
    span = jnp.zeros((1, HG_F), F32)
    for g in seqs:
        logf, kk = gates[g]
        for c, rows in enumerate(chunks):
            gl = _chunk_decay(logf[rows], tri2)
            gl_scr[g, rows, :] = gl
            km_scr[g, rows, :], kr_scr[g, rows, :], ptot, tmin = _chunk_key_operands(kk[rows], gl)
            pt_scr[g, c * SUBLANE:(c + 1) * SUBLANE, :] = jnp.broadcast_to(ptot, (SUBLANE, HG_F))
            span = jnp.minimum(span, tmin)
    for g in seqs:
        for c, rows in enumerate(chunks):
            lhs, qe = _chunk_query_operands(qf[g][rows], gl_scr[g, rows, :])
            lhs_scr[g, c * lhs_rows:(c + 1) * lhs_rows, :] = lhs
            qe_scr[g, rows, :] = qe
    for g in seqs:
        vb_scr[g] = _dot(h[g], win_ref[:, 2 * HG_F:2 * HG_F + HG_I]).astype(BF16)
    for g in seqs:
        z_scr[g] = _dot(h[g], win_ref[:, 2 * HG_F + HG_I:])

    def finish_rows(g, rows):
        y = _head_norm_gate(o_scr[g, rows, :], z_scr[g, rows, :], og_ref[...])
        xn = x_ref[g, rows, :] + mod[g][2] * _dot(y, wout_ref[...])
        if final:
            xn = _rmsnorm(xn, fg_ref[...])
        xo_ref[g, rows, :] = xn

    chains = [(c, hd, g) for c in range(n_chunks) for hd in range(HG_HEADS) for g in seqs]
    scores = {}
    for step in range(len(chains) + HGRN_PIPE_SKEW):
        if step < len(chains):
            c, hd, g = chains[step]
            hs = slice(hd * HG_DK, (hd + 1) * HG_DK)
            scores[step] = _dot_nt(lhs_scr[g, c * lhs_rows:(c + 1) * lhs_rows, hs], km_scr[g, chunks[c], hs])
        n = step - HGRN_PIPE_SKEW
        if n >= 0:
            c, hd, g = chains[n]
            rows = chunks[c]
            hs = slice(hd * HG_DK, (hd + 1) * HG_DK)
            o, s_new = _chunk_finish(scores.pop(n), qe_scr[g, rows, hs], kr_scr[g, rows, hs],
                                     vb_scr[g, rows, hs], pt_scr[g, c * SUBLANE:c * SUBLANE + 1, hs],
                                     s_ref[g, hd], causal, first_cols)
            o_scr[g, rows, hs] = o
            s_ref[g, hd] = s_new
            if n + 1 == len(chains) or chains[n + 1][0] != c:
                for gg in seqs:
                    finish_rows(gg, rows)

    @pl.when(jnp.min(span) < -HGRN_SAFE_LOG2_SPAN)
    def _():
        def exact(q_tmp, k_tmp, f_tmp, v_tmp, o_tmp):
            def one_seq(g, carry):
                shift, scale, gate = _split_mod(mod_ref[pl.ds(b * gb + g, 1), :])
                hg = _mod_norm_seq(x_ref[g], g_ref[...], shift, scale)
                qf = _silu(_dot(hg, win_ref[:, 0:HG_F]))
                logf, kk = _gate_math(_dot(hg, win_ref[:, HG_F:2 * HG_F]), lb)
                ff = jnp.exp2(logf)
                vv = _dot(hg, win_ref[:, 2 * HG_F:2 * HG_F + HG_I])
                for hd in range(HG_HEADS):
                    hs = slice(hd * HG_DK, (hd + 1) * HG_DK)
                    q_tmp[hd], k_tmp[hd], f_tmp[hd], v_tmp[hd] = qf[:, hs], kk[:, hs], ff[:, hs], vv[:, hs]

                def one_head(hd, c2):
                    s_ref[g, hd] = _exact_recurrence(q_tmp.at[hd], k_tmp.at[hd], f_tmp.at[hd], v_tmp.at[hd],
                                                     o_tmp.at[hd], s0_scr[g, hd], tb)
                    return c2

                lax.fori_loop(0, HG_HEADS, one_head, 0)
                o = jnp.concatenate([o_tmp[hd] for hd in range(HG_HEADS)], axis=1)
                y = _head_norm_gate(o, z_scr[g], og_ref[...])
                xn = x_ref[g] + gate * _dot(y, wout_ref[...])
                if final:
                    xn = _rmsnorm(xn, fg_ref[...])
                xo_ref[g] = xn
                return carry

            lax.fori_loop(0, gb, one_seq, 0)

        pl.run_scoped(exact, *[pltpu.VMEM((HG_HEADS, tb, HG_DK), F32)] * 5)


def _hgrn_prompt_call(x, mods, norm_g, w_in, lb_raw, og, w_out, fg, layer, mod_block):
    bsz, seq, _ = x.shape
    tb = HGRN_TB
    gb = HGRN_GB
    j = layer // 2
    final = layer == DEPTH - 1
    in_specs = [
        pl.BlockSpec((gb, tb, D_MODEL), lambda b, t: (b, t, 0)),
        _resident((None, bsz, 3 * D_MODEL), (layer, mod_block, 0), 2),
        _resident((None, 1, D_MODEL), (layer, 0, 0), 2),
        _resident((None, D_MODEL, 2 * HG_F + 2 * HG_I), (j, 0, 0), 2),
        _resident((DEPTH, HG_F), (0, 0), 2),
        _resident((None, 1, HG_I), (j, 0, 0), 2),
        _resident((None, HG_I, D_MODEL), (j, 0, 0), 2),
        _resident((1, D_MODEL), (0, 0), 2),
    ]
    return pl.pallas_call(
        functools.partial(_hgrn_prompt_kernel, tb=tb, gb=gb, layer=layer, final=final),
        grid=(bsz // gb, seq // tb),
        in_specs=in_specs,
        out_specs=[
            pl.BlockSpec((gb, tb, D_MODEL), lambda b, t: (b, t, 0)),
            pl.BlockSpec((gb, HG_HEADS, HG_DK, HG_DV), lambda b, t: (b, 0, 0, 0)),
        ],
        out_shape=[
            jax.ShapeDtypeStruct((bsz, seq, D_MODEL), F32),
            jax.ShapeDtypeStruct((bsz, HG_HEADS, HG_DK, HG_DV), F32),
        ],
        scratch_shapes=[
            pltpu.VMEM((gb, tb, HG_F), F32),
            pltpu.VMEM((gb, tb, HG_I), F32),
            pltpu.VMEM((gb, tb, HG_I), F32),
            pltpu.VMEM((gb, tb + tb // 2, HG_F), BF16),
            pltpu.VMEM((gb, tb, HG_F), BF16),
            pltpu.VMEM((gb, tb, HG_F), BF16),
            pltpu.VMEM((gb, tb, HG_F), BF16),
            pltpu.VMEM((gb, tb, HG_I), BF16),
            pltpu.VMEM((gb, (tb // CHUNK) * SUBLANE, HG_F), F32),
            pltpu.VMEM((gb, HG_HEADS, HG_DK, HG_DV), F32),
        ],
        compiler_params=_params(2),
        name="hgrn_prompt",
    )(x, mods, norm_g, w_in, lb_raw, og, w_out, fg)


def _hgrn_sample_pre_kernel(x_ref, mod_ref, g_ref, win_ref, lbraw_ref,
                            qe_ref, kr_ref, v_ref, el_ref, oi_ref, z_ref, *, n_tok, layer):
    ns = x_ref.shape[0]
    h = _sample_hidden(x_ref, mod_ref, g_ref, n_tok)
    lb = _lower_bound(lbraw_ref[...], layer)
    qf = _silu(_dot(h, win_ref[:, 0:HG_F]))
    lf, kk = _gate_math(_dot(h, win_ref[:, HG_F:2 * HG_F]), lb)
    vv = _dot(h, win_ref[:, 2 * HG_F:2 * HG_F + HG_I])
    z = _dot(h, win_ref[:, 2 * HG_F + HG_I:])

    def tok(a, s):
        return a[s * ns:(s + 1) * ns]

    g = [tok(lf, 0)]
    for s in range(1, n_tok):
        g.append(g[-1] + tok(lf, s))
    el_ref[...] = jnp.exp2(g[-1])
    for s in range(n_tok):
        qe_ref[:, _tok(s, HG_F)] = tok(qf, s) * jnp.exp2(g[s])
        kr_ref[:, _tok(s, HG_F)] = tok(kk, s) * jnp.exp2(g[-1] - g[s])
        v_ref[:, _tok(s, HG_I)] = tok(vv, s)
        z_ref[:, _tok(s, HG_I)] = tok(z, s)

    for t in range(n_tok):
        acc = jnp.zeros((ns, HG_I), F32)
        for s in range(t + 1):
            e = tok(qf, t) * tok(kk, s)
            if s < t:
                e = e * jnp.exp2(g[t] - g[s])
            vs = tok(vv, s)
            parts = []
            for hd in range(HG_HEADS):
                hs = slice(hd * HG_DK, (hd + 1) * HG_DK)
                parts.append(jnp.sum(e[:, hs], axis=-1, keepdims=True) * vs[:, hs])
            acc = acc + jnp.concatenate(parts, axis=1)
        oi_ref[:, _tok(t, HG_I)] = acc


def _hgrn_sample_pre_call(x, mods, norm_g, w_in, lb_raw, layer, n_tok):
    n_seq = x.shape[0]
    j = layer // 2
    act = jax.ShapeDtypeStruct((n_seq, n_tok * HG_F), F32)
    return pl.pallas_call(
        functools.partial(_hgrn_sample_pre_kernel, n_tok=n_tok, layer=layer),
        grid=(n_seq // SAMPLE_SEQ_BLOCK,),
        in_specs=[
            _seq_rows(n_tok * D_MODEL),
            _seq_rows(3 * D_MODEL, layer),
            _resident((None, 1, D_MODEL), (layer, 0, 0), 1),
            _resident((None, D_MODEL, 2 * HG_F + 2 * HG_I), (j, 0, 0), 1),
            _resident((DEPTH, HG_F), (0, 0), 1),
        ],
        out_specs=[_seq_rows(n_tok * HG_F)] * 3 + [_seq_rows(HG_F)] + [_seq_rows(n_tok * HG_I)] * 2,
        out_shape=[act, act, act, jax.ShapeDtypeStruct((n_seq, HG_F), F32), act, act],
        compiler_params=_params(1),
        name="hgrn_sample_pre",
    )(x, mods, norm_g, w_in, lb_raw)


def _hgrn_sample_state_kernel(*refs, n_tok, has_prev):
    n_in = 5 + int(has_prev)
    qe_ref, kr_ref, v_ref, el_ref, s_ref = refs[:5]
    oo_ref, so_ref = refs[n_in:n_in + 2]
    ns = SAMPLE_STATE_BLOCK
    seq_of_row = lax.broadcasted_iota(jnp.int32, (n_tok * ns, HG_DK), 0) % ns
    for hd in range(HG_HEADS):
        hs = slice(hd * HG_DK, (hd + 1) * HG_DK)
        q = _stack_tokens(qe_ref, n_tok, HG_F, hs).astype(BF16)
        k = _stack_tokens(kr_ref, n_tok, HG_F, hs)
        v = _stack_tokens(v_ref, n_tok, HG_I, hs).astype(BF16)
        el = el_ref[:, hs]
        o = jnp.zeros((n_tok * ns, HG_DV), F32)
        for b in range(ns):
            mine = seq_of_row == b
            s_old = s_ref[b, hd]
            o = jnp.where(mine, _dot(q, s_old.astype(BF16)), o)
            kb = jnp.where(mine, k, 0.0).astype(BF16)
            dec = jnp.broadcast_to(el[b:b + 1, :], (HG_DK, HG_DK)).T
            so_ref[b, hd] = s_old * dec + _dot_tn(kb, v)
        for t in range(n_tok):
            oo_ref[:, t * HG_I + hs.start:t * HG_I + hs.stop] = o[t * ns:(t + 1) * ns]


def _hgrn_sample_state_call(qe, kr, v, el, states, prev_states, layer, n_tok):
    n_seq = qe.shape[0]
    j = layer // 2
    nb = SAMPLE_STATE_BLOCK
    rows = lambda width: pl.BlockSpec((nb, width), lambda i: (i, 0))
    st_spec = pl.BlockSpec((None, nb, HG_HEADS, HG_DK, HG_DV), lambda i: (j, i, 0, 0, 0))
    in_specs = [rows(n_tok * HG_F), rows(n_tok * HG_F), rows(n_tok * HG_I), rows(HG_F), st_spec]
    args = [qe, kr, v, el, states]
    aliases = {}
    if prev_states is not None:
        in_specs.append(pl.BlockSpec(memory_space=pl.ANY))
        args.append(prev_states)
        aliases = {len(args) - 1: 1}
    return pl.pallas_call(
        functools.partial(_hgrn_sample_state_kernel, n_tok=n_tok, has_prev=prev_states is not None),
        grid=(n_seq // nb,),
        in_specs=in_specs,
        out_specs=[rows(n_tok * HG_I), st_spec],
        out_shape=[
            jax.ShapeDtypeStruct((n_seq, n_tok * HG_I), F32),
            jax.ShapeDtypeStruct(states.shape, F32),
        ],
        input_output_aliases=aliases,
        compiler_params=_params(1),
        name="hgrn_sample_state",
    )(*args)


def _hgrn_sample_post_kernel(x_ref, mod_ref, oa_ref, ob_ref, z_ref, og_ref, wout_ref, fg_ref, xo_ref,
                             *, n_tok, final):
    ns = x_ref.shape[0]
    o = _stack_tokens(oa_ref, n_tok, HG_I) + _stack_tokens(ob_ref, n_tok, HG_I)
    y = _head_norm_gate(o, _stack_tokens(z_ref, n_tok, HG_I), og_ref[...])
    out = _dot(y, wout_ref[...])
    gate = mod_ref[:, 2 * D_MODEL:]
    for s in range(n_tok):
        xn = x_ref[:, _tok(s, D_MODEL)] + gate * out[s * ns:(s + 1) * ns]
        if final:
            xn = _rmsnorm(xn, fg_ref[...])
        xo_ref[:, _tok(s, D_MODEL)] = xn


def _hgrn_sample_post_call(x, mods, o_inter, o_intra, z, og, w_out, fg, layer, n_tok):
    n_seq = x.shape[0]
    j = layer // 2
    return pl.pallas_call(
        functools.partial(_hgrn_sample_post_kernel, n_tok=n_tok, final=layer == DEPTH - 1),
        grid=(n_seq // SAMPLE_SEQ_BLOCK,),
        in_specs=[
            _seq_rows(n_tok * D_MODEL),
            _seq_rows(3 * D_MODEL, layer),
            _seq_rows(n_tok * HG_I), _seq_rows(n_tok * HG_I), _seq_rows(n_tok * HG_I),
            _resident((None, 1, HG_I), (j, 0, 0), 1),
            _resident((None, HG_I, D_MODEL), (j, 0, 0), 1),
            _resident((1, D_MODEL), (0, 0), 1),
        ],
        out_specs=_seq_rows(n_tok * D_MODEL),
        out_shape=jax.ShapeDtypeStruct((n_seq, n_tok * D_MODEL), F32),
        compiler_params=_params(1),
        name="hgrn_sample_post",
    )(x, mods, o_inter, o_intra, z, og, w_out, fg)


def kernel(x_prompt, x_sample, state_conv, state_hgrn, c_prompt, c_sample, norm_g, w_ada, b_ada, conv_w_in, conv_w, conv_w_out, hgrn_w_in, hgrn_lower_bounds, hgrn_onorm_g, hgrn_w_out, final_norm_g):
    n_p = x_prompt.shape[0]
    n_s, t_s, _ = x_sample.shape
    assert t_s >= CONV_WIDTH - 1 and n_s % SAMPLE_SEQ_BLOCK == 0 and n_s % n_p == 0

    mods = _ada_call(jnp.concatenate([c_sample, c_prompt], axis=0), w_ada, b_ada)
    mod_block_p = n_s // n_p

    conv_w_in_b = conv_w_in.astype(BF16)
    conv_w_out_b = conv_w_out.astype(BF16)
    hgrn_w_in_b = hgrn_w_in.astype(BF16)
    hgrn_w_out_b = hgrn_w_out.astype(BF16)
    norm_g3 = norm_g.reshape(DEPTH, 1, D_MODEL)
    og3 = hgrn_onorm_g.reshape(DEPTH // 2, 1, HG_I)
    fg = final_norm_g.reshape(1, D_MODEL)
    state_conv2 = state_conv.reshape(state_conv.shape[0], n_s, (CONV_WIDTH - 1) * E_CONV)

    xp = x_prompt
    xs = x_sample.reshape(n_s, t_s * D_MODEL)
    conv_p, conv_s, hgrn_p = [], [], []
    hgrn_s = None
    for layer in range(DEPTH):
        if layer % 2 == 0:
            xp, tail = _conv_prompt_call(xp, mods, norm_g3, conv_w_in_b, conv_w, conv_w_out_b, layer, mod_block_p)
            conv_p.append(tail[:, SUBLANE - (CONV_WIDTH - 1):])
            xs, st = _conv_sample_call(xs, mods, norm_g3, conv_w_in_b, conv_w, conv_w_out_b, state_conv2,
                                       layer, t_s)
            conv_s.append(st.reshape(n_s, CONV_WIDTH - 1, E_CONV))
        else:
            xp, s_new = _hgrn_prompt_call(xp, mods, norm_g3, hgrn_w_in_b, hgrn_lower_bounds, og3,
                                          hgrn_w_out_b, fg, layer, mod_block_p)
            hgrn_p.append(s_new)
            qe, kr, v, el, o_intra, z = _hgrn_sample_pre_call(xs, mods, norm_g3, hgrn_w_in_b,
                                                              hgrn_lower_bounds, layer, t_s)
            o_inter, hgrn_s = _hgrn_sample_state_call(qe, kr, v, el, state_hgrn, hgrn_s, layer, t_s)
            xs = _hgrn_sample_post_call(xs, mods, o_inter, o_intra, z, og3, hgrn_w_out_b, fg, layer, t_s)
    return (xp, xs.reshape(n_s, t_s, D_MODEL), jnp.stack(conv_p), jnp.stack(hgrn_p), jnp.stack(conv_s), hgrn_s)
```

```python
import functools

import jax
import jax.numpy as jnp
from jax import lax
from jax.experimental import pallas as pl
from jax.experimental.pallas import tpu as pltpu

F32 = jnp.float32
BF16 = jnp.bfloat16

D_MODEL = 1024
DEPTH = 4
CONV_WIDTH = 3
E_CONV = D_MODEL
HG_HEADS = 8
HG_DK = 128
HG_DV = 128
HG_F = HG_HEADS * HG_DK
HG_I = HG_HEADS * HG_DV
EPS = 1e-6

SUBLANE = 8
CHUNK = 128
SUB = CHUNK // 2
CONV_COL_TILE = 256
CONV_TB = 512
CONV_GB = 2
HGRN_TB = 256
HGRN_GB = 2
HGRN_PIPE_SKEW = 3
HGRN_SAFE_LOG2_SPAN = 200.0
SAMPLE_SEQ_BLOCK = 32
SAMPLE_STATE_BLOCK = 8
VMEM_LIMIT = 56 * 1024 * 1024


def _dot(a, b):
    return jnp.dot(a, b, preferred_element_type=F32)


def _dot_nt(a, b):
    return lax.dot_general(a, b, (((1,), (1,)), ((), ())), preferred_element_type=F32)


def _dot_tn(a, b):
    return lax.dot_general(a, b, (((0,), (0,)), ((), ())), preferred_element_type=F32)


def _silu(x):
    return x * jax.nn.sigmoid(x)


def _rmsnorm(x, g):
    ms = jnp.mean(x * x, axis=-1, keepdims=True)
    return x * lax.rsqrt(ms + EPS) * g


def _split_mod(mod):
    return mod[:, :D_MODEL], mod[:, D_MODEL:2 * D_MODEL], mod[:, 2 * D_MODEL:]


def _mod_norm(x, g, shift, scale):
    return (_rmsnorm(x, g) * (1.0 + scale) + shift).astype(BF16)


def _mod_norm_seq(x, g, shift, scale):
    ms = jnp.mean(x * x, axis=-1, keepdims=True)
    return (x * lax.rsqrt(ms + EPS) * (g * (1.0 + scale)) + shift).astype(BF16)


def _lower_bound(raw, layer):
    rows = [raw[i:i + 1, :] for i in range(DEPTH)]
    m = functools.reduce(jnp.maximum, rows)
    es = [jnp.exp(r - m) for r in rows]
    tot = functools.reduce(lambda a, b: a + b, es)
    acc = es[1]
    for i in range(2, layer + 1):
        acc = acc + es[i]
    return acc / tot


def _gate_math(fpre, lb):
    t = jnp.exp(-jnp.abs(fpre))
    r = 1.0 / (1.0 + t)
    pos = fpre >= 0.0
    sig = jnp.where(pos, 1.0, t) * r
    sig_neg = jnp.where(pos, t, 1.0) * r
    return jnp.log2(lb + (1.0 - lb) * sig), (1.0 - lb) * sig_neg


def _head_norm_gate(o, z, og):
    parts = []
    for hd in range(HG_HEADS):
        hs = slice(hd * HG_DV, (hd + 1) * HG_DV)
        oh = o[:, hs]
        ms = jnp.mean(oh * oh, axis=-1, keepdims=True)
        parts.append(oh * lax.rsqrt(ms + EPS))
    on = jnp.concatenate(parts, axis=1) * og
    return (on * _silu(z)).astype(BF16)


def _params(n_grid):
    return pltpu.CompilerParams(dimension_semantics=("arbitrary",) * n_grid, vmem_limit_bytes=VMEM_LIMIT)


def _resident(shape, index, n_grid):
    if n_grid == 1:
        return pl.BlockSpec(shape, lambda i: index, pipeline_mode=pl.Buffered(1))
    return pl.BlockSpec(shape, lambda b, t: index, pipeline_mode=pl.Buffered(1))


def _ada_kernel(c_ref, w_ref, b_ref, o_ref):
    s = _silu(c_ref[...]).astype(BF16)
    o_ref[...] = _dot(s, w_ref[...].astype(BF16)) + b_ref[...]


def _ada_call(c_all, w_ada, b_ada):
    rows = c_all.shape[0]
    n_col = 3
    col = 3 * D_MODEL // n_col
    return pl.pallas_call(
        _ada_kernel,
        grid=(DEPTH, n_col),
        in_specs=[
            pl.BlockSpec((rows, D_MODEL), lambda l, n: (0, 0)),
            pl.BlockSpec((None, D_MODEL, col), lambda l, n: (l, 0, n)),
            pl.BlockSpec((None, 1, col), lambda l, n: (l, 0, n)),
        ],
        out_specs=pl.BlockSpec((None, rows, col), lambda l, n: (l, 0, n)),
        out_shape=jax.ShapeDtypeStruct((DEPTH, rows, 3 * D_MODEL), F32),
        compiler_params=_params(2),
        name="ada_mod",
    )(c_all, w_ada, b_ada.reshape(DEPTH, 1, 3 * D_MODEL))


def _conv_columns(h, win_ref, cw_ref, y_scr, prev_fn, tail_fn):
    ct = CONV_COL_TILE
    for n in range(E_CONV // ct):
        def col(j):
            return slice(j * E_CONV + n * ct, j * E_CONV + (n + 1) * ct)
        cs = slice(n * ct, (n + 1) * ct)
        b_gate = _dot(h, win_ref[:, col(0)])
        c_gate = _dot(h, win_ref[:, col(1)])
        v = _dot(h, win_ref[:, col(2)])
        z = _dot(h, win_ref[:, col(3)])
        u = c_gate * v
        p1, p2 = prev_fn(cs, u)
        conv = cw_ref[0:1, cs] * p2 + cw_ref[1:2, cs] * p1 + cw_ref[2:3, cs] * u
        y_scr[:, cs] = (b_gate * conv * _silu(z)).astype(BF16)
        tail_fn(cs, u)


def _conv_prompt_kernel(x_ref, mod_ref, g_ref, win_ref, cw_ref, wout_ref, xo_ref, st_ref, y_scr, *, tb, gb):
    b = pl.program_id(0)
    t = pl.program_id(1)
    ct = CONV_COL_TILE

    @pl.when(t == 0)
    def _():
        st_ref[...] = jnp.zeros_like(st_ref)

    seqs = range(gb)
    x = [x_ref[g] for g in seqs]
    mod = [_split_mod(mod_ref[pl.ds(b * gb + g, 1), :]) for g in seqs]
    h = [_mod_norm_seq(x[g], g_ref[...], mod[g][0], mod[g][1]) for g in seqs]
    row = lax.broadcasted_iota(jnp.int32, (tb, ct), 0)

    for g in seqs:
        def prev_fn(cs, u, g=g):
            c0 = st_ref[g, SUBLANE - 2:SUBLANE - 1, cs]
            c1 = st_ref[g, SUBLANE - 1:SUBLANE, cs]
            p1 = jnp.where(row == 0, c1, pltpu.roll(u, 1, 0))
            p2 = jnp.where(row == 0, c0, jnp.where(row == 1, c1, pltpu.roll(u, 2, 0)))
            return p1, p2

        def tail_fn(cs, u, g=g):
            st_ref[g, :, cs] = u[tb - SUBLANE:tb, :]

        _conv_columns(h[g], win_ref, cw_ref, y_scr.at[g], prev_fn, tail_fn)
    for g in seqs:
        xo_ref[g] = x[g] + mod[g][2] * _dot(y_scr[g], wout_ref[...])


def _conv_prompt_call(x, mods, norm_g, w_in, cw, w_out, layer, mod_block):
    bsz, seq, _ = x.shape
    tb = CONV_TB
    gb = CONV_GB
    j = layer // 2
    return pl.pallas_call(
        functools.partial(_conv_prompt_kernel, tb=tb, gb=gb),
        grid=(bsz // gb, seq // tb),
        in_specs=[
            pl.BlockSpec((gb, tb, D_MODEL), lambda b, t: (b, t, 0)),
            _resident((None, bsz, 3 * D_MODEL), (layer, mod_block, 0), 2),
            _resident((None, 1, D_MODEL), (layer, 0, 0), 2),
            _resident((None, D_MODEL, 4 * E_CONV), (j, 0, 0), 2),
            _resident((None, CONV_WIDTH, E_CONV), (j, 0, 0), 2),
            _resident((None, E_CONV, D_MODEL), (j, 0, 0), 2),
        ],
        out_specs=[
            pl.BlockSpec((gb, tb, D_MODEL), lambda b, t: (b, t, 0)),
            pl.BlockSpec((gb, SUBLANE, E_CONV), lambda b, t: (b, 0, 0)),
        ],
        out_shape=[
            jax.ShapeDtypeStruct((bsz, seq, D_MODEL), F32),
            jax.ShapeDtypeStruct((bsz, SUBLANE, E_CONV), F32),
        ],
        scratch_shapes=[pltpu.VMEM((gb, tb, E_CONV), BF16)],
        compiler_params=_params(2),
        name="conv_prompt",
    )(x, mods, norm_g, w_in, cw, w_out)


def _tok(s, width):
    return slice(s * width, (s + 1) * width)


def _stack_tokens(ref, n_tok, width, cols=None):
    parts = []
    for s in range(n_tok):
        lo = s * width + (0 if cols is None else cols.start)
        hi = s * width + (width if cols is None else cols.stop)
        parts.append(ref[:, lo:hi])
    return jnp.concatenate(parts, axis=0)


def _sample_hidden(x_ref, mod_ref, g_ref, n_tok):
    shift, scale, _ = _split_mod(mod_ref[...])
    g = g_ref[...]
    return jnp.concatenate(
        [_mod_norm(x_ref[:, _tok(s, D_MODEL)], g, shift, scale) for s in range(n_tok)], axis=0)


def _conv_sample_kernel(x_ref, mod_ref, g_ref, win_ref, cw_ref, wout_ref, st_ref,
                        xo_ref, so_ref, y_scr, *, n_tok):
    ns = x_ref.shape[0]
    h = _sample_hidden(x_ref, mod_ref, g_ref, n_tok)

    def prev_fn(cs, u):
        st0 = st_ref[:, cs]
        st1 = st_ref[:, E_CONV + cs.start:E_CONV + cs.stop]
        p1 = jnp.concatenate([st1, u[:(n_tok - 1) * ns]], axis=0)
        p2 = jnp.concatenate([st0, st1, u[:(n_tok - 2) * ns]], axis=0)
        return p1, p2

    def tail_fn(cs, u):
        so_ref[:, cs] = u[(n_tok - 2) * ns:(n_tok - 1) * ns]
        so_ref[:, E_CONV + cs.start:E_CONV + cs.stop] = u[(n_tok - 1) * ns:]

    _conv_columns(h, win_ref, cw_ref, y_scr, prev_fn, tail_fn)
    out = _dot(y_scr[...], wout_ref[...])
    gate = mod_ref[:, 2 * D_MODEL:]
    for s in range(n_tok):
        xo_ref[:, _tok(s, D_MODEL)] = x_ref[:, _tok(s, D_MODEL)] + gate * out[s * ns:(s + 1) * ns]


def _seq_rows(width, layer=None):
    if layer is None:
        return pl.BlockSpec((SAMPLE_SEQ_BLOCK, width), lambda i: (i, 0))
    return pl.BlockSpec((None, SAMPLE_SEQ_BLOCK, width), lambda i: (layer, i, 0))


def _conv_sample_call(x, mods, norm_g, w_in, cw, w_out, state, layer, n_tok):
    n_seq = x.shape[0]
    j = layer // 2
    return pl.pallas_call(
        functools.partial(_conv_sample_kernel, n_tok=n_tok),
        grid=(n_seq // SAMPLE_SEQ_BLOCK,),
        in_specs=[
            _seq_rows(n_tok * D_MODEL),
            _seq_rows(3 * D_MODEL, layer),
            _resident((None, 1, D_MODEL), (layer, 0, 0), 1),
            _resident((None, D_MODEL, 4 * E_CONV), (j, 0, 0), 1),
            _resident((None, CONV_WIDTH, E_CONV), (j, 0, 0), 1),
            _resident((None, E_CONV, D_MODEL), (j, 0, 0), 1),
            _seq_rows((CONV_WIDTH - 1) * E_CONV, j),
        ],
        out_specs=[_seq_rows(n_tok * D_MODEL), _seq_rows((CONV_WIDTH - 1) * E_CONV)],
        out_shape=[
            jax.ShapeDtypeStruct((n_seq, n_tok * D_MODEL), F32),
            jax.ShapeDtypeStruct((n_seq, (CONV_WIDTH - 1) * E_CONV), F32),
        ],
        scratch_shapes=[pltpu.VMEM((n_tok * SAMPLE_SEQ_BLOCK, E_CONV), BF16)],
        compiler_params=_params(1),
        name="conv_sample",
    )(x, mods, norm_g, w_in, cw, w_out, state)


def _rows_from_blocks(vecs):
    return jnp.concatenate([jnp.broadcast_to(v, (SUB, v.shape[1])) for v in vecs], axis=0)


def _chunk_masks():
    r = lax.broadcasted_iota(jnp.int32, (CHUNK, CHUNK), 0)
    c = lax.broadcasted_iota(jnp.int32, (CHUNK, CHUNK), 1)
    causal = (r // SUB == c // SUB) & (c <= r)
    tri = jnp.where(causal, 1.0, 0.0).astype(BF16)
    first_cols = lax.broadcasted_iota(jnp.int32, (SUB, CHUNK), 1) < SUB
    return causal, first_cols, jnp.concatenate([tri, tri], axis=1)


def _chunk_decay(lf, tri2):
    hi = lf.astype(BF16)
    lo = (lf - hi.astype(F32)).astype(BF16)
    return _dot(tri2, jnp.concatenate([hi, lo], axis=0))


def _chunk_key_operands(kk, gl):
    t0 = gl[SUB - 1:SUB, :]
    t1 = gl[2 * SUB - 1:2 * SUB, :]
    k_mid = kk * jnp.exp2(_rows_from_blocks([0.5 * t0, 0.5 * t1]) - gl)
    k_state = k_mid * _rows_from_blocks([jnp.exp2(0.5 * t0 + t1), jnp.exp2(0.5 * t1)])
    return k_mid.astype(BF16), k_state.astype(BF16), jnp.exp2(t0 + t1), jnp.minimum(t0, t1)


def _chunk_query_operands(qf, gl):
    t0 = gl[SUB - 1:SUB, :]
    t1 = gl[2 * SUB - 1:2 * SUB, :]
    q_mid = qf * jnp.exp2(gl - _rows_from_blocks([0.5 * t0, 0.5 * t1]))
    q_next = q_mid[SUB:] * jnp.exp2(0.5 * (t0 + t1))
    q_state = q_mid * _rows_from_blocks([jnp.exp2(0.5 * t0), jnp.exp2(0.5 * t1 + t0)])
    return jnp.concatenate([q_mid, q_next], axis=0).astype(BF16), q_state.astype(BF16)


def _chunk_finish(ast, qe, kr, vb, ptot, s_old, causal, first_cols):
    same = jnp.where(causal, ast[0:CHUNK], 0.0)
    a = jnp.concatenate([same[:SUB], jnp.where(first_cols, ast[CHUNK:], same[SUB:])], axis=0)
    o = _dot(jnp.concatenate([qe, a.astype(BF16)], axis=1),
             jnp.concatenate([s_old.astype(BF16), vb], axis=0))
    dec = jnp.broadcast_to(ptot, (HG_DK, HG_DK)).T
    return o, s_old * dec + _dot_tn(kr, vb)


def _exact_recurrence(q_ref, k_ref, f_ref, v_ref, o_ref, s0, n_rows):
    def col(tile, r):
        return jnp.broadcast_to(tile[r:r + 1, :], (HG_DK, HG_DK)).T

    def step(i, s):
        rows = pl.ds(pl.multiple_of(i * SUBLANE, SUBLANE), SUBLANE)
        q, k, f, v = q_ref[rows, :], k_ref[rows, :], f_ref[rows, :], v_ref[rows, :]
        out = []
        for r in range(SUBLANE):
            s = s * col(f, r) + col(k, r) * v[r:r + 1, :]
            out.append(jnp.sum(col(q, r) * s, axis=0, keepdims=True))
        o_ref[rows, :] = jnp.concatenate(out, axis=0)
        return s
    return lax.fori_loop(0, n_rows // SUBLANE, step, s0)


def _hgrn_prompt_kernel(x_ref, mod_ref, g_ref, win_ref, lbraw_ref, og_ref, wout_ref, fg_ref,
                        xo_ref, s_ref, gl_scr, z_scr, o_scr, lhs_scr, km_scr, qe_scr, kr_scr, vb_scr, pt_scr,
                        s0_scr, *, tb, gb, layer, final):
    b = pl.program_id(0)
    t = pl.program_id(1)
    n_chunks = tb // CHUNK
    lhs_rows = CHUNK + SUB

    @pl.when(t == 0)
    def _():
        s_ref[...] = jnp.zeros_like(s_ref)


    seqs = range(gb)
    chunks = [slice(c * CHUNK, (c + 1) * CHUNK) for c in range(n_chunks)]
    causal, first_cols, tri2 = _chunk_masks()
    lb = _lower_bound(lbraw_ref[...], layer)
    mod = [_split_mod(mod_ref[pl.ds(b * gb + g, 1), :]) for g in seqs]
    h = [_mod_norm_seq(x_ref[g], g_ref[...], mod[g][0], mod[g][1]) for g in seqs]
    gates = [_gate_math(_dot(h[g], win_ref[:, HG_F:2 * HG_F]), lb) for g in seqs]
    qf = [_silu(_dot(h[g], win_ref[:, 0:HG_F])) for g in seqs]
    span = jnp.zeros((1, HG_F), F32)
    for g in seqs:
        logf, kk = gates[g]
        for c, rows in enumerate(chunks):
            gl = _chunk_decay(logf[rows], tri2)
            gl_scr[g, rows, :] = gl
            km_scr[g, rows, :], kr_scr[g, rows, :], ptot, tmin = _chunk_key_operands(kk[rows], gl)
            pt_scr[g, c * SUBLANE:(c + 1) * SUBLANE, :] = jnp.broadcast_to(ptot, (SUBLANE, HG_F))
            span = jnp.minimum(span, tmin)
    for g in seqs:
        for c, rows in enumerate(chunks):
            lhs, qe = _chunk_query_operands(qf[g][rows], gl_scr[g, rows, :])
            lhs_scr[g, c * lhs_rows:(c + 1) * lhs_rows, :] = lhs
            qe_scr[g, rows, :] = qe
    for g in seqs:
        vb_scr[g] = _dot(h[g], win_ref[:, 2 * HG_F:2 * HG_F + HG_I]).astype(BF16)
    for g in seqs:
        z_scr[g] = _dot(h[g], win_ref[:, 2 * HG_F + HG_I:])

    def finish_rows(g, rows):
        y = _head_norm_gate(o_scr[g, rows, :], z_scr[g, rows, :], og_ref[...])
        xn = x_ref[g, rows, :] + mod[g][2] * _dot(y, wout_ref[...])
        if final:
            xn = _rmsnorm(xn, fg_ref[...])
        xo_ref[g, rows, :] = xn

    chains = [(c, hd, g) for c in range(n_chunks) for hd in range(HG_HEADS) for g in seqs]
    scores = {}
    for step in range(len(chains) + HGRN_PIPE_SKEW):
        if step < len(chains):
            c, hd, g = chains[step]
            hs = slice(hd * HG_DK, (hd + 1) * HG_DK)
            scores[step] = _dot_nt(lhs_scr[g, c * lhs_rows:(c + 1) * lhs_rows, hs], km_scr[g, chunks[c], hs])
        n = step - HGRN_PIPE_SKEW
        if n >= 0:
            c, hd, g = chains[n]
            rows = chunks[c]
            hs = slice(hd * HG_DK, (hd + 1) * HG_DK)
            o, s_new = _chunk_finish(scores.pop(n), qe_scr[g, rows, hs], kr_scr[g, rows, hs],
                                     vb_scr[g, rows, hs], pt_scr[g, c * SUBLANE:c * SUBLANE + 1, hs],
                                     s_ref[g, hd], causal, first_cols)
            o_scr[g, rows, hs] = o
            s_ref[g, hd] = s_new
            if n + 1 == len(chains) or chains[n + 1][0] != c:
                for gg in seqs:
                    finish_rows(gg, rows)

    def _unused():
        def exact(q_tmp, k_tmp, f_tmp, v_tmp, o_tmp):
            def one_seq(g, carry):
                shift, scale, gate = _split_mod(mod_ref[pl.ds(b * gb + g, 1), :])
                hg = _mod_norm_seq(x_ref[g], g_ref[...], shift, scale)
                qf = _silu(_dot(hg, win_ref[:, 0:HG_F]))
                logf, kk = _gate_math(_dot(hg, win_ref[:, HG_F:2 * HG_F]), lb)
                ff = jnp.exp2(logf)
                vv = _dot(hg, win_ref[:, 2 * HG_F:2 * HG_F + HG_I])
                for hd in range(HG_HEADS):
                    hs = slice(hd * HG_DK, (hd + 1) * HG_DK)
                    q_tmp[hd], k_tmp[hd], f_tmp[hd], v_tmp[hd] = qf[:, hs], kk[:, hs], ff[:, hs], vv[:, hs]

                def one_head(hd, c2):
                    s_ref[g, hd] = _exact_recurrence(q_tmp.at[hd], k_tmp.at[hd], f_tmp.at[hd], v_tmp.at[hd],
                                                     o_tmp.at[hd], s0_scr[g, hd], tb)
                    return c2

                lax.fori_loop(0, HG_HEADS, one_head, 0)
                o = jnp.concatenate([o_tmp[hd] for hd in range(HG_HEADS)], axis=1)
                y = _head_norm_gate(o, z_scr[g], og_ref[...])
                xn = x_ref[g] + gate * _dot(y, wout_ref[...])
                if final:
                    xn = _rmsnorm(xn, fg_ref[...])
                xo_ref[g] = xn
                return carry

            lax.fori_loop(0, gb, one_seq, 0)

        pl.run_scoped(exact, *[pltpu.VMEM((HG_HEADS, tb, HG_DK), F32)] * 5)


def _hgrn_prompt_call(x, mods, norm_g, w_in, lb_raw, og, w_out, fg, layer, mod_block):
    bsz, seq, _ = x.shape
    tb = HGRN_TB
    gb = HGRN_GB
    j = layer // 2
    final = layer == DEPTH - 1
    in_specs = [
        pl.BlockSpec((gb, tb, D_MODEL), lambda b, t: (b, t, 0)),
        _resident((None, bsz, 3 * D_MODEL), (layer, mod_block, 0), 2),
        _resident((None, 1, D_MODEL), (layer, 0, 0), 2),
        _resident((None, D_MODEL, 2 * HG_F + 2 * HG_I), (j, 0, 0), 2),
        _resident((DEPTH, HG_F), (0, 0), 2),
        _resident((None, 1, HG_I), (j, 0, 0), 2),
        _resident((None, HG_I, D_MODEL), (j, 0, 0), 2),
        _resident((1, D_MODEL), (0, 0), 2),
    ]
    return pl.pallas_call(
        functools.partial(_hgrn_prompt_kernel, tb=tb, gb=gb, layer=layer, final=final),
        grid=(bsz // gb, seq // tb),
        in_specs=in_specs,
        out_specs=[
            pl.BlockSpec((gb, tb, D_MODEL), lambda b, t: (b, t, 0)),
            pl.BlockSpec((gb, HG_HEADS, HG_DK, HG_DV), lambda b, t: (b, 0, 0, 0)),
        ],
        out_shape=[
            jax.ShapeDtypeStruct((bsz, seq, D_MODEL), F32),
            jax.ShapeDtypeStruct((bsz, HG_HEADS, HG_DK, HG_DV), F32),
        ],
        scratch_shapes=[
            pltpu.VMEM((gb, tb, HG_F), F32),
            pltpu.VMEM((gb, tb, HG_I), F32),
            pltpu.VMEM((gb, tb, HG_I), F32),
            pltpu.VMEM((gb, tb + tb // 2, HG_F), BF16),
            pltpu.VMEM((gb, tb, HG_F), BF16),
            pltpu.VMEM((gb, tb, HG_F), BF16),
            pltpu.VMEM((gb, tb, HG_F), BF16),
            pltpu.VMEM((gb, tb, HG_I), BF16),
            pltpu.VMEM((gb, (tb // CHUNK) * SUBLANE, HG_F), F32),
            pltpu.VMEM((gb, HG_HEADS, HG_DK, HG_DV), F32),
        ],
        compiler_params=_params(2),
        name="hgrn_prompt",
    )(x, mods, norm_g, w_in, lb_raw, og, w_out, fg)


def _hgrn_sample_pre_kernel(x_ref, mod_ref, g_ref, win_ref, lbraw_ref,
                            qe_ref, kr_ref, v_ref, el_ref, oi_ref, z_ref, *, n_tok, layer):
    ns = x_ref.shape[0]
    h = _sample_hidden(x_ref, mod_ref, g_ref, n_tok)
    lb = _lower_bound(lbraw_ref[...], layer)
    qf = _silu(_dot(h, win_ref[:, 0:HG_F]))
    lf, kk = _gate_math(_dot(h, win_ref[:, HG_F:2 * HG_F]), lb)
    vv = _dot(h, win_ref[:, 2 * HG_F:2 * HG_F + HG_I])
    z = _dot(h, win_ref[:, 2 * HG_F + HG_I:])

    def tok(a, s):
        return a[s * ns:(s + 1) * ns]

    g = [tok(lf, 0)]
    for s in range(1, n_tok):
        g.append(g[-1] + tok(lf, s))
    el_ref[...] = jnp.exp2(g[-1])
    for s in range(n_tok):
        qe_ref[:, _tok(s, HG_F)] = tok(qf, s) * jnp.exp2(g[s])
        kr_ref[:, _tok(s, HG_F)] = tok(kk, s) * jnp.exp2(g[-1] - g[s])
        v_ref[:, _tok(s, HG_I)] = tok(vv, s)
        z_ref[:, _tok(s, HG_I)] = tok(z, s)

    for t in range(n_tok):
        acc = jnp.zeros((ns, HG_I), F32)
        for s in range(t + 1):
            e = tok(qf, t) * tok(kk, s)
            if s < t:
                e = e * jnp.exp2(g[t] - g[s])
            vs = tok(vv, s)
            parts = []
            for hd in range(HG_HEADS):
                hs = slice(hd * HG_DK, (hd + 1) * HG_DK)
                parts.append(jnp.sum(e[:, hs], axis=-1, keepdims=True) * vs[:, hs])
            acc = acc + jnp.concatenate(parts, axis=1)
        oi_ref[:, _tok(t, HG_I)] = acc


def _hgrn_sample_pre_call(x, mods, norm_g, w_in, lb_raw, layer, n_tok):
    n_seq = x.shape[0]
    j = layer // 2
    act = jax.ShapeDtypeStruct((n_seq, n_tok * HG_F), F32)
    return pl.pallas_call(
        functools.partial(_hgrn_sample_pre_kernel, n_tok=n_tok, layer=layer),
        grid=(n_seq // SAMPLE_SEQ_BLOCK,),
        in_specs=[
            _seq_rows(n_tok * D_MODEL),
            _seq_rows(3 * D_MODEL, layer),
            _resident((None, 1, D_MODEL), (layer, 0, 0), 1),
            _resident((None, D_MODEL, 2 * HG_F + 2 * HG_I), (j, 0, 0), 1),
            _resident((DEPTH, HG_F), (0, 0), 1),
        ],
        out_specs=[_seq_rows(n_tok * HG_F)] * 3 + [_seq_rows(HG_F)] + [_seq_rows(n_tok * HG_I)] * 2,
        out_shape=[act, act, act, jax.ShapeDtypeStruct((n_seq, HG_F), F32), act, act],
        compiler_params=_params(1),
        name="hgrn_sample_pre",
    )(x, mods, norm_g, w_in, lb_raw)


def _hgrn_sample_state_kernel(*refs, n_tok, has_prev):
    n_in = 5 + int(has_prev)
    qe_ref, kr_ref, v_ref, el_ref, s_ref = refs[:5]
    oo_ref, so_ref = refs[n_in:n_in + 2]
    ns = SAMPLE_STATE_BLOCK
    seq_of_row = lax.broadcasted_iota(jnp.int32, (n_tok * ns, HG_DK), 0) % ns
    for hd in range(HG_HEADS):
        hs = slice(hd * HG_DK, (hd + 1) * HG_DK)
        q = _stack_tokens(qe_ref, n_tok, HG_F, hs).astype(BF16)
        k = _stack_tokens(kr_ref, n_tok, HG_F, hs)
        v = _stack_tokens(v_ref, n_tok, HG_I, hs).astype(BF16)
        el = el_ref[:, hs]
        o = jnp.zeros((n_tok * ns, HG_DV), F32)
        for b in range(ns):
            mine = seq_of_row == b
            s_old = s_ref[b, hd]
            o = jnp.where(mine, _dot(q, s_old.astype(BF16)), o)
            kb = jnp.where(mine, k, 0.0).astype(BF16)
            dec = jnp.broadcast_to(el[b:b + 1, :], (HG_DK, HG_DK)).T
            so_ref[b, hd] = s_old * dec + _dot_tn(kb, v)
        for t in range(n_tok):
            oo_ref[:, t * HG_I + hs.start:t * HG_I + hs.stop] = o[t * ns:(t + 1) * ns]


def _hgrn_sample_state_call(qe, kr, v, el, states, prev_states, layer, n_tok):
    n_seq = qe.shape[0]
    j = layer // 2
    nb = SAMPLE_STATE_BLOCK
    rows = lambda width: pl.BlockSpec((nb, width), lambda i: (i, 0))
    st_spec = pl.BlockSpec((None, nb, HG_HEADS, HG_DK, HG_DV), lambda i: (j, i, 0, 0, 0))
    in_specs = [rows(n_tok * HG_F), rows(n_tok * HG_F), rows(n_tok * HG_I), rows(HG_F), st_spec]
    args = [qe, kr, v, el, states]
    aliases = {}
    if prev_states is not None:
        in_specs.append(pl.BlockSpec(memory_space=pl.ANY))
        args.append(prev_states)
        aliases = {len(args) - 1: 1}
    return pl.pallas_call(
        functools.partial(_hgrn_sample_state_kernel, n_tok=n_tok, has_prev=prev_states is not None),
        grid=(n_seq // nb,),
        in_specs=in_specs,
        out_specs=[rows(n_tok * HG_I), st_spec],
        out_shape=[
            jax.ShapeDtypeStruct((n_seq, n_tok * HG_I), F32),
            jax.ShapeDtypeStruct(states.shape, F32),
        ],
        input_output_aliases=aliases,
        compiler_params=_params(1),
        name="hgrn_sample_state",
    )(*args)


def _hgrn_sample_post_kernel(x_ref, mod_ref, oa_ref, ob_ref, z_ref, og_ref, wout_ref, fg_ref, xo_ref,
                             *, n_tok, final):
    ns = x_ref.shape[0]
    o = _stack_tokens(oa_ref, n_tok, HG_I) + _stack_tokens(ob_ref, n_tok, HG_I)
    y = _head_norm_gate(o, _stack_tokens(z_ref, n_tok, HG_I), og_ref[...])
    out = _dot(y, wout_ref[...])
    gate = mod_ref[:, 2 * D_MODEL:]
    for s in range(n_tok):
        xn = x_ref[:, _tok(s, D_MODEL)] + gate * out[s * ns:(s + 1) * ns]
        if final:
            xn = _rmsnorm(xn, fg_ref[...])
        xo_ref[:, _tok(s, D_MODEL)] = xn


def _hgrn_sample_post_call(x, mods, o_inter, o_intra, z, og, w_out, fg, layer, n_tok):
    n_seq = x.shape[0]
    j = layer // 2
    return pl.pallas_call(
        functools.partial(_hgrn_sample_post_kernel, n_tok=n_tok, final=layer == DEPTH - 1),
        grid=(n_seq // SAMPLE_SEQ_BLOCK,),
        in_specs=[
            _seq_rows(n_tok * D_MODEL),
            _seq_rows(3 * D_MODEL, layer),
            _seq_rows(n_tok * HG_I), _seq_rows(n_tok * HG_I), _seq_rows(n_tok * HG_I),
            _resident((None, 1, HG_I), (j, 0, 0), 1),
            _resident((None, HG_I, D_MODEL), (j, 0, 0), 1),
            _resident((1, D_MODEL), (0, 0), 1),
        ],
        out_specs=_seq_rows(n_tok * D_MODEL),
        out_shape=jax.ShapeDtypeStruct((n_seq, n_tok * D_MODEL), F32),
        compiler_params=_params(1),
        name="hgrn_sample_post",
    )(x, mods, o_inter, o_intra, z, og, w_out, fg)


def kernel(x_prompt, x_sample, state_conv, state_hgrn, c_prompt, c_sample, norm_g, w_ada, b_ada, conv_w_in, conv_w, conv_w_out, hgrn_w_in, hgrn_lower_bounds, hgrn_onorm_g, hgrn_w_out, final_norm_g):
    n_p = x_prompt.shape[0]
    n_s, t_s, _ = x_sample.shape
    assert t_s >= CONV_WIDTH - 1 and n_s % SAMPLE_SEQ_BLOCK == 0 and n_s % n_p == 0

    mods = _ada_call(jnp.concatenate([c_sample, c_prompt], axis=0), w_ada, b_ada)
    mod_block_p = n_s // n_p

    conv_w_in_b = conv_w_in.astype(BF16)
    conv_w_out_b = conv_w_out.astype(BF16)
    hgrn_w_in_b = hgrn_w_in.astype(BF16)
    hgrn_w_out_b = hgrn_w_out.astype(BF16)
    norm_g3 = norm_g.reshape(DEPTH, 1, D_MODEL)
    og3 = hgrn_onorm_g.reshape(DEPTH // 2, 1, HG_I)
    fg = final_norm_g.reshape(1, D_MODEL)
    state_conv2 = state_conv.reshape(state_conv.shape[0], n_s, (CONV_WIDTH - 1) * E_CONV)

    xp = x_prompt
    xs = x_sample.reshape(n_s, t_s * D_MODEL)
    conv_p, conv_s, hgrn_p = [], [], []
    hgrn_s = None
    for layer in range(DEPTH):
        if layer % 2 == 0:
            xp, tail = _conv_prompt_call(xp, mods, norm_g3, conv_w_in_b, conv_w, conv_w_out_b, layer, mod_block_p)
            conv_p.append(tail[:, SUBLANE - (CONV_WIDTH - 1):])
            xs, st = _conv_sample_call(xs, mods, norm_g3, conv_w_in_b, conv_w, conv_w_out_b, state_conv2,
                                       layer, t_s)
            conv_s.append(st.reshape(n_s, CONV_WIDTH - 1, E_CONV))
        else:
            xp, s_new = _hgrn_prompt_call(xp, mods, norm_g3, hgrn_w_in_b, hgrn_lower_bounds, og3,
                                          hgrn_w_out_b, fg, layer, mod_block_p)
            hgrn_p.append(s_new)
            qe, kr, v, el, o_intra, z = _hgrn_sample_pre_call(xs, mods, norm_g3, hgrn_w_in_b,
                                                              hgrn_lower_bounds, layer, t_s)
            o_inter, hgrn_s = _hgrn_sample_state_call(qe, kr, v, el, state_hgrn, hgrn_s, layer, t_s)
            xs = _hgrn_sample_post_call(xs, mods, o_inter, o_intra, z, og3, hgrn_w_out_b, fg, layer, t_s)
    return (xp, xs.reshape(n_s, t_s, D_MODEL), jnp.stack(conv_p), jnp.stack(hgrn_p), jnp.stack(conv_s), hgrn_s)
```

```python
import functools

import jax
import jax.numpy as jnp
from jax import lax
from jax.experimental import pallas as pl
from jax.experimental.pallas import tpu as pltpu

F32 = jnp.float32
BF16 = jnp.bfloat16

D_MODEL = 1024
DEPTH = 4
CONV_WIDTH = 3
E_CONV = D_MODEL
HG_HEADS = 8
HG_DK = 128
HG_DV = 128
HG_F = HG_HEADS * HG_DK
HG_I = HG_HEADS * HG_DV
EPS = 1e-6

SUBLANE = 8
CHUNK = 128
SUB = CHUNK // 2
CONV_COL_TILE = 256
CONV_TB = 512
CONV_GB = 2
HGRN_TB = 256
HGRN_GB = 2
HGRN_PIPE_SKEW = 3
HGRN_SAFE_LOG2_SPAN = 200.0
SAMPLE_SEQ_BLOCK = 32
SAMPLE_STATE_BLOCK = 8
VMEM_LIMIT = 56 * 1024 * 1024


def _dot(a, b):
    return jnp.dot(a, b, preferred_element_type=F32)


def _dot_nt(a, b):
    return lax.dot_general(a, b, (((1,), (1,)), ((), ())), preferred_element_type=F32)


def _dot_tn(a, b):
    return lax.dot_general(a, b, (((0,), (0,)), ((), ())), preferred_element_type=F32)


def _silu(x):
    return x * jax.nn.sigmoid(x)


def _rmsnorm(x, g):
    ms = jnp.mean(x * x, axis=-1, keepdims=True)
    return x * lax.rsqrt(ms + EPS) * g


def _split_mod(mod):
    return mod[:, :D_MODEL], mod[:, D_MODEL:2 * D_MODEL], mod[:, 2 * D_MODEL:]


def _mod_norm(x, g, shift, scale):
    return (_rmsnorm(x, g) * (1.0 + scale) + shift).astype(BF16)


def _mod_norm_seq(x, g, shift, scale):
    ms = jnp.mean(x * x, axis=-1, keepdims=True)
    return (x * lax.rsqrt(ms + EPS) * (g * (1.0 + scale)) + shift).astype(BF16)


def _lower_bound(raw, layer):
    rows = [raw[i:i + 1, :] for i in range(DEPTH)]
    m = functools.reduce(jnp.maximum, rows)
    es = [jnp.exp(r - m) for r in rows]
    tot = functools.reduce(lambda a, b: a + b, es)
    acc = es[1]
    for i in range(2, layer + 1):
        acc = acc + es[i]
    return acc / tot


def _gate_math(fpre, lb):
    t = jnp.exp(-jnp.abs(fpre))
    r = 1.0 / (1.0 + t)
    pos = fpre >= 0.0
    sig = jnp.where(pos, 1.0, t) * r
    sig_neg = jnp.where(pos, t, 1.0) * r
    return jnp.log2(lb + (1.0 - lb) * sig), (1.0 - lb) * sig_neg


def _head_norm_gate(o, z, og):
    parts = []
    for hd in range(HG_HEADS):
        hs = slice(hd * HG_DV, (hd + 1) * HG_DV)
        oh = o[:, hs]
        ms = jnp.mean(oh * oh, axis=-1, keepdims=True)
        parts.append(oh * lax.rsqrt(ms + EPS))
    on = jnp.concatenate(parts, axis=1) * og
    return (on * _silu(z)).astype(BF16)


def _params(n_grid):
    return pltpu.CompilerParams(dimension_semantics=("arbitrary",) * n_grid, vmem_limit_bytes=VMEM_LIMIT)


def _resident(shape, index, n_grid):
    if n_grid == 1:
        return pl.BlockSpec(shape, lambda i: index, pipeline_mode=pl.Buffered(1))
    return pl.BlockSpec(shape, lambda b, t: index, pipeline_mode=pl.Buffered(1))


def _ada_kernel(c_ref, w_ref, b_ref, o_ref):
    s = _silu(c_ref[...]).astype(BF16)
    o_ref[...] = _dot(s, w_ref[...].astype(BF16)) + b_ref[...]


def _ada_call(c_all, w_ada, b_ada):
    rows = c_all.shape[0]
    n_col = 3
    col = 3 * D_MODEL // n_col
    return pl.pallas_call(
        _ada_kernel,
        grid=(DEPTH, n_col),
        in_specs=[
            pl.BlockSpec((rows, D_MODEL), lambda l, n: (0, 0)),
            pl.BlockSpec((None, D_MODEL, col), lambda l, n: (l, 0, n)),
            pl.BlockSpec((None, 1, col), lambda l, n: (l, 0, n)),
        ],
        out_specs=pl.BlockSpec((None, rows, col), lambda l, n: (l, 0, n)),
        out_shape=jax.ShapeDtypeStruct((DEPTH, rows, 3 * D_MODEL), F32),
        compiler_params=_params(2),
        name="ada_mod",
    )(c_all, w_ada, b_ada.reshape(DEPTH, 1, 3 * D_MODEL))


def _conv_columns(h, win_ref, cw_ref, y_scr, prev_fn, tail_fn):
    ct = CONV_COL_TILE
    for n in range(E_CONV // ct):
        def col(j):
            return slice(j * E_CONV + n * ct, j * E_CONV + (n + 1) * ct)
        cs = slice(n * ct, (n + 1) * ct)
        b_gate = _dot(h, win_ref[:, col(0)])
        c_gate = _dot(h, win_ref[:, col(1)])
        v = _dot(h, win_ref[:, col(2)])
        z = _dot(h, win_ref[:, col(3)])
        u = c_gate * v
        p1, p2 = prev_fn(cs, u)
        conv = cw_ref[0:1, cs] * p2 + cw_ref[1:2, cs] * p1 + cw_ref[2:3, cs] * u
        y_scr[:, cs] = (b_gate * conv * _silu(z)).astype(BF16)
        tail_fn(cs, u)


def _conv_prompt_kernel(x_ref, mod_ref, g_ref, win_ref, cw_ref, wout_ref, xo_ref, st_ref, y_scr, *, tb, gb):
    b = pl.program_id(0)
    t = pl.program_id(1)
    ct = CONV_COL_TILE

    @pl.when(t == 0)
    def _():
        st_ref[...] = jnp.zeros_like(st_ref)

    seqs = range(gb)
    x = [x_ref[g] for g in seqs]
    mod = [_split_mod(mod_ref[pl.ds(b * gb + g, 1), :]) for g in seqs]
    h = [_mod_norm_seq(x[g], g_ref[...], mod[g][0], mod[g][1]) for g in seqs]
    row = lax.broadcasted_iota(jnp.int32, (tb, ct), 0)

    for g in seqs:
        def prev_fn(cs, u, g=g):
            c0 = st_ref[g, SUBLANE - 2:SUBLANE - 1, cs]
            c1 = st_ref[g, SUBLANE - 1:SUBLANE, cs]
            p1 = jnp.where(row == 0, c1, pltpu.roll(u, 1, 0))
            p2 = jnp.where(row == 0, c0, jnp.where(row == 1, c1, pltpu.roll(u, 2, 0)))
            return p1, p2

        def tail_fn(cs, u, g=g):
            st_ref[g, :, cs] = u[tb - SUBLANE:tb, :]

        _conv_columns(h[g], win_ref, cw_ref, y_scr.at[g], prev_fn, tail_fn)
    for g in seqs:
        xo_ref[g] = x[g] + mod[g][2] * _dot(y_scr[g], wout_ref[...])


def _conv_prompt_call(x, mods, norm_g, w_in, cw, w_out, layer, mod_block):
    bsz, seq, _ = x.shape
    tb = CONV_TB
    gb = CONV_GB
    j = layer // 2
    return pl.pallas_call(
        functools.partial(_conv_prompt_kernel, tb=tb, gb=gb),
        grid=(bsz // gb, seq // tb),
        in_specs=[
            pl.BlockSpec((gb, tb, D_MODEL), lambda b, t: (b, t, 0)),
            _resident((None, bsz, 3 * D_MODEL), (layer, mod_block, 0), 2),
            _resident((None, 1, D_MODEL), (layer, 0, 0), 2),
            _resident((None, D_MODEL, 4 * E_CONV), (j, 0, 0), 2),
            _resident((None, CONV_WIDTH, E_CONV), (j, 0, 0), 2),
            _resident((None, E_CONV, D_MODEL), (j, 0, 0), 2),
        ],
        out_specs=[
            pl.BlockSpec((gb, tb, D_MODEL), lambda b, t: (b, t, 0)),
            pl.BlockSpec((gb, SUBLANE, E_CONV), lambda b, t: (b, 0, 0)),
        ],
        out_shape=[
            jax.ShapeDtypeStruct((bsz, seq, D_MODEL), F32),
            jax.ShapeDtypeStruct((bsz, SUBLANE, E_CONV), F32),
        ],
        scratch_shapes=[pltpu.VMEM((gb, tb, E_CONV), BF16)],
        compiler_params=_params(2),
        name="conv_prompt",
    )(x, mods, norm_g, w_in, cw, w_out)


def _tok(s, width):
    return slice(s * width, (s + 1) * width)


def _stack_tokens(ref, n_tok, width, cols=None):
    parts = []
    for s in range(n_tok):
        lo = s * width + (0 if cols is None else cols.start)
        hi = s * width + (width if cols is None else cols.stop)
        parts.append(ref[:, lo:hi])
    return jnp.concatenate(parts, axis=0)


def _sample_hidden(x_ref, mod_ref, g_ref, n_tok):
    shift, scale, _ = _split_mod(mod_ref[...])
    g = g_ref[...]
    return jnp.concatenate(
        [_mod_norm(x_ref[:, _tok(s, D_MODEL)], g, shift, scale) for s in range(n_tok)], axis=0)


def _conv_sample_kernel(x_ref, mod_ref, g_ref, win_ref, cw_ref, wout_ref, st_ref,
                        xo_ref, so_ref, y_scr, *, n_tok):
    ns = x_ref.shape[0]
    h = _sample_hidden(x_ref, mod_ref, g_ref, n_tok)

    def prev_fn(cs, u):
        st0 = st_ref[:, cs]
        st1 = st_ref[:, E_CONV + cs.start:E_CONV + cs.stop]
        p1 = jnp.concatenate([st1, u[:(n_tok - 1) * ns]], axis=0)
        p2 = jnp.concatenate([st0, st1, u[:(n_tok - 2) * ns]], axis=0)
        return p1, p2

    def tail_fn(cs, u):
        so_ref[:, cs] = u[(n_tok - 2) * ns:(n_tok - 1) * ns]
        so_ref[:, E_CONV + cs.start:E_CONV + cs.stop] = u[(n_tok - 1) * ns:]

    _conv_columns(h, win_ref, cw_ref, y_scr, prev_fn, tail_fn)
    out = _dot(y_scr[...], wout_ref[...])
    gate = mod_ref[:, 2 * D_MODEL:]
    for s in range(n_tok):
        xo_ref[:, _tok(s, D_MODEL)] = x_ref[:, _tok(s, D_MODEL)] + gate * out[s * ns:(s + 1) * ns]


def _seq_rows(width, layer=None):
    if layer is None:
        return pl.BlockSpec((SAMPLE_SEQ_BLOCK, width), lambda i: (i, 0))
    return pl.BlockSpec((None, SAMPLE_SEQ_BLOCK, width), lambda i: (layer, i, 0))


def _conv_sample_call(x, mods, norm_g, w_in, cw, w_out, state, layer, n_tok):
    n_seq = x.shape[0]
    j = layer // 2
    return pl.pallas_call(
        functools.partial(_conv_sample_kernel, n_tok=n_tok),
        grid=(n_seq // SAMPLE_SEQ_BLOCK,),
        in_specs=[
            _seq_rows(n_tok * D_MODEL),
            _seq_rows(3 * D_MODEL, layer),
            _resident((None, 1, D_MODEL), (layer, 0, 0), 1),
            _resident((None, D_MODEL, 4 * E_CONV), (j, 0, 0), 1),
            _resident((None, CONV_WIDTH, E_CONV), (j, 0, 0), 1),
            _resident((None, E_CONV, D_MODEL), (j, 0, 0), 1),
            _seq_rows((CONV_WIDTH - 1) * E_CONV, j),
        ],
        out_specs=[_seq_rows(n_tok * D_MODEL), _seq_rows((CONV_WIDTH - 1) * E_CONV)],
        out_shape=[
            jax.ShapeDtypeStruct((n_seq, n_tok * D_MODEL), F32),
            jax.ShapeDtypeStruct((n_seq, (CONV_WIDTH - 1) * E_CONV), F32),
        ],
        scratch_shapes=[pltpu.VMEM((n_tok * SAMPLE_SEQ_BLOCK, E_CONV), BF16)],
        compiler_params=_params(1),
        name="conv_sample",
    )(x, mods, norm_g, w_in, cw, w_out, state)


def _rows_from_blocks(vecs):
    return jnp.concatenate([jnp.broadcast_to(v, (SUB, v.shape[1])) for v in vecs], axis=0)


def _chunk_masks():
    r = lax.broadcasted_iota(jnp.int32, (CHUNK, CHUNK), 0)
    c = lax.broadcasted_iota(jnp.int32, (CHUNK, CHUNK), 1)
    causal = (r // SUB == c // SUB) & (c <= r)
    tri = jnp.where(causal, 1.0, 0.0).astype(BF16)
    first_cols = lax.broadcasted_iota(jnp.int32, (SUB, CHUNK), 1) < SUB
    return causal, first_cols, jnp.concatenate([tri, tri], axis=1)


def _chunk_decay(lf, tri2):
    hi = lf.astype(BF16)
    lo = (lf - hi.astype(F32)).astype(BF16)
    return _dot(tri2, jnp.concatenate([hi, lo], axis=0))


def _chunk_key_operands(kk, gl):
    t0 = gl[SUB - 1:SUB, :]
    t1 = gl[2 * SUB - 1:2 * SUB, :]
    k_mid = kk * jnp.exp2(_rows_from_blocks([0.5 * t0, 0.5 * t1]) - gl)
    k_state = k_mid * _rows_from_blocks([jnp.exp2(0.5 * t0 + t1), jnp.exp2(0.5 * t1)])
    return k_mid.astype(BF16), k_state.astype(BF16), jnp.exp2(t0 + t1), jnp.minimum(t0, t1)


def _chunk_query_operands(qf, gl):
    t0 = gl[SUB - 1:SUB, :]
    t1 = gl[2 * SUB - 1:2 * SUB, :]
    q_mid = qf * jnp.exp2(gl - _rows_from_blocks([0.5 * t0, 0.5 * t1]))
    q_next = q_mid[SUB:] * jnp.exp2(0.5 * (t0 + t1))
    q_state = q_mid * _rows_from_blocks([jnp.exp2(0.5 * t0), jnp.exp2(0.5 * t1 + t0)])
    return jnp.concatenate([q_mid, q_next], axis=0).astype(BF16), q_state.astype(BF16)


def _chunk_finish(ast, qe, kr, vb, ptot, s_old, causal, first_cols):
    same = jnp.where(causal, ast[0:CHUNK], 0.0)
    a = jnp.concatenate([same[:SUB], jnp.where(first_cols, ast[CHUNK:], same[SUB:])], axis=0)
    o = _dot(jnp.concatenate([qe, a.astype(BF16)], axis=1),
             jnp.concatenate([s_old.astype(BF16), vb], axis=0))
    dec = jnp.broadcast_to(ptot, (HG_DK, HG_DK)).T
    return o, s_old * dec + _dot_tn(kr, vb)


def _exact_recurrence(q_ref, k_ref, f_ref, v_ref, o_ref, s0, n_rows):
    def col(tile, r):
        return jnp.broadcast_to(tile[r:r + 1, :], (HG_DK, HG_DK)).T

    def step(i, s):
        rows = pl.ds(pl.multiple_of(i * SUBLANE, SUBLANE), SUBLANE)
        q, k, f, v = q_ref[rows, :], k_ref[rows, :], f_ref[rows, :], v_ref[rows, :]
        out = []
        for r in range(SUBLANE):
            s = s * col(f, r) + col(k, r) * v[r:r + 1, :]
            out.append(jnp.sum(col(q, r) * s, axis=0, keepdims=True))
        o_ref[rows, :] = jnp.concatenate(out, axis=0)
        return s
    return lax.fori_loop(0, n_rows // SUBLANE, step, s0)


def _hgrn_prompt_kernel(x_ref, mod_ref, g_ref, win_ref, lbraw_ref, og_ref, wout_ref, fg_ref,
                        xo_ref, s_ref, q_scr, lf_scr, k_scr, v_scr, z_scr, o_scr, s0_scr,
                        *, tb, gb, layer, final):
    b = pl.program_id(0)
    t = pl.program_id(1)

    @pl.when(t == 0)
    def _():
        s_ref[...] = jnp.zeros_like(s_ref)

    s0_scr[...] = s_ref[...]

    seqs = range(gb)
    lb = _lower_bound(lbraw_ref[...], layer)
    x = [x_ref[g] for g in seqs]
    mod = [_split_mod(mod_ref[pl.ds(b * gb + g, 1), :]) for g in seqs]
    h = [_mod_norm_seq(x[g], g_ref[...], mod[g][0], mod[g][1]) for g in seqs]
    for g in seqs:
        q_scr[g] = _silu(_dot(h[g], win_ref[:, 0:HG_F]))
    for g in seqs:
        logf, kk = _gate_math(_dot(h[g], win_ref[:, HG_F:2 * HG_F]), lb)
        lf_scr[g] = logf
        k_scr[g] = kk
    for g in seqs:
        v_scr[g] = _dot(h[g], win_ref[:, 2 * HG_F:2 * HG_F + HG_I])
    for g in seqs:
        z_scr[g] = _dot(h[g], win_ref[:, 2 * HG_F + HG_I:])

    causal, first_cols, tri2 = _chunk_masks()
    chains = [(c, hd, g) for c in range(tb // CHUNK) for hd in range(HG_HEADS) for g in seqs]
    ops, scores = {}, {}
    span = jnp.zeros((1, HG_DK), F32)

    def where(n):
        c, hd, g = chains[n]
        return g, hd, slice(c * CHUNK, (c + 1) * CHUNK), slice(hd * HG_DK, (hd + 1) * HG_DK)

    for step in range(len(chains) + 2 * HGRN_PIPE_SKEW):
        n = step
        if n < len(chains):
            g, hd, rows, hs = where(n)
            gl = _chunk_decay(lf_scr[g, rows, hs], tri2)
            km, kr, ptot, tmin = _chunk_key_operands(k_scr[g, rows, hs], gl)
            lhs, qe = _chunk_query_operands(q_scr[g, rows, hs], gl)
            ops[n] = (lhs, km, qe, kr, v_scr[g, rows, hs].astype(BF16), ptot)
            span = jnp.minimum(span, tmin)
        n = step - HGRN_PIPE_SKEW
        if 0 <= n < len(chains):
            scores[n] = _dot_nt(ops[n][0], ops[n][1])
        n = step - 2 * HGRN_PIPE_SKEW
        if 0 <= n < len(chains):
            g, hd, rows, hs = where(n)
            _, _, qe, kr, vb, ptot = ops.pop(n)
            o, s_new = _chunk_finish(scores.pop(n), qe, kr, vb, ptot, s_ref[g, hd], causal, first_cols)
            o_scr[g, rows, hs] = o
            s_ref[g, hd] = s_new

    y = [_head_norm_gate(o_scr[g], z_scr[g], og_ref[...]) for g in seqs]
    for g in seqs:
        xn = x[g] + mod[g][2] * _dot(y[g], wout_ref[...])
        if final:
            xn = _rmsnorm(xn, fg_ref[...])
        xo_ref[g] = xn

    @pl.when(jnp.min(span) < -HGRN_SAFE_LOG2_SPAN)
    def _():
        def exact(q_tmp, k_tmp, f_tmp, v_tmp, o_tmp):
            def one_seq(g, carry):
                shift, scale, gate = _split_mod(mod_ref[pl.ds(b * gb + g, 1), :])
                hg = _mod_norm_seq(x_ref[g], g_ref[...], shift, scale)
                qf = _silu(_dot(hg, win_ref[:, 0:HG_F]))
                logf, kk = _gate_math(_dot(hg, win_ref[:, HG_F:2 * HG_F]), lb)
                ff = jnp.exp2(logf)
                vv = _dot(hg, win_ref[:, 2 * HG_F:2 * HG_F + HG_I])
                for hd in range(HG_HEADS):
                    hs = slice(hd * HG_DK, (hd + 1) * HG_DK)
                    q_tmp[hd], k_tmp[hd], f_tmp[hd], v_tmp[hd] = qf[:, hs], kk[:, hs], ff[:, hs], vv[:, hs]

                def one_head(hd, c2):
                    s_ref[g, hd] = _exact_recurrence(q_tmp.at[hd], k_tmp.at[hd], f_tmp.at[hd], v_tmp.at[hd],
                                                     o_tmp.at[hd], s0_scr[g, hd], tb)
                    return c2

                lax.fori_loop(0, HG_HEADS, one_head, 0)
                o = jnp.concatenate([o_tmp[hd] for hd in range(HG_HEADS)], axis=1)
                y = _head_norm_gate(o, z_scr[g], og_ref[...])
                xn = x_ref[g] + gate * _dot(y, wout_ref[...])
                if final:
                    xn = _rmsnorm(xn, fg_ref[...])
                xo_ref[g] = xn
                return carry

            lax.fori_loop(0, gb, one_seq, 0)

        pl.run_scoped(exact, *[pltpu.VMEM((HG_HEADS, tb, HG_DK), F32)] * 5)


def _hgrn_prompt_call(x, mods, norm_g, w_in, lb_raw, og, w_out, fg, layer, mod_block):
    bsz, seq, _ = x.shape
    tb = HGRN_TB
    gb = HGRN_GB
    j = layer // 2
    final = layer == DEPTH - 1
    in_specs = [
        pl.BlockSpec((gb, tb, D_MODEL), lambda b, t: (b, t, 0)),
        _resident((None, bsz, 3 * D_MODEL), (layer, mod_block, 0), 2),
        _resident((None, 1, D_MODEL), (layer, 0, 0), 2),
        _resident((None, D_MODEL, 2 * HG_F + 2 * HG_I), (j, 0, 0), 2),
        _resident((DEPTH, HG_F), (0, 0), 2),
        _resident((None, 1, HG_I), (j, 0, 0), 2),
        _resident((None, HG_I, D_MODEL), (j, 0, 0), 2),
        _resident((1, D_MODEL), (0, 0), 2),
    ]
    return pl.pallas_call(
        functools.partial(_hgrn_prompt_kernel, tb=tb, gb=gb, layer=layer, final=final),
        grid=(bsz // gb, seq // tb),
        in_specs=in_specs,
        out_specs=[
            pl.BlockSpec((gb, tb, D_MODEL), lambda b, t: (b, t, 0)),
            pl.BlockSpec((gb, HG_HEADS, HG_DK, HG_DV), lambda b, t: (b, 0, 0, 0)),
        ],
        out_shape=[
            jax.ShapeDtypeStruct((bsz, seq, D_MODEL), F32),
            jax.ShapeDtypeStruct((bsz, HG_HEADS, HG_DK, HG_DV), F32),
        ],
        scratch_shapes=[pltpu.VMEM((gb, tb, HG_F), F32)] * 6
        + [pltpu.VMEM((gb, HG_HEADS, HG_DK, HG_DV), F32)],
        compiler_params=_params(2),
        name="hgrn_prompt",
    )(x, mods, norm_g, w_in, lb_raw, og, w_out, fg)


def _hgrn_sample_pre_kernel(x_ref, mod_ref, g_ref, win_ref, lbraw_ref,
                            qe_ref, kr_ref, v_ref, el_ref, oi_ref, z_ref, *, n_tok, layer):
    ns = x_ref.shape[0]
    h = _sample_hidden(x_ref, mod_ref, g_ref, n_tok)
    lb = _lower_bound(lbraw_ref[...], layer)
    qf = _silu(_dot(h, win_ref[:, 0:HG_F]))
    lf, kk = _gate_math(_dot(h, win_ref[:, HG_F:2 * HG_F]), lb)
    vv = _dot(h, win_ref[:, 2 * HG_F:2 * HG_F + HG_I])
    z = _dot(h, win_ref[:, 2 * HG_F + HG_I:])

    def tok(a, s):
        return a[s * ns:(s + 1) * ns]

    g = [tok(lf, 0)]
    for s in range(1, n_tok):
        g.append(g[-1] + tok(lf, s))
    el_ref[...] = jnp.exp2(g[-1])
    for s in range(n_tok):
        qe_ref[:, _tok(s, HG_F)] = tok(qf, s) * jnp.exp2(g[s])
        kr_ref[:, _tok(s, HG_F)] = tok(kk, s) * jnp.exp2(g[-1] - g[s])
        v_ref[:, _tok(s, HG_I)] = tok(vv, s)
        z_ref[:, _tok(s, HG_I)] = tok(z, s)

    for t in range(n_tok):
        acc = jnp.zeros((ns, HG_I), F32)
        for s in range(t + 1):
            e = tok(qf, t) * tok(kk, s)
            if s < t:
                e = e * jnp.exp2(g[t] - g[s])
            vs = tok(vv, s)
            parts = []
            for hd in range(HG_HEADS):
                hs = slice(hd * HG_DK, (hd + 1) * HG_DK)
                parts.append(jnp.sum(e[:, hs], axis=-1, keepdims=True) * vs[:, hs])
            acc = acc + jnp.concatenate(parts, axis=1)
        oi_ref[:, _tok(t, HG_I)] = acc


def _hgrn_sample_pre_call(x, mods, norm_g, w_in, lb_raw, layer, n_tok):
    n_seq = x.shape[0]
    j = layer // 2
    act = jax.ShapeDtypeStruct((n_seq, n_tok * HG_F), F32)
    return pl.pallas_call(
        functools.partial(_hgrn_sample_pre_kernel, n_tok=n_tok, layer=layer),
        grid=(n_seq // SAMPLE_SEQ_BLOCK,),
        in_specs=[
            _seq_rows(n_tok * D_MODEL),
            _seq_rows(3 * D_MODEL, layer),
            _resident((None, 1, D_MODEL), (layer, 0, 0), 1),
            _resident((None, D_MODEL, 2 * HG_F + 2 * HG_I), (j, 0, 0), 1),
            _resident((DEPTH, HG_F), (0, 0), 1),
        ],
        out_specs=[_seq_rows(n_tok * HG_F)] * 3 + [_seq_rows(HG_F)] + [_seq_rows(n_tok * HG_I)] * 2,
        out_shape=[act, act, act, jax.ShapeDtypeStruct((n_seq, HG_F), F32), act, act],
        compiler_params=_params(1),
        name="hgrn_sample_pre",
    )(x, mods, norm_g, w_in, lb_raw)


def _hgrn_sample_state_kernel(*refs, n_tok, has_prev):
    n_in = 5 + int(has_prev)
    qe_ref, kr_ref, v_ref, el_ref, s_ref = refs[:5]
    oo_ref, so_ref = refs[n_in:n_in + 2]
    ns = SAMPLE_STATE_BLOCK
    seq_of_row = lax.broadcasted_iota(jnp.int32, (n_tok * ns, HG_DK), 0) % ns
    for hd in range(HG_HEADS):
        hs = slice(hd * HG_DK, (hd + 1) * HG_DK)
        q = _stack_tokens(qe_ref, n_tok, HG_F, hs).astype(BF16)
        k = _stack_tokens(kr_ref, n_tok, HG_F, hs)
        v = _stack_tokens(v_ref, n_tok, HG_I, hs).astype(BF16)
        el = el_ref[:, hs]
        o = jnp.zeros((n_tok * ns, HG_DV), F32)
        for b in range(ns):
            mine = seq_of_row == b
            s_old = s_ref[b, hd]
            o = jnp.where(mine, _dot(q, s_old.astype(BF16)), o)
            kb = jnp.where(mine, k, 0.0).astype(BF16)
            dec = jnp.broadcast_to(el[b:b + 1, :], (HG_DK, HG_DK)).T
            so_ref[b, hd] = s_old * dec + _dot_tn(kb, v)
        for t in range(n_tok):
            oo_ref[:, t * HG_I + hs.start:t * HG_I + hs.stop] = o[t * ns:(t + 1) * ns]


def _hgrn_sample_state_call(qe, kr, v, el, states, prev_states, layer, n_tok):
    n_seq = qe.shape[0]
    j = layer // 2
    nb = SAMPLE_STATE_BLOCK
    rows = lambda width: pl.BlockSpec((nb, width), lambda i: (i, 0))
    st_spec = pl.BlockSpec((None, nb, HG_HEADS, HG_DK, HG_DV), lambda i: (j, i, 0, 0, 0))
    in_specs = [rows(n_tok * HG_F), rows(n_tok * HG_F), rows(n_tok * HG_I), rows(HG_F), st_spec]
    args = [qe, kr, v, el, states]
    aliases = {}
    if prev_states is not None:
        in_specs.append(pl.BlockSpec(memory_space=pl.ANY))
        args.append(prev_states)
        aliases = {len(args) - 1: 1}
    return pl.pallas_call(
        functools.partial(_hgrn_sample_state_kernel, n_tok=n_tok, has_prev=prev_states is not None),
        grid=(n_seq // nb,),
        in_specs=in_specs,
        out_specs=[rows(n_tok * HG_I), st_spec],
        out_shape=[
            jax.ShapeDtypeStruct((n_seq, n_tok * HG_I), F32),
            jax.ShapeDtypeStruct(states.shape, F32),
        ],
        input_output_aliases=aliases,
        compiler_params=_params(1),
        name="hgrn_sample_state",
    )(*args)


def _hgrn_sample_post_kernel(x_ref, mod_ref, oa_ref, ob_ref, z_ref, og_ref, wout_ref, fg_ref, xo_ref,
                             *, n_tok, final):
    ns = x_ref.shape[0]
    o = _stack_tokens(oa_ref, n_tok, HG_I) + _stack_tokens(ob_ref, n_tok, HG_I)
    y = _head_norm_gate(o, _stack_tokens(z_ref, n_tok, HG_I), og_ref[...])
    out = _dot(y, wout_ref[...])
    gate = mod_ref[:, 2 * D_MODEL:]
    for s in range(n_tok):
        xn = x_ref[:, _tok(s, D_MODEL)] + gate * out[s * ns:(s + 1) * ns]
        if final:
            xn = _rmsnorm(xn, fg_ref[...])
        xo_ref[:, _tok(s, D_MODEL)] = xn


def _hgrn_sample_post_call(x, mods, o_inter, o_intra, z, og, w_out, fg, layer, n_tok):
    n_seq = x.shape[0]
    j = layer // 2
    return pl.pallas_call(
        functools.partial(_hgrn_sample_post_kernel, n_tok=n_tok, final=layer == DEPTH - 1),
        grid=(n_seq // SAMPLE_SEQ_BLOCK,),
        in_specs=[
            _seq_rows(n_tok * D_MODEL),
            _seq_rows(3 * D_MODEL, layer),
            _seq_rows(n_tok * HG_I), _seq_rows(n_tok * HG_I), _seq_rows(n_tok * HG_I),
            _resident((None, 1, HG_I), (j, 0, 0), 1),
            _resident((None, HG_I, D_MODEL), (j, 0, 0), 1),
            _resident((1, D_MODEL), (0, 0), 1),
        ],
        out_specs=_seq_rows(n_tok * D_MODEL),
        out_shape=jax.ShapeDtypeStruct((n_seq, n_tok * D_MODEL), F32),
        compiler_params=_params(1),
        name="hgrn_sample_post",
    )(x, mods, o_inter, o_intra, z, og, w_out, fg)


def kernel(x_prompt, x_sample, state_conv, state_hgrn, c_prompt, c_sample, norm_g, w_ada, b_ada, conv_w_in, conv_w, conv_w_out, hgrn_w_in, hgrn_lower_bounds, hgrn_onorm_g, hgrn_w_out, final_norm_g):
    n_p = x_prompt.shape[0]
    n_s, t_s, _ = x_sample.shape
    assert t_s >= CONV_WIDTH - 1 and n_s % SAMPLE_SEQ_BLOCK == 0 and n_s % n_p == 0

    mods = _ada_call(jnp.concatenate([c_sample, c_prompt], axis=0), w_ada, b_ada)
    mod_block_p = n_s // n_p

    conv_w_in_b = conv_w_in.astype(BF16)
    conv_w_out_b = conv_w_out.astype(BF16)
    hgrn_w_in_b = hgrn_w_in.astype(BF16)
    hgrn_w_out_b = hgrn_w_out.astype(BF16)
    norm_g3 = norm_g.reshape(DEPTH, 1, D_MODEL)
    og3 = hgrn_onorm_g.reshape(DEPTH // 2, 1, HG_I)
    fg = final_norm_g.reshape(1, D_MODEL)
    state_conv2 = state_conv.reshape(state_conv.shape[0], n_s, (CONV_WIDTH - 1) * E_CONV)

    xp = x_prompt
    xs = x_sample.reshape(n_s, t_s * D_MODEL)
    conv_p, conv_s, hgrn_p = [], [], []
    hgrn_s = None
    for layer in range(DEPTH):
        if layer % 2 == 0:
            xp, tail = _conv_prompt_call(xp, mods, norm_g3, conv_w_in_b, conv_w, conv_w_out_b, layer, mod_block_p)
            conv_p.append(tail[:, SUBLANE - (CONV_WIDTH - 1):])
            xs, st = _conv_sample_call(xs, mods, norm_g3, conv_w_in_b, conv_w, conv_w_out_b, state_conv2,
                                       layer, t_s)
            conv_s.append(st.reshape(n_s, CONV_WIDTH - 1, E_CONV))
        else:
            xp, s_new = _hgrn_prompt_call(xp, mods, norm_g3, hgrn_w_in_b, hgrn_lower_bounds, og3,
                                          hgrn_w_out_b, fg, layer, mod_block_p)
            hgrn_p.append(s_new)
            qe, kr, v, el, o_intra, z = _hgrn_sample_pre_call(xs, mods, norm_g3, hgrn_w_in_b,
                                                              hgrn_lower_bounds, layer, t_s)
            o_inter, hgrn_s = _hgrn_sample_state_call(qe, kr, v, el, state_hgrn, hgrn_s, layer, t_s)
            xs = _hgrn_sample_post_call(xs, mods, o_inter, o_intra, z, og3, hgrn_w_out_b, fg, layer, t_s)
    return (xp, xs.reshape(n_s, t_s, D_MODEL), jnp.stack(conv_p), jnp.stack(hgrn_p), jnp.stack(conv_s), hgrn_s)
```

```python
import functools

import jax
import jax.numpy as jnp
from jax import lax
from jax.experimental import pallas as pl
from jax.experimental.pallas import tpu as pltpu

F32 = jnp.float32
BF16 = jnp.bfloat16

D_MODEL = 1024
DEPTH = 4
CONV_WIDTH = 3
E_CONV = D_MODEL
HG_HEADS = 8
HG_DK = 128
HG_DV = 128
HG_F = HG_HEADS * HG_DK
HG_I = HG_HEADS * HG_DV
EPS = 1e-6

SUBLANE = 8
CHUNK = 128
SUB = CHUNK // 2
CONV_COL_TILE = 256
CONV_TB = 512
CONV_GB = 2
HGRN_TB = 256
HGRN_GB = 2
HGRN_PIPE_SKEW = 3
HGRN_SAFE_LOG2_SPAN = 200.0
SAMPLE_SEQ_BLOCK = 32
SAMPLE_STATE_BLOCK = 8
VMEM_LIMIT = 56 * 1024 * 1024


def _dot(a, b):
    return jnp.dot(a, b, preferred_element_type=F32)


def _dot_nt(a, b):
    return lax.dot_general(a, b, (((1,), (1,)), ((), ())), preferred_element_type=F32)


def _dot_tn(a, b):
    return lax.dot_general(a, b, (((0,), (0,)), ((), ())), preferred_element_type=F32)


def _silu(x):
    return x * jax.nn.sigmoid(x)


def _rmsnorm(x, g):
    ms = jnp.mean(x * x, axis=-1, keepdims=True)
    return x * lax.rsqrt(ms + EPS) * g


def _split_mod(mod):
    return mod[:, :D_MODEL], mod[:, D_MODEL:2 * D_MODEL], mod[:, 2 * D_MODEL:]


def _mod_norm(x, g, shift, scale):
    return (_rmsnorm(x, g) * (1.0 + scale) + shift).astype(BF16)


def _mod_norm_seq(x, g, shift, scale):
    ms = jnp.mean(x * x, axis=-1, keepdims=True)
    return (x * lax.rsqrt(ms + EPS) * (g * (1.0 + scale)) + shift).astype(BF16)


def _lower_bound(raw, layer):
    rows = [raw[i:i + 1, :] for i in range(DEPTH)]
    m = functools.reduce(jnp.maximum, rows)
    es = [jnp.exp(r - m) for r in rows]
    tot = functools.reduce(lambda a, b: a + b, es)
    acc = es[1]
    for i in range(2, layer + 1):
        acc = acc + es[i]
    return acc / tot


def _gate_math(fpre, lb):
    t = jnp.exp(-jnp.abs(fpre))
    r = 1.0 / (1.0 + t)
    pos = fpre >= 0.0
    sig = jnp.where(pos, 1.0, t) * r
    sig_neg = jnp.where(pos, t, 1.0) * r
    return jnp.log2(lb + (1.0 - lb) * sig), (1.0 - lb) * sig_neg


def _head_norm_gate(o, z, og):
    parts = []
    for hd in range(HG_HEADS):
        hs = slice(hd * HG_DV, (hd + 1) * HG_DV)
        oh = o[:, hs]
        ms = jnp.mean(oh * oh, axis=-1, keepdims=True)
        parts.append(oh * lax.rsqrt(ms + EPS))
    on = jnp.concatenate(parts, axis=1) * og
    return (on * _silu(z)).astype(BF16)


def _params(n_grid):
    return pltpu.CompilerParams(dimension_semantics=("arbitrary",) * n_grid, vmem_limit_bytes=VMEM_LIMIT)


def _resident(shape, index, n_grid):
    if n_grid == 1:
        return pl.BlockSpec(shape, lambda i: index, pipeline_mode=pl.Buffered(1))
    return pl.BlockSpec(shape, lambda b, t: index, pipeline_mode=pl.Buffered(1))


def _ada_kernel(c_ref, w_ref, b_ref, o_ref):
    s = _silu(c_ref[...]).astype(BF16)
    o_ref[...] = _dot(s, w_ref[...].astype(BF16)) + b_ref[...]


def _ada_call(c_all, w_ada, b_ada):
    rows = c_all.shape[0]
    n_col = 3
    col = 3 * D_MODEL // n_col
    return pl.pallas_call(
        _ada_kernel,
        grid=(DEPTH, n_col),
        in_specs=[
            pl.BlockSpec((rows, D_MODEL), lambda l, n: (0, 0)),
            pl.BlockSpec((None, D_MODEL, col), lambda l, n: (l, 0, n)),
            pl.BlockSpec((None, 1, col), lambda l, n: (l, 0, n)),
        ],
        out_specs=pl.BlockSpec((None, rows, col), lambda l, n: (l, 0, n)),
        out_shape=jax.ShapeDtypeStruct((DEPTH, rows, 3 * D_MODEL), F32),
        compiler_params=_params(2),
        name="ada_mod",
    )(c_all, w_ada, b_ada.reshape(DEPTH, 1, 3 * D_MODEL))


def _conv_columns(h, win_ref, cw_ref, y_scr, prev_fn, tail_fn):
    ct = CONV_COL_TILE
    for n in range(E_CONV // ct):
        def col(j):
            return slice(j * E_CONV + n * ct, j * E_CONV + (n + 1) * ct)
        cs = slice(n * ct, (n + 1) * ct)
        b_gate = _dot(h, win_ref[:, col(0)])
        c_gate = _dot(h, win_ref[:, col(1)])
        v = _dot(h, win_ref[:, col(2)])
        z = _dot(h, win_ref[:, col(3)])
        u = c_gate * v
        p1, p2 = prev_fn(cs, u)
        conv = cw_ref[0:1, cs] * p2 + cw_ref[1:2, cs] * p1 + cw_ref[2:3, cs] * u
        y_scr[:, cs] = (b_gate * conv * _silu(z)).astype(BF16)
        tail_fn(cs, u)


def _conv_prompt_kernel(x_ref, mod_ref, g_ref, win_ref, cw_ref, wout_ref, xo_ref, st_ref, y_scr, *, tb, gb):
    b = pl.program_id(0)
    t = pl.program_id(1)
    ct = CONV_COL_TILE

    @pl.when(t == 0)
    def _():
        st_ref[...] = jnp.zeros_like(st_ref)

    seqs = range(gb)
    x = [x_ref[g] for g in seqs]
    mod = [_split_mod(mod_ref[pl.ds(b * gb + g, 1), :]) for g in seqs]
    h = [_mod_norm_seq(x[g], g_ref[...], mod[g][0], mod[g][1]) for g in seqs]
    row = lax.broadcasted_iota(jnp.int32, (tb, ct), 0)

    for g in seqs:
        def prev_fn(cs, u, g=g):
            c0 = st_ref[g, SUBLANE - 2:SUBLANE - 1, cs]
            c1 = st_ref[g, SUBLANE - 1:SUBLANE, cs]
            p1 = jnp.where(row == 0, c1, pltpu.roll(u, 1, 0))
            p2 = jnp.where(row == 0, c0, jnp.where(row == 1, c1, pltpu.roll(u, 2, 0)))
            return p1, p2

        def tail_fn(cs, u, g=g):
            st_ref[g, :, cs] = u[tb - SUBLANE:tb, :]

        _conv_columns(h[g], win_ref, cw_ref, y_scr.at[g], prev_fn, tail_fn)
    for g in seqs:
        xo_ref[g] = x[g] + mod[g][2] * _dot(y_scr[g], wout_ref[...])


def _conv_prompt_call(x, mods, norm_g, w_in, cw, w_out, layer, mod_block):
    bsz, seq, _ = x.shape
    tb = CONV_TB
    gb = CONV_GB
    j = layer // 2
    return pl.pallas_call(
        functools.partial(_conv_prompt_kernel, tb=tb, gb=gb),
        grid=(bsz // gb, seq // tb),
        in_specs=[
            pl.BlockSpec((gb, tb, D_MODEL), lambda b, t: (b, t, 0)),
            _resident((None, bsz, 3 * D_MODEL), (layer, mod_block, 0), 2),
            _resident((None, 1, D_MODEL), (layer, 0, 0), 2),
            _resident((None, D_MODEL, 4 * E_CONV), (j, 0, 0), 2),
            _resident((None, CONV_WIDTH, E_CONV), (j, 0, 0), 2),
            _resident((None, E_CONV, D_MODEL), (j, 0, 0), 2),
        ],
        out_specs=[
            pl.BlockSpec((gb, tb, D_MODEL), lambda b, t: (b, t, 0)),
            pl.BlockSpec((gb, SUBLANE, E_CONV), lambda b, t: (b, 0, 0)),
        ],
        out_shape=[
            jax.ShapeDtypeStruct((bsz, seq, D_MODEL), F32),
            jax.ShapeDtypeStruct((bsz, SUBLANE, E_CONV), F32),
        ],
        scratch_shapes=[pltpu.VMEM((gb, tb, E_CONV), BF16)],
        compiler_params=_params(2),
        name="conv_prompt",
    )(x, mods, norm_g, w_in, cw, w_out)


def _tok(s, width):
    return slice(s * width, (s + 1) * width)


def _stack_tokens(ref, n_tok, width, cols=None):
    parts = []
    for s in range(n_tok):
        lo = s * width + (0 if cols is None else cols.start)
        hi = s * width + (width if cols is None else cols.stop)
        parts.append(ref[:, lo:hi])
    return jnp.concatenate(parts, axis=0)


def _sample_hidden(x_ref, mod_ref, g_ref, n_tok):
    shift, scale, _ = _split_mod(mod_ref[...])
    g = g_ref[...]
    return jnp.concatenate(
        [_mod_norm(x_ref[:, _tok(s, D_MODEL)], g, shift, scale) for s in range(n_tok)], axis=0)


def _conv_sample_kernel(x_ref, mod_ref, g_ref, win_ref, cw_ref, wout_ref, st_ref,
                        xo_ref, so_ref, y_scr, *, n_tok):
    ns = x_ref.shape[0]
    h = _sample_hidden(x_ref, mod_ref, g_ref, n_tok)

    def prev_fn(cs, u):
        st0 = st_ref[:, cs]
        st1 = st_ref[:, E_CONV + cs.start:E_CONV + cs.stop]
        p1 = jnp.concatenate([st1, u[:(n_tok - 1) * ns]], axis=0)
        p2 = jnp.concatenate([st0, st1, u[:(n_tok - 2) * ns]], axis=0)
        return p1, p2

    def tail_fn(cs, u):
        so_ref[:, cs] = u[(n_tok - 2) * ns:(n_tok - 1) * ns]
        so_ref[:, E_CONV + cs.start:E_CONV + cs.stop] = u[(n_tok - 1) * ns:]

    _conv_columns(h, win_ref, cw_ref, y_scr, prev_fn, tail_fn)
    out = _dot(y_scr[...], wout_ref[...])
    gate = mod_ref[:, 2 * D_MODEL:]
    for s in range(n_tok):
        xo_ref[:, _tok(s, D_MODEL)] = x_ref[:, _tok(s, D_MODEL)] + gate * out[s * ns:(s + 1) * ns]


def _seq_rows(width, layer=None):
    if layer is None:
        return pl.BlockSpec((SAMPLE_SEQ_BLOCK, width), lambda i: (i, 0))
    return pl.BlockSpec((None, SAMPLE_SEQ_BLOCK, width), lambda i: (layer, i, 0))


def _conv_sample_call(x, mods, norm_g, w_in, cw, w_out, state, layer, n_tok):
    n_seq = x.shape[0]
    j = layer // 2
    return pl.pallas_call(
        functools.partial(_conv_sample_kernel, n_tok=n_tok),
        grid=(n_seq // SAMPLE_SEQ_BLOCK,),
        in_specs=[
            _seq_rows(n_tok * D_MODEL),
            _seq_rows(3 * D_MODEL, layer),
            _resident((None, 1, D_MODEL), (layer, 0, 0), 1),
            _resident((None, D_MODEL, 4 * E_CONV), (j, 0, 0), 1),
            _resident((None, CONV_WIDTH, E_CONV), (j, 0, 0), 1),
            _resident((None, E_CONV, D_MODEL), (j, 0, 0), 1),
            _seq_rows((CONV_WIDTH - 1) * E_CONV, j),
        ],
        out_specs=[_seq_rows(n_tok * D_MODEL), _seq_rows((CONV_WIDTH - 1) * E_CONV)],
        out_shape=[
            jax.ShapeDtypeStruct((n_seq, n_tok * D_MODEL), F32),
            jax.ShapeDtypeStruct((n_seq, (CONV_WIDTH - 1) * E_CONV), F32),
        ],
        scratch_shapes=[pltpu.VMEM((n_tok * SAMPLE_SEQ_BLOCK, E_CONV), BF16)],
        compiler_params=_params(1),
        name="conv_sample",
    )(x, mods, norm_g, w_in, cw, w_out, state)


def _rows_from_blocks(vecs):
    return jnp.concatenate([jnp.broadcast_to(v, (SUB, v.shape[1])) for v in vecs], axis=0)


def _chunk_masks():
    r = lax.broadcasted_iota(jnp.int32, (CHUNK, CHUNK), 0)
    c = lax.broadcasted_iota(jnp.int32, (CHUNK, CHUNK), 1)
    causal = (r // SUB == c // SUB) & (c <= r)
    tri = jnp.where(causal, 1.0, 0.0).astype(BF16)
    first_cols = lax.broadcasted_iota(jnp.int32, (SUB, CHUNK), 1) < SUB
    return causal, first_cols, jnp.concatenate([tri, tri], axis=1)


def _chunk_decay(lf, tri2):
    hi = lf.astype(BF16)
    lo = (lf - hi.astype(F32)).astype(BF16)
    return _dot(tri2, jnp.concatenate([hi, lo], axis=0))


def _chunk_key_operands(kk, gl):
    t0 = gl[SUB - 1:SUB, :]
    t1 = gl[2 * SUB - 1:2 * SUB, :]
    k_mid = kk * jnp.exp2(_rows_from_blocks([0.5 * t0, 0.5 * t1]) - gl)
    k_state = k_mid * _rows_from_blocks([jnp.exp2(0.5 * t0 + t1), jnp.exp2(0.5 * t1)])
    return k_mid.astype(BF16), k_state.astype(BF16), jnp.exp2(t0 + t1), jnp.minimum(t0, t1)


def _chunk_query_operands(qf, gl):
    t0 = gl[SUB - 1:SUB, :]
    t1 = gl[2 * SUB - 1:2 * SUB, :]
    q_mid = qf * jnp.exp2(gl - _rows_from_blocks([0.5 * t0, 0.5 * t1]))
    q_next = q_mid[SUB:] * jnp.exp2(0.5 * (t0 + t1))
    q_state = q_mid * _rows_from_blocks([jnp.exp2(0.5 * t0), jnp.exp2(0.5 * t1 + t0)])
    return jnp.concatenate([q_mid, q_next], axis=0).astype(BF16), q_state.astype(BF16)


def _chunk_finish(ast, qe, kr, vb, ptot, s_old, causal, first_cols):
    same = jnp.where(causal, ast[0:CHUNK], 0.0)
    a = jnp.concatenate([same[:SUB], jnp.where(first_cols, ast[CHUNK:], same[SUB:])], axis=0)
    o = _dot(jnp.concatenate([qe, a.astype(BF16)], axis=1),
             jnp.concatenate([s_old.astype(BF16), vb], axis=0))
    dec = jnp.broadcast_to(ptot, (HG_DK, HG_DK)).T
    return o, s_old * dec + _dot_tn(kr, vb)


def _exact_recurrence(q_ref, k_ref, f_ref, v_ref, o_ref, s0, n_rows):
    def col(tile, r):
        return jnp.broadcast_to(tile[r:r + 1, :], (HG_DK, HG_DK)).T

    def step(i, s):
        rows = pl.ds(pl.multiple_of(i * SUBLANE, SUBLANE), SUBLANE)
        q, k, f, v = q_ref[rows, :], k_ref[rows, :], f_ref[rows, :], v_ref[rows, :]
        out = []
        for r in range(SUBLANE):
            s = s * col(f, r) + col(k, r) * v[r:r + 1, :]
            out.append(jnp.sum(col(q, r) * s, axis=0, keepdims=True))
        o_ref[rows, :] = jnp.concatenate(out, axis=0)
        return s
    return lax.fori_loop(0, n_rows // SUBLANE, step, s0)


def _decode_state_update(p, dq_ref, dk_ref, dv_ref, del_ref, ds_ref, do_ref, dso_ref, partial_o):
    nd = del_ref.shape[0]
    hd, sq = divmod(p, nd)
    hs = slice(hd * HG_DK, (hd + 1) * HG_DK)
    mine = lax.broadcasted_iota(jnp.int32, (dq_ref.shape[0], HG_DK), 0) % nd == sq
    s_old = ds_ref[sq, hd]
    o = _dot(dq_ref[:, hs].astype(BF16), s_old.astype(BF16))
    partial_o[hd] = jnp.where(mine, o, partial_o[hd]) if hd in partial_o else jnp.where(mine, o, 0.0)
    k_mine = jnp.where(mine, dk_ref[:, hs], 0.0).astype(BF16)
    dec = jnp.broadcast_to(del_ref[sq:sq + 1, hs], (HG_DK, HG_DK)).T
    dso_ref[sq, hd] = s_old * dec + _dot_tn(k_mine, dv_ref[:, hs].astype(BF16))
    if sq == nd - 1:
        do_ref[:, hs] = partial_o.pop(hd)


def _hgrn_prompt_kernel(*refs, tb, gb, layer, final, has_prev):
    n_in = 13 + int(has_prev)
    x_ref, mod_ref, g_ref, win_ref, lbraw_ref, og_ref, wout_ref, fg_ref = refs[:8]
    decode_in = refs[8:13]
    xo_ref, s_ref, do_ref, dso_ref = refs[n_in:n_in + 4]
    q_scr, lf_scr, k_scr, v_scr, z_scr, o_scr, s0_scr = refs[n_in + 4:]
    b = pl.program_id(0)
    t = pl.program_id(1)

    @pl.when(t == 0)
    def _():
        s_ref[...] = jnp.zeros_like(s_ref)

    s0_scr[...] = s_ref[...]

    seqs = range(gb)
    lb = _lower_bound(lbraw_ref[...], layer)
    x = [x_ref[g] for g in seqs]
    mod = [_split_mod(mod_ref[pl.ds(b * gb + g, 1), :]) for g in seqs]
    h = [_mod_norm_seq(x[g], g_ref[...], mod[g][0], mod[g][1]) for g in seqs]
    for g in seqs:
        q_scr[g] = _silu(_dot(h[g], win_ref[:, 0:HG_F]))
    for g in seqs:
        logf, kk = _gate_math(_dot(h[g], win_ref[:, HG_F:2 * HG_F]), lb)
        lf_scr[g] = logf
        k_scr[g] = kk
    for g in seqs:
        v_scr[g] = _dot(h[g], win_ref[:, 2 * HG_F:2 * HG_F + HG_I])
    for g in seqs:
        z_scr[g] = _dot(h[g], win_ref[:, 2 * HG_F + HG_I:])

    causal, first_cols, tri2 = _chunk_masks()
    chains = [(c, hd, g) for c in range(tb // CHUNK) for hd in range(HG_HEADS) for g in seqs]
    ops, scores = {}, {}
    span = jnp.zeros((1, HG_DK), F32)

    def where(n):
        c, hd, g = chains[n]
        return g, hd, slice(c * CHUNK, (c + 1) * CHUNK), slice(hd * HG_DK, (hd + 1) * HG_DK)

    n_decode = HG_HEADS * decode_in[3].shape[0]
    partial_o = {}
    for step in range(max(len(chains) + 2 * HGRN_PIPE_SKEW, n_decode)):
        if step < n_decode:
            _decode_state_update(step, *decode_in, do_ref, dso_ref, partial_o)
        n = step
        if n < len(chains):
            g, hd, rows, hs = where(n)
            gl = _chunk_decay(lf_scr[g, rows, hs], tri2)
            km, kr, ptot, tmin = _chunk_key_operands(k_scr[g, rows, hs], gl)
            lhs, qe = _chunk_query_operands(q_scr[g, rows, hs], gl)
            ops[n] = (lhs, km, qe, kr, v_scr[g, rows, hs].astype(BF16), ptot)
            span = jnp.minimum(span, tmin)
        n = step - HGRN_PIPE_SKEW
        if 0 <= n < len(chains):
            scores[n] = _dot_nt(ops[n][0], ops[n][1])
        n = step - 2 * HGRN_PIPE_SKEW
        if 0 <= n < len(chains):
            g, hd, rows, hs = where(n)
            _, _, qe, kr, vb, ptot = ops.pop(n)
            o, s_new = _chunk_finish(scores.pop(n), qe, kr, vb, ptot, s_ref[g, hd], causal, first_cols)
            o_scr[g, rows, hs] = o
            s_ref[g, hd] = s_new

    y = [_head_norm_gate(o_scr[g], z_scr[g], og_ref[...]) for g in seqs]
    for g in seqs:
        xn = x[g] + mod[g][2] * _dot(y[g], wout_ref[...])
        if final:
            xn = _rmsnorm(xn, fg_ref[...])
        xo_ref[g] = xn

    @pl.when(jnp.min(span) < -HGRN_SAFE_LOG2_SPAN)
    def _():
        def exact(q_tmp, k_tmp, f_tmp, v_tmp, o_tmp):
            def one_seq(g, carry):
                shift, scale, gate = _split_mod(mod_ref[pl.ds(b * gb + g, 1), :])
                hg = _mod_norm_seq(x_ref[g], g_ref[...], shift, scale)
                qf = _silu(_dot(hg, win_ref[:, 0:HG_F]))
                logf, kk = _gate_math(_dot(hg, win_ref[:, HG_F:2 * HG_F]), lb)
                ff = jnp.exp2(logf)
                vv = _dot(hg, win_ref[:, 2 * HG_F:2 * HG_F + HG_I])
                for hd in range(HG_HEADS):
                    hs = slice(hd * HG_DK, (hd + 1) * HG_DK)
                    q_tmp[hd], k_tmp[hd], f_tmp[hd], v_tmp[hd] = qf[:, hs], kk[:, hs], ff[:, hs], vv[:, hs]

                def one_head(hd, c2):
                    s_ref[g, hd] = _exact_recurrence(q_tmp.at[hd], k_tmp.at[hd], f_tmp.at[hd], v_tmp.at[hd],
                                                     o_tmp.at[hd], s0_scr[g, hd], tb)
                    return c2

                lax.fori_loop(0, HG_HEADS, one_head, 0)
                o = jnp.concatenate([o_tmp[hd] for hd in range(HG_HEADS)], axis=1)
                y = _head_norm_gate(o, z_scr[g], og_ref[...])
                xn = x_ref[g] + gate * _dot(y, wout_ref[...])
                if final:
                    xn = _rmsnorm(xn, fg_ref[...])
                xo_ref[g] = xn
                return carry

            lax.fori_loop(0, gb, one_seq, 0)

        pl.run_scoped(exact, *[pltpu.VMEM((HG_HEADS, tb, HG_DK), F32)] * 5)


def _hgrn_prompt_call(x, mods, norm_g, w_in, lb_raw, og, w_out, fg, decode, layer, mod_block):
    bsz, seq, _ = x.shape
    tb = HGRN_TB
    gb = HGRN_GB
    j = layer // 2
    final = layer == DEPTH - 1
    n_t = seq // tb
    qe, kr, v, el, states, prev_states = decode
    n_blocks, dec_rows, _ = qe.shape
    block_seqs = el.shape[1]
    assert n_blocks == (bsz // gb) * n_t and states.shape[1] == n_blocks * block_seqs
    dec_spec = pl.BlockSpec((None, dec_rows, HG_F), lambda b, t: (b * n_t + t, 0, 0))
    st_spec = pl.BlockSpec((None, block_seqs, HG_HEADS, HG_DK, HG_DV), lambda b, t: (j, b * n_t + t, 0, 0, 0))
    in_specs = [
        pl.BlockSpec((gb, tb, D_MODEL), lambda b, t: (b, t, 0)),
        _resident((None, bsz, 3 * D_MODEL), (layer, mod_block, 0), 2),
        _resident((None, 1, D_MODEL), (layer, 0, 0), 2),
        _resident((None, D_MODEL, 2 * HG_F + 2 * HG_I), (j, 0, 0), 2),
        _resident((DEPTH, HG_F), (0, 0), 2),
        _resident((None, 1, HG_I), (j, 0, 0), 2),
        _resident((None, HG_I, D_MODEL), (j, 0, 0), 2),
        _resident((1, D_MODEL), (0, 0), 2),
        dec_spec, dec_spec, dec_spec,
        pl.BlockSpec((None, block_seqs, HG_F), lambda b, t: (b * n_t + t, 0, 0)),
        st_spec,
    ]
    args = [x, mods, norm_g, w_in, lb_raw, og, w_out, fg, qe, kr, v, el, states]
    aliases = {}
    if prev_states is not None:
        in_specs.append(pl.BlockSpec(memory_space=pl.ANY))
        args.append(prev_states)
        aliases = {len(args) - 1: 3}
    return pl.pallas_call(
        functools.partial(_hgrn_prompt_kernel, tb=tb, gb=gb, layer=layer, final=final,
                          has_prev=prev_states is not None),
        grid=(bsz // gb, n_t),
        in_specs=in_specs,
        out_specs=[
            pl.BlockSpec((gb, tb, D_MODEL), lambda b, t: (b, t, 0)),
            pl.BlockSpec((gb, HG_HEADS, HG_DK, HG_DV), lambda b, t: (b, 0, 0, 0)),
            dec_spec,
            st_spec,
        ],
        out_shape=[
            jax.ShapeDtypeStruct((bsz, seq, D_MODEL), F32),
            jax.ShapeDtypeStruct((bsz, HG_HEADS, HG_DK, HG_DV), F32),
            jax.ShapeDtypeStruct(qe.shape, F32),
            jax.ShapeDtypeStruct(states.shape, F32),
        ],
        scratch_shapes=[pltpu.VMEM((gb, tb, HG_F), F32)] * 6
        + [pltpu.VMEM((gb, HG_HEADS, HG_DK, HG_DV), F32)],
        input_output_aliases=aliases,
        compiler_params=_params(2),
        name="hgrn_prompt",
    )(*args)


def _hgrn_sample_pre_kernel(x_ref, mod_ref, g_ref, win_ref, lbraw_ref,
                            qe_ref, kr_ref, v_ref, el_ref, oi_ref, z_ref, *, n_tok, layer):
    ns = x_ref.shape[0]
    h = _sample_hidden(x_ref, mod_ref, g_ref, n_tok)
    lb = _lower_bound(lbraw_ref[...], layer)
    qf = _silu(_dot(h, win_ref[:, 0:HG_F]))
    lf, kk = _gate_math(_dot(h, win_ref[:, HG_F:2 * HG_F]), lb)
    vv = _dot(h, win_ref[:, 2 * HG_F:2 * HG_F + HG_I])
    z = _dot(h, win_ref[:, 2 * HG_F + HG_I:])

    def tok(a, s):
        return a[s * ns:(s + 1) * ns]

    g = [tok(lf, 0)]
    for s in range(1, n_tok):
        g.append(g[-1] + tok(lf, s))
    el_ref[...] = jnp.exp2(g[-1])
    for s in range(n_tok):
        qe_ref[:, _tok(s, HG_F)] = tok(qf, s) * jnp.exp2(g[s])
        kr_ref[:, _tok(s, HG_F)] = tok(kk, s) * jnp.exp2(g[-1] - g[s])
        v_ref[:, _tok(s, HG_I)] = tok(vv, s)
        z_ref[:, _tok(s, HG_I)] = tok(z, s)

    for t in range(n_tok):
        acc = jnp.zeros((ns, HG_I), F32)
        for s in range(t + 1):
            e = tok(qf, t) * tok(kk, s)
            if s < t:
                e = e * jnp.exp2(g[t] - g[s])
            vs = tok(vv, s)
            parts = []
            for hd in range(HG_HEADS):
                hs = slice(hd * HG_DK, (hd + 1) * HG_DK)
                parts.append(jnp.sum(e[:, hs], axis=-1, keepdims=True) * vs[:, hs])
            acc = acc + jnp.concatenate(parts, axis=1)
        oi_ref[:, _tok(t, HG_I)] = acc


def _hgrn_sample_pre_call(x, mods, norm_g, w_in, lb_raw, layer, n_tok):
    n_seq = x.shape[0]
    j = layer // 2
    act = jax.ShapeDtypeStruct((n_seq, n_tok * HG_F), F32)
    return pl.pallas_call(
        functools.partial(_hgrn_sample_pre_kernel, n_tok=n_tok, layer=layer),
        grid=(n_seq // SAMPLE_SEQ_BLOCK,),
        in_specs=[
            _seq_rows(n_tok * D_MODEL),
            _seq_rows(3 * D_MODEL, layer),
            _resident((None, 1, D_MODEL), (layer, 0, 0), 1),
            _resident((None, D_MODEL, 2 * HG_F + 2 * HG_I), (j, 0, 0), 1),
            _resident((DEPTH, HG_F), (0, 0), 1),
        ],
        out_specs=[_seq_rows(n_tok * HG_F)] * 3 + [_seq_rows(HG_F)] + [_seq_rows(n_tok * HG_I)] * 2,
        out_shape=[act, act, act, jax.ShapeDtypeStruct((n_seq, HG_F), F32), act, act],
        compiler_params=_params(1),
        name="hgrn_sample_pre",
    )(x, mods, norm_g, w_in, lb_raw)


def _hgrn_sample_state_kernel(*refs, n_tok, has_prev):
    n_in = 5 + int(has_prev)
    qe_ref, kr_ref, v_ref, el_ref, s_ref = refs[:5]
    oo_ref, so_ref = refs[n_in:n_in + 2]
    ns = SAMPLE_STATE_BLOCK
    seq_of_row = lax.broadcasted_iota(jnp.int32, (n_tok * ns, HG_DK), 0) % ns
    for hd in range(HG_HEADS):
        hs = slice(hd * HG_DK, (hd + 1) * HG_DK)
        q = _stack_tokens(qe_ref, n_tok, HG_F, hs).astype(BF16)
        k = _stack_tokens(kr_ref, n_tok, HG_F, hs)
        v = _stack_tokens(v_ref, n_tok, HG_I, hs).astype(BF16)
        el = el_ref[:, hs]
        o = jnp.zeros((n_tok * ns, HG_DV), F32)
        for b in range(ns):
            mine = seq_of_row == b
            s_old = s_ref[b, hd]
            o = jnp.where(mine, _dot(q, s_old.astype(BF16)), o)
            kb = jnp.where(mine, k, 0.0).astype(BF16)
            dec = jnp.broadcast_to(el[b:b + 1, :], (HG_DK, HG_DK)).T
            so_ref[b, hd] = s_old * dec + _dot_tn(kb, v)
        for t in range(n_tok):
            oo_ref[:, t * HG_I + hs.start:t * HG_I + hs.stop] = o[t * ns:(t + 1) * ns]


def _hgrn_sample_state_call(qe, kr, v, el, states, prev_states, layer, n_tok):
    n_seq = qe.shape[0]
    j = layer // 2
    nb = SAMPLE_STATE_BLOCK
    rows = lambda width: pl.BlockSpec((nb, width), lambda i: (i, 0))
    st_spec = pl.BlockSpec((None, nb, HG_HEADS, HG_DK, HG_DV), lambda i: (j, i, 0, 0, 0))
    in_specs = [rows(n_tok * HG_F), rows(n_tok * HG_F), rows(n_tok * HG_I), rows(HG_F), st_spec]
    args = [qe, kr, v, el, states]
    aliases = {}
    if prev_states is not None:
        in_specs.append(pl.BlockSpec(memory_space=pl.ANY))
        args.append(prev_states)
        aliases = {len(args) - 1: 1}
    return pl.pallas_call(
        functools.partial(_hgrn_sample_state_kernel, n_tok=n_tok, has_prev=prev_states is not None),
        grid=(n_seq // nb,),
        in_specs=in_specs,
        out_specs=[rows(n_tok * HG_I), st_spec],
        out_shape=[
            jax.ShapeDtypeStruct((n_seq, n_tok * HG_I), F32),
            jax.ShapeDtypeStruct(states.shape, F32),
        ],
        input_output_aliases=aliases,
        compiler_params=_params(1),
        name="hgrn_sample_state",
    )(*args)


def _hgrn_sample_post_kernel(x_ref, mod_ref, oa_ref, ob_ref, z_ref, og_ref, wout_ref, fg_ref, xo_ref,
                             *, n_tok, final):
    ns = x_ref.shape[0]
    o = _stack_tokens(oa_ref, n_tok, HG_I) + _stack_tokens(ob_ref, n_tok, HG_I)
    y = _head_norm_gate(o, _stack_tokens(z_ref, n_tok, HG_I), og_ref[...])
    out = _dot(y, wout_ref[...])
    gate = mod_ref[:, 2 * D_MODEL:]
    for s in range(n_tok):
        xn = x_ref[:, _tok(s, D_MODEL)] + gate * out[s * ns:(s + 1) * ns]
        if final:
            xn = _rmsnorm(xn, fg_ref[...])
        xo_ref[:, _tok(s, D_MODEL)] = xn


def _hgrn_sample_post_call(x, mods, o_inter, o_intra, z, og, w_out, fg, layer, n_tok):
    n_seq = x.shape[0]
    j = layer // 2
    return pl.pallas_call(
        functools.partial(_hgrn_sample_post_kernel, n_tok=n_tok, final=layer == DEPTH - 1),
        grid=(n_seq // SAMPLE_SEQ_BLOCK,),
        in_specs=[
            _seq_rows(n_tok * D_MODEL),
            _seq_rows(3 * D_MODEL, layer),
            _seq_rows(n_tok * HG_I), _seq_rows(n_tok * HG_I), _seq_rows(n_tok * HG_I),
            _resident((None, 1, HG_I), (j, 0, 0), 1),
            _resident((None, HG_I, D_MODEL), (j, 0, 0), 1),
            _resident((1, D_MODEL), (0, 0), 1),
        ],
        out_specs=_seq_rows(n_tok * D_MODEL),
        out_shape=jax.ShapeDtypeStruct((n_seq, n_tok * D_MODEL), F32),
        compiler_params=_params(1),
        name="hgrn_sample_post",
    )(x, mods, o_inter, o_intra, z, og, w_out, fg)


def kernel(x_prompt, x_sample, state_conv, state_hgrn, c_prompt, c_sample, norm_g, w_ada, b_ada, conv_w_in, conv_w, conv_w_out, hgrn_w_in, hgrn_lower_bounds, hgrn_onorm_g, hgrn_w_out, final_norm_g):
    n_p = x_prompt.shape[0]
    n_s, t_s, _ = x_sample.shape
    assert t_s >= CONV_WIDTH - 1 and n_s % SAMPLE_SEQ_BLOCK == 0 and n_s % n_p == 0

    mods = _ada_call(jnp.concatenate([c_sample, c_prompt], axis=0), w_ada, b_ada)
    mod_block_p = n_s // n_p

    conv_w_in_b = conv_w_in.astype(BF16)
    conv_w_out_b = conv_w_out.astype(BF16)
    hgrn_w_in_b = hgrn_w_in.astype(BF16)
    hgrn_w_out_b = hgrn_w_out.astype(BF16)
    norm_g3 = norm_g.reshape(DEPTH, 1, D_MODEL)
    og3 = hgrn_onorm_g.reshape(DEPTH // 2, 1, HG_I)
    fg = final_norm_g.reshape(1, D_MODEL)
    state_conv2 = state_conv.reshape(state_conv.shape[0], n_s, (CONV_WIDTH - 1) * E_CONV)

    xp = x_prompt
    xs = x_sample.reshape(n_s, t_s * D_MODEL)
    conv_p, conv_s, hgrn_p = [], [], []
    hgrn_s = None
    for layer in range(DEPTH):
        if layer % 2 == 0:
            xp, tail = _conv_prompt_call(xp, mods, norm_g3, conv_w_in_b, conv_w, conv_w_out_b, layer, mod_block_p)
            conv_p.append(tail[:, SUBLANE - (CONV_WIDTH - 1):])
            xs, st = _conv_sample_call(xs, mods, norm_g3, conv_w_in_b, conv_w, conv_w_out_b, state_conv2,
                                       layer, t_s)
            conv_s.append(st.reshape(n_s, CONV_WIDTH - 1, E_CONV))
        else:
            qe, kr, v, el, o_intra, z = _hgrn_sample_pre_call(xs, mods, norm_g3, hgrn_w_in_b,
                                                              hgrn_lower_bounds, layer, t_s)
            n_blocks = (n_p // HGRN_GB) * (x_prompt.shape[1] // HGRN_TB)
            per_block = n_s // n_blocks

            def to_blocks(a):
                a = a.reshape(n_blocks, per_block, t_s, HG_F)
                return jnp.swapaxes(a, 1, 2).reshape(n_blocks, t_s * per_block, HG_F)

            decode = (to_blocks(qe), to_blocks(kr), to_blocks(v), el.reshape(n_blocks, per_block, HG_F),
                      state_hgrn, hgrn_s)
            xp, s_new, o_inter, hgrn_s = _hgrn_prompt_call(xp, mods, norm_g3, hgrn_w_in_b, hgrn_lower_bounds,
                                                           og3, hgrn_w_out_b, fg, decode, layer, mod_block_p)
            hgrn_p.append(s_new)
            o_inter = jnp.swapaxes(o_inter.reshape(n_blocks, t_s, per_block, HG_I), 1, 2)
            xs = _hgrn_sample_post_call(xs, mods, o_inter.reshape(n_s, t_s * HG_I), o_intra, z, og3,
                                        hgrn_w_out_b, fg, layer, t_s)
    return (xp, xs.reshape(n_s, t_s, D_MODEL), jnp.stack(conv_p), jnp.stack(hgrn_p), jnp.stack(conv_s), hgrn_s)
```

```python
import functools

import jax
import jax.numpy as jnp
from jax import lax
from jax.experimental import pallas as pl
from jax.experimental.pallas import tpu as pltpu

F32 = jnp.float32
BF16 = jnp.bfloat16

D_MODEL = 1024
DEPTH = 4
CONV_WIDTH = 3
E_CONV = D_MODEL
HG_HEADS = 8
HG_DK = 128
HG_DV = 128
HG_F = HG_HEADS * HG_DK
HG_I = HG_HEADS * HG_DV
EPS = 1e-6

SUBLANE = 8
CHUNK = 128
SUB = CHUNK // 2
CONV_COL_TILE = 256
CONV_TB = 512
CONV_GB = 2
HGRN_TB = 256
HGRN_GB = 2
HGRN_PIPE_SKEW = 3
HGRN_SAFE_LOG2_SPAN = 200.0
SAMPLE_SEQ_BLOCK = 32
VMEM_LIMIT = 56 * 1024 * 1024


def _dot(a, b):
    return jnp.dot(a, b, preferred_element_type=F32)


def _dot_nt(a, b):
    return lax.dot_general(a, b, (((1,), (1,)), ((), ())), preferred_element_type=F32)


def _dot_tn(a, b):
    return lax.dot_general(a, b, (((0,), (0,)), ((), ())), preferred_element_type=F32)


def _silu(x):
    return x * jax.nn.sigmoid(x)


def _rmsnorm(x, g):
    ms = jnp.mean(x * x, axis=-1, keepdims=True)
    return x * lax.rsqrt(ms + EPS) * g


def _split_mod(mod):
    return mod[:, :D_MODEL], mod[:, D_MODEL:2 * D_MODEL], mod[:, 2 * D_MODEL:]


def _mod_norm(x, g, shift, scale):
    return (_rmsnorm(x, g) * (1.0 + scale) + shift).astype(BF16)


def _mod_norm_seq(x, g, shift, scale):
    ms = jnp.mean(x * x, axis=-1, keepdims=True)
    return (x * lax.rsqrt(ms + EPS) * (g * (1.0 + scale)) + shift).astype(BF16)


def _lower_bound(raw, layer):
    rows = [raw[i:i + 1, :] for i in range(DEPTH)]
    m = functools.reduce(jnp.maximum, rows)
    es = [jnp.exp(r - m) for r in rows]
    tot = functools.reduce(lambda a, b: a + b, es)
    acc = es[1]
    for i in range(2, layer + 1):
        acc = acc + es[i]
    return acc / tot


def _gate_math(fpre, lb):
    t = jnp.exp(-jnp.abs(fpre))
    r = 1.0 / (1.0 + t)
    pos = fpre >= 0.0
    sig = jnp.where(pos, 1.0, t) * r
    sig_neg = jnp.where(pos, t, 1.0) * r
    return jnp.log2(lb + (1.0 - lb) * sig), (1.0 - lb) * sig_neg


def _head_norm_gate(o, z, og):
    parts = []
    for hd in range(HG_HEADS):
        hs = slice(hd * HG_DV, (hd + 1) * HG_DV)
        oh = o[:, hs]
        ms = jnp.mean(oh * oh, axis=-1, keepdims=True)
        parts.append(oh * lax.rsqrt(ms + EPS))
    on = jnp.concatenate(parts, axis=1) * og
    return (on * _silu(z)).astype(BF16)


def _params(n_grid):
    return pltpu.CompilerParams(dimension_semantics=("arbitrary",) * n_grid, vmem_limit_bytes=VMEM_LIMIT)


def _resident(shape, index, n_grid):
    if n_grid == 1:
        return pl.BlockSpec(shape, lambda i: index, pipeline_mode=pl.Buffered(1))
    return pl.BlockSpec(shape, lambda b, t: index, pipeline_mode=pl.Buffered(1))


def _ada_kernel(c_ref, w_ref, b_ref, o_ref):
    s = _silu(c_ref[...]).astype(BF16)
    o_ref[...] = _dot(s, w_ref[...].astype(BF16)) + b_ref[...]


def _ada_call(c_all, w_ada, b_ada):
    rows = c_all.shape[0]
    n_col = 3
    col = 3 * D_MODEL // n_col
    return pl.pallas_call(
        _ada_kernel,
        grid=(DEPTH, n_col),
        in_specs=[
            pl.BlockSpec((rows, D_MODEL), lambda l, n: (0, 0)),
            pl.BlockSpec((None, D_MODEL, col), lambda l, n: (l, 0, n)),
            pl.BlockSpec((None, 1, col), lambda l, n: (l, 0, n)),
        ],
        out_specs=pl.BlockSpec((None, rows, col), lambda l, n: (l, 0, n)),
        out_shape=jax.ShapeDtypeStruct((DEPTH, rows, 3 * D_MODEL), F32),
        compiler_params=_params(2),
        name="ada_mod",
    )(c_all, w_ada, b_ada.reshape(DEPTH, 1, 3 * D_MODEL))


def _conv_columns(h, win_ref, cw_ref, y_scr, prev_fn, tail_fn):
    ct = CONV_COL_TILE
    for n in range(E_CONV // ct):
        def col(j):
            return slice(j * E_CONV + n * ct, j * E_CONV + (n + 1) * ct)
        cs = slice(n * ct, (n + 1) * ct)
        b_gate = _dot(h, win_ref[:, col(0)])
        c_gate = _dot(h, win_ref[:, col(1)])
        v = _dot(h, win_ref[:, col(2)])
        z = _dot(h, win_ref[:, col(3)])
        u = c_gate * v
        p1, p2 = prev_fn(cs, u)
        conv = cw_ref[0:1, cs] * p2 + cw_ref[1:2, cs] * p1 + cw_ref[2:3, cs] * u
        y_scr[:, cs] = (b_gate * conv * _silu(z)).astype(BF16)
        tail_fn(cs, u)


def _conv_prompt_kernel(x_ref, mod_ref, g_ref, win_ref, cw_ref, wout_ref, xo_ref, st_ref, y_scr, *, tb, gb):
    b = pl.program_id(0)
    t = pl.program_id(1)
    ct = CONV_COL_TILE

    @pl.when(t == 0)
    def _():
        st_ref[...] = jnp.zeros_like(st_ref)

    seqs = range(gb)
    x = [x_ref[g] for g in seqs]
    mod = [_split_mod(mod_ref[pl.ds(b * gb + g, 1), :]) for g in seqs]
    h = [_mod_norm_seq(x[g], g_ref[...], mod[g][0], mod[g][1]) for g in seqs]
    row = lax.broadcasted_iota(jnp.int32, (tb, ct), 0)

    for g in seqs:
        def prev_fn(cs, u, g=g):
            c0 = st_ref[g, SUBLANE - 2:SUBLANE - 1, cs]
            c1 = st_ref[g, SUBLANE - 1:SUBLANE, cs]
            p1 = jnp.where(row == 0, c1, pltpu.roll(u, 1, 0))
            p2 = jnp.where(row == 0, c0, jnp.where(row == 1, c1, pltpu.roll(u, 2, 0)))
            return p1, p2

        def tail_fn(cs, u, g=g):
            st_ref[g, :, cs] = u[tb - SUBLANE:tb, :]

        _conv_columns(h[g], win_ref, cw_ref, y_scr.at[g], prev_fn, tail_fn)
    for g in seqs:
        xo_ref[g] = x[g] + mod[g][2] * _dot(y_scr[g], wout_ref[...])


def _conv_prompt_call(x, mods, norm_g, w_in, cw, w_out, layer, mod_block):
    bsz, seq, _ = x.shape
    tb = CONV_TB
    gb = CONV_GB
    j = layer // 2
    return pl.pallas_call(
        functools.partial(_conv_prompt_kernel, tb=tb, gb=gb),
        grid=(bsz // gb, seq // tb),
        in_specs=[
            pl.BlockSpec((gb, tb, D_MODEL), lambda b, t: (b, t, 0)),
            _resident((None, bsz, 3 * D_MODEL), (layer, mod_block, 0), 2),
            _resident((None, 1, D_MODEL), (layer, 0, 0), 2),
            _resident((None, D_MODEL, 4 * E_CONV), (j, 0, 0), 2),
            _resident((None, CONV_WIDTH, E_CONV), (j, 0, 0), 2),
            _resident((None, E_CONV, D_MODEL), (j, 0, 0), 2),
        ],
        out_specs=[
            pl.BlockSpec((gb, tb, D_MODEL), lambda b, t: (b, t, 0)),
            pl.BlockSpec((gb, SUBLANE, E_CONV), lambda b, t: (b, 0, 0)),
        ],
        out_shape=[
            jax.ShapeDtypeStruct((bsz, seq, D_MODEL), F32),
            jax.ShapeDtypeStruct((bsz, SUBLANE, E_CONV), F32),
        ],
        scratch_shapes=[pltpu.VMEM((gb, tb, E_CONV), BF16)],
        compiler_params=_params(2),
        name="conv_prompt",
    )(x, mods, norm_g, w_in, cw, w_out)


def _tok(s, width):
    return slice(s * width, (s + 1) * width)


def _stack_tokens(ref, n_tok, width, cols=None):
    parts = []
    for s in range(n_tok):
        lo = s * width + (0 if cols is None else cols.start)
        hi = s * width + (width if cols is None else cols.stop)
        parts.append(ref[:, lo:hi])
    return jnp.concatenate(parts, axis=0)


def _sample_hidden(x_ref, mod_ref, g_ref, n_tok):
    shift, scale, _ = _split_mod(mod_ref[...])
    g = g_ref[...]
    return jnp.concatenate(
        [_mod_norm(x_ref[:, _tok(s, D_MODEL)], g, shift, scale) for s in range(n_tok)], axis=0)


def _conv_sample_kernel(x_ref, mod_ref, g_ref, win_ref, cw_ref, wout_ref, st_ref,
                        xo_ref, so_ref, y_scr, *, n_tok):
    ns = x_ref.shape[0]
    h = _sample_hidden(x_ref, mod_ref, g_ref, n_tok)

    def prev_fn(cs, u):
        st0 = st_ref[:, cs]
        st1 = st_ref[:, E_CONV + cs.start:E_CONV + cs.stop]
        p1 = jnp.concatenate([st1, u[:(n_tok - 1) * ns]], axis=0)
        p2 = jnp.concatenate([st0, st1, u[:(n_tok - 2) * ns]], axis=0)
        return p1, p2

    def tail_fn(cs, u):
        so_ref[:, cs] = u[(n_tok - 2) * ns:(n_tok - 1) * ns]
        so_ref[:, E_CONV + cs.start:E_CONV + cs.stop] = u[(n_tok - 1) * ns:]

    _conv_columns(h, win_ref, cw_ref, y_scr, prev_fn, tail_fn)
    out = _dot(y_scr[...], wout_ref[...])
    gate = mod_ref[:, 2 * D_MODEL:]
    for s in range(n_tok):
        xo_ref[:, _tok(s, D_MODEL)] = x_ref[:, _tok(s, D_MODEL)] + gate * out[s * ns:(s + 1) * ns]


def _seq_rows(width, layer=None):
    if layer is None:
        return pl.BlockSpec((SAMPLE_SEQ_BLOCK, width), lambda i: (i, 0))
    return pl.BlockSpec((None, SAMPLE_SEQ_BLOCK, width), lambda i: (layer, i, 0))


def _conv_sample_call(x, mods, norm_g, w_in, cw, w_out, state, layer, n_tok):
    n_seq = x.shape[0]
    j = layer // 2
    return pl.pallas_call(
        functools.partial(_conv_sample_kernel, n_tok=n_tok),
        grid=(n_seq // SAMPLE_SEQ_BLOCK,),
        in_specs=[
            _seq_rows(n_tok * D_MODEL),
            _seq_rows(3 * D_MODEL, layer),
            _resident((None, 1, D_MODEL), (layer, 0, 0), 1),
            _resident((None, D_MODEL, 4 * E_CONV), (j, 0, 0), 1),
            _resident((None, CONV_WIDTH, E_CONV), (j, 0, 0), 1),
            _resident((None, E_CONV, D_MODEL), (j, 0, 0), 1),
            _seq_rows((CONV_WIDTH - 1) * E_CONV, j),
        ],
        out_specs=[_seq_rows(n_tok * D_MODEL), _seq_rows((CONV_WIDTH - 1) * E_CONV)],
        out_shape=[
            jax.ShapeDtypeStruct((n_seq, n_tok * D_MODEL), F32),
            jax.ShapeDtypeStruct((n_seq, (CONV_WIDTH - 1) * E_CONV), F32),
        ],
        scratch_shapes=[pltpu.VMEM((n_tok * SAMPLE_SEQ_BLOCK, E_CONV), BF16)],
        compiler_params=_params(1),
        name="conv_sample",
    )(x, mods, norm_g, w_in, cw, w_out, state)


def _rows_from_blocks(vecs):
    return jnp.concatenate([jnp.broadcast_to(v, (SUB, v.shape[1])) for v in vecs], axis=0)


def _chunk_masks():
    r = lax.broadcasted_iota(jnp.int32, (CHUNK, CHUNK), 0)
    c = lax.broadcasted_iota(jnp.int32, (CHUNK, CHUNK), 1)
    causal = (r // SUB == c // SUB) & (c <= r)
    tri = jnp.where(causal, 1.0, 0.0).astype(BF16)
    first_cols = lax.broadcasted_iota(jnp.int32, (SUB, CHUNK), 1) < SUB
    return causal, first_cols, jnp.concatenate([tri, tri], axis=1)


def _chunk_decay(lf, tri2):
    hi = lf.astype(BF16)
    lo = (lf - hi.astype(F32)).astype(BF16)
    return _dot(tri2, jnp.concatenate([hi, lo], axis=0))


def _chunk_key_operands(kk, gl):
    t0 = gl[SUB - 1:SUB, :]
    t1 = gl[2 * SUB - 1:2 * SUB, :]
    k_mid = kk * jnp.exp2(_rows_from_blocks([0.5 * t0, 0.5 * t1]) - gl)
    k_state = k_mid * _rows_from_blocks([jnp.exp2(0.5 * t0 + t1), jnp.exp2(0.5 * t1)])
    return k_mid.astype(BF16), k_state.astype(BF16), jnp.exp2(t0 + t1), jnp.minimum(t0, t1)


def _chunk_query_operands(qf, gl):
    t0 = gl[SUB - 1:SUB, :]
    t1 = gl[2 * SUB - 1:2 * SUB, :]
    q_mid = qf * jnp.exp2(gl - _rows_from_blocks([0.5 * t0, 0.5 * t1]))
    q_next = q_mid[SUB:] * jnp.exp2(0.5 * (t0 + t1))
    q_state = q_mid * _rows_from_blocks([jnp.exp2(0.5 * t0), jnp.exp2(0.5 * t1 + t0)])
    return jnp.concatenate([q_mid, q_next], axis=0).astype(BF16), q_state.astype(BF16)


def _chunk_finish(ast, qe, kr, vb, ptot, s_old, causal, first_cols):
    same = jnp.where(causal, ast[0:CHUNK], 0.0)
    a = jnp.concatenate([same[:SUB], jnp.where(first_cols, ast[CHUNK:], same[SUB:])], axis=0)
    o = _dot(jnp.concatenate([qe, a.astype(BF16)], axis=1),
             jnp.concatenate([s_old.astype(BF16), vb], axis=0))
    dec = jnp.broadcast_to(ptot, (HG_DK, HG_DK)).T
    return o, s_old * dec + _dot_tn(kr, vb)


def _exact_recurrence(q_ref, k_ref, f_ref, v_ref, o_ref, s0, n_rows):
    def col(tile, r):
        return jnp.broadcast_to(tile[r:r + 1, :], (HG_DK, HG_DK)).T

    def step(i, s):
        rows = pl.ds(pl.multiple_of(i * SUBLANE, SUBLANE), SUBLANE)
        q, k, f, v = q_ref[rows, :], k_ref[rows, :], f_ref[rows, :], v_ref[rows, :]
        out = []
        for r in range(SUBLANE):
            s = s * col(f, r) + col(k, r) * v[r:r + 1, :]
            out.append(jnp.sum(col(q, r) * s, axis=0, keepdims=True))
        o_ref[rows, :] = jnp.concatenate(out, axis=0)
        return s
    return lax.fori_loop(0, n_rows // SUBLANE, step, s0)


def _decode_state_update(p, el_row0, dq_ref, dk_ref, dv_ref, del_ref, ds_ref, do_ref, dso_ref, partial_o):
    nd = ds_ref.shape[0]
    n_tok = dq_ref.shape[0] // nd
    hd, sq = divmod(p, nd)
    hs = slice(hd * HG_DK, (hd + 1) * HG_DK)
    mine = lax.broadcasted_iota(jnp.int32, (dq_ref.shape[0], HG_DK), 0) // n_tok == sq
    s_old = ds_ref[sq, hd]
    o = _dot(dq_ref[:, hs].astype(BF16), s_old.astype(BF16))
    partial_o[hd] = jnp.where(mine, o, partial_o[hd]) if hd in partial_o else jnp.where(mine, o, 0.0)
    k_mine = jnp.where(mine, dk_ref[:, hs], 0.0).astype(BF16)
    dec = jnp.broadcast_to(del_ref[pl.ds(el_row0 + sq, 1), :][:, hs], (HG_DK, HG_DK)).T
    dso_ref[sq, hd] = s_old * dec + _dot_tn(k_mine, dv_ref[:, hs].astype(BF16))
    if sq == nd - 1:
        do_ref[:, hs] = partial_o.pop(hd)


def _hgrn_prompt_kernel(*refs, tb, gb, layer, final, has_prev):
    n_in = 13 + int(has_prev)
    x_ref, mod_ref, g_ref, win_ref, lbraw_ref, og_ref, wout_ref, fg_ref = refs[:8]
    decode_in = refs[8:13]
    xo_ref, s_ref, do_ref, dso_ref = refs[n_in:n_in + 4]
    q_scr, lf_scr, k_scr, v_scr, z_scr, o_scr, s0_scr = refs[n_in + 4:]
    b = pl.program_id(0)
    t = pl.program_id(1)

    @pl.when(t == 0)
    def _():
        s_ref[...] = jnp.zeros_like(s_ref)

    s0_scr[...] = s_ref[...]

    seqs = range(gb)
    lb = _lower_bound(lbraw_ref[...], layer)
    x = [x_ref[g] for g in seqs]
    mod = [_split_mod(mod_ref[pl.ds(b * gb + g, 1), :]) for g in seqs]
    h = [_mod_norm_seq(x[g], g_ref[...], mod[g][0], mod[g][1]) for g in seqs]
    for g in seqs:
        q_scr[g] = _silu(_dot(h[g], win_ref[:, 0:HG_F]))
    for g in seqs:
        logf, kk = _gate_math(_dot(h[g], win_ref[:, HG_F:2 * HG_F]), lb)
        lf_scr[g] = logf
        k_scr[g] = kk

    def late_dots(step):
        if step == HGRN_PIPE_SKEW:
            for g in seqs:
                v_scr[g] = _dot(h[g], win_ref[:, 2 * HG_F:2 * HG_F + HG_I])
        if step == 2 * HGRN_PIPE_SKEW + 2:
            for g in seqs:
                z_scr[g] = _dot(h[g], win_ref[:, 2 * HG_F + HG_I:])

    causal, first_cols, tri2 = _chunk_masks()
    chains = [(c, hd, g) for c in range(tb // CHUNK) for hd in range(HG_HEADS) for g in seqs]
    ops, scores = {}, {}
    span = jnp.zeros((1, HG_DK), F32)

    def where(n):
        c, hd, g = chains[n]
        return g, hd, slice(c * CHUNK, (c + 1) * CHUNK), slice(hd * HG_DK, (hd + 1) * HG_DK)

    block_seqs = decode_in[4].shape[0]
    n_decode = HG_HEADS * block_seqs
    blocks_per_el = decode_in[3].shape[0] // block_seqs
    el_row0 = ((b * pl.num_programs(1) + t) % blocks_per_el) * block_seqs
    partial_o = {}
    for step in range(max(len(chains) + 2 * HGRN_PIPE_SKEW, n_decode)):
        if step < n_decode:
            _decode_state_update(step, el_row0, *decode_in, do_ref, dso_ref, partial_o)
        n = step
        if n < len(chains):
            g, hd, rows, hs = where(n)
            gl = _chunk_decay(lf_scr[g, rows, hs], tri2)
            km, kr, ptot, tmin = _chunk_key_operands(k_scr[g, rows, hs], gl)
            lhs, qe = _chunk_query_operands(q_scr[g, rows, hs], gl)
            ops[n] = (lhs, km, qe, kr, ptot)
            span = jnp.minimum(span, tmin)
        late_dots(step)
        n = step - HGRN_PIPE_SKEW
        if 0 <= n < len(chains):
            scores[n] = _dot_nt(ops[n][0], ops[n][1])
        n = step - 2 * HGRN_PIPE_SKEW
        if 0 <= n < len(chains):
            g, hd, rows, hs = where(n)
            _, _, qe, kr, ptot = ops.pop(n)
            vb = v_scr[g, rows, hs].astype(BF16)
            o, s_new = _chunk_finish(scores.pop(n), qe, kr, vb, ptot, s_ref[g, hd], causal, first_cols)
            o_scr[g, rows, hs] = o
            s_ref[g, hd] = s_new

    y = [_head_norm_gate(o_scr[g], z_scr[g], og_ref[...]) for g in seqs]
    for g in seqs:
        xn = x[g] + mod[g][2] * _dot(y[g], wout_ref[...])
        if final:
            xn = _rmsnorm(xn, fg_ref[...])
        xo_ref[g] = xn

    @pl.when(jnp.min(span) < -HGRN_SAFE_LOG2_SPAN)
    def _():
        def exact(q_tmp, k_tmp, f_tmp, v_tmp, o_tmp):
            def one_seq(g, carry):
                shift, scale, gate = _split_mod(mod_ref[pl.ds(b * gb + g, 1), :])
                hg = _mod_norm_seq(x_ref[g], g_ref[...], shift, scale)
                qf = _silu(_dot(hg, win_ref[:, 0:HG_F]))
                logf, kk = _gate_math(_dot(hg, win_ref[:, HG_F:2 * HG_F]), lb)
                ff = jnp.exp2(logf)
                vv = _dot(hg, win_ref[:, 2 * HG_F:2 * HG_F + HG_I])
                for hd in range(HG_HEADS):
                    hs = slice(hd * HG_DK, (hd + 1) * HG_DK)
                    q_tmp[hd], k_tmp[hd], f_tmp[hd], v_tmp[hd] = qf[:, hs], kk[:, hs], ff[:, hs], vv[:, hs]

                def one_head(hd, c2):
                    s_ref[g, hd] = _exact_recurrence(q_tmp.at[hd], k_tmp.at[hd], f_tmp.at[hd], v_tmp.at[hd],
                                                     o_tmp.at[hd], s0_scr[g, hd], tb)
                    return c2

                lax.fori_loop(0, HG_HEADS, one_head, 0)
                o = jnp.concatenate([o_tmp[hd] for hd in range(HG_HEADS)], axis=1)
                y = _head_norm_gate(o, z_scr[g], og_ref[...])
                xn = x_ref[g] + gate * _dot(y, wout_ref[...])
                if final:
                    xn = _rmsnorm(xn, fg_ref[...])
                xo_ref[g] = xn
                return carry

            lax.fori_loop(0, gb, one_seq, 0)

        pl.run_scoped(exact, *[pltpu.VMEM((HG_HEADS, tb, HG_DK), F32)] * 5)


def _hgrn_prompt_call(x, mods, norm_g, w_in, lb_raw, og, w_out, fg, decode, layer, mod_block):
    bsz, seq, _ = x.shape
    tb = HGRN_TB
    gb = HGRN_GB
    j = layer // 2
    final = layer == DEPTH - 1
    n_t = seq // tb
    qe, kr, v, el, states, prev_states = decode
    n_blocks = (bsz // gb) * n_t
    n_seq = el.shape[0]
    block_seqs = n_seq // n_blocks
    dec_rows = qe.shape[0] // n_blocks
    el_rows = max(block_seqs, SUBLANE)
    assert n_seq == n_blocks * block_seqs and el_rows % block_seqs == 0 and dec_rows % SUBLANE == 0
    dec_spec = pl.BlockSpec((dec_rows, HG_F), lambda b, t: (b * n_t + t, 0))
    st_spec = pl.BlockSpec((None, block_seqs, HG_HEADS, HG_DK, HG_DV), lambda b, t: (j, b * n_t + t, 0, 0, 0))
    in_specs = [
        pl.BlockSpec((gb, tb, D_MODEL), lambda b, t: (b, t, 0)),
        _resident((None, bsz, 3 * D_MODEL), (layer, mod_block, 0), 2),
        _resident((None, 1, D_MODEL), (layer, 0, 0), 2),
        _resident((None, D_MODEL, 2 * HG_F + 2 * HG_I), (j, 0, 0), 2),
        _resident((DEPTH, HG_F), (0, 0), 2),
        _resident((None, 1, HG_I), (j, 0, 0), 2),
        _resident((None, HG_I, D_MODEL), (j, 0, 0), 2),
        _resident((1, D_MODEL), (0, 0), 2),
        dec_spec, dec_spec, dec_spec,
        pl.BlockSpec((el_rows, HG_F), lambda b, t: ((b * n_t + t) * block_seqs // el_rows, 0)),
        st_spec,
    ]
    args = [x, mods, norm_g, w_in, lb_raw, og, w_out, fg, qe, kr, v, el, states]
    aliases = {}
    if prev_states is not None:
        in_specs.append(pl.BlockSpec(memory_space=pl.ANY))
        args.append(prev_states)
        aliases = {len(args) - 1: 3}
    return pl.pallas_call(
        functools.partial(_hgrn_prompt_kernel, tb=tb, gb=gb, layer=layer, final=final,
                          has_prev=prev_states is not None),
        grid=(bsz // gb, n_t),
        in_specs=in_specs,
        out_specs=[
            pl.BlockSpec((gb, tb, D_MODEL), lambda b, t: (b, t, 0)),
            pl.BlockSpec((gb, HG_HEADS, HG_DK, HG_DV), lambda b, t: (b, 0, 0, 0)),
            dec_spec,
            st_spec,
        ],
        out_shape=[
            jax.ShapeDtypeStruct((bsz, seq, D_MODEL), F32),
            jax.ShapeDtypeStruct((bsz, HG_HEADS, HG_DK, HG_DV), F32),
            jax.ShapeDtypeStruct(qe.shape, F32),
            jax.ShapeDtypeStruct(states.shape, F32),
        ],
        scratch_shapes=[pltpu.VMEM((gb, tb, HG_F), F32)] * 6
        + [pltpu.VMEM((gb, HG_HEADS, HG_DK, HG_DV), F32)],
        input_output_aliases=aliases,
        compiler_params=_params(2),
        name="hgrn_prompt",
    )(*args)


def _token_row_permutation(n_seq, n_tok, seq_major_out):
    n = n_seq * n_tok
    r = lax.broadcasted_iota(jnp.int32, (n, n), 0)
    c = lax.broadcasted_iota(jnp.int32, (n, n), 1)
    sm, tm = (r, c) if seq_major_out else (c, r)
    return jnp.where(tm == (sm % n_tok) * n_seq + sm // n_tok, 1.0, 0.0).astype(BF16)


def _permute_rows_f32(p, a):
    hi = a.astype(BF16)
    rest = a - hi.astype(F32)
    mid = rest.astype(BF16)
    lo = (rest - mid.astype(F32)).astype(BF16)
    return (_dot(p, hi) + _dot(p, mid)) + _dot(p, lo)


def _hgrn_sample_pre_kernel(x_ref, mod_ref, g_ref, win_ref, lbraw_ref,
                            qe_ref, kr_ref, v_ref, el_ref, oi_ref, z_ref, *, n_tok, layer):
    ns = x_ref.shape[0]
    h = _sample_hidden(x_ref, mod_ref, g_ref, n_tok)
    lb = _lower_bound(lbraw_ref[...], layer)
    qf = _silu(_dot(h, win_ref[:, 0:HG_F]))
    lf, kk = _gate_math(_dot(h, win_ref[:, HG_F:2 * HG_F]), lb)
    vv = _dot(h, win_ref[:, 2 * HG_F:2 * HG_F + HG_I])
    z = _dot(h, win_ref[:, 2 * HG_F + HG_I:])

    def tok(a, s):
        return a[s * ns:(s + 1) * ns]

    g = [tok(lf, 0)]
    for s in range(1, n_tok):
        g.append(g[-1] + tok(lf, s))
    el_ref[...] = jnp.exp2(g[-1])
    to_seq_major = _token_row_permutation(ns, n_tok, seq_major_out=True)
    qe = jnp.concatenate([tok(qf, s) * jnp.exp2(g[s]) for s in range(n_tok)], axis=0)
    kr = jnp.concatenate([tok(kk, s) * jnp.exp2(g[-1] - g[s]) for s in range(n_tok)], axis=0)
    qe_ref[...] = _dot(to_seq_major, qe.astype(BF16)).astype(BF16)
    kr_ref[...] = _dot(to_seq_major, kr.astype(BF16)).astype(BF16)
    v_ref[...] = _dot(to_seq_major, vv.astype(BF16)).astype(BF16)
    for s in range(n_tok):
        z_ref[:, _tok(s, HG_I)] = tok(z, s)

    for t in range(n_tok):
        acc = jnp.zeros((ns, HG_I), F32)
        for s in range(t + 1):
            e = tok(qf, t) * tok(kk, s)
            if s < t:
                e = e * jnp.exp2(g[t] - g[s])
            vs = tok(vv, s)
            parts = []
            for hd in range(HG_HEADS):
                hs = slice(hd * HG_DK, (hd + 1) * HG_DK)
                parts.append(jnp.sum(e[:, hs], axis=-1, keepdims=True) * vs[:, hs])
            acc = acc + jnp.concatenate(parts, axis=1)
        oi_ref[:, _tok(t, HG_I)] = acc


def _hgrn_sample_pre_call(x, mods, norm_g, w_in, lb_raw, layer, n_tok):
    n_seq = x.shape[0]
    j = layer // 2
    act = jax.ShapeDtypeStruct((n_seq, n_tok * HG_F), F32)
    tok_rows = jax.ShapeDtypeStruct((n_seq * n_tok, HG_F), BF16)
    tok_spec = pl.BlockSpec((SAMPLE_SEQ_BLOCK * n_tok, HG_F), lambda i: (i, 0))
    return pl.pallas_call(
        functools.partial(_hgrn_sample_pre_kernel, n_tok=n_tok, layer=layer),
        grid=(n_seq // SAMPLE_SEQ_BLOCK,),
        in_specs=[
            _seq_rows(n_tok * D_MODEL),
            _seq_rows(3 * D_MODEL, layer),
            _resident((None, 1, D_MODEL), (layer, 0, 0), 1),
            _resident((None, D_MODEL, 2 * HG_F + 2 * HG_I), (j, 0, 0), 1),
            _resident((DEPTH, HG_F), (0, 0), 1),
        ],
        out_specs=[tok_spec] * 3 + [_seq_rows(HG_F)] + [_seq_rows(n_tok * HG_I)] * 2,
        out_shape=[tok_rows, tok_rows, tok_rows, jax.ShapeDtypeStruct((n_seq, HG_F), F32), act, act],
        compiler_params=_params(1),
        name="hgrn_sample_pre",
    )(x, mods, norm_g, w_in, lb_raw)


def _hgrn_sample_post_kernel(x_ref, mod_ref, oa_ref, ob_ref, z_ref, og_ref, wout_ref, fg_ref, xo_ref,
                             *, n_tok, final):
    ns = x_ref.shape[0]
    o_state = _permute_rows_f32(_token_row_permutation(ns, n_tok, seq_major_out=False), oa_ref[...])
    o = o_state + _stack_tokens(ob_ref, n_tok, HG_I)
    y = _head_norm_gate(o, _stack_tokens(z_ref, n_tok, HG_I), og_ref[...])
    out = _dot(y, wout_ref[...])
    gate = mod_ref[:, 2 * D_MODEL:]
    for s in range(n_tok):
        xn = x_ref[:, _tok(s, D_MODEL)] + gate * out[s * ns:(s + 1) * ns]
        if final:
            xn = _rmsnorm(xn, fg_ref[...])
        xo_ref[:, _tok(s, D_MODEL)] = xn


def _hgrn_sample_post_call(x, mods, o_inter, o_intra, z, og, w_out, fg, layer, n_tok):
    n_seq = x.shape[0]
    j = layer // 2
    return pl.pallas_call(
        functools.partial(_hgrn_sample_post_kernel, n_tok=n_tok, final=layer == DEPTH - 1),
        grid=(n_seq // SAMPLE_SEQ_BLOCK,),
        in_specs=[
            _seq_rows(n_tok * D_MODEL),
            _seq_rows(3 * D_MODEL, layer),
            pl.BlockSpec((SAMPLE_SEQ_BLOCK * n_tok, HG_I), lambda i: (i, 0)),
            _seq_rows(n_tok * HG_I), _seq_rows(n_tok * HG_I),
            _resident((None, 1, HG_I), (j, 0, 0), 1),
            _resident((None, HG_I, D_MODEL), (j, 0, 0), 1),
            _resident((1, D_MODEL), (0, 0), 1),
        ],
        out_specs=_seq_rows(n_tok * D_MODEL),
        out_shape=jax.ShapeDtypeStruct((n_seq, n_tok * D_MODEL), F32),
        compiler_params=_params(1),
        name="hgrn_sample_post",
    )(x, mods, o_inter, o_intra, z, og, w_out, fg)


def kernel(x_prompt, x_sample, state_conv, state_hgrn, c_prompt, c_sample, norm_g, w_ada, b_ada, conv_w_in, conv_w, conv_w_out, hgrn_w_in, hgrn_lower_bounds, hgrn_onorm_g, hgrn_w_out, final_norm_g):
    n_p = x_prompt.shape[0]
    n_s, t_s, _ = x_sample.shape
    assert t_s >= CONV_WIDTH - 1 and n_s % SAMPLE_SEQ_BLOCK == 0 and n_s % n_p == 0

    mods = _ada_call(jnp.concatenate([c_sample, c_prompt], axis=0), w_ada, b_ada)
    mod_block_p = n_s // n_p

    conv_w_in_b = conv_w_in.astype(BF16)
    conv_w_out_b = conv_w_out.astype(BF16)
    hgrn_w_in_b = hgrn_w_in.astype(BF16)
    hgrn_w_out_b = hgrn_w_out.astype(BF16)
    norm_g3 = norm_g.reshape(DEPTH, 1, D_MODEL)
    og3 = hgrn_onorm_g.reshape(DEPTH // 2, 1, HG_I)
    fg = final_norm_g.reshape(1, D_MODEL)
    state_conv2 = state_conv.reshape(state_conv.shape[0], n_s, (CONV_WIDTH - 1) * E_CONV)

    xp = x_prompt
    xs = x_sample.reshape(n_s, t_s * D_MODEL)
    conv_p, conv_s, hgrn_p = [], [], []
    hgrn_s = None
    for layer in range(DEPTH):
        if layer % 2 == 0:
            xp, tail = _conv_prompt_call(xp, mods, norm_g3, conv_w_in_b, conv_w, conv_w_out_b, layer, mod_block_p)
            conv_p.append(tail[:, SUBLANE - (CONV_WIDTH - 1):])
            xs, st = _conv_sample_call(xs, mods, norm_g3, conv_w_in_b, conv_w, conv_w_out_b, state_conv2,
                                       layer, t_s)
            conv_s.append(st.reshape(n_s, CONV_WIDTH - 1, E_CONV))
        else:
            qe, kr, v, el, o_intra, z = _hgrn_sample_pre_call(xs, mods, norm_g3, hgrn_w_in_b,
                                                              hgrn_lower_bounds, layer, t_s)
            decode = (qe, kr, v, el, state_hgrn, hgrn_s)
            xp, s_new, o_inter, hgrn_s = _hgrn_prompt_call(xp, mods, norm_g3, hgrn_w_in_b, hgrn_lower_bounds,
                                                           og3, hgrn_w_out_b, fg, decode, layer, mod_block_p)
            hgrn_p.append(s_new)
            xs = _hgrn_sample_post_call(xs, mods, o_inter, o_intra, z, og3, hgrn_w_out_b, fg, layer, t_s)
    return (xp, xs.reshape(n_s, t_s, D_MODEL), jnp.stack(conv_p), jnp.stack(hgrn_p), jnp.stack(conv_s), hgrn_s)
```

```python
import functools

import jax
import jax.numpy as jnp
from jax import lax
from jax.experimental import pallas as pl
from jax.experimental.pallas import tpu as pltpu

F32 = jnp.float32
BF16 = jnp.bfloat16

D_MODEL = 1024
DEPTH = 4
CONV_WIDTH = 3
E_CONV = D_MODEL
HG_HEADS = 8
HG_DK = 128
HG_DV = 128
HG_F = HG_HEADS * HG_DK
HG_I = HG_HEADS * HG_DV
EPS = 1e-6

SUBLANE = 8
CHUNK = 128
SUB = CHUNK // 2
CONV_COL_TILE = 256
CONV_TB = 512
CONV_GB = 2
HGRN_TB = 256
HGRN_GB = 2
HGRN_PIPE_SKEW = 3
HGRN_SAFE_LOG2_SPAN = 200.0
SAMPLE_SEQ_BLOCK = 32
VMEM_LIMIT = 56 * 1024 * 1024


def _dot(a, b):
    return jnp.dot(a, b, preferred_element_type=F32)


def _dot_nt(a, b):
    return lax.dot_general(a, b, (((1,), (1,)), ((), ())), preferred_element_type=F32)


def _dot_tn(a, b):
    return lax.dot_general(a, b, (((0,), (0,)), ((), ())), preferred_element_type=F32)


def _silu(x):
    return x * jax.nn.sigmoid(x)


def _rmsnorm(x, g):
    ms = jnp.mean(x * x, axis=-1, keepdims=True)
    return x * lax.rsqrt(ms + EPS) * g


def _split_mod(mod):
    return mod[:, :D_MODEL], mod[:, D_MODEL:2 * D_MODEL], mod[:, 2 * D_MODEL:]


def _mod_norm(x, g, shift, scale):
    return (_rmsnorm(x, g) * (1.0 + scale) + shift).astype(BF16)


def _mod_norm_seq(x, g, shift, scale):
    ms = jnp.mean(x * x, axis=-1, keepdims=True)
    return (x * lax.rsqrt(ms + EPS) * (g * (1.0 + scale)) + shift).astype(BF16)


def _lower_bound(raw, layer):
    rows = [raw[i:i + 1, :] for i in range(DEPTH)]
    m = functools.reduce(jnp.maximum, rows)
    es = [jnp.exp(r - m) for r in rows]
    tot = functools.reduce(lambda a, b: a + b, es)
    acc = es[1]
    for i in range(2, layer + 1):
        acc = acc + es[i]
    return acc / tot


def _gate_math(fpre, lb):
    t = jnp.exp(-jnp.abs(fpre))
    r = 1.0 / (1.0 + t)
    pos = fpre >= 0.0
    sig = jnp.where(pos, 1.0, t) * r
    sig_neg = jnp.where(pos, t, 1.0) * r
    return jnp.log2(lb + (1.0 - lb) * sig), (1.0 - lb) * sig_neg


def _head_norm_gate(o, z, og):
    parts = []
    for hd in range(HG_HEADS):
        hs = slice(hd * HG_DV, (hd + 1) * HG_DV)
        oh = o[:, hs]
        ms = jnp.mean(oh * oh, axis=-1, keepdims=True)
        parts.append(oh * lax.rsqrt(ms + EPS))
    on = jnp.concatenate(parts, axis=1) * og
    return (on * _silu(z)).astype(BF16)


def _params(n_grid):
    return pltpu.CompilerParams(dimension_semantics=("arbitrary",) * n_grid, vmem_limit_bytes=VMEM_LIMIT)


def _resident(shape, index, n_grid):
    if n_grid == 1:
        return pl.BlockSpec(shape, lambda i: index, pipeline_mode=pl.Buffered(1))
    return pl.BlockSpec(shape, lambda b, t: index, pipeline_mode=pl.Buffered(1))


def _ada_kernel(c_ref, w_ref, b_ref, o_ref):
    s = _silu(c_ref[...]).astype(BF16)
    o_ref[...] = _dot(s, w_ref[...].astype(BF16)) + b_ref[...]


def _ada_call(c_all, w_ada, b_ada):
    rows = c_all.shape[0]
    n_col = 3
    col = 3 * D_MODEL // n_col
    return pl.pallas_call(
        _ada_kernel,
        grid=(DEPTH, n_col),
        in_specs=[
            pl.BlockSpec((rows, D_MODEL), lambda l, n: (0, 0)),
            pl.BlockSpec((None, D_MODEL, col), lambda l, n: (l, 0, n)),
            pl.BlockSpec((None, 1, col), lambda l, n: (l, 0, n)),
        ],
        out_specs=pl.BlockSpec((None, rows, col), lambda l, n: (l, 0, n)),
        out_shape=jax.ShapeDtypeStruct((DEPTH, rows, 3 * D_MODEL), F32),
        compiler_params=_params(2),
        name="ada_mod",
    )(c_all, w_ada, b_ada.reshape(DEPTH, 1, 3 * D_MODEL))


def _conv_columns(h, win_ref, cw_ref, y_scr, prev_fn, tail_fn, after_tile=None):
    ct = CONV_COL_TILE
    for n in range(E_CONV // ct):
        def col(j):
            return slice(j * E_CONV + n * ct, j * E_CONV + (n + 1) * ct)
        cs = slice(n * ct, (n + 1) * ct)
        b_gate = _dot(h, win_ref[:, col(0)])
        c_gate = _dot(h, win_ref[:, col(1)])
        v = _dot(h, win_ref[:, col(2)])
        z = _dot(h, win_ref[:, col(3)])
        u = c_gate * v
        p1, p2 = prev_fn(cs, u)
        conv = cw_ref[0:1, cs] * p2 + cw_ref[1:2, cs] * p1 + cw_ref[2:3, cs] * u
        y_scr[:, cs] = (b_gate * conv * _silu(z)).astype(BF16)
        tail_fn(cs, u)
        if after_tile is not None:
            after_tile()


def _decode_state_update(p, el_row0, dq_ref, dk_ref, dv_ref, del_ref, ds_ref, do_ref, dso_ref, partial_o):
    nd = ds_ref.shape[0]
    n_tok = dq_ref.shape[0] // nd
    hd, sq = divmod(p, nd)
    hs = slice(hd * HG_DK, (hd + 1) * HG_DK)
    mine = lax.broadcasted_iota(jnp.int32, (dq_ref.shape[0], HG_DK), 0) // n_tok == sq
    s_old = ds_ref[sq, hd]
    o = _dot(dq_ref[:, hs], s_old.astype(BF16))
    partial_o[hd] = jnp.where(mine, o, partial_o[hd]) if hd in partial_o else jnp.where(mine, o, 0.0)
    k_mine = jnp.where(mine, dk_ref[:, hs], 0.0).astype(BF16)
    dec = jnp.broadcast_to(del_ref[pl.ds(el_row0 + sq, 1), :][:, hs], (HG_DK, HG_DK)).T
    dso_ref[sq, hd] = s_old * dec + _dot_tn(k_mine, dv_ref[:, hs])
    if sq == nd - 1:
        do_ref[:, hs] = partial_o.pop(hd)


def _conv_prompt_kernel(*refs, tb, gb, has_prev):
    n_in = 11 + int(has_prev)
    x_ref, mod_ref, g_ref, win_ref, cw_ref, wout_ref = refs[:6]
    decode_in = refs[6:11]
    xo_ref, st_ref, do_ref, dso_ref = refs[n_in:n_in + 4]
    y_scr, = refs[n_in + 4:]
    b = pl.program_id(0)
    t = pl.program_id(1)
    ct = CONV_COL_TILE

    @pl.when(t == 0)
    def _():
        st_ref[...] = jnp.zeros_like(st_ref)

    seqs = range(gb)
    x = [x_ref[g] for g in seqs]
    mod = [_split_mod(mod_ref[pl.ds(b * gb + g, 1), :]) for g in seqs]
    h = [_mod_norm_seq(x[g], g_ref[...], mod[g][0], mod[g][1]) for g in seqs]
    row = lax.broadcasted_iota(jnp.int32, (tb, ct), 0)

    block_seqs = decode_in[4].shape[0]
    el_row0 = ((b * pl.num_programs(1) + t) % (decode_in[3].shape[0] // block_seqs)) * block_seqs
    pending = list(range(HG_HEADS * block_seqs))
    n_tiles = gb * (E_CONV // ct)
    per_tile = -(-len(pending) * 3 // (4 * n_tiles))
    partial_o = {}

    def decode_updates(count):
        for _ in range(min(count, len(pending))):
            _decode_state_update(pending.pop(0), el_row0, *decode_in, do_ref, dso_ref, partial_o)

    decode_updates(len(pending) - per_tile * n_tiles)
    for g in seqs:
        def prev_fn(cs, u, g=g):
            c0 = st_ref[g, SUBLANE - 2:SUBLANE - 1, cs]
            c1 = st_ref[g, SUBLANE - 1:SUBLANE, cs]
            p1 = jnp.where(row == 0, c1, pltpu.roll(u, 1, 0))
            p2 = jnp.where(row == 0, c0, jnp.where(row == 1, c1, pltpu.roll(u, 2, 0)))
            return p1, p2

        def tail_fn(cs, u, g=g):
            st_ref[g, :, cs] = u[tb - SUBLANE:tb, :]

        _conv_columns(h[g], win_ref, cw_ref, y_scr.at[g], prev_fn, tail_fn,
                      after_tile=functools.partial(decode_updates, per_tile))
    decode_updates(len(pending))
    for g in seqs:
        xo_ref[g] = x[g] + mod[g][2] * _dot(y_scr[g], wout_ref[...])


def _conv_prompt_call(x, mods, norm_g, w_in, cw, w_out, decode, layer, mod_block):
    bsz, seq, _ = x.shape
    tb = CONV_TB
    gb = CONV_GB
    j = layer // 2
    n_t = seq // tb
    qe, kr, v, el, states, prev_states = decode
    n_blocks = (bsz // gb) * n_t
    n_seq = el.shape[0]
    block_seqs = n_seq // n_blocks
    dec_rows = qe.shape[0] // n_blocks
    el_rows = max(block_seqs, SUBLANE)
    assert n_seq == n_blocks * block_seqs and el_rows % block_seqs == 0 and dec_rows % (2 * SUBLANE) == 0
    dec_spec = pl.BlockSpec((dec_rows, HG_F), lambda b, t: (b * n_t + t, 0))
    st_spec = pl.BlockSpec((None, block_seqs, HG_HEADS, HG_DK, HG_DV), lambda b, t: (j, b * n_t + t, 0, 0, 0))
    in_specs = [
        pl.BlockSpec((gb, tb, D_MODEL), lambda b, t: (b, t, 0)),
        _resident((None, bsz, 3 * D_MODEL), (layer, mod_block, 0), 2),
        _resident((None, 1, D_MODEL), (layer, 0, 0), 2),
        _resident((None, D_MODEL, 4 * E_CONV), (j, 0, 0), 2),
        _resident((None, CONV_WIDTH, E_CONV), (j, 0, 0), 2),
        _resident((None, E_CONV, D_MODEL), (j, 0, 0), 2),
        dec_spec, dec_spec, dec_spec,
        pl.BlockSpec((el_rows, HG_F), lambda b, t: ((b * n_t + t) * block_seqs // el_rows, 0)),
        st_spec,
    ]
    args = [x, mods, norm_g, w_in, cw, w_out, qe, kr, v, el, states]
    aliases = {}
    if prev_states is not None:
        in_specs.append(pl.BlockSpec(memory_space=pl.ANY))
        args.append(prev_states)
        aliases = {len(args) - 1: 3}
    return pl.pallas_call(
        functools.partial(_conv_prompt_kernel, tb=tb, gb=gb, has_prev=prev_states is not None),
        grid=(bsz // gb, n_t),
        in_specs=in_specs,
        out_specs=[
            pl.BlockSpec((gb, tb, D_MODEL), lambda b, t: (b, t, 0)),
            pl.BlockSpec((gb, SUBLANE, E_CONV), lambda b, t: (b, 0, 0)),
            dec_spec,
            st_spec,
        ],
        out_shape=[
            jax.ShapeDtypeStruct((bsz, seq, D_MODEL), F32),
            jax.ShapeDtypeStruct((bsz, SUBLANE, E_CONV), F32),
            jax.ShapeDtypeStruct(qe.shape, F32),
            jax.ShapeDtypeStruct(states.shape, F32),
        ],
        scratch_shapes=[pltpu.VMEM((gb, tb, E_CONV), BF16)],
        input_output_aliases=aliases,
        compiler_params=_params(2),
        name="conv_prompt",
    )(*args)


def _tok(s, width):
    return slice(s * width, (s + 1) * width)


def _stack_tokens(ref, n_tok, width, cols=None):
    parts = []
    for s in range(n_tok):
        lo = s * width + (0 if cols is None else cols.start)
        hi = s * width + (width if cols is None else cols.stop)
        parts.append(ref[:, lo:hi])
    return jnp.concatenate(parts, axis=0)


def _sample_hidden(x_ref, mod_ref, g_ref, n_tok):
    shift, scale, _ = _split_mod(mod_ref[...])
    g = g_ref[...]
    return jnp.concatenate(
        [_mod_norm(x_ref[:, _tok(s, D_MODEL)], g, shift, scale) for s in range(n_tok)], axis=0)


def _conv_sample_kernel(x_ref, mod_ref, g_ref, win_ref, cw_ref, wout_ref, st_ref,
                        xo_ref, so_ref, y_scr, *, n_tok):
    ns = x_ref.shape[0]
    h = _sample_hidden(x_ref, mod_ref, g_ref, n_tok)

    def prev_fn(cs, u):
        st0 = st_ref[:, cs]
        st1 = st_ref[:, E_CONV + cs.start:E_CONV + cs.stop]
        p1 = jnp.concatenate([st1, u[:(n_tok - 1) * ns]], axis=0)
        p2 = jnp.concatenate([st0, st1, u[:(n_tok - 2) * ns]], axis=0)
        return p1, p2

    def tail_fn(cs, u):
        so_ref[:, cs] = u[(n_tok - 2) * ns:(n_tok - 1) * ns]
        so_ref[:, E_CONV + cs.start:E_CONV + cs.stop] = u[(n_tok - 1) * ns:]

    _conv_columns(h, win_ref, cw_ref, y_scr, prev_fn, tail_fn)
    out = _dot(y_scr[...], wout_ref[...])
    gate = mod_ref[:, 2 * D_MODEL:]
    for s in range(n_tok):
        xo_ref[:, _tok(s, D_MODEL)] = x_ref[:, _tok(s, D_MODEL)] + gate * out[s * ns:(s + 1) * ns]


def _seq_rows(width, layer=None):
    if layer is None:
        return pl.BlockSpec((SAMPLE_SEQ_BLOCK, width), lambda i: (i, 0))
    return pl.BlockSpec((None, SAMPLE_SEQ_BLOCK, width), lambda i: (layer, i, 0))


def _conv_sample_call(x, mods, norm_g, w_in, cw, w_out, state, layer, n_tok):
    n_seq = x.shape[0]
    j = layer // 2
    return pl.pallas_call(
        functools.partial(_conv_sample_kernel, n_tok=n_tok),
        grid=(n_seq // SAMPLE_SEQ_BLOCK,),
        in_specs=[
            _seq_rows(n_tok * D_MODEL),
            _seq_rows(3 * D_MODEL, layer),
            _resident((None, 1, D_MODEL), (layer, 0, 0), 1),
            _resident((None, D_MODEL, 4 * E_CONV), (j, 0, 0), 1),
            _resident((None, CONV_WIDTH, E_CONV), (j, 0, 0), 1),
            _resident((None, E_CONV, D_MODEL), (j, 0, 0), 1),
            _seq_rows((CONV_WIDTH - 1) * E_CONV, j),
        ],
        out_specs=[_seq_rows(n_tok * D_MODEL), _seq_rows((CONV_WIDTH - 1) * E_CONV)],
        out_shape=[
            jax.ShapeDtypeStruct((n_seq, n_tok * D_MODEL), F32),
            jax.ShapeDtypeStruct((n_seq, (CONV_WIDTH - 1) * E_CONV), F32),
        ],
        scratch_shapes=[pltpu.VMEM((n_tok * SAMPLE_SEQ_BLOCK, E_CONV), BF16)],
        compiler_params=_params(1),
        name="conv_sample",
    )(x, mods, norm_g, w_in, cw, w_out, state)


def _rows_from_blocks(vecs):
    return jnp.concatenate([jnp.broadcast_to(v, (SUB, v.shape[1])) for v in vecs], axis=0)


def _chunk_masks():
    r = lax.broadcasted_iota(jnp.int32, (CHUNK, CHUNK), 0)
    c = lax.broadcasted_iota(jnp.int32, (CHUNK, CHUNK), 1)
    causal = (r // SUB == c // SUB) & (c <= r)
    tri = jnp.where(causal, 1.0, 0.0).astype(BF16)
    first_cols = lax.broadcasted_iota(jnp.int32, (SUB, CHUNK), 1) < SUB
    return causal, first_cols, jnp.concatenate([tri, tri], axis=1)


def _chunk_decay(lf, tri2):
    hi = lf.astype(BF16)
    lo = (lf - hi.astype(F32)).astype(BF16)
    return _dot(tri2, jnp.concatenate([hi, lo], axis=0))


def _chunk_key_operands(kk, gl):
    t0 = gl[SUB - 1:SUB, :]
    t1 = gl[2 * SUB - 1:2 * SUB, :]
    k_mid = kk * jnp.exp2(_rows_from_blocks([0.5 * t0, 0.5 * t1]) - gl)
    k_state = k_mid * _rows_from_blocks([jnp.exp2(0.5 * t0 + t1), jnp.exp2(0.5 * t1)])
    return k_mid.astype(BF16), k_state.astype(BF16), jnp.exp2(t0 + t1), jnp.minimum(t0, t1)


def _chunk_query_operands(qf, gl):
    t0 = gl[SUB - 1:SUB, :]
    t1 = gl[2 * SUB - 1:2 * SUB, :]
    q_mid = qf * jnp.exp2(gl - _rows_from_blocks([0.5 * t0, 0.5 * t1]))
    q_next = q_mid[SUB:] * jnp.exp2(0.5 * (t0 + t1))
    q_state = q_mid * _rows_from_blocks([jnp.exp2(0.5 * t0), jnp.exp2(0.5 * t1 + t0)])
    return jnp.concatenate([q_mid, q_next], axis=0).astype(BF16), q_state.astype(BF16)


def _chunk_finish(ast, qe, kr, vb, ptot, s_old, causal, first_cols):
    same = jnp.where(causal, ast[0:CHUNK], 0.0)
    a = jnp.concatenate([same[:SUB], jnp.where(first_cols, ast[CHUNK:], same[SUB:])], axis=0)
    o = _dot(jnp.concatenate([qe, a.astype(BF16)], axis=1),
             jnp.concatenate([s_old.astype(BF16), vb], axis=0))
    dec = jnp.broadcast_to(ptot, (HG_DK, HG_DK)).T
    return o, s_old * dec + _dot_tn(kr, vb)


def _exact_recurrence(q_ref, k_ref, f_ref, v_ref, o_ref, s0, n_rows):
    def col(tile, r):
        return jnp.broadcast_to(tile[r:r + 1, :], (HG_DK, HG_DK)).T

    def step(i, s):
        rows = pl.ds(pl.multiple_of(i * SUBLANE, SUBLANE), SUBLANE)
        q, k, f, v = q_ref[rows, :], k_ref[rows, :], f_ref[rows, :], v_ref[rows, :]
        out = []
        for r in range(SUBLANE):
            s = s * col(f, r) + col(k, r) * v[r:r + 1, :]
            out.append(jnp.sum(col(q, r) * s, axis=0, keepdims=True))
        o_ref[rows, :] = jnp.concatenate(out, axis=0)
        return s
    return lax.fori_loop(0, n_rows // SUBLANE, step, s0)


def _hgrn_prompt_kernel(x_ref, mod_ref, g_ref, win_ref, lbraw_ref, og_ref, wout_ref, fg_ref,
                        xo_ref, s_ref, q_scr, lf_scr, k_scr, v_scr, z_scr, o_scr, s0_scr,
                        *, tb, gb, layer, final):
    b = pl.program_id(0)
    t = pl.program_id(1)

    @pl.when(t == 0)
    def _():
        s_ref[...] = jnp.zeros_like(s_ref)

    s0_scr[...] = s_ref[...]

    seqs = range(gb)
    lb = _lower_bound(lbraw_ref[...], layer)
    x = [x_ref[g] for g in seqs]
    mod = [_split_mod(mod_ref[pl.ds(b * gb + g, 1), :]) for g in seqs]
    h = [_mod_norm_seq(x[g], g_ref[...], mod[g][0], mod[g][1]) for g in seqs]
    for g in seqs:
        q_scr[g] = _silu(_dot(h[g], win_ref[:, 0:HG_F]))
    for g in seqs:
        logf, kk = _gate_math(_dot(h[g], win_ref[:, HG_F:2 * HG_F]), lb)
        lf_scr[g] = logf
        k_scr[g] = kk

    def late_dots(step):
        if step == HGRN_PIPE_SKEW:
            for g in seqs:
                v_scr[g] = _dot(h[g], win_ref[:, 2 * HG_F:2 * HG_F + HG_I])
        if step == 2 * HGRN_PIPE_SKEW + 2:
            for g in seqs:
                z_scr[g] = _dot(h[g], win_ref[:, 2 * HG_F + HG_I:])

    causal, first_cols, tri2 = _chunk_masks()
    chains = [(c, hd, g) for c in range(tb // CHUNK) for hd in range(HG_HEADS) for g in seqs]
    ops, scores = {}, {}
    span = jnp.zeros((1, HG_DK), F32)

    def where(n):
        c, hd, g = chains[n]
        return g, hd, slice(c * CHUNK, (c + 1) * CHUNK), slice(hd * HG_DK, (hd + 1) * HG_DK)

    for step in range(len(chains) + 2 * HGRN_PIPE_SKEW):
        n = step
        if n < len(chains):
            g, hd, rows, hs = where(n)
            gl = _chunk_decay(lf_scr[g, rows, hs], tri2)
            km, kr, ptot, tmin = _chunk_key_operands(k_scr[g, rows, hs], gl)
            lhs, qe = _chunk_query_operands(q_scr[g, rows, hs], gl)
            ops[n] = (lhs, km, qe, kr, ptot)
            span = jnp.minimum(span, tmin)
        late_dots(step)
        n = step - HGRN_PIPE_SKEW
        if 0 <= n < len(chains):
            scores[n] = _dot_nt(ops[n][0], ops[n][1])
        n = step - 2 * HGRN_PIPE_SKEW
        if 0 <= n < len(chains):
            g, hd, rows, hs = where(n)
            _, _, qe, kr, ptot = ops.pop(n)
            vb = v_scr[g, rows, hs].astype(BF16)
            o, s_new = _chunk_finish(scores.pop(n), qe, kr, vb, ptot, s_ref[g, hd], causal, first_cols)
            o_scr[g, rows, hs] = o
            s_ref[g, hd] = s_new

    y = [_head_norm_gate(o_scr[g], z_scr[g], og_ref[...]) for g in seqs]
    for g in seqs:
        xn = x[g] + mod[g][2] * _dot(y[g], wout_ref[...])
        if final:
            xn = _rmsnorm(xn, fg_ref[...])
        xo_ref[g] = xn

    @pl.when(jnp.min(span) < -HGRN_SAFE_LOG2_SPAN)
    def _():
        def exact(q_tmp, k_tmp, f_tmp, v_tmp, o_tmp):
            def one_seq(g, carry):
                shift, scale, gate = _split_mod(mod_ref[pl.ds(b * gb + g, 1), :])
                hg = _mod_norm_seq(x_ref[g], g_ref[...], shift, scale)
                qf = _silu(_dot(hg, win_ref[:, 0:HG_F]))
                logf, kk = _gate_math(_dot(hg, win_ref[:, HG_F:2 * HG_F]), lb)
                ff = jnp.exp2(logf)
                vv = _dot(hg, win_ref[:, 2 * HG_F:2 * HG_F + HG_I])
                for hd in range(HG_HEADS):
                    hs = slice(hd * HG_DK, (hd + 1) * HG_DK)
                    q_tmp[hd], k_tmp[hd], f_tmp[hd], v_tmp[hd] = qf[:, hs], kk[:, hs], ff[:, hs], vv[:, hs]

                def one_head(hd, c2):
                    s_ref[g, hd] = _exact_recurrence(q_tmp.at[hd], k_tmp.at[hd], f_tmp.at[hd], v_tmp.at[hd],
                                                     o_tmp.at[hd], s0_scr[g, hd], tb)
                    return c2

                lax.fori_loop(0, HG_HEADS, one_head, 0)
                o = jnp.concatenate([o_tmp[hd] for hd in range(HG_HEADS)], axis=1)
                y = _head_norm_gate(o, z_scr[g], og_ref[...])
                xn = x_ref[g] + gate * _dot(y, wout_ref[...])
                if final:
                    xn = _rmsnorm(xn, fg_ref[...])
                xo_ref[g] = xn
                return carry

            lax.fori_loop(0, gb, one_seq, 0)

        pl.run_scoped(exact, *[pltpu.VMEM((HG_HEADS, tb, HG_DK), F32)] * 5)


def _hgrn_prompt_call(x, mods, norm_g, w_in, lb_raw, og, w_out, fg, layer, mod_block):
    bsz, seq, _ = x.shape
    tb = HGRN_TB
    gb = HGRN_GB
    j = layer // 2
    final = layer == DEPTH - 1
    in_specs = [
        pl.BlockSpec((gb, tb, D_MODEL), lambda b, t: (b, t, 0)),
        _resident((None, bsz, 3 * D_MODEL), (layer, mod_block, 0), 2),
        _resident((None, 1, D_MODEL), (layer, 0, 0), 2),
        _resident((None, D_MODEL, 2 * HG_F + 2 * HG_I), (j, 0, 0), 2),
        _resident((DEPTH, HG_F), (0, 0), 2),
        _resident((None, 1, HG_I), (j, 0, 0), 2),
        _resident((None, HG_I, D_MODEL), (j, 0, 0), 2),
        _resident((1, D_MODEL), (0, 0), 2),
    ]
    return pl.pallas_call(
        functools.partial(_hgrn_prompt_kernel, tb=tb, gb=gb, layer=layer, final=final),
        grid=(bsz // gb, seq // tb),
        in_specs=in_specs,
        out_specs=[
            pl.BlockSpec((gb, tb, D_MODEL), lambda b, t: (b, t, 0)),
            pl.BlockSpec((gb, HG_HEADS, HG_DK, HG_DV), lambda b, t: (b, 0, 0, 0)),
        ],
        out_shape=[
            jax.ShapeDtypeStruct((bsz, seq, D_MODEL), F32),
            jax.ShapeDtypeStruct((bsz, HG_HEADS, HG_DK, HG_DV), F32),
        ],
        scratch_shapes=[pltpu.VMEM((gb, tb, HG_F), F32)] * 6
        + [pltpu.VMEM((gb, HG_HEADS, HG_DK, HG_DV), F32)],
        compiler_params=_params(2),
        name="hgrn_prompt",
    )(x, mods, norm_g, w_in, lb_raw, og, w_out, fg)


def _token_row_permutation(n_seq, n_tok, seq_major_out):
    n = n_seq * n_tok
    r = lax.broadcasted_iota(jnp.int32, (n, n), 0)
    c = lax.broadcasted_iota(jnp.int32, (n, n), 1)
    sm, tm = (r, c) if seq_major_out else (c, r)
    return jnp.where(tm == (sm % n_tok) * n_seq + sm // n_tok, 1.0, 0.0).astype(BF16)


def _permute_rows_f32(p, a):
    hi = a.astype(BF16)
    rest = a - hi.astype(F32)
    mid = rest.astype(BF16)
    lo = (rest - mid.astype(F32)).astype(BF16)
    return (_dot(p, hi) + _dot(p, mid)) + _dot(p, lo)


def _hgrn_sample_pre_kernel(x_ref, mod_ref, g_ref, win_ref, lbraw_ref,
                            qe_ref, kr_ref, v_ref, el_ref, oi_ref, z_ref, *, n_tok, layer):
    ns = x_ref.shape[0]
    h = _sample_hidden(x_ref, mod_ref, g_ref, n_tok)
    lb = _lower_bound(lbraw_ref[...], layer)
    qf = _silu(_dot(h, win_ref[:, 0:HG_F]))
    lf, kk = _gate_math(_dot(h, win_ref[:, HG_F:2 * HG_F]), lb)
    vv = _dot(h, win_ref[:, 2 * HG_F:2 * HG_F + HG_I])
    z = _dot(h, win_ref[:, 2 * HG_F + HG_I:])

    def tok(a, s):
        return a[s * ns:(s + 1) * ns]

    g = [tok(lf, 0)]
    for s in range(1, n_tok):
        g.append(g[-1] + tok(lf, s))
    el_ref[...] = jnp.exp2(g[-1])
    to_seq_major = _token_row_permutation(ns, n_tok, seq_major_out=True)
    qe = jnp.concatenate([tok(qf, s) * jnp.exp2(g[s]) for s in range(n_tok)], axis=0)
    kr = jnp.concatenate([tok(kk, s) * jnp.exp2(g[-1] - g[s]) for s in range(n_tok)], axis=0)
    qe_ref[...] = _dot(to_seq_major, qe.astype(BF16)).astype(BF16)
    kr_ref[...] = _dot(to_seq_major, kr.astype(BF16)).astype(BF16)
    v_ref[...] = _dot(to_seq_major, vv.astype(BF16)).astype(BF16)
    for s in range(n_tok):
        z_ref[:, _tok(s, HG_I)] = tok(z, s)

    for t in range(n_tok):
        acc = jnp.zeros((ns, HG_I), F32)
        for s in range(t + 1):
            e = tok(qf, t) * tok(kk, s)
            if s < t:
                e = e * jnp.exp2(g[t] - g[s])
            vs = tok(vv, s)
            parts = []
            for hd in range(HG_HEADS):
                hs = slice(hd * HG_DK, (hd + 1) * HG_DK)
                parts.append(jnp.sum(e[:, hs], axis=-1, keepdims=True) * vs[:, hs])
            acc = acc + jnp.concatenate(parts, axis=1)
        oi_ref[:, _tok(t, HG_I)] = acc


def _hgrn_sample_pre_call(x, mods, norm_g, w_in, lb_raw, layer, n_tok):
    n_seq = x.shape[0]
    j = layer // 2
    act = jax.ShapeDtypeStruct((n_seq, n_tok * HG_F), F32)
    tok_rows = jax.ShapeDtypeStruct((n_seq * n_tok, HG_F), BF16)
    tok_spec = pl.BlockSpec((SAMPLE_SEQ_BLOCK * n_tok, HG_F), lambda i: (i, 0))
    return pl.pallas_call(
        functools.partial(_hgrn_sample_pre_kernel, n_tok=n_tok, layer=layer),
        grid=(n_seq // SAMPLE_SEQ_BLOCK,),
        in_specs=[
            _seq_rows(n_tok * D_MODEL),
            _seq_rows(3 * D_MODEL, layer),
            _resident((None, 1, D_MODEL), (layer, 0, 0), 1),
            _resident((None, D_MODEL, 2 * HG_F + 2 * HG_I), (j, 0, 0), 1),
            _resident((DEPTH, HG_F), (0, 0), 1),
        ],
        out_specs=[tok_spec] * 3 + [_seq_rows(HG_F)] + [_seq_rows(n_tok * HG_I)] * 2,
        out_shape=[tok_rows, tok_rows, tok_rows, jax.ShapeDtypeStruct((n_seq, HG_F), F32), act, act],
        compiler_params=_params(1),
        name="hgrn_sample_pre",
    )(x, mods, norm_g, w_in, lb_raw)


def _hgrn_sample_post_kernel(x_ref, mod_ref, oa_ref, ob_ref, z_ref, og_ref, wout_ref, fg_ref, xo_ref,
                             *, n_tok, final):
    ns = x_ref.shape[0]
    o_state = _permute_rows_f32(_token_row_permutation(ns, n_tok, seq_major_out=False), oa_ref[...])
    o = o_state + _stack_tokens(ob_ref, n_tok, HG_I)
    y = _head_norm_gate(o, _stack_tokens(z_ref, n_tok, HG_I), og_ref[...])
    out = _dot(y, wout_ref[...])
    gate = mod_ref[:, 2 * D_MODEL:]
    for s in range(n_tok):
        xn = x_ref[:, _tok(s, D_MODEL)] + gate * out[s * ns:(s + 1) * ns]
        if final:
            xn = _rmsnorm(xn, fg_ref[...])
        xo_ref[:, _tok(s, D_MODEL)] = xn


def _hgrn_sample_post_call(x, mods, o_inter, o_intra, z, og, w_out, fg, layer, n_tok):
    n_seq = x.shape[0]
    j = layer // 2
    return pl.pallas_call(
        functools.partial(_hgrn_sample_post_kernel, n_tok=n_tok, final=layer == DEPTH - 1),
        grid=(n_seq // SAMPLE_SEQ_BLOCK,),
        in_specs=[
            _seq_rows(n_tok * D_MODEL),
            _seq_rows(3 * D_MODEL, layer),
            pl.BlockSpec((SAMPLE_SEQ_BLOCK * n_tok, HG_I), lambda i: (i, 0)),
            _seq_rows(n_tok * HG_I), _seq_rows(n_tok * HG_I),
            _resident((None, 1, HG_I), (j, 0, 0), 1),
            _resident((None, HG_I, D_MODEL), (j, 0, 0), 1),
            _resident((1, D_MODEL), (0, 0), 1),
        ],
        out_specs=_seq_rows(n_tok * D_MODEL),
        out_shape=jax.ShapeDtypeStruct((n_seq, n_tok * D_MODEL), F32),
        compiler_params=_params(1),
        name="hgrn_sample_post",
    )(x, mods, o_inter, o_intra, z, og, w_out, fg)


def kernel(x_prompt, x_sample, state_conv, state_hgrn, c_prompt, c_sample, norm_g, w_ada, b_ada, conv_w_in, conv_w, conv_w_out, hgrn_w_in, hgrn_lower_bounds, hgrn_onorm_g, hgrn_w_out, final_norm_g):
    n_p = x_prompt.shape[0]
    n_s, t_s, _ = x_sample.shape
    assert t_s >= CONV_WIDTH - 1 and n_s % SAMPLE_SEQ_BLOCK == 0 and n_s % n_p == 0

    mods = _ada_call(jnp.concatenate([c_sample, c_prompt], axis=0), w_ada, b_ada)
    mod_block_p = n_s // n_p

    conv_w_in_b = conv_w_in.astype(BF16)
    conv_w_out_b = conv_w_out.astype(BF16)
    hgrn_w_in_b = hgrn_w_in.astype(BF16)
    hgrn_w_out_b = hgrn_w_out.astype(BF16)
    norm_g3 = norm_g.reshape(DEPTH, 1, D_MODEL)
    og3 = hgrn_onorm_g.reshape(DEPTH // 2, 1, HG_I)
    fg = final_norm_g.reshape(1, D_MODEL)
    state_conv2 = state_conv.reshape(state_conv.shape[0], n_s, (CONV_WIDTH - 1) * E_CONV)

    xp = x_prompt
    xs = x_sample.reshape(n_s, t_s * D_MODEL)
    conv_p, conv_s, hgrn_p = [], [], []
    hgrn_s = None
    for layer in range(0, DEPTH, 2):
        xs, st = _conv_sample_call(xs, mods, norm_g3, conv_w_in_b, conv_w, conv_w_out_b, state_conv2,
                                   layer, t_s)
        conv_s.append(st.reshape(n_s, CONV_WIDTH - 1, E_CONV))
        qe, kr, v, el, o_intra, z = _hgrn_sample_pre_call(xs, mods, norm_g3, hgrn_w_in_b,
                                                          hgrn_lower_bounds, layer + 1, t_s)
        decode = (qe, kr, v, el, state_hgrn, hgrn_s)
        xp, tail, o_inter, hgrn_s = _conv_prompt_call(xp, mods, norm_g3, conv_w_in_b, conv_w, conv_w_out_b,
                                                      decode, layer, mod_block_p)
        conv_p.append(tail[:, SUBLANE - (CONV_WIDTH - 1):])
        xs = _hgrn_sample_post_call(xs, mods, o_inter, o_intra, z, og3, hgrn_w_out_b, fg, layer + 1, t_s)
        xp, s_new = _hgrn_prompt_call(xp, mods, norm_g3, hgrn_w_in_b, hgrn_lower_bounds, og3,
                                      hgrn_w_out_b, fg, layer + 1, mod_block_p)
        hgrn_p.append(s_new)
    return (xp, xs.reshape(n_s, t_s, D_MODEL), jnp.stack(conv_p), jnp.stack(hgrn_p), jnp.stack(conv_s), hgrn_s)
```

```python
import functools

import jax
import jax.numpy as jnp
from jax import lax
from jax.experimental import pallas as pl
from jax.experimental.pallas import tpu as pltpu

F32 = jnp.float32
BF16 = jnp.bfloat16

D_MODEL = 1024
DEPTH = 4
CONV_WIDTH = 3
E_CONV = D_MODEL
HG_HEADS = 8
HG_DK = 128
HG_DV = 128
HG_F = HG_HEADS * HG_DK
HG_I = HG_HEADS * HG_DV
EPS = 1e-6

SUBLANE = 8
CHUNK = 128
SUB = CHUNK // 2
CONV_COL_TILE = 256
CONV_TB = 512
CONV_GB = 2
HGRN_TB = 256
HGRN_GB = 2
HGRN_PIPE_SKEW = 3
HGRN_SAFE_LOG2_SPAN = 200.0
SAMPLE_SEQ_BLOCK = 32
VMEM_LIMIT = 56 * 1024 * 1024


def _dot(a, b):
    return jnp.dot(a, b, preferred_element_type=F32)


def _dot_nt(a, b):
    return lax.dot_general(a, b, (((1,), (1,)), ((), ())), preferred_element_type=F32)


def _dot_tn(a, b):
    return lax.dot_general(a, b, (((0,), (0,)), ((), ())), preferred_element_type=F32)


def _silu(x):
    return x * jax.nn.sigmoid(x)


def _rmsnorm(x, g):
    ms = jnp.mean(x * x, axis=-1, keepdims=True)
    return x * lax.rsqrt(ms + EPS) * g


def _split_mod(mod):
    return mod[:, :D_MODEL], mod[:, D_MODEL:2 * D_MODEL], mod[:, 2 * D_MODEL:]


def _mod_norm(x, g, shift, scale):
    return (_rmsnorm(x, g) * (1.0 + scale) + shift).astype(BF16)


def _mod_norm_seq(x, g, shift, scale):
    ms = jnp.mean(x * x, axis=-1, keepdims=True)
    return (x * lax.rsqrt(ms + EPS) * (g * (1.0 + scale)) + shift).astype(BF16)


def _lower_bound(raw, layer):
    rows = [raw[i:i + 1, :] for i in range(DEPTH)]
    m = functools.reduce(jnp.maximum, rows)
    es = [jnp.exp(r - m) for r in rows]
    tot = functools.reduce(lambda a, b: a + b, es)
    acc = es[1]
    for i in range(2, layer + 1):
        acc = acc + es[i]
    return acc / tot


def _gate_math(fpre, lb):
    t = jnp.exp(-jnp.abs(fpre))
    r = 1.0 / (1.0 + t)
    pos = fpre >= 0.0
    sig = jnp.where(pos, 1.0, t) * r
    sig_neg = jnp.where(pos, t, 1.0) * r
    return jnp.log2(lb + (1.0 - lb) * sig), (1.0 - lb) * sig_neg


def _head_norm_gate(o, z, og):
    parts = []
    for hd in range(HG_HEADS):
        hs = slice(hd * HG_DV, (hd + 1) * HG_DV)
        oh = o[:, hs]
        ms = jnp.mean(oh * oh, axis=-1, keepdims=True)
        parts.append(oh * lax.rsqrt(ms + EPS))
    on = jnp.concatenate(parts, axis=1) * og
    return (on * _silu(z)).astype(BF16)


def _params(n_grid):
    return pltpu.CompilerParams(dimension_semantics=("arbitrary",) * n_grid, vmem_limit_bytes=VMEM_LIMIT)


def _resident(shape, index, n_grid):
    if n_grid == 1:
        return pl.BlockSpec(shape, lambda i: index, pipeline_mode=pl.Buffered(1))
    return pl.BlockSpec(shape, lambda b, t: index, pipeline_mode=pl.Buffered(1))


def _ada_kernel(c_ref, w_ref, b_ref, o_ref):
    s = _silu(c_ref[...]).astype(BF16)
    o_ref[...] = _dot(s, w_ref[...].astype(BF16)) + b_ref[...]


def _ada_call(c_all, w_ada, b_ada):
    rows = c_all.shape[0]
    n_col = 3
    col = 3 * D_MODEL // n_col
    return pl.pallas_call(
        _ada_kernel,
        grid=(DEPTH, n_col),
        in_specs=[
            pl.BlockSpec((rows, D_MODEL), lambda l, n: (0, 0)),
            pl.BlockSpec((None, D_MODEL, col), lambda l, n: (l, 0, n)),
            pl.BlockSpec((None, 1, col), lambda l, n: (l, 0, n)),
        ],
        out_specs=pl.BlockSpec((None, rows, col), lambda l, n: (l, 0, n)),
        out_shape=jax.ShapeDtypeStruct((DEPTH, rows, 3 * D_MODEL), F32),
        compiler_params=_params(2),
        name="ada_mod",
    )(c_all, w_ada, b_ada.reshape(DEPTH, 1, 3 * D_MODEL))


def _conv_columns(h, win_ref, cw_ref, y_scr, prev_fn, tail_fn):
    ct = CONV_COL_TILE
    for n in range(E_CONV // ct):
        def col(j):
            return slice(j * E_CONV + n * ct, j * E_CONV + (n + 1) * ct)
        cs = slice(n * ct, (n + 1) * ct)
        b_gate = _dot(h, win_ref[:, col(0)])
        c_gate = _dot(h, win_ref[:, col(1)])
        v = _dot(h, win_ref[:, col(2)])
        z = _dot(h, win_ref[:, col(3)])
        u = c_gate * v
        p1, p2 = prev_fn(cs, u)
        conv = cw_ref[0:1, cs] * p2 + cw_ref[1:2, cs] * p1 + cw_ref[2:3, cs] * u
        y_scr[:, cs] = (b_gate * conv * _silu(z)).astype(BF16)
        tail_fn(cs, u)


def _conv_prompt_kernel(x_ref, mod_ref, g_ref, win_ref, cw_ref, wout_ref, xo_ref, st_ref, y_scr, *, tb, gb):
    b = pl.program_id(0)
    t = pl.program_id(1)
    ct = CONV_COL_TILE

    @pl.when(t == 0)
    def _():
        st_ref[...] = jnp.zeros_like(st_ref)

    seqs = range(gb)
    x = [x_ref[g] for g in seqs]
    mod = [_split_mod(mod_ref[pl.ds(b * gb + g, 1), :]) for g in seqs]
    h = [_mod_norm_seq(x[g], g_ref[...], mod[g][0], mod[g][1]) for g in seqs]
    row = lax.broadcasted_iota(jnp.int32, (tb, ct), 0)

    for g in seqs:
        def prev_fn(cs, u, g=g):
            c0 = st_ref[g, SUBLANE - 2:SUBLANE - 1, cs]
            c1 = st_ref[g, SUBLANE - 1:SUBLANE, cs]
            p1 = jnp.where(row == 0, c1, pltpu.roll(u, 1, 0))
            p2 = jnp.where(row == 0, c0, jnp.where(row == 1, c1, pltpu.roll(u, 2, 0)))
            return p1, p2

        def tail_fn(cs, u, g=g):
            st_ref[g, :, cs] = u[tb - SUBLANE:tb, :]

        _conv_columns(h[g], win_ref, cw_ref, y_scr.at[g], prev_fn, tail_fn)
    for g in seqs:
        xo_ref[g] = x[g] + mod[g][2] * _dot(y_scr[g], wout_ref[...])


def _conv_prompt_call(x, mods, norm_g, w_in, cw, w_out, layer, mod_block):
    bsz, seq, _ = x.shape
    tb = CONV_TB
    gb = CONV_GB
    j = layer // 2
    return pl.pallas_call(
        functools.partial(_conv_prompt_kernel, tb=tb, gb=gb),
        grid=(bsz // gb, seq // tb),
        in_specs=[
            pl.BlockSpec((gb, tb, D_MODEL), lambda b, t: (b, t, 0)),
            _resident((None, bsz, 3 * D_MODEL), (layer, mod_block, 0), 2),
            _resident((None, 1, D_MODEL), (layer, 0, 0), 2),
            _resident((None, D_MODEL, 4 * E_CONV), (j, 0, 0), 2),
            _resident((None, CONV_WIDTH, E_CONV), (j, 0, 0), 2),
            _resident((None, E_CONV, D_MODEL), (j, 0, 0), 2),
        ],
        out_specs=[
            pl.BlockSpec((gb, tb, D_MODEL), lambda b, t: (b, t, 0)),
            pl.BlockSpec((gb, SUBLANE, E_CONV), lambda b, t: (b, 0, 0)),
        ],
        out_shape=[
            jax.ShapeDtypeStruct((bsz, seq, D_MODEL), F32),
            jax.ShapeDtypeStruct((bsz, SUBLANE, E_CONV), F32),
        ],
        scratch_shapes=[pltpu.VMEM((gb, tb, E_CONV), BF16)],
        compiler_params=_params(2),
        name="conv_prompt",
    )(x, mods, norm_g, w_in, cw, w_out)


def _tok(s, width):
    return slice(s * width, (s + 1) * width)


def _stack_tokens(ref, n_tok, width, cols=None):
    parts = []
    for s in range(n_tok):
        lo = s * width + (0 if cols is None else cols.start)
        hi = s * width + (width if cols is None else cols.stop)
        parts.append(ref[:, lo:hi])
    return jnp.concatenate(parts, axis=0)


def _sample_hidden(x_ref, mod_ref, g_ref, n_tok):
    shift, scale, _ = _split_mod(mod_ref[...])
    g = g_ref[...]
    return jnp.concatenate(
        [_mod_norm(x_ref[:, _tok(s, D_MODEL)], g, shift, scale) for s in range(n_tok)], axis=0)


def _conv_sample_kernel(x_ref, mod_ref, g_ref, win_ref, cw_ref, wout_ref, st_ref,
                        xo_ref, so_ref, y_scr, *, n_tok):
    ns = x_ref.shape[0]
    h = _sample_hidden(x_ref, mod_ref, g_ref, n_tok)

    def prev_fn(cs, u):
        st0 = st_ref[:, cs]
        st1 = st_ref[:, E_CONV + cs.start:E_CONV + cs.stop]
        p1 = jnp.concatenate([st1, u[:(n_tok - 1) * ns]], axis=0)
        p2 = jnp.concatenate([st0, st1, u[:(n_tok - 2) * ns]], axis=0)
        return p1, p2

    def tail_fn(cs, u):
        so_ref[:, cs] = u[(n_tok - 2) * ns:(n_tok - 1) * ns]
        so_ref[:, E_CONV + cs.start:E_CONV + cs.stop] = u[(n_tok - 1) * ns:]

    _conv_columns(h, win_ref, cw_ref, y_scr, prev_fn, tail_fn)
    out = _dot(y_scr[...], wout_ref[...])
    gate = mod_ref[:, 2 * D_MODEL:]
    for s in range(n_tok):
        xo_ref[:, _tok(s, D_MODEL)] = x_ref[:, _tok(s, D_MODEL)] + gate * out[s * ns:(s + 1) * ns]


def _seq_rows(width, layer=None):
    if layer is None:
        return pl.BlockSpec((SAMPLE_SEQ_BLOCK, width), lambda i: (i, 0))
    return pl.BlockSpec((None, SAMPLE_SEQ_BLOCK, width), lambda i: (layer, i, 0))


def _conv_sample_call(x, mods, norm_g, w_in, cw, w_out, state, layer, n_tok):
    n_seq = x.shape[0]
    j = layer // 2
    return pl.pallas_call(
        functools.partial(_conv_sample_kernel, n_tok=n_tok),
        grid=(n_seq // SAMPLE_SEQ_BLOCK,),
        in_specs=[
            _seq_rows(n_tok * D_MODEL),
            _seq_rows(3 * D_MODEL, layer),
            _resident((None, 1, D_MODEL), (layer, 0, 0), 1),
            _resident((None, D_MODEL, 4 * E_CONV), (j, 0, 0), 1),
            _resident((None, CONV_WIDTH, E_CONV), (j, 0, 0), 1),
            _resident((None, E_CONV, D_MODEL), (j, 0, 0), 1),
            _seq_rows((CONV_WIDTH - 1) * E_CONV, j),
        ],
        out_specs=[_seq_rows(n_tok * D_MODEL), _seq_rows((CONV_WIDTH - 1) * E_CONV)],
        out_shape=[
            jax.ShapeDtypeStruct((n_seq, n_tok * D_MODEL), F32),
            jax.ShapeDtypeStruct((n_seq, (CONV_WIDTH - 1) * E_CONV), F32),
        ],
        scratch_shapes=[pltpu.VMEM((n_tok * SAMPLE_SEQ_BLOCK, E_CONV), BF16)],
        compiler_params=_params(1),
        name="conv_sample",
    )(x, mods, norm_g, w_in, cw, w_out, state)


def _rows_from_blocks(vecs):
    return jnp.concatenate([jnp.broadcast_to(v, (SUB, v.shape[1])) for v in vecs], axis=0)


def _chunk_masks():
    r = lax.broadcasted_iota(jnp.int32, (CHUNK, CHUNK), 0)
    c = lax.broadcasted_iota(jnp.int32, (CHUNK, CHUNK), 1)
    causal = (r // SUB == c // SUB) & (c <= r)
    tri = jnp.where(causal, 1.0, 0.0).astype(BF16)
    first_cols = lax.broadcasted_iota(jnp.int32, (SUB, CHUNK), 1) < SUB
    return causal, first_cols, jnp.concatenate([tri, tri], axis=1)


def _chunk_decay(lf, tri2):
    hi = lf.astype(BF16)
    lo = (lf - hi.astype(F32)).astype(BF16)
    return _dot(tri2, jnp.concatenate([hi, lo], axis=0))


def _chunk_key_operands(kk, gl):
    t0 = gl[SUB - 1:SUB, :]
    t1 = gl[2 * SUB - 1:2 * SUB, :]
    k_mid = kk * jnp.exp2(_rows_from_blocks([0.5 * t0, 0.5 * t1]) - gl)
    k_state = k_mid * _rows_from_blocks([jnp.exp2(0.5 * t0 + t1), jnp.exp2(0.5 * t1)])
    return k_mid.astype(BF16), k_state.astype(BF16), jnp.exp2(t0 + t1), jnp.minimum(t0, t1)


def _chunk_query_operands(qf, gl):
    t0 = gl[SUB - 1:SUB, :]
    t1 = gl[2 * SUB - 1:2 * SUB, :]
    q_mid = qf * jnp.exp2(gl - _rows_from_blocks([0.5 * t0, 0.5 * t1]))
    q_next = q_mid[SUB:] * jnp.exp2(0.5 * (t0 + t1))
    q_state = q_mid * _rows_from_blocks([jnp.exp2(0.5 * t0), jnp.exp2(0.5 * t1 + t0)])
    return jnp.concatenate([q_mid, q_next], axis=0).astype(BF16), q_state.astype(BF16)


def _chunk_finish(ast, qe, kr, vb, ptot, s_old, causal, first_cols):
    same = jnp.where(causal, ast[0:CHUNK], 0.0)
    a = jnp.concatenate([same[:SUB], jnp.where(first_cols, ast[CHUNK:], same[SUB:])], axis=0)
    o = _dot(jnp.concatenate([qe, a.astype(BF16)], axis=1),
             jnp.concatenate([s_old.astype(BF16), vb], axis=0))
    dec = jnp.broadcast_to(ptot, (HG_DK, HG_DK)).T
    return o, s_old * dec + _dot_tn(kr, vb)


def _exact_recurrence(q_ref, k_ref, f_ref, v_ref, o_ref, s0, n_rows):
    def col(tile, r):
        return jnp.broadcast_to(tile[r:r + 1, :], (HG_DK, HG_DK)).T

    def step(i, s):
        rows = pl.ds(pl.multiple_of(i * SUBLANE, SUBLANE), SUBLANE)
        q, k, f, v = q_ref[rows, :], k_ref[rows, :], f_ref[rows, :], v_ref[rows, :]
        out = []
        for r in range(SUBLANE):
            s = s * col(f, r) + col(k, r) * v[r:r + 1, :]
            out.append(jnp.sum(col(q, r) * s, axis=0, keepdims=True))
        o_ref[rows, :] = jnp.concatenate(out, axis=0)
        return s
    return lax.fori_loop(0, n_rows // SUBLANE, step, s0)


def _decode_state_update(p, el_row0, dq_ref, dk_ref, dv_ref, del_ref, ds_ref, do_ref, dso_ref, partial_o):
    nd = ds_ref.shape[0]
    n_tok = dq_ref.shape[0] // nd
    hd, sq = divmod(p, nd)
    hs = slice(hd * HG_DK, (hd + 1) * HG_DK)
    mine = lax.broadcasted_iota(jnp.int32, (dq_ref.shape[0], HG_DK), 0) // n_tok == sq
    s_old = ds_ref[sq, hd]
    o = _dot(dq_ref[:, hs], s_old.astype(BF16))
    partial_o[hd] = jnp.where(mine, o, partial_o[hd]) if hd in partial_o else jnp.where(mine, o, 0.0)
    k_mine = jnp.where(mine, dk_ref[:, hs], 0.0).astype(BF16)
    dec = jnp.broadcast_to(del_ref[pl.ds(el_row0 + sq, 1), :][:, hs], (HG_DK, HG_DK)).T
    dso_ref[sq, hd] = s_old * dec + _dot_tn(k_mine, dv_ref[:, hs])
    if sq == nd - 1:
        do_ref[:, hs] = partial_o.pop(hd)


def _hgrn_prompt_kernel(*refs, tb, gb, layer, final, has_prev):
    n_in = 13 + int(has_prev)
    x_ref, mod_ref, g_ref, win_ref, lbraw_ref, og_ref, wout_ref, fg_ref = refs[:8]
    decode_in = refs[8:13]
    xo_ref, s_ref, do_ref, dso_ref = refs[n_in:n_in + 4]
    q_scr, lf_scr, k_scr, v_scr, z_scr, o_scr, s0_scr = refs[n_in + 4:]
    b = pl.program_id(0)
    t = pl.program_id(1)

    @pl.when(t == 0)
    def _():
        s_ref[...] = jnp.zeros_like(s_ref)

    s0_scr[...] = s_ref[...]

    seqs = range(gb)
    lb = _lower_bound(lbraw_ref[...], layer)
    x = [x_ref[g] for g in seqs]
    mod = [_split_mod(mod_ref[pl.ds(b * gb + g, 1), :]) for g in seqs]
    h = [_mod_norm_seq(x[g], g_ref[...], mod[g][0], mod[g][1]) for g in seqs]
    for g in seqs:
        q_scr[g] = _silu(_dot(h[g], win_ref[:, 0:HG_F]))
    for g in seqs:
        logf, kk = _gate_math(_dot(h[g], win_ref[:, HG_F:2 * HG_F]), lb)
        lf_scr[g] = logf
        k_scr[g] = kk
    for g in seqs:
        v_scr[g] = _dot(h[g], win_ref[:, 2 * HG_F:2 * HG_F + HG_I])
    for g in seqs:
        z_scr[g] = _dot(h[g], win_ref[:, 2 * HG_F + HG_I:])

    causal, first_cols, tri2 = _chunk_masks()
    chains = [(c, hd, g) for c in range(tb // CHUNK) for hd in range(HG_HEADS) for g in seqs]
    ops, scores = {}, {}
    span = jnp.zeros((1, HG_DK), F32)

    def where(n):
        c, hd, g = chains[n]
        return g, hd, slice(c * CHUNK, (c + 1) * CHUNK), slice(hd * HG_DK, (hd + 1) * HG_DK)

    block_seqs = decode_in[4].shape[0]
    n_decode = HG_HEADS * block_seqs
    blocks_per_el = decode_in[3].shape[0] // block_seqs
    el_row0 = ((b * pl.num_programs(1) + t) % blocks_per_el) * block_seqs
    partial_o = {}
    for step in range(max(len(chains) + 2 * HGRN_PIPE_SKEW, n_decode)):
        if step < n_decode:
            _decode_state_update(step, el_row0, *decode_in, do_ref, dso_ref, partial_o)
        n = step
        if n < len(chains):
            g, hd, rows, hs = where(n)
            gl = _chunk_decay(lf_scr[g, rows, hs], tri2)
            km, kr, ptot, tmin = _chunk_key_operands(k_scr[g, rows, hs], gl)
            lhs, qe = _chunk_query_operands(q_scr[g, rows, hs], gl)
            ops[n] = (lhs, km, qe, kr, ptot)
            span = jnp.minimum(span, tmin)
        n = step - HGRN_PIPE_SKEW
        if 0 <= n < len(chains):
            scores[n] = _dot_nt(ops[n][0], ops[n][1])
        n = step - 2 * HGRN_PIPE_SKEW
        if 0 <= n < len(chains):
            g, hd, rows, hs = where(n)
            _, _, qe, kr, ptot = ops.pop(n)
            vb = v_scr[g, rows, hs].astype(BF16)
            o, s_new = _chunk_finish(scores.pop(n), qe, kr, vb, ptot, s_ref[g, hd], causal, first_cols)
            o_scr[g, rows, hs] = o
            s_ref[g, hd] = s_new

    y = [_head_norm_gate(o_scr[g], z_scr[g], og_ref[...]) for g in seqs]
    for g in seqs:
        xn = x[g] + mod[g][2] * _dot(y[g], wout_ref[...])
        if final:
            xn = _rmsnorm(xn, fg_ref[...])
        xo_ref[g] = xn

    @pl.when(jnp.min(span) < -HGRN_SAFE_LOG2_SPAN)
    def _():
        def exact(q_tmp, k_tmp, f_tmp, v_tmp, o_tmp):
            def one_seq(g, carry):
                shift, scale, gate = _split_mod(mod_ref[pl.ds(b * gb + g, 1), :])
                hg = _mod_norm_seq(x_ref[g], g_ref[...], shift, scale)
                qf = _silu(_dot(hg, win_ref[:, 0:HG_F]))
                logf, kk = _gate_math(_dot(hg, win_ref[:, HG_F:2 * HG_F]), lb)
                ff = jnp.exp2(logf)
                vv = _dot(hg, win_ref[:, 2 * HG_F:2 * HG_F + HG_I])
                for hd in range(HG_HEADS):
                    hs = slice(hd * HG_DK, (hd + 1) * HG_DK)
                    q_tmp[hd], k_tmp[hd], f_tmp[hd], v_tmp[hd] = qf[:, hs], kk[:, hs], ff[:, hs], vv[:, hs]

                def one_head(hd, c2):
                    s_ref[g, hd] = _exact_recurrence(q_tmp.at[hd], k_tmp.at[hd], f_tmp.at[hd], v_tmp.at[hd],
                                                     o_tmp.at[hd], s0_scr[g, hd], tb)
                    return c2

                lax.fori_loop(0, HG_HEADS, one_head, 0)
                o = jnp.concatenate([o_tmp[hd] for hd in range(HG_HEADS)], axis=1)
                y = _head_norm_gate(o, z_scr[g], og_ref[...])
                xn = x_ref[g] + gate * _dot(y, wout_ref[...])
                if final:
                    xn = _rmsnorm(xn, fg_ref[...])
                xo_ref[g] = xn
                return carry

            lax.fori_loop(0, gb, one_seq, 0)

        pl.run_scoped(exact, *[pltpu.VMEM((HG_HEADS, tb, HG_DK), F32)] * 5)


def _hgrn_prompt_call(x, mods, norm_g, w_in, lb_raw, og, w_out, fg, decode, layer, mod_block):
    bsz, seq, _ = x.shape
    tb = HGRN_TB
    gb = HGRN_GB
    j = layer // 2
    final = layer == DEPTH - 1
    n_t = seq // tb
    qe, kr, v, el, states, prev_states = decode
    n_blocks = (bsz // gb) * n_t
    n_seq = el.shape[0]
    block_seqs = n_seq // n_blocks
    dec_rows = qe.shape[0] // n_blocks
    el_rows = max(block_seqs, SUBLANE)
    assert n_seq == n_blocks * block_seqs and el_rows % block_seqs == 0 and dec_rows % SUBLANE == 0
    dec_spec = pl.BlockSpec((dec_rows, HG_F), lambda b, t: (b * n_t + t, 0))
    st_spec = pl.BlockSpec((None, block_seqs, HG_HEADS, HG_DK, HG_DV), lambda b, t: (j, b * n_t + t, 0, 0, 0))
    in_specs = [
        pl.BlockSpec((gb, tb, D_MODEL), lambda b, t: (b, t, 0)),
        _resident((None, bsz, 3 * D_MODEL), (layer, mod_block, 0), 2),
        _resident((None, 1, D_MODEL), (layer, 0, 0), 2),
        _resident((None, D_MODEL, 2 * HG_F + 2 * HG_I), (j, 0, 0), 2),
        _resident((DEPTH, HG_F), (0, 0), 2),
        _resident((None, 1, HG_I), (j, 0, 0), 2),
        _resident((None, HG_I, D_MODEL), (j, 0, 0), 2),
        _resident((1, D_MODEL), (0, 0), 2),
        dec_spec, dec_spec, dec_spec,
        pl.BlockSpec((el_rows, HG_F), lambda b, t: ((b * n_t + t) * block_seqs // el_rows, 0)),
        st_spec,
    ]
    args = [x, mods, norm_g, w_in, lb_raw, og, w_out, fg, qe, kr, v, el, states]
    aliases = {}
    if prev_states is not None:
        in_specs.append(pl.BlockSpec(memory_space=pl.ANY))
        args.append(prev_states)
        aliases = {len(args) - 1: 3}
    return pl.pallas_call(
        functools.partial(_hgrn_prompt_kernel, tb=tb, gb=gb, layer=layer, final=final,
                          has_prev=prev_states is not None),
        grid=(bsz // gb, n_t),
        in_specs=in_specs,
        out_specs=[
            pl.BlockSpec((gb, tb, D_MODEL), lambda b, t: (b, t, 0)),
            pl.BlockSpec((gb, HG_HEADS, HG_DK, HG_DV), lambda b, t: (b, 0, 0, 0)),
            dec_spec,
            st_spec,
        ],
        out_shape=[
            jax.ShapeDtypeStruct((bsz, seq, D_MODEL), F32),
            jax.ShapeDtypeStruct((bsz, HG_HEADS, HG_DK, HG_DV), F32),
            jax.ShapeDtypeStruct(qe.shape, F32),
            jax.ShapeDtypeStruct(states.shape, F32),
        ],
        scratch_shapes=[pltpu.VMEM((gb, tb, HG_F), F32)] * 6
        + [pltpu.VMEM((gb, HG_HEADS, HG_DK, HG_DV), F32)],
        input_output_aliases=aliases,
        compiler_params=_params(2),
        name="hgrn_prompt",
    )(*args)


def _token_row_permutation(n_seq, n_tok, seq_major_out):
    n = n_seq * n_tok
    r = lax.broadcasted_iota(jnp.int32, (n, n), 0)
    c = lax.broadcasted_iota(jnp.int32, (n, n), 1)
    sm, tm = (r, c) if seq_major_out else (c, r)
    return jnp.where(tm == (sm % n_tok) * n_seq + sm // n_tok, 1.0, 0.0).astype(BF16)


def _permute_rows_f32(p, a):
    hi = a.astype(BF16)
    rest = a - hi.astype(F32)
    mid = rest.astype(BF16)
    lo = (rest - mid.astype(F32)).astype(BF16)
    return (_dot(p, hi) + _dot(p, mid)) + _dot(p, lo)


def _hgrn_sample_pre_kernel(x_ref, mod_ref, g_ref, win_ref, lbraw_ref,
                            qe_ref, kr_ref, v_ref, el_ref, oi_ref, z_ref, *, n_tok, layer):
    ns = x_ref.shape[0]
    h = _sample_hidden(x_ref, mod_ref, g_ref, n_tok)
    lb = _lower_bound(lbraw_ref[...], layer)
    qf = _silu(_dot(h, win_ref[:, 0:HG_F]))
    lf, kk = _gate_math(_dot(h, win_ref[:, HG_F:2 * HG_F]), lb)
    vv = _dot(h, win_ref[:, 2 * HG_F:2 * HG_F + HG_I])
    z = _dot(h, win_ref[:, 2 * HG_F + HG_I:])

    def tok(a, s):
        return a[s * ns:(s + 1) * ns]

    g = [tok(lf, 0)]
    for s in range(1, n_tok):
        g.append(g[-1] + tok(lf, s))
    el_ref[...] = jnp.exp2(g[-1])
    to_seq_major = _token_row_permutation(ns, n_tok, seq_major_out=True)
    qe = jnp.concatenate([tok(qf, s) * jnp.exp2(g[s]) for s in range(n_tok)], axis=0)
    kr = jnp.concatenate([tok(kk, s) * jnp.exp2(g[-1] - g[s]) for s in range(n_tok)], axis=0)
    qe_ref[...] = _dot(to_seq_major, qe.astype(BF16)).astype(BF16)
    kr_ref[...] = _dot(to_seq_major, kr.astype(BF16)).astype(BF16)
    v_ref[...] = _dot(to_seq_major, vv.astype(BF16)).astype(BF16)
    for s in range(n_tok):
        z_ref[:, _tok(s, HG_I)] = tok(z, s)

    for t in range(n_tok):
        acc = jnp.zeros((ns, HG_I), F32)
        for s in range(t + 1):
            e = tok(qf, t) * tok(kk, s)
            if s < t:
                e = e * jnp.exp2(g[t] - g[s])
            vs = tok(vv, s)
            parts = []
            for hd in range(HG_HEADS):
                hs = slice(hd * HG_DK, (hd + 1) * HG_DK)
                parts.append(jnp.sum(e[:, hs], axis=-1, keepdims=True) * vs[:, hs])
            acc = acc + jnp.concatenate(parts, axis=1)
        oi_ref[:, _tok(t, HG_I)] = acc


def _hgrn_sample_pre_call(x, mods, norm_g, w_in, lb_raw, layer, n_tok):
    n_seq = x.shape[0]
    j = layer // 2
    act = jax.ShapeDtypeStruct((n_seq, n_tok * HG_F), F32)
    tok_rows = jax.ShapeDtypeStruct((n_seq * n_tok, HG_F), BF16)
    tok_spec = pl.BlockSpec((SAMPLE_SEQ_BLOCK * n_tok, HG_F), lambda i: (i, 0))
    return pl.pallas_call(
        functools.partial(_hgrn_sample_pre_kernel, n_tok=n_tok, layer=layer),
        grid=(n_seq // SAMPLE_SEQ_BLOCK,),
        in_specs=[
            _seq_rows(n_tok * D_MODEL),
            _seq_rows(3 * D_MODEL, layer),
            _resident((None, 1, D_MODEL), (layer, 0, 0), 1),
            _resident((None, D_MODEL, 2 * HG_F + 2 * HG_I), (j, 0, 0), 1),
            _resident((DEPTH, HG_F), (0, 0), 1),
        ],
        out_specs=[tok_spec] * 3 + [_seq_rows(HG_F)] + [_seq_rows(n_tok * HG_I)] * 2,
        out_shape=[tok_rows, tok_rows, tok_rows, jax.ShapeDtypeStruct((n_seq, HG_F), F32), act, act],
        compiler_params=_params(1),
        name="hgrn_sample_pre",
    )(x, mods, norm_g, w_in, lb_raw)


def _hgrn_sample_post_kernel(x_ref, mod_ref, oa_ref, ob_ref, z_ref, og_ref, wout_ref, fg_ref, xo_ref,
                             *, n_tok, final):
    ns = x_ref.shape[0]
    o_state = _permute_rows_f32(_token_row_permutation(ns, n_tok, seq_major_out=False), oa_ref[...])
    o = o_state + _stack_tokens(ob_ref, n_tok, HG_I)
    y = _head_norm_gate(o, _stack_tokens(z_ref, n_tok, HG_I), og_ref[...])
    out = _dot(y, wout_ref[...])
    gate = mod_ref[:, 2 * D_MODEL:]
    for s in range(n_tok):
        xn = x_ref[:, _tok(s, D_MODEL)] + gate * out[s * ns:(s + 1) * ns]
        if final:
            xn = _rmsnorm(xn, fg_ref[...])
        xo_ref[:, _tok(s, D_MODEL)] = xn


def _hgrn_sample_post_call(x, mods, o_inter, o_intra, z, og, w_out, fg, layer, n_tok):
    n_seq = x.shape[0]
    j = layer // 2
    return pl.pallas_call(
        functools.partial(_hgrn_sample_post_kernel, n_tok=n_tok, final=layer == DEPTH - 1),
        grid=(n_seq // SAMPLE_SEQ_BLOCK,),
        in_specs=[
            _seq_rows(n_tok * D_MODEL),
            _seq_rows(3 * D_MODEL, layer),
            pl.BlockSpec((SAMPLE_SEQ_BLOCK * n_tok, HG_I), lambda i: (i, 0)),
            _seq_rows(n_tok * HG_I), _seq_rows(n_tok * HG_I),
            _resident((None, 1, HG_I), (j, 0, 0), 1),
            _resident((None, HG_I, D_MODEL), (j, 0, 0), 1),
            _resident((1, D_MODEL), (0, 0), 1),
        ],
        out_specs=_seq_rows(n_tok * D_MODEL),
        out_shape=jax.ShapeDtypeStruct((n_seq, n_tok * D_MODEL), F32),
        compiler_params=_params(1),
        name="hgrn_sample_post",
    )(x, mods, o_inter, o_intra, z, og, w_out, fg)


def kernel(x_prompt, x_sample, state_conv, state_hgrn, c_prompt, c_sample, norm_g, w_ada, b_ada, conv_w_in, conv_w, conv_w_out, hgrn_w_in, hgrn_lower_bounds, hgrn_onorm_g, hgrn_w_out, final_norm_g):
    n_p = x_prompt.shape[0]
    n_s, t_s, _ = x_sample.shape
    assert t_s >= CONV_WIDTH - 1 and n_s % SAMPLE_SEQ_BLOCK == 0 and n_s % n_p == 0

    mods = _ada_call(jnp.concatenate([c_sample, c_prompt], axis=0), w_ada, b_ada)
    mod_block_p = n_s // n_p

    conv_w_in_b = conv_w_in.astype(BF16)
    conv_w_out_b = conv_w_out.astype(BF16)
    hgrn_w_in_b = hgrn_w_in.astype(BF16)
    hgrn_w_out_b = hgrn_w_out.astype(BF16)
    norm_g3 = norm_g.reshape(DEPTH, 1, D_MODEL)
    og3 = hgrn_onorm_g.reshape(DEPTH // 2, 1, HG_I)
    fg = final_norm_g.reshape(1, D_MODEL)
    state_conv2 = state_conv.reshape(state_conv.shape[0], n_s, (CONV_WIDTH - 1) * E_CONV)

    xp = x_prompt
    xs = x_sample.reshape(n_s, t_s * D_MODEL)
    conv_p, conv_s, hgrn_p = [], [], []
    hgrn_s = None
    for layer in range(DEPTH):
        if layer % 2 == 0:
            xp, tail = _conv_prompt_call(xp, mods, norm_g3, conv_w_in_b, conv_w, conv_w_out_b, layer, mod_block_p)
            conv_p.append(tail[:, SUBLANE - (CONV_WIDTH - 1):])
            xs, st = _conv_sample_call(xs, mods, norm_g3, conv_w_in_b, conv_w, conv_w_out_b, state_conv2,
                                       layer, t_s)
            conv_s.append(st.reshape(n_s, CONV_WIDTH - 1, E_CONV))
        else:
            qe, kr, v, el, o_intra, z = _hgrn_sample_pre_call(xs, mods, norm_g3, hgrn_w_in_b,
                                                              hgrn_lower_bounds, layer, t_s)
            decode = (qe, kr, v, el, state_hgrn, hgrn_s)
            xp, s_new, o_inter, hgrn_s = _hgrn_prompt_call(xp, mods, norm_g3, hgrn_w_in_b, hgrn_lower_bounds,
                                                           og3, hgrn_w_out_b, fg, decode, layer, mod_block_p)
            hgrn_p.append(s_new)
            xs = _hgrn_sample_post_call(xs, mods, o_inter, o_intra, z, og3, hgrn_w_out_b, fg, layer, t_s)
    return (xp, xs.reshape(n_s, t_s, D_MODEL), jnp.stack(conv_p), jnp.stack(hgrn_p), jnp.stack(conv_s), hgrn_s)
```

```python
import functools

import jax
import jax.numpy as jnp
from jax import lax
from jax.experimental import pallas as pl
from jax.experimental.pallas import tpu as pltpu

F32 = jnp.float32
BF16 = jnp.bfloat16

D_MODEL = 1024
DEPTH = 4
CONV_WIDTH = 3
E_CONV = D_MODEL
HG_HEADS = 8
HG_DK = 128
HG_DV = 128
HG_F = HG_HEADS * HG_DK
HG_I = HG_HEADS * HG_DV
EPS = 1e-6

SUBLANE = 8
CHUNK = 128
SUB = CHUNK // 2
CONV_COL_TILE = 256
CONV_TB = 512
CONV_GB = 2
HGRN_TB = 256
HGRN_GB = 2
HGRN_PIPE_SKEW = 3
HGRN_SAFE_LOG2_SPAN = 200.0
SAMPLE_SEQ_BLOCK = 32
VMEM_LIMIT = 56 * 1024 * 1024


def _dot(a, b):
    return jnp.dot(a, b, preferred_element_type=F32)


def _dot_nt(a, b):
    return lax.dot_general(a, b, (((1,), (1,)), ((), ())), preferred_element_type=F32)


def _dot_tn(a, b):
    return lax.dot_general(a, b, (((0,), (0,)), ((), ())), preferred_element_type=F32)


def _silu(x):
    return x * jax.nn.sigmoid(x)


def _rmsnorm(x, g):
    ms = jnp.mean(x * x, axis=-1, keepdims=True)
    return x * lax.rsqrt(ms + EPS) * g


def _split_mod(mod):
    return mod[:, :D_MODEL], mod[:, D_MODEL:2 * D_MODEL], mod[:, 2 * D_MODEL:]


def _mod_norm(x, g, shift, scale):
    return (_rmsnorm(x, g) * (1.0 + scale) + shift).astype(BF16)


def _mod_norm_seq(x, g, shift, scale):
    ms = jnp.mean(x * x, axis=-1, keepdims=True)
    return (x * lax.rsqrt(ms + EPS) * (g * (1.0 + scale)) + shift).astype(BF16)


def _lower_bound(raw, layer):
    rows = [raw[i:i + 1, :] for i in range(DEPTH)]
    m = functools.reduce(jnp.maximum, rows)
    es = [jnp.exp(r - m) for r in rows]
    tot = functools.reduce(lambda a, b: a + b, es)
    acc = es[1]
    for i in range(2, layer + 1):
        acc = acc + es[i]
    return acc / tot


def _gate_math(fpre, lb):
    t = jnp.exp(-jnp.abs(fpre))
    r = 1.0 / (1.0 + t)
    pos = fpre >= 0.0
    sig = jnp.where(pos, 1.0, t) * r
    sig_neg = jnp.where(pos, t, 1.0) * r
    return jnp.log2(lb + (1.0 - lb) * sig), (1.0 - lb) * sig_neg


def _head_norm_gate(o, z, og):
    parts = []
    for hd in range(HG_HEADS):
        hs = slice(hd * HG_DV, (hd + 1) * HG_DV)
        oh = o[:, hs]
        ms = jnp.mean(oh * oh, axis=-1, keepdims=True)
        parts.append(oh * lax.rsqrt(ms + EPS))
    on = jnp.concatenate(parts, axis=1) * og
    return (on * _silu(z)).astype(BF16)


def _params(n_grid):
    return pltpu.CompilerParams(dimension_semantics=("arbitrary",) * n_grid, vmem_limit_bytes=VMEM_LIMIT)


def _resident(shape, index, n_grid):
    if n_grid == 1:
        return pl.BlockSpec(shape, lambda i: index, pipeline_mode=pl.Buffered(1))
    return pl.BlockSpec(shape, lambda b, t: index, pipeline_mode=pl.Buffered(1))


def _ada_kernel(c_ref, w_ref, b_ref, o_ref):
    s = _silu(c_ref[...]).astype(BF16)
    o_ref[...] = _dot(s, w_ref[...].astype(BF16)) + b_ref[...]


def _ada_call(c_all, w_ada, b_ada):
    rows = c_all.shape[0]
    n_col = 3
    col = 3 * D_MODEL // n_col
    return pl.pallas_call(
        _ada_kernel,
        grid=(DEPTH, n_col),
        in_specs=[
            pl.BlockSpec((rows, D_MODEL), lambda l, n: (0, 0)),
            pl.BlockSpec((None, D_MODEL, col), lambda l, n: (l, 0, n)),
            pl.BlockSpec((None, 1, col), lambda l, n: (l, 0, n)),
        ],
        out_specs=pl.BlockSpec((None, rows, col), lambda l, n: (l, 0, n)),
        out_shape=jax.ShapeDtypeStruct((DEPTH, rows, 3 * D_MODEL), F32),
        compiler_params=_params(2),
        name="ada_mod",
    )(c_all, w_ada, b_ada.reshape(DEPTH, 1, 3 * D_MODEL))


def _conv_columns(h, win_ref, cw_ref, y_scr, prev_fn, tail_fn):
    ct = CONV_COL_TILE
    for n in range(E_CONV // ct):
        def col(j):
            return slice(j * E_CONV + n * ct, j * E_CONV + (n + 1) * ct)
        cs = slice(n * ct, (n + 1) * ct)
        b_gate = _dot(h, win_ref[:, col(0)])
        c_gate = _dot(h, win_ref[:, col(1)])
        v = _dot(h, win_ref[:, col(2)])
        z = _dot(h, win_ref[:, col(3)])
        u = c_gate * v
        p1, p2 = prev_fn(cs, u)
        conv = cw_ref[0:1, cs] * p2 + cw_ref[1:2, cs] * p1 + cw_ref[2:3, cs] * u
        y_scr[:, cs] = (b_gate * conv * _silu(z)).astype(BF16)
        tail_fn(cs, u)


def _conv_prompt_kernel(x_ref, mod_ref, g_ref, win_ref, cw_ref, wout_ref, xo_ref, st_ref, y_scr, *, tb, gb):
    b = pl.program_id(0)
    t = pl.program_id(1)
    ct = CONV_COL_TILE

    @pl.when(t == 0)
    def _():
        st_ref[...] = jnp.zeros_like(st_ref)

    seqs = range(gb)
    x = [x_ref[g] for g in seqs]
    mod = [_split_mod(mod_ref[pl.ds(b * gb + g, 1), :]) for g in seqs]
    h = [_mod_norm_seq(x[g], g_ref[...], mod[g][0], mod[g][1]) for g in seqs]
    row = lax.broadcasted_iota(jnp.int32, (tb, ct), 0)

    for g in seqs:
        def prev_fn(cs, u, g=g):
            c0 = st_ref[g, SUBLANE - 2:SUBLANE - 1, cs]
            c1 = st_ref[g, SUBLANE - 1:SUBLANE, cs]
            p1 = jnp.where(row == 0, c1, pltpu.roll(u, 1, 0))
            p2 = jnp.where(row == 0, c0, jnp.where(row == 1, c1, pltpu.roll(u, 2, 0)))
            return p1, p2

        def tail_fn(cs, u, g=g):
            st_ref[g, :, cs] = u[tb - SUBLANE:tb, :]

        _conv_columns(h[g], win_ref, cw_ref, y_scr.at[g], prev_fn, tail_fn)
    for g in seqs:
        xo_ref[g] = x[g] + mod[g][2] * _dot(y_scr[g], wout_ref[...])


def _conv_prompt_call(x, mods, norm_g, w_in, cw, w_out, layer, mod_block):
    bsz, seq, _ = x.shape
    tb = CONV_TB
    gb = CONV_GB
    j = layer // 2
    return pl.pallas_call(
        functools.partial(_conv_prompt_kernel, tb=tb, gb=gb),
        grid=(bsz // gb, seq // tb),
        in_specs=[
            pl.BlockSpec((gb, tb, D_MODEL), lambda b, t: (b, t, 0)),
            _resident((None, bsz, 3 * D_MODEL), (layer, mod_block, 0), 2),
            _resident((None, 1, D_MODEL), (layer, 0, 0), 2),
            _resident((None, D_MODEL, 4 * E_CONV), (j, 0, 0), 2),
            _resident((None, CONV_WIDTH, E_CONV), (j, 0, 0), 2),
            _resident((None, E_CONV, D_MODEL), (j, 0, 0), 2),
        ],
        out_specs=[
            pl.BlockSpec((gb, tb, D_MODEL), lambda b, t: (b, t, 0)),
            pl.BlockSpec((gb, SUBLANE, E_CONV), lambda b, t: (b, 0, 0)),
        ],
        out_shape=[
            jax.ShapeDtypeStruct((bsz, seq, D_MODEL), F32),
            jax.ShapeDtypeStruct((bsz, SUBLANE, E_CONV), F32),
        ],
        scratch_shapes=[pltpu.VMEM((gb, tb, E_CONV), BF16)],
        compiler_params=_params(2),
        name="conv_prompt",
    )(x, mods, norm_g, w_in, cw, w_out)


def _tok(s, width):
    return slice(s * width, (s + 1) * width)


def _stack_tokens(ref, n_tok, width, cols=None):
    parts = []
    for s in range(n_tok):
        lo = s * width + (0 if cols is None else cols.start)
        hi = s * width + (width if cols is None else cols.stop)
        parts.append(ref[:, lo:hi])
    return jnp.concatenate(parts, axis=0)


def _sample_hidden(x_ref, mod_ref, g_ref, n_tok):
    shift, scale, _ = _split_mod(mod_ref[...])
    g = g_ref[...]
    return jnp.concatenate(
        [_mod_norm(x_ref[:, _tok(s, D_MODEL)], g, shift, scale) for s in range(n_tok)], axis=0)


def _conv_sample_kernel(x_ref, mod_ref, g_ref, win_ref, cw_ref, wout_ref, st_ref,
                        xo_ref, so_ref, y_scr, *, n_tok):
    ns = x_ref.shape[0]
    h = _sample_hidden(x_ref, mod_ref, g_ref, n_tok)

    def prev_fn(cs, u):
        st0 = st_ref[:, cs]
        st1 = st_ref[:, E_CONV + cs.start:E_CONV + cs.stop]
        p1 = jnp.concatenate([st1, u[:(n_tok - 1) * ns]], axis=0)
        p2 = jnp.concatenate([st0, st1, u[:(n_tok - 2) * ns]], axis=0)
        return p1, p2

    def tail_fn(cs, u):
        so_ref[:, cs] = u[(n_tok - 2) * ns:(n_tok - 1) * ns]
        so_ref[:, E_CONV + cs.start:E_CONV + cs.stop] = u[(n_tok - 1) * ns:]

    _conv_columns(h, win_ref, cw_ref, y_scr, prev_fn, tail_fn)
    out = _dot(y_scr[...], wout_ref[...])
    gate = mod_ref[:, 2 * D_MODEL:]
    for s in range(n_tok):
        xo_ref[:, _tok(s, D_MODEL)] = x_ref[:, _tok(s, D_MODEL)] + gate * out[s * ns:(s + 1) * ns]


def _seq_rows(width, layer=None):
    if layer is None:
        return pl.BlockSpec((SAMPLE_SEQ_BLOCK, width), lambda i: (i, 0))
    return pl.BlockSpec((None, SAMPLE_SEQ_BLOCK, width), lambda i: (layer, i, 0))


def _conv_sample_call(x, mods, norm_g, w_in, cw, w_out, state, layer, n_tok):
    n_seq = x.shape[0]
    j = layer // 2
    return pl.pallas_call(
        functools.partial(_conv_sample_kernel, n_tok=n_tok),
        grid=(n_seq // SAMPLE_SEQ_BLOCK,),
        in_specs=[
            _seq_rows(n_tok * D_MODEL),
            _seq_rows(3 * D_MODEL, layer),
            _resident((None, 1, D_MODEL), (layer, 0, 0), 1),
            _resident((None, D_MODEL, 4 * E_CONV), (j, 0, 0), 1),
            _resident((None, CONV_WIDTH, E_CONV), (j, 0, 0), 1),
            _resident((None, E_CONV, D_MODEL), (j, 0, 0), 1),
            _seq_rows((CONV_WIDTH - 1) * E_CONV, j),
        ],
        out_specs=[_seq_rows(n_tok * D_MODEL), _seq_rows((CONV_WIDTH - 1) * E_CONV)],
        out_shape=[
            jax.ShapeDtypeStruct((n_seq, n_tok * D_MODEL), F32),
            jax.ShapeDtypeStruct((n_seq, (CONV_WIDTH - 1) * E_CONV), F32),
        ],
        scratch_shapes=[pltpu.VMEM((n_tok * SAMPLE_SEQ_BLOCK, E_CONV), BF16)],
        compiler_params=_params(1),
        name="conv_sample",
    )(x, mods, norm_g, w_in, cw, w_out, state)


def _rows_from_blocks(vecs):
    return jnp.concatenate([jnp.broadcast_to(v, (SUB, v.shape[1])) for v in vecs], axis=0)


def _chunk_masks():
    r = lax.broadcasted_iota(jnp.int32, (CHUNK, CHUNK), 0)
    c = lax.broadcasted_iota(jnp.int32, (CHUNK, CHUNK), 1)
    causal = (r // SUB == c // SUB) & (c <= r)
    tri = jnp.where(causal, 1.0, 0.0).astype(BF16)
    first_cols = lax.broadcasted_iota(jnp.int32, (SUB, CHUNK), 1) < SUB
    return causal, first_cols, jnp.concatenate([tri, tri], axis=1)


def _chunk_decay(lf, tri2):
    hi = lf.astype(BF16)
    lo = (lf - hi.astype(F32)).astype(BF16)
    return _dot(tri2, jnp.concatenate([hi, lo], axis=0))


def _chunk_key_operands(kk, gl):
    t0 = gl[SUB - 1:SUB, :]
    t1 = gl[2 * SUB - 1:2 * SUB, :]
    k_mid = kk * jnp.exp2(_rows_from_blocks([0.5 * t0, 0.5 * t1]) - gl)
    k_state = k_mid * _rows_from_blocks([jnp.exp2(0.5 * t0 + t1), jnp.exp2(0.5 * t1)])
    in_span = (t0 >= -HGRN_SAFE_LOG2_SPAN) & (t1 >= -HGRN_SAFE_LOG2_SPAN)
    return k_mid.astype(BF16), k_state.astype(BF16), jnp.exp2(t0 + t1), jnp.where(in_span, 0.0, 1.0)


def _chunk_query_operands(qf, gl):
    t0 = gl[SUB - 1:SUB, :]
    t1 = gl[2 * SUB - 1:2 * SUB, :]
    q_mid = qf * jnp.exp2(gl - _rows_from_blocks([0.5 * t0, 0.5 * t1]))
    q_next = q_mid[SUB:] * jnp.exp2(0.5 * (t0 + t1))
    q_state = q_mid * _rows_from_blocks([jnp.exp2(0.5 * t0), jnp.exp2(0.5 * t1 + t0)])
    return jnp.concatenate([q_mid, q_next], axis=0).astype(BF16), q_state.astype(BF16)


def _chunk_finish(ast, qe, kr, vb, ptot, s_old, causal, first_cols):
    same = jnp.where(causal, ast[0:CHUNK], 0.0)
    a = jnp.concatenate([same[:SUB], jnp.where(first_cols, ast[CHUNK:], same[SUB:])], axis=0)
    o = _dot(jnp.concatenate([qe, a.astype(BF16)], axis=1),
             jnp.concatenate([s_old.astype(BF16), vb], axis=0))
    dec = jnp.broadcast_to(ptot, (HG_DK, HG_DK)).T
    return o, s_old * dec + _dot_tn(kr, vb)


def _exact_recurrence(q_ref, k_ref, f_ref, v_ref, o_ref, s0, n_rows):
    def col(tile, r):
        return jnp.broadcast_to(tile[r:r + 1, :], (HG_DK, HG_DK)).T

    def step(i, s):
        rows = pl.ds(pl.multiple_of(i * SUBLANE, SUBLANE), SUBLANE)
        q, k, f, v = q_ref[rows, :], k_ref[rows, :], f_ref[rows, :], v_ref[rows, :]
        out = []
        for r in range(SUBLANE):
            s = s * col(f, r) + col(k, r) * v[r:r + 1, :]
            out.append(jnp.sum(col(q, r) * s, axis=0, keepdims=True))
        o_ref[rows, :] = jnp.concatenate(out, axis=0)
        return s
    return lax.fori_loop(0, n_rows // SUBLANE, step, s0)


def _decode_state_update(p, el_row0, dq_ref, dk_ref, dv_ref, del_ref, ds_ref, do_ref, dso_ref, partial_o):
    nd = ds_ref.shape[0]
    n_tok = dq_ref.shape[0] // nd
    hd, sq = divmod(p, nd)
    hs = slice(hd * HG_DK, (hd + 1) * HG_DK)
    mine = lax.broadcasted_iota(jnp.int32, (dq_ref.shape[0], HG_DK), 0) // n_tok == sq
    s_old = ds_ref[sq, hd]
    o = _dot(dq_ref[:, hs], s_old.astype(BF16))
    partial_o[hd] = jnp.where(mine, o, partial_o[hd]) if hd in partial_o else jnp.where(mine, o, 0.0)
    k_mine = jnp.where(mine, dk_ref[:, hs], 0.0).astype(BF16)
    dec = jnp.broadcast_to(del_ref[pl.ds(el_row0 + sq, 1), :][:, hs], (HG_DK, HG_DK)).T
    dso_ref[sq, hd] = s_old * dec + _dot_tn(k_mine, dv_ref[:, hs])
    if sq == nd - 1:
        do_ref[:, hs] = partial_o.pop(hd)


def _hgrn_prompt_kernel(*refs, tb, gb, layer, final, has_prev):
    n_in = 13 + int(has_prev)
    x_ref, mod_ref, g_ref, win_ref, lbraw_ref, og_ref, wout_ref, fg_ref = refs[:8]
    decode_in = refs[8:13]
    xo_ref, s_ref, do_ref, dso_ref = refs[n_in:n_in + 4]
    q_scr, lf_scr, k_scr, v_scr, z_scr, o_scr, s0_scr = refs[n_in + 4:]
    b = pl.program_id(0)
    t = pl.program_id(1)

    @pl.when(t == 0)
    def _():
        s_ref[...] = jnp.zeros_like(s_ref)

    s0_scr[...] = s_ref[...]

    seqs = range(gb)
    lb = _lower_bound(lbraw_ref[...], layer)
    x = [x_ref[g] for g in seqs]
    mod = [_split_mod(mod_ref[pl.ds(b * gb + g, 1), :]) for g in seqs]
    h = [_mod_norm_seq(x[g], g_ref[...], mod[g][0], mod[g][1]) for g in seqs]
    for g in seqs:
        q_scr[g] = _silu(_dot(h[g], win_ref[:, 0:HG_F]))
    for g in seqs:
        logf, kk = _gate_math(_dot(h[g], win_ref[:, HG_F:2 * HG_F]), lb)
        lf_scr[g] = logf
        k_scr[g] = kk
    for g in seqs:
        v_scr[g] = _dot(h[g], win_ref[:, 2 * HG_F:2 * HG_F + HG_I])
    for g in seqs:
        z_scr[g] = _dot(h[g], win_ref[:, 2 * HG_F + HG_I:])

    causal, first_cols, tri2 = _chunk_masks()
    chains = [(c, hd, g) for c in range(tb // CHUNK) for hd in range(HG_HEADS) for g in seqs]
    ops, scores = {}, {}
    unsafe = jnp.zeros((1, HG_DK), F32)

    def where(n):
        c, hd, g = chains[n]
        return g, hd, slice(c * CHUNK, (c + 1) * CHUNK), slice(hd * HG_DK, (hd + 1) * HG_DK)

    block_seqs = decode_in[4].shape[0]
    n_decode = HG_HEADS * block_seqs
    blocks_per_el = decode_in[3].shape[0] // block_seqs
    el_row0 = ((b * pl.num_programs(1) + t) % blocks_per_el) * block_seqs
    partial_o = {}
    for step in range(max(len(chains) + 2 * HGRN_PIPE_SKEW, n_decode)):
        if step < n_decode:
            _decode_state_update(step, el_row0, *decode_in, do_ref, dso_ref, partial_o)
        n = step
        if n < len(chains):
            g, hd, rows, hs = where(n)
            gl = _chunk_decay(lf_scr[g, rows, hs], tri2)
            km, kr, ptot, bad = _chunk_key_operands(k_scr[g, rows, hs], gl)
            lhs, qe = _chunk_query_operands(q_scr[g, rows, hs], gl)
            ops[n] = (lhs, km, qe, kr, ptot)
            unsafe = jnp.maximum(unsafe, bad)
        n = step - HGRN_PIPE_SKEW
        if 0 <= n < len(chains):
            scores[n] = _dot_nt(ops[n][0], ops[n][1])
        n = step - 2 * HGRN_PIPE_SKEW
        if 0 <= n < len(chains):
            g, hd, rows, hs = where(n)
            _, _, qe, kr, ptot = ops.pop(n)
            vb = v_scr[g, rows, hs].astype(BF16)
            o, s_new = _chunk_finish(scores.pop(n), qe, kr, vb, ptot, s_ref[g, hd], causal, first_cols)
            o_scr[g, rows, hs] = o
            s_ref[g, hd] = s_new

    y = [_head_norm_gate(o_scr[g], z_scr[g], og_ref[...]) for g in seqs]
    for g in seqs:
        xn = x[g] + mod[g][2] * _dot(y[g], wout_ref[...])
        if final:
            xn = _rmsnorm(xn, fg_ref[...])
        xo_ref[g] = xn

    @pl.when(jnp.max(unsafe) > 0.0)
    def _():
        def exact(q_tmp, k_tmp, f_tmp, v_tmp, o_tmp):
            def one_seq(g, carry):
                shift, scale, gate = _split_mod(mod_ref[pl.ds(b * gb + g, 1), :])
                hg = _mod_norm_seq(x_ref[g], g_ref[...], shift, scale)
                qf = _silu(_dot(hg, win_ref[:, 0:HG_F]))
                logf, kk = _gate_math(_dot(hg, win_ref[:, HG_F:2 * HG_F]), lb)
                ff = jnp.exp2(logf)
                vv = _dot(hg, win_ref[:, 2 * HG_F:2 * HG_F + HG_I])
                for hd in range(HG_HEADS):
                    hs = slice(hd * HG_DK, (hd + 1) * HG_DK)
                    q_tmp[hd], k_tmp[hd], f_tmp[hd], v_tmp[hd] = qf[:, hs], kk[:, hs], ff[:, hs], vv[:, hs]

                def one_head(hd, c2):
                    s_ref[g, hd] = _exact_recurrence(q_tmp.at[hd], k_tmp.at[hd], f_tmp.at[hd], v_tmp.at[hd],
                                                     o_tmp.at[hd], s0_scr[g, hd], tb)
                    return c2

                lax.fori_loop(0, HG_HEADS, one_head, 0)
                o = jnp.concatenate([o_tmp[hd] for hd in range(HG_HEADS)], axis=1)
                y = _head_norm_gate(o, z_scr[g], og_ref[...])
                xn = x_ref[g] + gate * _dot(y, wout_ref[...])
                if final:
                    xn = _rmsnorm(xn, fg_ref[...])
                xo_ref[g] = xn
                return carry

            lax.fori_loop(0, gb, one_seq, 0)

        pl.run_scoped(exact, *[pltpu.VMEM((HG_HEADS, tb, HG_DK), F32)] * 5)


def _hgrn_prompt_call(x, mods, norm_g, w_in, lb_raw, og, w_out, fg, decode, layer, mod_block):
    bsz, seq, _ = x.shape
    tb = HGRN_TB
    gb = HGRN_GB
    j = layer // 2
    final = layer == DEPTH - 1
    n_t = seq // tb
    qe, kr, v, el, states, prev_states = decode
    n_blocks = (bsz // gb) * n_t
    n_seq = el.shape[0]
    block_seqs = n_seq // n_blocks
    dec_rows = qe.shape[0] // n_blocks
    el_rows = max(block_seqs, SUBLANE)
    assert n_seq == n_blocks * block_seqs and el_rows % block_seqs == 0 and dec_rows % SUBLANE == 0
    dec_spec = pl.BlockSpec((dec_rows, HG_F), lambda b, t: (b * n_t + t, 0))
    st_spec = pl.BlockSpec((None, block_seqs, HG_HEADS, HG_DK, HG_DV), lambda b, t: (j, b * n_t + t, 0, 0, 0))
    in_specs = [
        pl.BlockSpec((gb, tb, D_MODEL), lambda b, t: (b, t, 0)),
        _resident((None, bsz, 3 * D_MODEL), (layer, mod_block, 0), 2),
        _resident((None, 1, D_MODEL), (layer, 0, 0), 2),
        _resident((None, D_MODEL, 2 * HG_F + 2 * HG_I), (j, 0, 0), 2),
        _resident((DEPTH, HG_F), (0, 0), 2),
        _resident((None, 1, HG_I), (j, 0, 0), 2),
        _resident((None, HG_I, D_MODEL), (j, 0, 0), 2),
        _resident((1, D_MODEL), (0, 0), 2),
        dec_spec, dec_spec, dec_spec,
        pl.BlockSpec((el_rows, HG_F), lambda b, t: ((b * n_t + t) * block_seqs // el_rows, 0)),
        st_spec,
    ]
    args = [x, mods, norm_g, w_in, lb_raw, og, w_out, fg, qe, kr, v, el, states]
    aliases = {}
    if prev_states is not None:
        in_specs.append(pl.BlockSpec(memory_space=pl.ANY))
        args.append(prev_states)
        aliases = {len(args) - 1: 3}
    return pl.pallas_call(
        functools.partial(_hgrn_prompt_kernel, tb=tb, gb=gb, layer=layer, final=final,
                          has_prev=prev_states is not None),
        grid=(bsz // gb, n_t),
        in_specs=in_specs,
        out_specs=[
            pl.BlockSpec((gb, tb, D_MODEL), lambda b, t: (b, t, 0)),
            pl.BlockSpec((gb, HG_HEADS, HG_DK, HG_DV), lambda b, t: (b, 0, 0, 0)),
            dec_spec,
            st_spec,
        ],
        out_shape=[
            jax.ShapeDtypeStruct((bsz, seq, D_MODEL), F32),
            jax.ShapeDtypeStruct((bsz, HG_HEADS, HG_DK, HG_DV), F32),
            jax.ShapeDtypeStruct(qe.shape, F32),
            jax.ShapeDtypeStruct(states.shape, F32),
        ],
        scratch_shapes=[pltpu.VMEM((gb, tb, HG_F), F32)] * 6
        + [pltpu.VMEM((gb, HG_HEADS, HG_DK, HG_DV), F32)],
        input_output_aliases=aliases,
        compiler_params=_params(2),
        name="hgrn_prompt",
    )(*args)


def _token_row_permutation(n_seq, n_tok, seq_major_out):
    n = n_seq * n_tok
    r = lax.broadcasted_iota(jnp.int32, (n, n), 0)
    c = lax.broadcasted_iota(jnp.int32, (n, n), 1)
    sm, tm = (r, c) if seq_major_out else (c, r)
    return jnp.where(tm == (sm % n_tok) * n_seq + sm // n_tok, 1.0, 0.0).astype(BF16)


def _permute_rows_f32(p, a):
    hi = a.astype(BF16)
    rest = a - hi.astype(F32)
    mid = rest.astype(BF16)
    lo = (rest - mid.astype(F32)).astype(BF16)
    return (_dot(p, hi) + _dot(p, mid)) + _dot(p, lo)


def _hgrn_sample_pre_kernel(x_ref, mod_ref, g_ref, win_ref, lbraw_ref,
                            qe_ref, kr_ref, v_ref, el_ref, oi_ref, z_ref, *, n_tok, layer):
    ns = x_ref.shape[0]
    h = _sample_hidden(x_ref, mod_ref, g_ref, n_tok)
    lb = _lower_bound(lbraw_ref[...], layer)
    qf = _silu(_dot(h, win_ref[:, 0:HG_F]))
    lf, kk = _gate_math(_dot(h, win_ref[:, HG_F:2 * HG_F]), lb)
    vv = _dot(h, win_ref[:, 2 * HG_F:2 * HG_F + HG_I])
    z = _dot(h, win_ref[:, 2 * HG_F + HG_I:])

    def tok(a, s):
        return a[s * ns:(s + 1) * ns]

    def decay(s, t):
        if s == t:
            return 1.0
        return jnp.exp2(functools.reduce(lambda a, b: a + b, [tok(lf, u) for u in range(s + 1, t + 1)]))

    last = n_tok - 1
    el_ref[...] = decay(-1, last)
    to_seq_major = _token_row_permutation(ns, n_tok, seq_major_out=True)
    qe = jnp.concatenate([tok(qf, s) * decay(-1, s) for s in range(n_tok)], axis=0)
    kr = jnp.concatenate([tok(kk, s) * decay(s, last) for s in range(n_tok)], axis=0)
    qe_ref[...] = _dot(to_seq_major, qe.astype(BF16)).astype(BF16)
    kr_ref[...] = _dot(to_seq_major, kr.astype(BF16)).astype(BF16)
    v_ref[...] = _dot(to_seq_major, vv.astype(BF16)).astype(BF16)
    for s in range(n_tok):
        z_ref[:, _tok(s, HG_I)] = tok(z, s)

    for t in range(n_tok):
        acc = jnp.zeros((ns, HG_I), F32)
        for s in range(t + 1):
            e = tok(qf, t) * tok(kk, s) * decay(s, t)
            vs = tok(vv, s)
            parts = []
            for hd in range(HG_HEADS):
                hs = slice(hd * HG_DK, (hd + 1) * HG_DK)
                parts.append(jnp.sum(e[:, hs], axis=-1, keepdims=True) * vs[:, hs])
            acc = acc + jnp.concatenate(parts, axis=1)
        oi_ref[:, _tok(t, HG_I)] = acc


def _hgrn_sample_pre_call(x, mods, norm_g, w_in, lb_raw, layer, n_tok):
    n_seq = x.shape[0]
    j = layer // 2
    act = jax.ShapeDtypeStruct((n_seq, n_tok * HG_F), F32)
    tok_rows = jax.ShapeDtypeStruct((n_seq * n_tok, HG_F), BF16)
    tok_spec = pl.BlockSpec((SAMPLE_SEQ_BLOCK * n_tok, HG_F), lambda i: (i, 0))
    return pl.pallas_call(
        functools.partial(_hgrn_sample_pre_kernel, n_tok=n_tok, layer=layer),
        grid=(n_seq // SAMPLE_SEQ_BLOCK,),
        in_specs=[
            _seq_rows(n_tok * D_MODEL),
            _seq_rows(3 * D_MODEL, layer),
            _resident((None, 1, D_MODEL), (layer, 0, 0), 1),
            _resident((None, D_MODEL, 2 * HG_F + 2 * HG_I), (j, 0, 0), 1),
            _resident((DEPTH, HG_F), (0, 0), 1),
        ],
        out_specs=[tok_spec] * 3 + [_seq_rows(HG_F)] + [_seq_rows(n_tok * HG_I)] * 2,
        out_shape=[tok_rows, tok_rows, tok_rows, jax.ShapeDtypeStruct((n_seq, HG_F), F32), act, act],
        compiler_params=_params(1),
        name="hgrn_sample_pre",
    )(x, mods, norm_g, w_in, lb_raw)


def _hgrn_sample_post_kernel(x_ref, mod_ref, oa_ref, ob_ref, z_ref, og_ref, wout_ref, fg_ref, xo_ref,
                             *, n_tok, final):
    ns = x_ref.shape[0]
    o_state = _permute_rows_f32(_token_row_permutation(ns, n_tok, seq_major_out=False), oa_ref[...])
    o = o_state + _stack_tokens(ob_ref, n_tok, HG_I)
    y = _head_norm_gate(o, _stack_tokens(z_ref, n_tok, HG_I), og_ref[...])
    out = _dot(y, wout_ref[...])
    gate = mod_ref[:, 2 * D_MODEL:]
    for s in range(n_tok):
        xn = x_ref[:, _tok(s, D_MODEL)] + gate * out[s * ns:(s + 1) * ns]
        if final:
            xn = _rmsnorm(xn, fg_ref[...])
        xo_ref[:, _tok(s, D_MODEL)] = xn


def _hgrn_sample_post_call(x, mods, o_inter, o_intra, z, og, w_out, fg, layer, n_tok):
    n_seq = x.shape[0]
    j = layer // 2
    return pl.pallas_call(
        functools.partial(_hgrn_sample_post_kernel, n_tok=n_tok, final=layer == DEPTH - 1),
        grid=(n_seq // SAMPLE_SEQ_BLOCK,),
        in_specs=[
            _seq_rows(n_tok * D_MODEL),
            _seq_rows(3 * D_MODEL, layer),
            pl.BlockSpec((SAMPLE_SEQ_BLOCK * n_tok, HG_I), lambda i: (i, 0)),
            _seq_rows(n_tok * HG_I), _seq_rows(n_tok * HG_I),
            _resident((None, 1, HG_I), (j, 0, 0), 1),
            _resident((None, HG_I, D_MODEL), (j, 0, 0), 1),
            _resident((1, D_MODEL), (0, 0), 1),
        ],
        out_specs=_seq_rows(n_tok * D_MODEL),
        out_shape=jax.ShapeDtypeStruct((n_seq, n_tok * D_MODEL), F32),
        compiler_params=_params(1),
        name="hgrn_sample_post",
    )(x, mods, o_inter, o_intra, z, og, w_out, fg)


def kernel(x_prompt, x_sample, state_conv, state_hgrn, c_prompt, c_sample, norm_g, w_ada, b_ada, conv_w_in, conv_w, conv_w_out, hgrn_w_in, hgrn_lower_bounds, hgrn_onorm_g, hgrn_w_out, final_norm_g):
    n_p = x_prompt.shape[0]
    n_s, t_s, _ = x_sample.shape
    assert t_s >= CONV_WIDTH - 1 and n_s % SAMPLE_SEQ_BLOCK == 0 and n_s % n_p == 0

    mods = _ada_call(jnp.concatenate([c_sample, c_prompt], axis=0), w_ada, b_ada)
    mod_block_p = n_s // n_p

    conv_w_in_b = conv_w_in.astype(BF16)
    conv_w_out_b = conv_w_out.astype(BF16)
    hgrn_w_in_b = hgrn_w_in.astype(BF16)
    hgrn_w_out_b = hgrn_w_out.astype(BF16)
    norm_g3 = norm_g.reshape(DEPTH, 1, D_MODEL)
    og3 = hgrn_onorm_g.reshape(DEPTH // 2, 1, HG_I)
    fg = final_norm_g.reshape(1, D_MODEL)
    state_conv2 = state_conv.reshape(state_conv.shape[0], n_s, (CONV_WIDTH - 1) * E_CONV)

    xp = x_prompt
    xs = x_sample.reshape(n_s, t_s * D_MODEL)
    conv_p, conv_s, hgrn_p = [], [], []
    hgrn_s = None
    for layer in range(DEPTH):
        if layer % 2 == 0:
            xp, tail = _conv_prompt_call(xp, mods, norm_g3, conv_w_in_b, conv_w, conv_w_out_b, layer, mod_block_p)
            conv_p.append(tail[:, SUBLANE - (CONV_WIDTH - 1):])
            xs, st = _conv_sample_call(xs, mods, norm_g3, conv_w_in_b, conv_w, conv_w_out_b, state_conv2,
                                       layer, t_s)
            conv_s.append(st.reshape(n_s, CONV_WIDTH - 1, E_CONV))
        else:
            qe, kr, v, el, o_intra, z = _hgrn_sample_pre_call(xs, mods, norm_g3, hgrn_w_in_b,
                                                              hgrn_lower_bounds, layer, t_s)
            decode = (qe, kr, v, el, state_hgrn, hgrn_s)
            xp, s_new, o_inter, hgrn_s = _hgrn_prompt_call(xp, mods, norm_g3, hgrn_w_in_b, hgrn_lower_bounds,
                                                           og3, hgrn_w_out_b, fg, decode, layer, mod_block_p)
            hgrn_p.append(s_new)
            xs = _hgrn_sample_post_call(xs, mods, o_inter, o_intra, z, og3, hgrn_w_out_b, fg, layer, t_s)
    return (xp, xs.reshape(n_s, t_s, D_MODEL), jnp.stack(conv_p), jnp.stack(hgrn_p), jnp.stack(conv_s), hgrn_s)
```

```python
import functools

import jax
import jax.numpy as jnp
from jax import lax
from jax.experimental import pallas as pl
from jax.experimental.pallas import tpu as pltpu

F32 = jnp.float32
BF16 = jnp.bfloat16

D_MODEL = 1024
DEPTH = 4
CONV_WIDTH = 3
E_CONV = D_MODEL
HG_HEADS = 8
HG_DK = 128
HG_DV = 128
HG_F = HG_HEADS * HG_DK
HG_I = HG_HEADS * HG_DV
EPS = 1e-6

SUBLANE = 8
CHUNK = 128
SUB = CHUNK // 2
CONV_COL_TILE = 256
CONV_TB = 512
CONV_GB = 2
HGRN_TB = 256
HGRN_GB = 2
HGRN_PIPE_SKEW = 3
HGRN_SAFE_LOG2_SPAN = 200.0
SAMPLE_SEQ_BLOCK = 32
VMEM_LIMIT = 56 * 1024 * 1024


def _dot(a, b):
    return jnp.dot(a, b, preferred_element_type=F32)


def _dot_nt(a, b):
    return lax.dot_general(a, b, (((1,), (1,)), ((), ())), preferred_element_type=F32)


def _dot_tn(a, b):
    return lax.dot_general(a, b, (((0,), (0,)), ((), ())), preferred_element_type=F32)


def _silu(x):
    return x * jax.nn.sigmoid(x)


def _rmsnorm(x, g):
    ms = jnp.mean(x * x, axis=-1, keepdims=True)
    return x * lax.rsqrt(ms + EPS) * g


def _split_mod(mod):
    return mod[:, :D_MODEL], mod[:, D_MODEL:2 * D_MODEL], mod[:, 2 * D_MODEL:]


def _mod_norm(x, g, shift, scale):
    return (_rmsnorm(x, g) * (1.0 + scale) + shift).astype(BF16)


def _mod_norm_seq(x, g, shift, scale):
    ms = jnp.mean(x * x, axis=-1, keepdims=True)
    return (x * lax.rsqrt(ms + EPS) * (g * (1.0 + scale)) + shift).astype(BF16)


def _lower_bound(raw, layer):
    rows = [raw[i:i + 1, :] for i in range(DEPTH)]
    m = functools.reduce(jnp.maximum, rows)
    es = [jnp.exp(r - m) for r in rows]
    tot = functools.reduce(lambda a, b: a + b, es)
    acc = es[1]
    for i in range(2, layer + 1):
        acc = acc + es[i]
    return acc / tot


def _gate_math(fpre, lb):
    t = jnp.exp(-jnp.abs(fpre))
    r = 1.0 / (1.0 + t)
    pos = fpre >= 0.0
    sig = jnp.where(pos, 1.0, t) * r
    sig_neg = jnp.where(pos, t, 1.0) * r
    return jnp.log2(lb + (1.0 - lb) * sig), (1.0 - lb) * sig_neg


def _head_norm_gate(o, z, og):
    parts = []
    for hd in range(HG_HEADS):
        hs = slice(hd * HG_DV, (hd + 1) * HG_DV)
        oh = o[:, hs]
        ms = jnp.mean(oh * oh, axis=-1, keepdims=True)
        parts.append(oh * lax.rsqrt(ms + EPS))
    on = jnp.concatenate(parts, axis=1) * og
    return (on * _silu(z)).astype(BF16)


def _params(n_grid):
    return pltpu.CompilerParams(dimension_semantics=("arbitrary",) * n_grid, vmem_limit_bytes=VMEM_LIMIT)


def _resident(shape, index, n_grid):
    if n_grid == 1:
        return pl.BlockSpec(shape, lambda i: index, pipeline_mode=pl.Buffered(1))
    return pl.BlockSpec(shape, lambda b, t: index, pipeline_mode=pl.Buffered(1))


def _ada_kernel(c_ref, w_ref, b_ref, o_ref):
    s = _silu(c_ref[...]).astype(BF16)
    o_ref[...] = _dot(s, w_ref[...].astype(BF16)) + b_ref[...]


def _ada_call(c_all, w_ada, b_ada):
    rows = c_all.shape[0]
    n_col = 3
    col = 3 * D_MODEL // n_col
    return pl.pallas_call(
        _ada_kernel,
        grid=(DEPTH, n_col),
        in_specs=[
            pl.BlockSpec((rows, D_MODEL), lambda l, n: (0, 0)),
            pl.BlockSpec((None, D_MODEL, col), lambda l, n: (l, 0, n)),
            pl.BlockSpec((None, 1, col), lambda l, n: (l, 0, n)),
        ],
        out_specs=pl.BlockSpec((None, rows, col), lambda l, n: (l, 0, n)),
        out_shape=jax.ShapeDtypeStruct((DEPTH, rows, 3 * D_MODEL), F32),
        compiler_params=_params(2),
        name="ada_mod",
    )(c_all, w_ada, b_ada.reshape(DEPTH, 1, 3 * D_MODEL))


def _conv_columns(h, win_ref, cw_ref, y_scr, prev_fn, tail_fn):
    ct = CONV_COL_TILE
    for n in range(E_CONV // ct):
        def col(j):
            return slice(j * E_CONV + n * ct, j * E_CONV + (n + 1) * ct)
        cs = slice(n * ct, (n + 1) * ct)
        b_gate = _dot(h, win_ref[:, col(0)])
        c_gate = _dot(h, win_ref[:, col(1)])
        v = _dot(h, win_ref[:, col(2)])
        z = _dot(h, win_ref[:, col(3)])
        u = c_gate * v
        p1, p2 = prev_fn(cs, u)
        conv = cw_ref[0:1, cs] * p2 + cw_ref[1:2, cs] * p1 + cw_ref[2:3, cs] * u
        y_scr[:, cs] = (b_gate * conv * _silu(z)).astype(BF16)
        tail_fn(cs, u)


def _conv_prompt_kernel(*refs, tb, gb, n_cast):
    x_ref, mod_ref, g_ref, win_ref, cw_ref, wout_ref = refs[:6]
    cast_in = refs[6:6 + n_cast]
    xo_ref, st_ref = refs[6 + n_cast:8 + n_cast]
    cast_out = refs[8 + n_cast:8 + 2 * n_cast]
    y_scr, = refs[8 + 2 * n_cast:]
    b = pl.program_id(0)
    t = pl.program_id(1)
    ct = CONV_COL_TILE
    for src, dst in zip(cast_in, cast_out):
        dst[...] = src[...].astype(BF16)

    @pl.when(t == 0)
    def _():
        st_ref[...] = jnp.zeros_like(st_ref)

    seqs = range(gb)
    x = [x_ref[g] for g in seqs]
    mod = [_split_mod(mod_ref[pl.ds(b * gb + g, 1), :]) for g in seqs]
    h = [_mod_norm_seq(x[g], g_ref[...], mod[g][0], mod[g][1]) for g in seqs]
    row = lax.broadcasted_iota(jnp.int32, (tb, ct), 0)

    for g in seqs:
        def prev_fn(cs, u, g=g):
            c0 = st_ref[g, SUBLANE - 2:SUBLANE - 1, cs]
            c1 = st_ref[g, SUBLANE - 1:SUBLANE, cs]
            p1 = jnp.where(row == 0, c1, pltpu.roll(u, 1, 0))
            p2 = jnp.where(row == 0, c0, jnp.where(row == 1, c1, pltpu.roll(u, 2, 0)))
            return p1, p2

        def tail_fn(cs, u, g=g):
            st_ref[g, :, cs] = u[tb - SUBLANE:tb, :]

        _conv_columns(h[g], win_ref, cw_ref, y_scr.at[g], prev_fn, tail_fn)
    for g in seqs:
        xo_ref[g] = x[g] + mod[g][2] * _dot(y_scr[g], wout_ref[...])


def _conv_prompt_call(x, mods, norm_g, w_in, cw, w_out, layer, mod_block, casts=()):
    bsz, seq, _ = x.shape
    tb = CONV_TB
    gb = CONV_GB
    j = layer // 2
    n_t = seq // tb
    n_steps = (bsz // gb) * n_t
    cast_in_specs, cast_out_specs, cast_shapes = [], [], []
    for mat, first, rows in casts:
        blk = rows // n_steps
        assert rows % n_steps == 0 and blk % (2 * SUBLANE) == 0 and first % blk == 0
        cast_in_specs.append(pl.BlockSpec((blk, mat.shape[1]), lambda b, t, o=first // blk: (o + b * n_t + t, 0)))
        cast_out_specs.append(pl.BlockSpec((blk, mat.shape[1]), lambda b, t: (b * n_t + t, 0)))
        cast_shapes.append(jax.ShapeDtypeStruct((rows, mat.shape[1]), BF16))
    return pl.pallas_call(
        functools.partial(_conv_prompt_kernel, tb=tb, gb=gb, n_cast=len(casts)),
        grid=(bsz // gb, n_t),
        in_specs=[
            pl.BlockSpec((gb, tb, D_MODEL), lambda b, t: (b, t, 0)),
            _resident((None, bsz, 3 * D_MODEL), (layer, mod_block, 0), 2),
            _resident((None, 1, D_MODEL), (layer, 0, 0), 2),
            _resident((None, D_MODEL, 4 * E_CONV), (0, 0, 0), 2),
            _resident((None, CONV_WIDTH, E_CONV), (j, 0, 0), 2),
            _resident((None, E_CONV, D_MODEL), (0, 0, 0), 2),
        ] + cast_in_specs,
        out_specs=[
            pl.BlockSpec((gb, tb, D_MODEL), lambda b, t: (b, t, 0)),
            pl.BlockSpec((gb, SUBLANE, E_CONV), lambda b, t: (b, 0, 0)),
        ] + cast_out_specs,
        out_shape=[
            jax.ShapeDtypeStruct((bsz, seq, D_MODEL), F32),
            jax.ShapeDtypeStruct((bsz, SUBLANE, E_CONV), F32),
        ] + cast_shapes,
        scratch_shapes=[pltpu.VMEM((gb, tb, E_CONV), BF16)],
        compiler_params=_params(2),
        name="conv_prompt",
    )(x, mods, norm_g, w_in, cw, w_out, *[mat for mat, _, _ in casts])


def _tok(s, width):
    return slice(s * width, (s + 1) * width)


def _stack_tokens(ref, n_tok, width, cols=None):
    parts = []
    for s in range(n_tok):
        lo = s * width + (0 if cols is None else cols.start)
        hi = s * width + (width if cols is None else cols.stop)
        parts.append(ref[:, lo:hi])
    return jnp.concatenate(parts, axis=0)


def _sample_hidden(x_ref, mod_ref, g_ref, n_tok):
    shift, scale, _ = _split_mod(mod_ref[...])
    g = g_ref[...]
    return jnp.concatenate(
        [_mod_norm(x_ref[:, _tok(s, D_MODEL)], g, shift, scale) for s in range(n_tok)], axis=0)


def _conv_sample_kernel(x_ref, mod_ref, g_ref, win_ref, cw_ref, wout_ref, st_ref,
                        xo_ref, so_ref, y_scr, *, n_tok):
    ns = x_ref.shape[0]
    h = _sample_hidden(x_ref, mod_ref, g_ref, n_tok)

    def prev_fn(cs, u):
        st0 = st_ref[:, cs]
        st1 = st_ref[:, E_CONV + cs.start:E_CONV + cs.stop]
        p1 = jnp.concatenate([st1, u[:(n_tok - 1) * ns]], axis=0)
        p2 = jnp.concatenate([st0, st1, u[:(n_tok - 2) * ns]], axis=0)
        return p1, p2

    def tail_fn(cs, u):
        so_ref[:, cs] = u[(n_tok - 2) * ns:(n_tok - 1) * ns]
        so_ref[:, E_CONV + cs.start:E_CONV + cs.stop] = u[(n_tok - 1) * ns:]

    _conv_columns(h, win_ref, cw_ref, y_scr, prev_fn, tail_fn)
    out = _dot(y_scr[...], wout_ref[...])
    gate = mod_ref[:, 2 * D_MODEL:]
    for s in range(n_tok):
        xo_ref[:, _tok(s, D_MODEL)] = x_ref[:, _tok(s, D_MODEL)] + gate * out[s * ns:(s + 1) * ns]


def _seq_rows(width, layer=None):
    if layer is None:
        return pl.BlockSpec((SAMPLE_SEQ_BLOCK, width), lambda i: (i, 0))
    return pl.BlockSpec((None, SAMPLE_SEQ_BLOCK, width), lambda i: (layer, i, 0))


def _conv_sample_call(x, mods, norm_g, w_in, cw, w_out, state, layer, n_tok):
    n_seq = x.shape[0]
    j = layer // 2
    return pl.pallas_call(
        functools.partial(_conv_sample_kernel, n_tok=n_tok),
        grid=(n_seq // SAMPLE_SEQ_BLOCK,),
        in_specs=[
            _seq_rows(n_tok * D_MODEL),
            _seq_rows(3 * D_MODEL, layer),
            _resident((None, 1, D_MODEL), (layer, 0, 0), 1),
            _resident((None, D_MODEL, 4 * E_CONV), (0, 0, 0), 1),
            _resident((None, CONV_WIDTH, E_CONV), (j, 0, 0), 1),
            _resident((None, E_CONV, D_MODEL), (0, 0, 0), 1),
            _seq_rows((CONV_WIDTH - 1) * E_CONV, j),
        ],
        out_specs=[_seq_rows(n_tok * D_MODEL), _seq_rows((CONV_WIDTH - 1) * E_CONV)],
        out_shape=[
            jax.ShapeDtypeStruct((n_seq, n_tok * D_MODEL), F32),
            jax.ShapeDtypeStruct((n_seq, (CONV_WIDTH - 1) * E_CONV), F32),
        ],
        scratch_shapes=[pltpu.VMEM((n_tok * SAMPLE_SEQ_BLOCK, E_CONV), BF16)],
        compiler_params=_params(1),
        name="conv_sample",
    )(x, mods, norm_g, w_in, cw, w_out, state)


def _rows_from_blocks(vecs):
    return jnp.concatenate([jnp.broadcast_to(v, (SUB, v.shape[1])) for v in vecs], axis=0)


def _chunk_masks():
    r = lax.broadcasted_iota(jnp.int32, (CHUNK, CHUNK), 0)
    c = lax.broadcasted_iota(jnp.int32, (CHUNK, CHUNK), 1)
    causal = (r // SUB == c // SUB) & (c <= r)
    tri = jnp.where(causal, 1.0, 0.0).astype(BF16)
    first_cols = lax.broadcasted_iota(jnp.int32, (SUB, CHUNK), 1) < SUB
    return causal, first_cols, jnp.concatenate([tri, tri], axis=1)


def _chunk_decay(lf, tri2):
    hi = lf.astype(BF16)
    lo = (lf - hi.astype(F32)).astype(BF16)
    return _dot(tri2, jnp.concatenate([hi, lo], axis=0))


def _chunk_key_operands(kk, gl):
    t0 = gl[SUB - 1:SUB, :]
    t1 = gl[2 * SUB - 1:2 * SUB, :]
    k_mid = kk * jnp.exp2(_rows_from_blocks([0.5 * t0, 0.5 * t1]) - gl)
    k_state = k_mid * _rows_from_blocks([jnp.exp2(0.5 * t0 + t1), jnp.exp2(0.5 * t1)])
    in_span = (t0 >= -HGRN_SAFE_LOG2_SPAN) & (t1 >= -HGRN_SAFE_LOG2_SPAN)
    return k_mid.astype(BF16), k_state.astype(BF16), jnp.exp2(t0 + t1), jnp.where(in_span, 0.0, 1.0)


def _chunk_query_operands(qf, gl):
    t0 = gl[SUB - 1:SUB, :]
    t1 = gl[2 * SUB - 1:2 * SUB, :]
    q_mid = qf * jnp.exp2(gl - _rows_from_blocks([0.5 * t0, 0.5 * t1]))
    q_next = q_mid[SUB:] * jnp.exp2(0.5 * (t0 + t1))
    q_state = q_mid * _rows_from_blocks([jnp.exp2(0.5 * t0), jnp.exp2(0.5 * t1 + t0)])
    return jnp.concatenate([q_mid, q_next], axis=0).astype(BF16), q_state.astype(BF16)


def _chunk_finish(ast, qe, kr, vb, ptot, s_old, causal, first_cols):
    same = jnp.where(causal, ast[0:CHUNK], 0.0)
    a = jnp.concatenate([same[:SUB], jnp.where(first_cols, ast[CHUNK:], same[SUB:])], axis=0)
    o = _dot(jnp.concatenate([qe, a.astype(BF16)], axis=1),
             jnp.concatenate([s_old.astype(BF16), vb], axis=0))
    dec = jnp.broadcast_to(ptot, (HG_DK, HG_DK)).T
    return o, s_old * dec + _dot_tn(kr, vb)


def _exact_recurrence(q_ref, k_ref, f_ref, v_ref, o_ref, s0, n_rows):
    def col(tile, r):
        return jnp.broadcast_to(tile[r:r + 1, :], (HG_DK, HG_DK)).T

    def step(i, s):
        rows = pl.ds(pl.multiple_of(i * SUBLANE, SUBLANE), SUBLANE)
        q, k, f, v = q_ref[rows, :], k_ref[rows, :], f_ref[rows, :], v_ref[rows, :]
        out = []
        for r in range(SUBLANE):
            s = s * col(f, r) + col(k, r) * v[r:r + 1, :]
            out.append(jnp.sum(col(q, r) * s, axis=0, keepdims=True))
        o_ref[rows, :] = jnp.concatenate(out, axis=0)
        return s
    return lax.fori_loop(0, n_rows // SUBLANE, step, s0)


def _decode_state_update(p, el_row0, dq_ref, dk_ref, dv_ref, del_ref, ds_ref, do_ref, dso_ref, partial_o):
    nd = ds_ref.shape[0]
    n_tok = dq_ref.shape[0] // nd
    hd, sq = divmod(p, nd)
    hs = slice(hd * HG_DK, (hd + 1) * HG_DK)
    mine = lax.broadcasted_iota(jnp.int32, (dq_ref.shape[0], HG_DK), 0) // n_tok == sq
    s_old = ds_ref[sq, hd]
    o = _dot(dq_ref[:, hs], s_old.astype(BF16))
    partial_o[hd] = jnp.where(mine, o, partial_o[hd]) if hd in partial_o else jnp.where(mine, o, 0.0)
    k_mine = jnp.where(mine, dk_ref[:, hs], 0.0).astype(BF16)
    dec = jnp.broadcast_to(del_ref[pl.ds(el_row0 + sq, 1), :][:, hs], (HG_DK, HG_DK)).T
    dso_ref[sq, hd] = s_old * dec + _dot_tn(k_mine, dv_ref[:, hs])
    if sq == nd - 1:
        do_ref[:, hs] = partial_o.pop(hd)


def _hgrn_prompt_kernel(*refs, tb, gb, layer, final, has_prev):
    n_in = 13 + int(has_prev)
    x_ref, mod_ref, g_ref, win_ref, lbraw_ref, og_ref, wout_ref, fg_ref = refs[:8]
    decode_in = refs[8:13]
    xo_ref, s_ref, do_ref, dso_ref = refs[n_in:n_in + 4]
    q_scr, lf_scr, k_scr, v_scr, z_scr, o_scr, s0_scr = refs[n_in + 4:]
    b = pl.program_id(0)
    t = pl.program_id(1)

    @pl.when(t == 0)
    def _():
        s_ref[...] = jnp.zeros_like(s_ref)

    s0_scr[...] = s_ref[...]

    seqs = range(gb)
    lb = _lower_bound(lbraw_ref[...], layer)
    x = [x_ref[g] for g in seqs]
    mod = [_split_mod(mod_ref[pl.ds(b * gb + g, 1), :]) for g in seqs]
    h = [_mod_norm_seq(x[g], g_ref[...], mod[g][0], mod[g][1]) for g in seqs]
    for g in seqs:
        q_scr[g] = _silu(_dot(h[g], win_ref[:, 0:HG_F]))
    for g in seqs:
        logf, kk = _gate_math(_dot(h[g], win_ref[:, HG_F:2 * HG_F]), lb)
        lf_scr[g] = logf
        k_scr[g] = kk
    for g in seqs:
        v_scr[g] = _dot(h[g], win_ref[:, 2 * HG_F:2 * HG_F + HG_I])
    for g in seqs:
        z_scr[g] = _dot(h[g], win_ref[:, 2 * HG_F + HG_I:])

    causal, first_cols, tri2 = _chunk_masks()
    chains = [(c, hd, g) for c in range(tb // CHUNK) for hd in range(HG_HEADS) for g in seqs]
    ops, scores = {}, {}
    unsafe = jnp.zeros((1, HG_DK), F32)

    def where(n):
        c, hd, g = chains[n]
        return g, hd, slice(c * CHUNK, (c + 1) * CHUNK), slice(hd * HG_DK, (hd + 1) * HG_DK)

    block_seqs = decode_in[4].shape[0]
    n_decode = HG_HEADS * block_seqs
    blocks_per_el = decode_in[3].shape[0] // block_seqs
    el_row0 = ((b * pl.num_programs(1) + t) % blocks_per_el) * block_seqs
    partial_o = {}
    for step in range(max(len(chains) + 2 * HGRN_PIPE_SKEW, n_decode)):
        if step < n_decode:
            _decode_state_update(step, el_row0, *decode_in, do_ref, dso_ref, partial_o)
        n = step
        if n < len(chains):
            g, hd, rows, hs = where(n)
            gl = _chunk_decay(lf_scr[g, rows, hs], tri2)
            km, kr, ptot, bad = _chunk_key_operands(k_scr[g, rows, hs], gl)
            lhs, qe = _chunk_query_operands(q_scr[g, rows, hs], gl)
            ops[n] = (lhs, km, qe, kr, ptot)
            unsafe = jnp.maximum(unsafe, bad)
        n = step - HGRN_PIPE_SKEW
        if 0 <= n < len(chains):
            scores[n] = _dot_nt(ops[n][0], ops[n][1])
        n = step - 2 * HGRN_PIPE_SKEW
        if 0 <= n < len(chains):
            g, hd, rows, hs = where(n)
            _, _, qe, kr, ptot = ops.pop(n)
            vb = v_scr[g, rows, hs].astype(BF16)
            o, s_new = _chunk_finish(scores.pop(n), qe, kr, vb, ptot, s_ref[g, hd], causal, first_cols)
            o_scr[g, rows, hs] = o
            s_ref[g, hd] = s_new

    y = [_head_norm_gate(o_scr[g], z_scr[g], og_ref[...]) for g in seqs]
    for g in seqs:
        xn = x[g] + mod[g][2] * _dot(y[g], wout_ref[...])
        if final:
            xn = _rmsnorm(xn, fg_ref[...])
        xo_ref[g] = xn

    @pl.when(jnp.max(unsafe) > 0.0)
    def _():
        def exact(q_tmp, k_tmp, f_tmp, v_tmp, o_tmp):
            def one_seq(g, carry):
                shift, scale, gate = _split_mod(mod_ref[pl.ds(b * gb + g, 1), :])
                hg = _mod_norm_seq(x_ref[g], g_ref[...], shift, scale)
                qf = _silu(_dot(hg, win_ref[:, 0:HG_F]))
                logf, kk = _gate_math(_dot(hg, win_ref[:, HG_F:2 * HG_F]), lb)
                ff = jnp.exp2(logf)
                vv = _dot(hg, win_ref[:, 2 * HG_F:2 * HG_F + HG_I])
                for hd in range(HG_HEADS):
                    hs = slice(hd * HG_DK, (hd + 1) * HG_DK)
                    q_tmp[hd], k_tmp[hd], f_tmp[hd], v_tmp[hd] = qf[:, hs], kk[:, hs], ff[:, hs], vv[:, hs]

                def one_head(hd, c2):
                    s_ref[g, hd] = _exact_recurrence(q_tmp.at[hd], k_tmp.at[hd], f_tmp.at[hd], v_tmp.at[hd],
                                                     o_tmp.at[hd], s0_scr[g, hd], tb)
                    return c2

                lax.fori_loop(0, HG_HEADS, one_head, 0)
                o = jnp.concatenate([o_tmp[hd] for hd in range(HG_HEADS)], axis=1)
                y = _head_norm_gate(o, z_scr[g], og_ref[...])
                xn = x_ref[g] + gate * _dot(y, wout_ref[...])
                if final:
                    xn = _rmsnorm(xn, fg_ref[...])
                xo_ref[g] = xn
                return carry

            lax.fori_loop(0, gb, one_seq, 0)

        pl.run_scoped(exact, *[pltpu.VMEM((HG_HEADS, tb, HG_DK), F32)] * 5)


def _hgrn_prompt_call(x, mods, norm_g, w_in, lb_raw, og, w_out, fg, decode, layer, mod_block):
    bsz, seq, _ = x.shape
    tb = HGRN_TB
    gb = HGRN_GB
    j = layer // 2
    final = layer == DEPTH - 1
    n_t = seq // tb
    qe, kr, v, el, states, prev_states = decode
    n_blocks = (bsz // gb) * n_t
    n_seq = el.shape[0]
    block_seqs = n_seq // n_blocks
    dec_rows = qe.shape[0] // n_blocks
    el_rows = max(block_seqs, SUBLANE)
    assert n_seq == n_blocks * block_seqs and el_rows % block_seqs == 0 and dec_rows % SUBLANE == 0
    dec_spec = pl.BlockSpec((dec_rows, HG_F), lambda b, t: (b * n_t + t, 0))
    st_spec = pl.BlockSpec((None, block_seqs, HG_HEADS, HG_DK, HG_DV), lambda b, t: (j, b * n_t + t, 0, 0, 0))
    in_specs = [
        pl.BlockSpec((gb, tb, D_MODEL), lambda b, t: (b, t, 0)),
        _resident((None, bsz, 3 * D_MODEL), (layer, mod_block, 0), 2),
        _resident((None, 1, D_MODEL), (layer, 0, 0), 2),
        _resident((None, D_MODEL, 2 * HG_F + 2 * HG_I), (j, 0, 0), 2),
        _resident((DEPTH, HG_F), (0, 0), 2),
        _resident((None, 1, HG_I), (j, 0, 0), 2),
        _resident((None, HG_I, D_MODEL), (j, 0, 0), 2),
        _resident((1, D_MODEL), (0, 0), 2),
        dec_spec, dec_spec, dec_spec,
        pl.BlockSpec((el_rows, HG_F), lambda b, t: ((b * n_t + t) * block_seqs // el_rows, 0)),
        st_spec,
    ]
    args = [x, mods, norm_g, w_in, lb_raw, og, w_out, fg, qe, kr, v, el, states]
    aliases = {}
    if prev_states is not None:
        in_specs.append(pl.BlockSpec(memory_space=pl.ANY))
        args.append(prev_states)
        aliases = {len(args) - 1: 3}
    return pl.pallas_call(
        functools.partial(_hgrn_prompt_kernel, tb=tb, gb=gb, layer=layer, final=final,
                          has_prev=prev_states is not None),
        grid=(bsz // gb, n_t),
        in_specs=in_specs,
        out_specs=[
            pl.BlockSpec((gb, tb, D_MODEL), lambda b, t: (b, t, 0)),
            pl.BlockSpec((gb, HG_HEADS, HG_DK, HG_DV), lambda b, t: (b, 0, 0, 0)),
            dec_spec,
            st_spec,
        ],
        out_shape=[
            jax.ShapeDtypeStruct((bsz, seq, D_MODEL), F32),
            jax.ShapeDtypeStruct((bsz, HG_HEADS, HG_DK, HG_DV), F32),
            jax.ShapeDtypeStruct(qe.shape, F32),
            jax.ShapeDtypeStruct(states.shape, F32),
        ],
        scratch_shapes=[pltpu.VMEM((gb, tb, HG_F), F32)] * 6
        + [pltpu.VMEM((gb, HG_HEADS, HG_DK, HG_DV), F32)],
        input_output_aliases=aliases,
        compiler_params=_params(2),
        name="hgrn_prompt",
    )(*args)


def _token_row_permutation(n_seq, n_tok, seq_major_out):
    n = n_seq * n_tok
    r = lax.broadcasted_iota(jnp.int32, (n, n), 0)
    c = lax.broadcasted_iota(jnp.int32, (n, n), 1)
    sm, tm = (r, c) if seq_major_out else (c, r)
    return jnp.where(tm == (sm % n_tok) * n_seq + sm // n_tok, 1.0, 0.0).astype(BF16)


def _permute_rows_f32(p, a):
    hi = a.astype(BF16)
    rest = a - hi.astype(F32)
    mid = rest.astype(BF16)
    lo = (rest - mid.astype(F32)).astype(BF16)
    return (_dot(p, hi) + _dot(p, mid)) + _dot(p, lo)


def _hgrn_sample_pre_kernel(x_ref, mod_ref, g_ref, win_ref, lbraw_ref,
                            qe_ref, kr_ref, v_ref, el_ref, oi_ref, z_ref, *, n_tok, layer):
    ns = x_ref.shape[0]
    h = _sample_hidden(x_ref, mod_ref, g_ref, n_tok)
    lb = _lower_bound(lbraw_ref[...], layer)
    qf = _silu(_dot(h, win_ref[:, 0:HG_F]))
    lf, kk = _gate_math(_dot(h, win_ref[:, HG_F:2 * HG_F]), lb)
    vv = _dot(h, win_ref[:, 2 * HG_F:2 * HG_F + HG_I])
    z = _dot(h, win_ref[:, 2 * HG_F + HG_I:])

    def tok(a, s):
        return a[s * ns:(s + 1) * ns]

    def decay(s, t):
        if s == t:
            return 1.0
        return jnp.exp2(functools.reduce(lambda a, b: a + b, [tok(lf, u) for u in range(s + 1, t + 1)]))

    last = n_tok - 1
    el_ref[...] = decay(-1, last)
    to_seq_major = _token_row_permutation(ns, n_tok, seq_major_out=True)
    qe = jnp.concatenate([tok(qf, s) * decay(-1, s) for s in range(n_tok)], axis=0)
    kr = jnp.concatenate([tok(kk, s) * decay(s, last) for s in range(n_tok)], axis=0)
    qe_ref[...] = _dot(to_seq_major, qe.astype(BF16)).astype(BF16)
    kr_ref[...] = _dot(to_seq_major, kr.astype(BF16)).astype(BF16)
    v_ref[...] = _dot(to_seq_major, vv.astype(BF16)).astype(BF16)
    for s in range(n_tok):
        z_ref[:, _tok(s, HG_I)] = tok(z, s)

    for t in range(n_tok):
        acc = jnp.zeros((ns, HG_I), F32)
        for s in range(t + 1):
            e = tok(qf, t) * tok(kk, s) * decay(s, t)
            vs = tok(vv, s)
            parts = []
            for hd in range(HG_HEADS):
                hs = slice(hd * HG_DK, (hd + 1) * HG_DK)
                parts.append(jnp.sum(e[:, hs], axis=-1, keepdims=True) * vs[:, hs])
            acc = acc + jnp.concatenate(parts, axis=1)
        oi_ref[:, _tok(t, HG_I)] = acc


def _hgrn_sample_pre_call(x, mods, norm_g, w_in, lb_raw, layer, n_tok):
    n_seq = x.shape[0]
    j = layer // 2
    act = jax.ShapeDtypeStruct((n_seq, n_tok * HG_F), F32)
    tok_rows = jax.ShapeDtypeStruct((n_seq * n_tok, HG_F), BF16)
    tok_spec = pl.BlockSpec((SAMPLE_SEQ_BLOCK * n_tok, HG_F), lambda i: (i, 0))
    return pl.pallas_call(
        functools.partial(_hgrn_sample_pre_kernel, n_tok=n_tok, layer=layer),
        grid=(n_seq // SAMPLE_SEQ_BLOCK,),
        in_specs=[
            _seq_rows(n_tok * D_MODEL),
            _seq_rows(3 * D_MODEL, layer),
            _resident((None, 1, D_MODEL), (layer, 0, 0), 1),
            _resident((None, D_MODEL, 2 * HG_F + 2 * HG_I), (j, 0, 0), 1),
            _resident((DEPTH, HG_F), (0, 0), 1),
        ],
        out_specs=[tok_spec] * 3 + [_seq_rows(HG_F)] + [_seq_rows(n_tok * HG_I)] * 2,
        out_shape=[tok_rows, tok_rows, tok_rows, jax.ShapeDtypeStruct((n_seq, HG_F), F32), act, act],
        compiler_params=_params(1),
        name="hgrn_sample_pre",
    )(x, mods, norm_g, w_in, lb_raw)


def _hgrn_sample_post_kernel(x_ref, mod_ref, oa_ref, ob_ref, z_ref, og_ref, wout_ref, fg_ref, xo_ref,
                             *, n_tok, final):
    ns = x_ref.shape[0]
    o_state = _permute_rows_f32(_token_row_permutation(ns, n_tok, seq_major_out=False), oa_ref[...])
    o = o_state + _stack_tokens(ob_ref, n_tok, HG_I)
    y = _head_norm_gate(o, _stack_tokens(z_ref, n_tok, HG_I), og_ref[...])
    out = _dot(y, wout_ref[...])
    gate = mod_ref[:, 2 * D_MODEL:]
    for s in range(n_tok):
        xn = x_ref[:, _tok(s, D_MODEL)] + gate * out[s * ns:(s + 1) * ns]
        if final:
            xn = _rmsnorm(xn, fg_ref[...])
        xo_ref[:, _tok(s, D_MODEL)] = xn


def _hgrn_sample_post_call(x, mods, o_inter, o_intra, z, og, w_out, fg, layer, n_tok):
    n_seq = x.shape[0]
    j = layer // 2
    return pl.pallas_call(
        functools.partial(_hgrn_sample_post_kernel, n_tok=n_tok, final=layer == DEPTH - 1),
        grid=(n_seq // SAMPLE_SEQ_BLOCK,),
        in_specs=[
            _seq_rows(n_tok * D_MODEL),
            _seq_rows(3 * D_MODEL, layer),
            pl.BlockSpec((SAMPLE_SEQ_BLOCK * n_tok, HG_I), lambda i: (i, 0)),
            _seq_rows(n_tok * HG_I), _seq_rows(n_tok * HG_I),
            _resident((None, 1, HG_I), (j, 0, 0), 1),
            _resident((None, HG_I, D_MODEL), (j, 0, 0), 1),
            _resident((1, D_MODEL), (0, 0), 1),
        ],
        out_specs=_seq_rows(n_tok * D_MODEL),
        out_shape=jax.ShapeDtypeStruct((n_seq, n_tok * D_MODEL), F32),
        compiler_params=_params(1),
        name="hgrn_sample_post",
    )(x, mods, o_inter, o_intra, z, og, w_out, fg)


def kernel(x_prompt, x_sample, state_conv, state_hgrn, c_prompt, c_sample, norm_g, w_ada, b_ada, conv_w_in, conv_w, conv_w_out, hgrn_w_in, hgrn_lower_bounds, hgrn_onorm_g, hgrn_w_out, final_norm_g):
    n_p = x_prompt.shape[0]
    n_s, t_s, _ = x_sample.shape
    assert t_s >= CONV_WIDTH - 1 and n_s % SAMPLE_SEQ_BLOCK == 0 and n_s % n_p == 0

    mods = _ada_call(jnp.concatenate([c_sample, c_prompt], axis=0), w_ada, b_ada)
    mod_block_p = n_s // n_p

    n_conv, n_hgrn = conv_w_in.shape[0], hgrn_w_in.shape[0]
    assert n_conv == 2 and DEPTH == 4
    conv_w_in_b = [conv_w_in[:1].astype(BF16)]
    conv_w_out_b = [conv_w_out[:1].astype(BF16)]
    later_weights = [
        (hgrn_w_in.reshape(n_hgrn * D_MODEL, -1), 0, n_hgrn * D_MODEL),
        (hgrn_w_out.reshape(n_hgrn * HG_I, D_MODEL), 0, n_hgrn * HG_I),
        (conv_w_in.reshape(n_conv * D_MODEL, -1), D_MODEL, (n_conv - 1) * D_MODEL),
        (conv_w_out.reshape(n_conv * E_CONV, D_MODEL), E_CONV, (n_conv - 1) * E_CONV),
    ]
    norm_g3 = norm_g.reshape(DEPTH, 1, D_MODEL)
    og3 = hgrn_onorm_g.reshape(DEPTH // 2, 1, HG_I)
    fg = final_norm_g.reshape(1, D_MODEL)
    state_conv2 = state_conv.reshape(state_conv.shape[0], n_s, (CONV_WIDTH - 1) * E_CONV)

    xp = x_prompt
    xs = x_sample.reshape(n_s, t_s * D_MODEL)
    conv_p, conv_s, hgrn_p = [], [], []
    hgrn_s = None
    for layer in range(DEPTH):
        if layer % 2 == 0:
            j = layer // 2
            xp, tail, *rounded = _conv_prompt_call(xp, mods, norm_g3, conv_w_in_b[j], conv_w, conv_w_out_b[j],
                                                   layer, mod_block_p, later_weights if layer == 0 else ())
            if layer == 0:
                hgrn_w_in_b = rounded[0].reshape(hgrn_w_in.shape)
                hgrn_w_out_b = rounded[1].reshape(hgrn_w_out.shape)
                conv_w_in_b.append(rounded[2].reshape((1,) + conv_w_in.shape[1:]))
                conv_w_out_b.append(rounded[3].reshape((1,) + conv_w_out.shape[1:]))
            conv_p.append(tail[:, SUBLANE - (CONV_WIDTH - 1):])
            xs, st = _conv_sample_call(xs, mods, norm_g3, conv_w_in_b[j], conv_w, conv_w_out_b[j], state_conv2,
                                       layer, t_s)
            conv_s.append(st.reshape(n_s, CONV_WIDTH - 1, E_CONV))
        else:
            qe, kr, v, el, o_intra, z = _hgrn_sample_pre_call(xs, mods, norm_g3, hgrn_w_in_b,
                                                              hgrn_lower_bounds, layer, t_s)
            decode = (qe, kr, v, el, state_hgrn, hgrn_s)
            xp, s_new, o_inter, hgrn_s = _hgrn_prompt_call(xp, mods, norm_g3, hgrn_w_in_b, hgrn_lower_bounds,
                                                           og3, hgrn_w_out_b, fg, decode, layer, mod_block_p)
            hgrn_p.append(s_new)
            xs = _hgrn_sample_post_call(xs, mods, o_inter, o_intra, z, og3, hgrn_w_out_b, fg, layer, t_s)
    return (xp, xs.reshape(n_s, t_s, D_MODEL), jnp.stack(conv_p), jnp.stack(hgrn_p), jnp.stack(conv_s), hgrn_s)
```

```python
import functools

import jax
import jax.numpy as jnp
from jax import lax
from jax.experimental import pallas as pl
from jax.experimental.pallas import tpu as pltpu

F32 = jnp.float32
BF16 = jnp.bfloat16

D_MODEL = 1024
DEPTH = 4
CONV_WIDTH = 3
E_CONV = D_MODEL
HG_HEADS = 8
HG_DK = 128
HG_DV = 128
HG_F = HG_HEADS * HG_DK
HG_I = HG_HEADS * HG_DV
EPS = 1e-6

SUBLANE = 8
CHUNK = 128
SUB = CHUNK // 2
CONV_COL_TILE = 256
CONV_TB = 512
CONV_GB = 2
HGRN_TB = 256
HGRN_GB = 2
HGRN_PIPE_SKEW = 3
HGRN_SAFE_LOG2_SPAN = 200.0
SAMPLE_SEQ_BLOCK = 32
VMEM_LIMIT = 56 * 1024 * 1024


def _dot(a, b):
    return jnp.dot(a, b, preferred_element_type=F32)


def _dot_nt(a, b):
    return lax.dot_general(a, b, (((1,), (1,)), ((), ())), preferred_element_type=F32)


def _dot_tn(a, b):
    return lax.dot_general(a, b, (((0,), (0,)), ((), ())), preferred_element_type=F32)


def _silu(x):
    return x * jax.nn.sigmoid(x)


def _rmsnorm(x, g):
    ms = jnp.mean(x * x, axis=-1, keepdims=True)
    return x * lax.rsqrt(ms + EPS) * g


def _split_mod(mod):
    return mod[:, :D_MODEL], mod[:, D_MODEL:2 * D_MODEL], mod[:, 2 * D_MODEL:]


def _mod_norm(x, g, shift, scale):
    return (_rmsnorm(x, g) * (1.0 + scale) + shift).astype(BF16)


def _mod_norm_seq(x, g, shift, scale):
    ms = jnp.mean(x * x, axis=-1, keepdims=True)
    return (x * lax.rsqrt(ms + EPS) * (g * (1.0 + scale)) + shift).astype(BF16)


def _lower_bound(raw, layer):
    rows = [raw[i:i + 1, :] for i in range(DEPTH)]
    m = functools.reduce(jnp.maximum, rows)
    es = [jnp.exp(r - m) for r in rows]
    tot = functools.reduce(lambda a, b: a + b, es)
    acc = es[1]
    for i in range(2, layer + 1):
        acc = acc + es[i]
    return acc / tot


def _gate_math(fpre, lb):
    t = jnp.exp(-jnp.abs(fpre))
    r = 1.0 / (1.0 + t)
    pos = fpre >= 0.0
    sig = jnp.where(pos, 1.0, t) * r
    sig_neg = jnp.where(pos, t, 1.0) * r
    return jnp.log2(lb + (1.0 - lb) * sig), (1.0 - lb) * sig_neg


def _head_norm_gate(o, z, og):
    parts = []
    for hd in range(HG_HEADS):
        hs = slice(hd * HG_DV, (hd + 1) * HG_DV)
        oh = o[:, hs]
        ms = jnp.mean(oh * oh, axis=-1, keepdims=True)
        parts.append(oh * lax.rsqrt(ms + EPS))
    on = jnp.concatenate(parts, axis=1) * og
    return (on * _silu(z)).astype(BF16)


def _params(n_grid):
    return pltpu.CompilerParams(dimension_semantics=("arbitrary",) * n_grid, vmem_limit_bytes=VMEM_LIMIT)


def _resident(shape, index, n_grid):
    if n_grid == 1:
        return pl.BlockSpec(shape, lambda i: index, pipeline_mode=pl.Buffered(1))
    return pl.BlockSpec(shape, lambda b, t: index, pipeline_mode=pl.Buffered(1))


def _ada_kernel(c_ref, w_ref, b_ref, o_ref):
    s = _silu(c_ref[...]).astype(BF16)
    o_ref[...] = _dot(s, w_ref[...].astype(BF16)) + b_ref[...]


def _ada_call(c_all, w_ada, b_ada):
    rows = c_all.shape[0]
    n_col = 1
    col = 3 * D_MODEL // n_col
    return pl.pallas_call(
        _ada_kernel,
        grid=(DEPTH, n_col),
        in_specs=[
            pl.BlockSpec((rows, D_MODEL), lambda l, n: (0, 0)),
            pl.BlockSpec((None, D_MODEL, col), lambda l, n: (l, 0, n)),
            pl.BlockSpec((None, 1, col), lambda l, n: (l, 0, n)),
        ],
        out_specs=pl.BlockSpec((None, rows, col), lambda l, n: (l, 0, n)),
        out_shape=jax.ShapeDtypeStruct((DEPTH, rows, 3 * D_MODEL), F32),
        compiler_params=_params(2),
        name="ada_mod",
    )(c_all, w_ada, b_ada.reshape(DEPTH, 1, 3 * D_MODEL))


def _conv_columns(h, win_ref, cw_ref, y_scr, prev_fn, tail_fn):
    ct = CONV_COL_TILE
    for n in range(E_CONV // ct):
        def col(j):
            return slice(j * E_CONV + n * ct, j * E_CONV + (n + 1) * ct)
        cs = slice(n * ct, (n + 1) * ct)
        b_gate = _dot(h, win_ref[:, col(0)])
        c_gate = _dot(h, win_ref[:, col(1)])
        v = _dot(h, win_ref[:, col(2)])
        z = _dot(h, win_ref[:, col(3)])
        u = c_gate * v
        p1, p2 = prev_fn(cs, u)
        conv = cw_ref[0:1, cs] * p2 + cw_ref[1:2, cs] * p1 + cw_ref[2:3, cs] * u
        y_scr[:, cs] = (b_gate * conv * _silu(z)).astype(BF16)
        tail_fn(cs, u)


def _conv_prompt_kernel(*refs, tb, gb, n_cast):
    x_ref, mod_ref, g_ref, win_ref, cw_ref, wout_ref = refs[:6]
    cast_in = refs[6:6 + n_cast]
    xo_ref, st_ref = refs[6 + n_cast:8 + n_cast]
    cast_out = refs[8 + n_cast:8 + 2 * n_cast]
    y_scr, = refs[8 + 2 * n_cast:]
    b = pl.program_id(0)
    t = pl.program_id(1)
    ct = CONV_COL_TILE
    for src, dst in zip(cast_in, cast_out):
        dst[...] = src[...].astype(BF16)

    @pl.when(t == 0)
    def _():
        st_ref[...] = jnp.zeros_like(st_ref)

    seqs = range(gb)
    x = [x_ref[g] for g in seqs]
    mod = [_split_mod(mod_ref[pl.ds(b * gb + g, 1), :]) for g in seqs]
    h = [_mod_norm_seq(x[g], g_ref[...], mod[g][0], mod[g][1]) for g in seqs]
    row = lax.broadcasted_iota(jnp.int32, (tb, ct), 0)

    for g in seqs:
        def prev_fn(cs, u, g=g):
            c0 = st_ref[g, SUBLANE - 2:SUBLANE - 1, cs]
            c1 = st_ref[g, SUBLANE - 1:SUBLANE, cs]
            p1 = jnp.where(row == 0, c1, pltpu.roll(u, 1, 0))
            p2 = jnp.where(row == 0, c0, jnp.where(row == 1, c1, pltpu.roll(u, 2, 0)))
            return p1, p2

        def tail_fn(cs, u, g=g):
            st_ref[g, :, cs] = u[tb - SUBLANE:tb, :]

        _conv_columns(h[g], win_ref, cw_ref, y_scr.at[g], prev_fn, tail_fn)
    for g in seqs:
        xo_ref[g] = x[g] + mod[g][2] * _dot(y_scr[g], wout_ref[...])


def _conv_prompt_call(x, mods, norm_g, w_in, cw, w_out, layer, mod_block, casts=()):
    bsz, seq, _ = x.shape
    tb = CONV_TB
    gb = CONV_GB
    j = layer // 2
    n_t = seq // tb
    n_steps = (bsz // gb) * n_t
    cast_in_specs, cast_out_specs, cast_shapes = [], [], []
    for mat, first, rows in casts:
        blk = rows // n_steps
        assert rows % n_steps == 0 and blk % (2 * SUBLANE) == 0 and first % blk == 0
        cast_in_specs.append(pl.BlockSpec((blk, mat.shape[1]), lambda b, t, o=first // blk: (o + b * n_t + t, 0)))
        cast_out_specs.append(pl.BlockSpec((blk, mat.shape[1]), lambda b, t: (b * n_t + t, 0)))
        cast_shapes.append(jax.ShapeDtypeStruct((rows, mat.shape[1]), BF16))
    return pl.pallas_call(
        functools.partial(_conv_prompt_kernel, tb=tb, gb=gb, n_cast=len(casts)),
        grid=(bsz // gb, n_t),
        in_specs=[
            pl.BlockSpec((gb, tb, D_MODEL), lambda b, t: (b, t, 0)),
            _resident((None, bsz, 3 * D_MODEL), (layer, mod_block, 0), 2),
            _resident((None, 1, D_MODEL), (layer, 0, 0), 2),
            _resident((None, D_MODEL, 4 * E_CONV), (0, 0, 0), 2),
            _resident((None, CONV_WIDTH, E_CONV), (j, 0, 0), 2),
            _resident((None, E_CONV, D_MODEL), (0, 0, 0), 2),
        ] + cast_in_specs,
        out_specs=[
            pl.BlockSpec((gb, tb, D_MODEL), lambda b, t: (b, t, 0)),
            pl.BlockSpec((gb, SUBLANE, E_CONV), lambda b, t: (b, 0, 0)),
        ] + cast_out_specs,
        out_shape=[
            jax.ShapeDtypeStruct((bsz, seq, D_MODEL), F32),
            jax.ShapeDtypeStruct((bsz, SUBLANE, E_CONV), F32),
        ] + cast_shapes,
        scratch_shapes=[pltpu.VMEM((gb, tb, E_CONV), BF16)],
        compiler_params=_params(2),
        name="conv_prompt",
    )(x, mods, norm_g, w_in, cw, w_out, *[mat for mat, _, _ in casts])


def _tok(s, width):
    return slice(s * width, (s + 1) * width)


def _stack_tokens(ref, n_tok, width, cols=None):
    parts = []
    for s in range(n_tok):
        lo = s * width + (0 if cols is None else cols.start)
        hi = s * width + (width if cols is None else cols.stop)
        parts.append(ref[:, lo:hi])
    return jnp.concatenate(parts, axis=0)


def _sample_hidden(x_ref, mod_ref, g_ref, n_tok):
    shift, scale, _ = _split_mod(mod_ref[...])
    g = g_ref[...]
    return jnp.concatenate(
        [_mod_norm(x_ref[:, _tok(s, D_MODEL)], g, shift, scale) for s in range(n_tok)], axis=0)


def _conv_sample_kernel(x_ref, mod_ref, g_ref, win_ref, cw_ref, wout_ref, st_ref,
                        xo_ref, so_ref, y_scr, *, n_tok):
    ns = x_ref.shape[0]
    h = _sample_hidden(x_ref, mod_ref, g_ref, n_tok)

    def prev_fn(cs, u):
        st0 = st_ref[:, cs]
        st1 = st_ref[:, E_CONV + cs.start:E_CONV + cs.stop]
        p1 = jnp.concatenate([st1, u[:(n_tok - 1) * ns]], axis=0)
        p2 = jnp.concatenate([st0, st1, u[:(n_tok - 2) * ns]], axis=0)
        return p1, p2

    def tail_fn(cs, u):
        so_ref[:, cs] = u[(n_tok - 2) * ns:(n_tok - 1) * ns]
        so_ref[:, E_CONV + cs.start:E_CONV + cs.stop] = u[(n_tok - 1) * ns:]

    _conv_columns(h, win_ref, cw_ref, y_scr, prev_fn, tail_fn)
    out = _dot(y_scr[...], wout_ref[...])
    gate = mod_ref[:, 2 * D_MODEL:]
    for s in range(n_tok):
        xo_ref[:, _tok(s, D_MODEL)] = x_ref[:, _tok(s, D_MODEL)] + gate * out[s * ns:(s + 1) * ns]


def _seq_rows(width, layer=None):
    if layer is None:
        return pl.BlockSpec((SAMPLE_SEQ_BLOCK, width), lambda i: (i, 0))
    return pl.BlockSpec((None, SAMPLE_SEQ_BLOCK, width), lambda i: (layer, i, 0))


def _conv_sample_call(x, mods, norm_g, w_in, cw, w_out, state, layer, n_tok):
    n_seq = x.shape[0]
    j = layer // 2
    return pl.pallas_call(
        functools.partial(_conv_sample_kernel, n_tok=n_tok),
        grid=(n_seq // SAMPLE_SEQ_BLOCK,),
        in_specs=[
            _seq_rows(n_tok * D_MODEL),
            _seq_rows(3 * D_MODEL, layer),
            _resident((None, 1, D_MODEL), (layer, 0, 0), 1),
            _resident((None, D_MODEL, 4 * E_CONV), (0, 0, 0), 1),
            _resident((None, CONV_WIDTH, E_CONV), (j, 0, 0), 1),
            _resident((None, E_CONV, D_MODEL), (0, 0, 0), 1),
            _seq_rows((CONV_WIDTH - 1) * E_CONV, j),
        ],
        out_specs=[_seq_rows(n_tok * D_MODEL), _seq_rows((CONV_WIDTH - 1) * E_CONV)],
        out_shape=[
            jax.ShapeDtypeStruct((n_seq, n_tok * D_MODEL), F32),
            jax.ShapeDtypeStruct((n_seq, (CONV_WIDTH - 1) * E_CONV), F32),
        ],
        scratch_shapes=[pltpu.VMEM((n_tok * SAMPLE_SEQ_BLOCK, E_CONV), BF16)],
        compiler_params=_params(1),
        name="conv_sample",
    )(x, mods, norm_g, w_in, cw, w_out, state)


def _rows_from_blocks(vecs):
    return jnp.concatenate([jnp.broadcast_to(v, (SUB, v.shape[1])) for v in vecs], axis=0)


def _chunk_masks():
    r = lax.broadcasted_iota(jnp.int32, (CHUNK, CHUNK), 0)
    c = lax.broadcasted_iota(jnp.int32, (CHUNK, CHUNK), 1)
    causal = (r // SUB == c // SUB) & (c <= r)
    tri = jnp.where(causal, 1.0, 0.0).astype(BF16)
    first_cols = lax.broadcasted_iota(jnp.int32, (SUB, CHUNK), 1) < SUB
    return causal, first_cols, jnp.concatenate([tri, tri], axis=1)


def _chunk_decay(lf, tri2):
    hi = lf.astype(BF16)
    lo = (lf - hi.astype(F32)).astype(BF16)
    return _dot(tri2, jnp.concatenate([hi, lo], axis=0))


def _chunk_key_operands(kk, gl):
    t0 = gl[SUB - 1:SUB, :]
    t1 = gl[2 * SUB - 1:2 * SUB, :]
    k_mid = kk * jnp.exp2(_rows_from_blocks([0.5 * t0, 0.5 * t1]) - gl)
    k_state = k_mid * _rows_from_blocks([jnp.exp2(0.5 * t0 + t1), jnp.exp2(0.5 * t1)])
    in_span = (t0 >= -HGRN_SAFE_LOG2_SPAN) & (t1 >= -HGRN_SAFE_LOG2_SPAN)
    return k_mid.astype(BF16), k_state.astype(BF16), jnp.exp2(t0 + t1), jnp.where(in_span, 0.0, 1.0)


def _chunk_query_operands(qf, gl):
    t0 = gl[SUB - 1:SUB, :]
    t1 = gl[2 * SUB - 1:2 * SUB, :]
    q_mid = qf * jnp.exp2(gl - _rows_from_blocks([0.5 * t0, 0.5 * t1]))
    q_next = q_mid[SUB:] * jnp.exp2(0.5 * (t0 + t1))
    q_state = q_mid * _rows_from_blocks([jnp.exp2(0.5 * t0), jnp.exp2(0.5 * t1 + t0)])
    return jnp.concatenate([q_mid, q_next], axis=0).astype(BF16), q_state.astype(BF16)


def _chunk_finish(ast, qe, kr, vb, ptot, s_old, causal, first_cols):
    same = jnp.where(causal, ast[0:CHUNK], 0.0)
    a = jnp.concatenate([same[:SUB], jnp.where(first_cols, ast[CHUNK:], same[SUB:])], axis=0)
    o = _dot(jnp.concatenate([qe, a.astype(BF16)], axis=1),
             jnp.concatenate([s_old.astype(BF16), vb], axis=0))
    dec = jnp.broadcast_to(ptot, (HG_DK, HG_DK)).T
    return o, s_old * dec + _dot_tn(kr, vb)


def _exact_recurrence(q_ref, k_ref, f_ref, v_ref, o_ref, s0, n_rows):
    def col(tile, r):
        return jnp.broadcast_to(tile[r:r + 1, :], (HG_DK, HG_DK)).T

    def step(i, s):
        rows = pl.ds(pl.multiple_of(i * SUBLANE, SUBLANE), SUBLANE)
        q, k, f, v = q_ref[rows, :], k_ref[rows, :], f_ref[rows, :], v_ref[rows, :]
        out = []
        for r in range(SUBLANE):
            s = s * col(f, r) + col(k, r) * v[r:r + 1, :]
            out.append(jnp.sum(col(q, r) * s, axis=0, keepdims=True))
        o_ref[rows, :] = jnp.concatenate(out, axis=0)
        return s
    return lax.fori_loop(0, n_rows // SUBLANE, step, s0)


def _decode_state_update(p, el_row0, dq_ref, dk_ref, dv_ref, del_ref, ds_ref, do_ref, dso_ref, partial_o):
    nd = ds_ref.shape[0]
    n_tok = dq_ref.shape[0] // nd
    hd, sq = divmod(p, nd)
    hs = slice(hd * HG_DK, (hd + 1) * HG_DK)
    mine = lax.broadcasted_iota(jnp.int32, (dq_ref.shape[0], HG_DK), 0) // n_tok == sq
    s_old = ds_ref[sq, hd]
    o = _dot(dq_ref[:, hs], s_old.astype(BF16))
    partial_o[hd] = jnp.where(mine, o, partial_o[hd]) if hd in partial_o else jnp.where(mine, o, 0.0)
    k_mine = jnp.where(mine, dk_ref[:, hs], 0.0).astype(BF16)
    dec = jnp.broadcast_to(del_ref[pl.ds(el_row0 + sq, 1), :][:, hs], (HG_DK, HG_DK)).T
    dso_ref[sq, hd] = s_old * dec + _dot_tn(k_mine, dv_ref[:, hs])
    if sq == nd - 1:
        do_ref[:, hs] = partial_o.pop(hd)


def _hgrn_prompt_kernel(*refs, tb, gb, layer, final, has_prev):
    n_in = 13 + int(has_prev)
    x_ref, mod_ref, g_ref, win_ref, lbraw_ref, og_ref, wout_ref, fg_ref = refs[:8]
    decode_in = refs[8:13]
    xo_ref, s_ref, do_ref, dso_ref = refs[n_in:n_in + 4]
    q_scr, lf_scr, k_scr, v_scr, z_scr, o_scr, s0_scr = refs[n_in + 4:]
    b = pl.program_id(0)
    t = pl.program_id(1)

    @pl.when(t == 0)
    def _():
        s_ref[...] = jnp.zeros_like(s_ref)

    s0_scr[...] = s_ref[...]

    seqs = range(gb)
    lb = _lower_bound(lbraw_ref[...], layer)
    x = [x_ref[g] for g in seqs]
    mod = [_split_mod(mod_ref[pl.ds(b * gb + g, 1), :]) for g in seqs]
    h = [_mod_norm_seq(x[g], g_ref[...], mod[g][0], mod[g][1]) for g in seqs]
    for g in seqs:
        q_scr[g] = _silu(_dot(h[g], win_ref[:, 0:HG_F]))
    for g in seqs:
        logf, kk = _gate_math(_dot(h[g], win_ref[:, HG_F:2 * HG_F]), lb)
        lf_scr[g] = logf
        k_scr[g] = kk
    for g in seqs:
        v_scr[g] = _dot(h[g], win_ref[:, 2 * HG_F:2 * HG_F + HG_I])
    for g in seqs:
        z_scr[g] = _dot(h[g], win_ref[:, 2 * HG_F + HG_I:])

    causal, first_cols, tri2 = _chunk_masks()
    chains = [(c, hd, g) for c in range(tb // CHUNK) for hd in range(HG_HEADS) for g in seqs]
    ops, scores = {}, {}
    unsafe = jnp.zeros((1, HG_DK), F32)

    def where(n):
        c, hd, g = chains[n]
        return g, hd, slice(c * CHUNK, (c + 1) * CHUNK), slice(hd * HG_DK, (hd + 1) * HG_DK)

    block_seqs = decode_in[4].shape[0]
    n_decode = HG_HEADS * block_seqs
    blocks_per_el = decode_in[3].shape[0] // block_seqs
    el_row0 = ((b * pl.num_programs(1) + t) % blocks_per_el) * block_seqs
    partial_o = {}
    for step in range(max(len(chains) + 2 * HGRN_PIPE_SKEW, n_decode)):
        if step < n_decode:
            _decode_state_update(step, el_row0, *decode_in, do_ref, dso_ref, partial_o)
        n = step
        if n < len(chains):
            g, hd, rows, hs = where(n)
            gl = _chunk_decay(lf_scr[g, rows, hs], tri2)
            km, kr, ptot, bad = _chunk_key_operands(k_scr[g, rows, hs], gl)
            lhs, qe = _chunk_query_operands(q_scr[g, rows, hs], gl)
            ops[n] = (lhs, km, qe, kr, ptot)
            unsafe = jnp.maximum(unsafe, bad)
        n = step - HGRN_PIPE_SKEW
        if 0 <= n < len(chains):
            scores[n] = _dot_nt(ops[n][0], ops[n][1])
        n = step - 2 * HGRN_PIPE_SKEW
        if 0 <= n < len(chains):
            g, hd, rows, hs = where(n)
            _, _, qe, kr, ptot = ops.pop(n)
            vb = v_scr[g, rows, hs].astype(BF16)
            o, s_new = _chunk_finish(scores.pop(n), qe, kr, vb, ptot, s_ref[g, hd], causal, first_cols)
            o_scr[g, rows, hs] = o
            s_ref[g, hd] = s_new

    y = [_head_norm_gate(o_scr[g], z_scr[g], og_ref[...]) for g in seqs]
    for g in seqs:
        xn = x[g] + mod[g][2] * _dot(y[g], wout_ref[...])
        if final:
            xn = _rmsnorm(xn, fg_ref[...])
        xo_ref[g] = xn

    @pl.when(jnp.max(unsafe) > 0.0)
    def _():
        def exact(q_tmp, k_tmp, f_tmp, v_tmp, o_tmp):
            def one_seq(g, carry):
                shift, scale, gate = _split_mod(mod_ref[pl.ds(b * gb + g, 1), :])
                hg = _mod_norm_seq(x_ref[g], g_ref[...], shift, scale)
                qf = _silu(_dot(hg, win_ref[:, 0:HG_F]))
                logf, kk = _gate_math(_dot(hg, win_ref[:, HG_F:2 * HG_F]), lb)
                ff = jnp.exp2(logf)
                vv = _dot(hg, win_ref[:, 2 * HG_F:2 * HG_F + HG_I])
                for hd in range(HG_HEADS):
                    hs = slice(hd * HG_DK, (hd + 1) * HG_DK)
                    q_tmp[hd], k_tmp[hd], f_tmp[hd], v_tmp[hd] = qf[:, hs], kk[:, hs], ff[:, hs], vv[:, hs]

                def one_head(hd, c2):
                    s_ref[g, hd] = _exact_recurrence(q_tmp.at[hd], k_tmp.at[hd], f_tmp.at[hd], v_tmp.at[hd],
                                                     o_tmp.at[hd], s0_scr[g, hd], tb)
                    return c2

                lax.fori_loop(0, HG_HEADS, one_head, 0)
                o = jnp.concatenate([o_tmp[hd] for hd in range(HG_HEADS)], axis=1)
                y = _head_norm_gate(o, z_scr[g], og_ref[...])
                xn = x_ref[g] + gate * _dot(y, wout_ref[...])
                if final:
                    xn = _rmsnorm(xn, fg_ref[...])
                xo_ref[g] = xn
                return carry

            lax.fori_loop(0, gb, one_seq, 0)

        pl.run_scoped(exact, *[pltpu.VMEM((HG_HEADS, tb, HG_DK), F32)] * 5)


def _hgrn_prompt_call(x, mods, norm_g, w_in, lb_raw, og, w_out, fg, decode, layer, mod_block):
    bsz, seq, _ = x.shape
    tb = HGRN_TB
    gb = HGRN_GB
    j = layer // 2
    final = layer == DEPTH - 1
    n_t = seq // tb
    qe, kr, v, el, states, prev_states = decode
    n_blocks = (bsz // gb) * n_t
    n_seq = el.shape[0]
    block_seqs = n_seq // n_blocks
    dec_rows = qe.shape[0] // n_blocks
    el_rows = max(block_seqs, SUBLANE)
    assert n_seq == n_blocks * block_seqs and el_rows % block_seqs == 0 and dec_rows % SUBLANE == 0
    dec_spec = pl.BlockSpec((dec_rows, HG_F), lambda b, t: (b * n_t + t, 0))
    st_spec = pl.BlockSpec((None, block_seqs, HG_HEADS, HG_DK, HG_DV), lambda b, t: (j, b * n_t + t, 0, 0, 0))
    in_specs = [
        pl.BlockSpec((gb, tb, D_MODEL), lambda b, t: (b, t, 0)),
        _resident((None, bsz, 3 * D_MODEL), (layer, mod_block, 0), 2),
        _resident((None, 1, D_MODEL), (layer, 0, 0), 2),
        _resident((None, D_MODEL, 2 * HG_F + 2 * HG_I), (j, 0, 0), 2),
        _resident((DEPTH, HG_F), (0, 0), 2),
        _resident((None, 1, HG_I), (j, 0, 0), 2),
        _resident((None, HG_I, D_MODEL), (j, 0, 0), 2),
        _resident((1, D_MODEL), (0, 0), 2),
        dec_spec, dec_spec, dec_spec,
        pl.BlockSpec((el_rows, HG_F), lambda b, t: ((b * n_t + t) * block_seqs // el_rows, 0)),
        st_spec,
    ]
    args = [x, mods, norm_g, w_in, lb_raw, og, w_out, fg, qe, kr, v, el, states]
    aliases = {}
    if prev_states is not None:
        in_specs.append(pl.BlockSpec(memory_space=pl.ANY))
        args.append(prev_states)
        aliases = {len(args) - 1: 3}
    return pl.pallas_call(
        functools.partial(_hgrn_prompt_kernel, tb=tb, gb=gb, layer=layer, final=final,
                          has_prev=prev_states is not None),
        grid=(bsz // gb, n_t),
        in_specs=in_specs,
        out_specs=[
            pl.BlockSpec((gb, tb, D_MODEL), lambda b, t: (b, t, 0)),
            pl.BlockSpec((gb, HG_HEADS, HG_DK, HG_DV), lambda b, t: (b, 0, 0, 0)),
            dec_spec,
            st_spec,
        ],
        out_shape=[
            jax.ShapeDtypeStruct((bsz, seq, D_MODEL), F32),
            jax.ShapeDtypeStruct((bsz, HG_HEADS, HG_DK, HG_DV), F32),
            jax.ShapeDtypeStruct(qe.shape, F32),
            jax.ShapeDtypeStruct(states.shape, F32),
        ],
        scratch_shapes=[pltpu.VMEM((gb, tb, HG_F), F32)] * 6
        + [pltpu.VMEM((gb, HG_HEADS, HG_DK, HG_DV), F32)],
        input_output_aliases=aliases,
        compiler_params=_params(2),
        name="hgrn_prompt",
    )(*args)


def _token_row_permutation(n_seq, n_tok, seq_major_out):
    n = n_seq * n_tok
    r = lax.broadcasted_iota(jnp.int32, (n, n), 0)
    c = lax.broadcasted_iota(jnp.int32, (n, n), 1)
    sm, tm = (r, c) if seq_major_out else (c, r)
    return jnp.where(tm == (sm % n_tok) * n_seq + sm // n_tok, 1.0, 0.0).astype(BF16)


def _permute_rows_f32(p, a):
    hi = a.astype(BF16)
    rest = a - hi.astype(F32)
    mid = rest.astype(BF16)
    lo = (rest - mid.astype(F32)).astype(BF16)
    return (_dot(p, hi) + _dot(p, mid)) + _dot(p, lo)


def _hgrn_sample_pre_kernel(x_ref, mod_ref, g_ref, win_ref, lbraw_ref,
                            qe_ref, kr_ref, v_ref, el_ref, oi_ref, z_ref, *, n_tok, layer):
    ns = x_ref.shape[0]
    h = _sample_hidden(x_ref, mod_ref, g_ref, n_tok)
    lb = _lower_bound(lbraw_ref[...], layer)
    qf = _silu(_dot(h, win_ref[:, 0:HG_F]))
    lf, kk = _gate_math(_dot(h, win_ref[:, HG_F:2 * HG_F]), lb)
    vv = _dot(h, win_ref[:, 2 * HG_F:2 * HG_F + HG_I])
    z = _dot(h, win_ref[:, 2 * HG_F + HG_I:])

    def tok(a, s):
        return a[s * ns:(s + 1) * ns]

    def decay(s, t):
        if s == t:
            return 1.0
        return jnp.exp2(functools.reduce(lambda a, b: a + b, [tok(lf, u) for u in range(s + 1, t + 1)]))

    last = n_tok - 1
    el_ref[...] = decay(-1, last)
    to_seq_major = _token_row_permutation(ns, n_tok, seq_major_out=True)
    qe = jnp.concatenate([tok(qf, s) * decay(-1, s) for s in range(n_tok)], axis=0)
    kr = jnp.concatenate([tok(kk, s) * decay(s, last) for s in range(n_tok)], axis=0)
    qe_ref[...] = _dot(to_seq_major, qe.astype(BF16)).astype(BF16)
    kr_ref[...] = _dot(to_seq_major, kr.astype(BF16)).astype(BF16)
    v_ref[...] = _dot(to_seq_major, vv.astype(BF16)).astype(BF16)
    for s in range(n_tok):
        z_ref[:, _tok(s, HG_I)] = tok(z, s)

    for t in range(n_tok):
        acc = jnp.zeros((ns, HG_I), F32)
        for s in range(t + 1):
            e = tok(qf, t) * tok(kk, s) * decay(s, t)
            vs = tok(vv, s)
            parts = []
            for hd in range(HG_HEADS):
                hs = slice(hd * HG_DK, (hd + 1) * HG_DK)
                parts.append(jnp.sum(e[:, hs], axis=-1, keepdims=True) * vs[:, hs])
            acc = acc + jnp.concatenate(parts, axis=1)
        oi_ref[:, _tok(t, HG_I)] = acc


def _hgrn_sample_pre_call(x, mods, norm_g, w_in, lb_raw, layer, n_tok):
    n_seq = x.shape[0]
    j = layer // 2
    act = jax.ShapeDtypeStruct((n_seq, n_tok * HG_F), F32)
    tok_rows = jax.ShapeDtypeStruct((n_seq * n_tok, HG_F), BF16)
    tok_spec = pl.BlockSpec((SAMPLE_SEQ_BLOCK * n_tok, HG_F), lambda i: (i, 0))
    return pl.pallas_call(
        functools.partial(_hgrn_sample_pre_kernel, n_tok=n_tok, layer=layer),
        grid=(n_seq // SAMPLE_SEQ_BLOCK,),
        in_specs=[
            _seq_rows(n_tok * D_MODEL),
            _seq_rows(3 * D_MODEL, layer),
            _resident((None, 1, D_MODEL), (layer, 0, 0), 1),
            _resident((None, D_MODEL, 2 * HG_F + 2 * HG_I), (j, 0, 0), 1),
            _resident((DEPTH, HG_F), (0, 0), 1),
        ],
        out_specs=[tok_spec] * 3 + [_seq_rows(HG_F)] + [_seq_rows(n_tok * HG_I)] * 2,
        out_shape=[tok_rows, tok_rows, tok_rows, jax.ShapeDtypeStruct((n_seq, HG_F), F32), act, act],
        compiler_params=_params(1),
        name="hgrn_sample_pre",
    )(x, mods, norm_g, w_in, lb_raw)


def _hgrn_sample_post_kernel(x_ref, mod_ref, oa_ref, ob_ref, z_ref, og_ref, wout_ref, fg_ref, xo_ref,
                             *, n_tok, final):
    ns = x_ref.shape[0]
    o_state = _permute_rows_f32(_token_row_permutation(ns, n_tok, seq_major_out=False), oa_ref[...])
    o = o_state + _stack_tokens(ob_ref, n_tok, HG_I)
    y = _head_norm_gate(o, _stack_tokens(z_ref, n_tok, HG_I), og_ref[...])
    out = _dot(y, wout_ref[...])
    gate = mod_ref[:, 2 * D_MODEL:]
    for s in range(n_tok):
        xn = x_ref[:, _tok(s, D_MODEL)] + gate * out[s * ns:(s + 1) * ns]
        if final:
            xn = _rmsnorm(xn, fg_ref[...])
        xo_ref[:, _tok(s, D_MODEL)] = xn


def _hgrn_sample_post_call(x, mods, o_inter, o_intra, z, og, w_out, fg, layer, n_tok):
    n_seq = x.shape[0]
    j = layer // 2
    return pl.pallas_call(
        functools.partial(_hgrn_sample_post_kernel, n_tok=n_tok, final=layer == DEPTH - 1),
        grid=(n_seq // SAMPLE_SEQ_BLOCK,),
        in_specs=[
            _seq_rows(n_tok * D_MODEL),
            _seq_rows(3 * D_MODEL, layer),
            pl.BlockSpec((SAMPLE_SEQ_BLOCK * n_tok, HG_I), lambda i: (i, 0)),
            _seq_rows(n_tok * HG_I), _seq_rows(n_tok * HG_I),
            _resident((None, 1, HG_I), (j, 0, 0), 1),
            _resident((None, HG_I, D_MODEL), (j, 0, 0), 1),
            _resident((1, D_MODEL), (0, 0), 1),
        ],
        out_specs=_seq_rows(n_tok * D_MODEL),
        out_shape=jax.ShapeDtypeStruct((n_seq, n_tok * D_MODEL), F32),
        compiler_params=_params(1),
        name="hgrn_sample_post",
    )(x, mods, o_inter, o_intra, z, og, w_out, fg)


def kernel(x_prompt, x_sample, state_conv, state_hgrn, c_prompt, c_sample, norm_g, w_ada, b_ada, conv_w_in, conv_w, conv_w_out, hgrn_w_in, hgrn_lower_bounds, hgrn_onorm_g, hgrn_w_out, final_norm_g):
    n_p = x_prompt.shape[0]
    n_s, t_s, _ = x_sample.shape
    assert t_s >= CONV_WIDTH - 1 and n_s % SAMPLE_SEQ_BLOCK == 0 and n_s % n_p == 0

    mods = _ada_call(jnp.concatenate([c_sample, c_prompt], axis=0), w_ada, b_ada)
    mod_block_p = n_s // n_p

    n_conv, n_hgrn = conv_w_in.shape[0], hgrn_w_in.shape[0]
    assert n_conv == 2 and DEPTH == 4
    conv_w_in_b = [conv_w_in[:1].astype(BF16)]
    conv_w_out_b = [conv_w_out[:1].astype(BF16)]
    later_weights = [
        (hgrn_w_in.reshape(n_hgrn * D_MODEL, -1), 0, n_hgrn * D_MODEL),
        (hgrn_w_out.reshape(n_hgrn * HG_I, D_MODEL), 0, n_hgrn * HG_I),
        (conv_w_in.reshape(n_conv * D_MODEL, -1), D_MODEL, (n_conv - 1) * D_MODEL),
        (conv_w_out.reshape(n_conv * E_CONV, D_MODEL), E_CONV, (n_conv - 1) * E_CONV),
    ]
    norm_g3 = norm_g.reshape(DEPTH, 1, D_MODEL)
    og3 = hgrn_onorm_g.reshape(DEPTH // 2, 1, HG_I)
    fg = final_norm_g.reshape(1, D_MODEL)
    state_conv2 = state_conv.reshape(state_conv.shape[0], n_s, (CONV_WIDTH - 1) * E_CONV)

    xp = x_prompt
    xs = x_sample.reshape(n_s, t_s * D_MODEL)
    conv_p, conv_s, hgrn_p = [], [], []
    hgrn_s = None
    for layer in range(DEPTH):
        if layer % 2 == 0:
            j = layer // 2
            xp, tail, *rounded = _conv_prompt_call(xp, mods, norm_g3, conv_w_in_b[j], conv_w, conv_w_out_b[j],
                                                   layer, mod_block_p, later_weights if layer == 0 else ())
            if layer == 0:
                hgrn_w_in_b = rounded[0].reshape(hgrn_w_in.shape)
                hgrn_w_out_b = rounded[1].reshape(hgrn_w_out.shape)
                conv_w_in_b.append(rounded[2].reshape((1,) + conv_w_in.shape[1:]))
                conv_w_out_b.append(rounded[3].reshape((1,) + conv_w_out.shape[1:]))
            conv_p.append(tail[:, SUBLANE - (CONV_WIDTH - 1):])
            xs, st = _conv_sample_call(xs, mods, norm_g3, conv_w_in_b[j], conv_w, conv_w_out_b[j], state_conv2,
                                       layer, t_s)
            conv_s.append(st.reshape(n_s, CONV_WIDTH - 1, E_CONV))
        else:
            qe, kr, v, el, o_intra, z = _hgrn_sample_pre_call(xs, mods, norm_g3, hgrn_w_in_b,
                                                              hgrn_lower_bounds, layer, t_s)
            decode = (qe, kr, v, el, state_hgrn, hgrn_s)
            xp, s_new, o_inter, hgrn_s = _hgrn_prompt_call(xp, mods, norm_g3, hgrn_w_in_b, hgrn_lower_bounds,
                                                           og3, hgrn_w_out_b, fg, decode, layer, mod_block_p)
            hgrn_p.append(s_new)
            xs = _hgrn_sample_post_call(xs, mods, o_inter, o_intra, z, og3, hgrn_w_out_b, fg, layer, t_s)
    return (xp, xs.reshape(n_s, t_s, D_MODEL), jnp.stack(conv_p), jnp.stack(hgrn_p), jnp.stack(conv_s), hgrn_s)
```

```python
import functools

import jax
import jax.numpy as jnp
from jax import lax
from jax.experimental import pallas as pl
from jax.experimental.pallas import tpu as pltpu

F32 = jnp.float32
BF16 = jnp.bfloat16

D_MODEL = 1024
DEPTH = 4
CONV_WIDTH = 3
E_CONV = D_MODEL
HG_HEADS = 8
HG_DK = 128
HG_DV = 128
HG_F = HG_HEADS * HG_DK
HG_I = HG_HEADS * HG_DV
EPS = 1e-6

SUBLANE = 8
CHUNK = 128
SUB = CHUNK // 2
CONV_COL_TILE = 256
CONV_TB = 512
CONV_GB = 2
HGRN_TB = 256
HGRN_GB = 2
HGRN_PIPE_SKEW = 3
HGRN_SAFE_LOG2_SPAN = 200.0
SAMPLE_SEQ_BLOCK = 32
VMEM_LIMIT = 56 * 1024 * 1024


def _dot(a, b):
    return jnp.dot(a, b, preferred_element_type=F32)


def _dot_nt(a, b):
    return lax.dot_general(a, b, (((1,), (1,)), ((), ())), preferred_element_type=F32)


def _dot_tn(a, b):
    return lax.dot_general(a, b, (((0,), (0,)), ((), ())), preferred_element_type=F32)


def _silu(x):
    return x * jax.nn.sigmoid(x)


def _rmsnorm(x, g):
    ms = jnp.mean(x * x, axis=-1, keepdims=True)
    return x * lax.rsqrt(ms + EPS) * g


def _split_mod(mod):
    return mod[:, :D_MODEL], mod[:, D_MODEL:2 * D_MODEL], mod[:, 2 * D_MODEL:]


def _mod_norm(x, g, shift, scale):
    return (_rmsnorm(x, g) * (1.0 + scale) + shift).astype(BF16)


def _mod_norm_seq(x, g, shift, scale):
    ms = jnp.mean(x * x, axis=-1, keepdims=True)
    return (x * lax.rsqrt(ms + EPS) * (g * (1.0 + scale)) + shift).astype(BF16)


def _lower_bound(raw, layer):
    rows = [raw[i:i + 1, :] for i in range(DEPTH)]
    m = functools.reduce(jnp.maximum, rows)
    es = [jnp.exp(r - m) for r in rows]
    tot = functools.reduce(lambda a, b: a + b, es)
    acc = es[1]
    for i in range(2, layer + 1):
        acc = acc + es[i]
    return acc / tot


def _gate_math(fpre, lb):
    t = jnp.exp(-jnp.abs(fpre))
    r = 1.0 / (1.0 + t)
    pos = fpre >= 0.0
    sig = jnp.where(pos, 1.0, t) * r
    sig_neg = jnp.where(pos, t, 1.0) * r
    return jnp.log2(lb + (1.0 - lb) * sig), (1.0 - lb) * sig_neg


def _head_norm_gate(o, z, og):
    parts = []
    for hd in range(HG_HEADS):
        hs = slice(hd * HG_DV, (hd + 1) * HG_DV)
        oh = o[:, hs]
        ms = jnp.mean(oh * oh, axis=-1, keepdims=True)
        parts.append(oh * lax.rsqrt(ms + EPS))
    on = jnp.concatenate(parts, axis=1) * og
    return (on * _silu(z)).astype(BF16)


def _params(n_grid):
    return pltpu.CompilerParams(dimension_semantics=("arbitrary",) * n_grid, vmem_limit_bytes=VMEM_LIMIT)


def _resident(shape, index, n_grid):
    if n_grid == 1:
        return pl.BlockSpec(shape, lambda i: index, pipeline_mode=pl.Buffered(1))
    return pl.BlockSpec(shape, lambda b, t: index, pipeline_mode=pl.Buffered(1))


def _ada_kernel(c_ref, w_ref, b_ref, o_ref):
    s = _silu(c_ref[...]).astype(BF16)
    o_ref[...] = _dot(s, w_ref[...].astype(BF16)) + b_ref[...]


def _ada_call(c_all, w_ada, b_ada):
    rows = c_all.shape[0]
    n_col = 1
    col = 3 * D_MODEL // n_col
    return pl.pallas_call(
        _ada_kernel,
        grid=(DEPTH, n_col),
        in_specs=[
            pl.BlockSpec((rows, D_MODEL), lambda l, n: (0, 0)),
            pl.BlockSpec((None, D_MODEL, col), lambda l, n: (l, 0, n)),
            pl.BlockSpec((None, 1, col), lambda l, n: (l, 0, n)),
        ],
        out_specs=pl.BlockSpec((None, rows, col), lambda l, n: (l, 0, n)),
        out_shape=jax.ShapeDtypeStruct((DEPTH, rows, 3 * D_MODEL), F32),
        compiler_params=_params(2),
        name="ada_mod",
    )(c_all, w_ada, b_ada.reshape(DEPTH, 1, 3 * D_MODEL))


def _conv_columns(h, win_ref, cw_ref, y_scr, prev_fn, tail_fn):
    ct = CONV_COL_TILE
    for n in range(E_CONV // ct):
        def col(j):
            return slice(j * E_CONV + n * ct, j * E_CONV + (n + 1) * ct)
        cs = slice(n * ct, (n + 1) * ct)
        b_gate = _dot(h, win_ref[:, col(0)])
        c_gate = _dot(h, win_ref[:, col(1)])
        v = _dot(h, win_ref[:, col(2)])
        z = _dot(h, win_ref[:, col(3)])
        u = c_gate * v
        p1, p2 = prev_fn(cs, u)
        conv = cw_ref[0:1, cs] * p2 + cw_ref[1:2, cs] * p1 + cw_ref[2:3, cs] * u
        y_scr[:, cs] = (b_gate * conv * _silu(z)).astype(BF16)
        tail_fn(cs, u)


def _conv_prompt_kernel(*refs, tb, gb, n_cast):
    x_ref, mod_ref, g_ref, win_ref, cw_ref, wout_ref = refs[:6]
    cast_in = refs[6:6 + n_cast]
    xo_ref, st_ref = refs[6 + n_cast:8 + n_cast]
    cast_out = refs[8 + n_cast:8 + 2 * n_cast]
    y_scr, = refs[8 + 2 * n_cast:]
    b = pl.program_id(0)
    t = pl.program_id(1)
    ct = CONV_COL_TILE
    for src, dst in zip(cast_in, cast_out):
        dst[...] = src[...].astype(BF16)

    @pl.when(t == 0)
    def _():
        st_ref[...] = jnp.zeros_like(st_ref)

    seqs = range(gb)
    x = [x_ref[g] for g in seqs]
    mod = [_split_mod(mod_ref[pl.ds(b * gb + g, 1), :]) for g in seqs]
    h = [_mod_norm_seq(x[g], g_ref[...], mod[g][0], mod[g][1]) for g in seqs]
    row = lax.broadcasted_iota(jnp.int32, (tb, ct), 0)

    for g in seqs:
        def prev_fn(cs, u, g=g):
            c0 = st_ref[g, SUBLANE - 2:SUBLANE - 1, cs]
            c1 = st_ref[g, SUBLANE - 1:SUBLANE, cs]
            p1 = jnp.where(row == 0, c1, pltpu.roll(u, 1, 0))
            p2 = jnp.where(row == 0, c0, jnp.where(row == 1, c1, pltpu.roll(u, 2, 0)))
            return p1, p2

        def tail_fn(cs, u, g=g):
            st_ref[g, :, cs] = u[tb - SUBLANE:tb, :]

        _conv_columns(h[g], win_ref, cw_ref, y_scr.at[g], prev_fn, tail_fn)
    for g in seqs:
        xo_ref[g] = x[g] + mod[g][2] * _dot(y_scr[g], wout_ref[...])


def _conv_prompt_call(x, mods, norm_g, w_in, cw, w_out, layer, mod_block, casts=()):
    bsz, seq, _ = x.shape
    tb = CONV_TB
    gb = CONV_GB
    j = layer // 2
    n_t = seq // tb
    n_steps = (bsz // gb) * n_t
    cast_in_specs, cast_out_specs, cast_shapes = [], [], []
    for mat, first, rows in casts:
        blk = rows // n_steps
        assert rows % n_steps == 0 and blk % (2 * SUBLANE) == 0 and first % blk == 0
        cast_in_specs.append(pl.BlockSpec((blk, mat.shape[1]), lambda b, t, o=first // blk: (o + b * n_t + t, 0)))
        cast_out_specs.append(pl.BlockSpec((blk, mat.shape[1]), lambda b, t: (b * n_t + t, 0)))
        cast_shapes.append(jax.ShapeDtypeStruct((rows, mat.shape[1]), BF16))
    return pl.pallas_call(
        functools.partial(_conv_prompt_kernel, tb=tb, gb=gb, n_cast=len(casts)),
        grid=(bsz // gb, n_t),
        in_specs=[
            pl.BlockSpec((gb, tb, D_MODEL), lambda b, t: (b, t, 0)),
            _resident((None, bsz, 3 * D_MODEL), (layer, mod_block, 0), 2),
            _resident((None, 1, D_MODEL), (layer, 0, 0), 2),
            _resident((None, D_MODEL, 4 * E_CONV), (0, 0, 0), 2),
            _resident((None, CONV_WIDTH, E_CONV), (j, 0, 0), 2),
            _resident((None, E_CONV, D_MODEL), (0, 0, 0), 2),
        ] + cast_in_specs,
        out_specs=[
            pl.BlockSpec((gb, tb, D_MODEL), lambda b, t: (b, t, 0)),
            pl.BlockSpec((gb, SUBLANE, E_CONV), lambda b, t: (b, 0, 0)),
        ] + cast_out_specs,
        out_shape=[
            jax.ShapeDtypeStruct((bsz, seq, D_MODEL), F32),
            jax.ShapeDtypeStruct((bsz, SUBLANE, E_CONV), F32),
        ] + cast_shapes,
        scratch_shapes=[pltpu.VMEM((gb, tb, E_CONV), BF16)],
        compiler_params=_params(2),
        name="conv_prompt",
    )(x, mods, norm_g, w_in, cw, w_out, *[mat for mat, _, _ in casts])


def _tok(s, width):
    return slice(s * width, (s + 1) * width)


def _stack_tokens(ref, n_tok, width, cols=None):
    parts = []
    for s in range(n_tok):
        lo = s * width + (0 if cols is None else cols.start)
        hi = s * width + (width if cols is None else cols.stop)
        parts.append(ref[:, lo:hi])
    return jnp.concatenate(parts, axis=0)


def _read_tok(ref, s, width):
    return ref[:, s, :] if len(ref.shape) == 3 else ref[:, _tok(s, width)]


def _write_tok(ref, s, width, val):
    if len(ref.shape) == 3:
        ref[:, s, :] = val
    else:
        ref[:, _tok(s, width)] = val


def _sample_hidden(x_ref, mod_ref, g_ref, n_tok):
    shift, scale, _ = _split_mod(mod_ref[...])
    g = g_ref[...]
    return jnp.concatenate(
        [_mod_norm(_read_tok(x_ref, s, D_MODEL), g, shift, scale) for s in range(n_tok)], axis=0)


def _conv_sample_kernel(x_ref, mod_ref, g_ref, win_ref, cw_ref, wout_ref, st_ref,
                        xo_ref, so_ref, y_scr, *, n_tok):
    ns = x_ref.shape[0]
    h = _sample_hidden(x_ref, mod_ref, g_ref, n_tok)

    def prev_fn(cs, u):
        st0 = st_ref[:, 0, cs]
        st1 = st_ref[:, 1, cs]
        p1 = jnp.concatenate([st1, u[:(n_tok - 1) * ns]], axis=0)
        p2 = jnp.concatenate([st0, st1, u[:(n_tok - 2) * ns]], axis=0)
        return p1, p2

    def tail_fn(cs, u):
        so_ref[:, 0, cs] = u[(n_tok - 2) * ns:(n_tok - 1) * ns]
        so_ref[:, 1, cs] = u[(n_tok - 1) * ns:]

    _conv_columns(h, win_ref, cw_ref, y_scr, prev_fn, tail_fn)
    out = _dot(y_scr[...], wout_ref[...])
    gate = mod_ref[:, 2 * D_MODEL:]
    for s in range(n_tok):
        xo_ref[:, _tok(s, D_MODEL)] = _read_tok(x_ref, s, D_MODEL) + gate * out[s * ns:(s + 1) * ns]


def _seq_rows(width, layer=None):
    if layer is None:
        return pl.BlockSpec((SAMPLE_SEQ_BLOCK, width), lambda i: (i, 0))
    return pl.BlockSpec((None, SAMPLE_SEQ_BLOCK, width), lambda i: (layer, i, 0))


def _conv_sample_call(x, mods, norm_g, w_in, cw, w_out, state, layer, n_tok):
    n_seq = x.shape[0]
    j = layer // 2
    return pl.pallas_call(
        functools.partial(_conv_sample_kernel, n_tok=n_tok),
        grid=(n_seq // SAMPLE_SEQ_BLOCK,),
        in_specs=[
            _seq_rows(n_tok * D_MODEL) if x.ndim == 2 else
            pl.BlockSpec((SAMPLE_SEQ_BLOCK, n_tok, D_MODEL), lambda i: (i, 0, 0)),
            _seq_rows(3 * D_MODEL, layer),
            _resident((None, 1, D_MODEL), (layer, 0, 0), 1),
            _resident((None, D_MODEL, 4 * E_CONV), (0, 0, 0), 1),
            _resident((None, CONV_WIDTH, E_CONV), (j, 0, 0), 1),
            _resident((None, E_CONV, D_MODEL), (0, 0, 0), 1),
            pl.BlockSpec((None, SAMPLE_SEQ_BLOCK, CONV_WIDTH - 1, E_CONV), lambda i: (j, i, 0, 0)),
        ],
        out_specs=[_seq_rows(n_tok * D_MODEL),
                   pl.BlockSpec((SAMPLE_SEQ_BLOCK, CONV_WIDTH - 1, E_CONV), lambda i: (i, 0, 0))],
        out_shape=[
            jax.ShapeDtypeStruct((n_seq, n_tok * D_MODEL), F32),
            jax.ShapeDtypeStruct((n_seq, CONV_WIDTH - 1, E_CONV), F32),
        ],
        scratch_shapes=[pltpu.VMEM((n_tok * SAMPLE_SEQ_BLOCK, E_CONV), BF16)],
        compiler_params=_params(1),
        name="conv_sample",
    )(x, mods, norm_g, w_in, cw, w_out, state)


def _rows_from_blocks(vecs):
    return jnp.concatenate([jnp.broadcast_to(v, (SUB, v.shape[1])) for v in vecs], axis=0)


def _chunk_masks():
    r = lax.broadcasted_iota(jnp.int32, (CHUNK, CHUNK), 0)
    c = lax.broadcasted_iota(jnp.int32, (CHUNK, CHUNK), 1)
    causal = (r // SUB == c // SUB) & (c <= r)
    tri = jnp.where(causal, 1.0, 0.0).astype(BF16)
    first_cols = lax.broadcasted_iota(jnp.int32, (SUB, CHUNK), 1) < SUB
    return causal, first_cols, jnp.concatenate([tri, tri], axis=1)


def _chunk_decay(lf, tri2):
    hi = lf.astype(BF16)
    lo = (lf - hi.astype(F32)).astype(BF16)
    return _dot(tri2, jnp.concatenate([hi, lo], axis=0))


def _chunk_key_operands(kk, gl):
    t0 = gl[SUB - 1:SUB, :]
    t1 = gl[2 * SUB - 1:2 * SUB, :]
    k_mid = kk * jnp.exp2(_rows_from_blocks([0.5 * t0, 0.5 * t1]) - gl)
    k_state = k_mid * _rows_from_blocks([jnp.exp2(0.5 * t0 + t1), jnp.exp2(0.5 * t1)])
    in_span = (t0 >= -HGRN_SAFE_LOG2_SPAN) & (t1 >= -HGRN_SAFE_LOG2_SPAN)
    return k_mid.astype(BF16), k_state.astype(BF16), jnp.exp2(t0 + t1), jnp.where(in_span, 0.0, 1.0)


def _chunk_query_operands(qf, gl):
    t0 = gl[SUB - 1:SUB, :]
    t1 = gl[2 * SUB - 1:2 * SUB, :]
    q_mid = qf * jnp.exp2(gl - _rows_from_blocks([0.5 * t0, 0.5 * t1]))
    q_next = q_mid[SUB:] * jnp.exp2(0.5 * (t0 + t1))
    q_state = q_mid * _rows_from_blocks([jnp.exp2(0.5 * t0), jnp.exp2(0.5 * t1 + t0)])
    return jnp.concatenate([q_mid, q_next], axis=0).astype(BF16), q_state.astype(BF16)


def _chunk_finish(ast, qe, kr, vb, ptot, s_old, causal, first_cols):
    same = jnp.where(causal, ast[0:CHUNK], 0.0)
    a = jnp.concatenate([same[:SUB], jnp.where(first_cols, ast[CHUNK:], same[SUB:])], axis=0)
    o = _dot(jnp.concatenate([qe, a.astype(BF16)], axis=1),
             jnp.concatenate([s_old.astype(BF16), vb], axis=0))
    dec = jnp.broadcast_to(ptot, (HG_DK, HG_DK)).T
    return o, s_old * dec + _dot_tn(kr, vb)


def _exact_recurrence(q_ref, k_ref, f_ref, v_ref, o_ref, s0, n_rows):
    def col(tile, r):
        return jnp.broadcast_to(tile[r:r + 1, :], (HG_DK, HG_DK)).T

    def step(i, s):
        rows = pl.ds(pl.multiple_of(i * SUBLANE, SUBLANE), SUBLANE)
        q, k, f, v = q_ref[rows, :], k_ref[rows, :], f_ref[rows, :], v_ref[rows, :]
        out = []
        for r in range(SUBLANE):
            s = s * col(f, r) + col(k, r) * v[r:r + 1, :]
            out.append(jnp.sum(col(q, r) * s, axis=0, keepdims=True))
        o_ref[rows, :] = jnp.concatenate(out, axis=0)
        return s
    return lax.fori_loop(0, n_rows // SUBLANE, step, s0)


def _decode_state_update(p, el_row0, dq_ref, dk_ref, dv_ref, del_ref, ds_ref, do_ref, dso_ref, partial_o):
    nd = ds_ref.shape[0]
    n_tok = dq_ref.shape[0] // nd
    hd, sq = divmod(p, nd)
    hs = slice(hd * HG_DK, (hd + 1) * HG_DK)
    mine = lax.broadcasted_iota(jnp.int32, (dq_ref.shape[0], HG_DK), 0) // n_tok == sq
    s_old = ds_ref[sq, hd]
    o = _dot(dq_ref[:, hs], s_old.astype(BF16))
    partial_o[hd] = jnp.where(mine, o, partial_o[hd]) if hd in partial_o else jnp.where(mine, o, 0.0)
    k_mine = jnp.where(mine, dk_ref[:, hs], 0.0).astype(BF16)
    dec = jnp.broadcast_to(del_ref[pl.ds(el_row0 + sq, 1), :][:, hs], (HG_DK, HG_DK)).T
    dso_ref[sq, hd] = s_old * dec + _dot_tn(k_mine, dv_ref[:, hs])
    if sq == nd - 1:
        do_ref[:, hs] = partial_o.pop(hd)


def _hgrn_prompt_kernel(*refs, tb, gb, layer, final, has_prev):
    n_in = 13 + int(has_prev)
    x_ref, mod_ref, g_ref, win_ref, lbraw_ref, og_ref, wout_ref, fg_ref = refs[:8]
    decode_in = refs[8:13]
    xo_ref, s_ref, do_ref, dso_ref = refs[n_in:n_in + 4]
    q_scr, lf_scr, k_scr, v_scr, z_scr, o_scr, s0_scr = refs[n_in + 4:]
    b = pl.program_id(0)
    t = pl.program_id(1)

    @pl.when(t == 0)
    def _():
        s_ref[...] = jnp.zeros_like(s_ref)

    s0_scr[...] = s_ref[...]

    seqs = range(gb)
    lb = _lower_bound(lbraw_ref[...], layer)
    x = [x_ref[g] for g in seqs]
    mod = [_split_mod(mod_ref[pl.ds(b * gb + g, 1), :]) for g in seqs]
    h = [_mod_norm_seq(x[g], g_ref[...], mod[g][0], mod[g][1]) for g in seqs]
    for g in seqs:
        q_scr[g] = _silu(_dot(h[g], win_ref[:, 0:HG_F]))
    for g in seqs:
        logf, kk = _gate_math(_dot(h[g], win_ref[:, HG_F:2 * HG_F]), lb)
        lf_scr[g] = logf
        k_scr[g] = kk
    for g in seqs:
        v_scr[g] = _dot(h[g], win_ref[:, 2 * HG_F:2 * HG_F + HG_I])
    for g in seqs:
        z_scr[g] = _dot(h[g], win_ref[:, 2 * HG_F + HG_I:])

    causal, first_cols, tri2 = _chunk_masks()
    chains = [(c, hd, g) for c in range(tb // CHUNK) for hd in range(HG_HEADS) for g in seqs]
    ops, scores = {}, {}
    unsafe = jnp.zeros((1, HG_DK), F32)

    def where(n):
        c, hd, g = chains[n]
        return g, hd, slice(c * CHUNK, (c + 1) * CHUNK), slice(hd * HG_DK, (hd + 1) * HG_DK)

    block_seqs = decode_in[4].shape[0]
    n_decode = HG_HEADS * block_seqs
    blocks_per_el = decode_in[3].shape[0] // block_seqs
    el_row0 = ((b * pl.num_programs(1) + t) % blocks_per_el) * block_seqs
    partial_o = {}
    for step in range(max(len(chains) + 2 * HGRN_PIPE_SKEW, n_decode)):
        if step < n_decode:
            _decode_state_update(step, el_row0, *decode_in, do_ref, dso_ref, partial_o)
        n = step
        if n < len(chains):
            g, hd, rows, hs = where(n)
            gl = _chunk_decay(lf_scr[g, rows, hs], tri2)
            km, kr, ptot, bad = _chunk_key_operands(k_scr[g, rows, hs], gl)
            lhs, qe = _chunk_query_operands(q_scr[g, rows, hs], gl)
            ops[n] = (lhs, km, qe, kr, ptot)
            unsafe = jnp.maximum(unsafe, bad)
        n = step - HGRN_PIPE_SKEW
        if 0 <= n < len(chains):
            scores[n] = _dot_nt(ops[n][0], ops[n][1])
        n = step - 2 * HGRN_PIPE_SKEW
        if 0 <= n < len(chains):
            g, hd, rows, hs = where(n)
            _, _, qe, kr, ptot = ops.pop(n)
            vb = v_scr[g, rows, hs].astype(BF16)
            o, s_new = _chunk_finish(scores.pop(n), qe, kr, vb, ptot, s_ref[g, hd], causal, first_cols)
            o_scr[g, rows, hs] = o
            s_ref[g, hd] = s_new

    y = [_head_norm_gate(o_scr[g], z_scr[g], og_ref[...]) for g in seqs]
    for g in seqs:
        xn = x[g] + mod[g][2] * _dot(y[g], wout_ref[...])
        if final:
            xn = _rmsnorm(xn, fg_ref[...])
        xo_ref[g] = xn

    @pl.when(jnp.max(unsafe) > 0.0)
    def _():
        def exact(q_tmp, k_tmp, f_tmp, v_tmp, o_tmp):
            def one_seq(g, carry):
                shift, scale, gate = _split_mod(mod_ref[pl.ds(b * gb + g, 1), :])
                hg = _mod_norm_seq(x_ref[g], g_ref[...], shift, scale)
                qf = _silu(_dot(hg, win_ref[:, 0:HG_F]))
                logf, kk = _gate_math(_dot(hg, win_ref[:, HG_F:2 * HG_F]), lb)
                ff = jnp.exp2(logf)
                vv = _dot(hg, win_ref[:, 2 * HG_F:2 * HG_F + HG_I])
                for hd in range(HG_HEADS):
                    hs = slice(hd * HG_DK, (hd + 1) * HG_DK)
                    q_tmp[hd], k_tmp[hd], f_tmp[hd], v_tmp[hd] = qf[:, hs], kk[:, hs], ff[:, hs], vv[:, hs]

                def one_head(hd, c2):
                    s_ref[g, hd] = _exact_recurrence(q_tmp.at[hd], k_tmp.at[hd], f_tmp.at[hd], v_tmp.at[hd],
                                                     o_tmp.at[hd], s0_scr[g, hd], tb)
                    return c2

                lax.fori_loop(0, HG_HEADS, one_head, 0)
                o = jnp.concatenate([o_tmp[hd] for hd in range(HG_HEADS)], axis=1)
                y = _head_norm_gate(o, z_scr[g], og_ref[...])
                xn = x_ref[g] + gate * _dot(y, wout_ref[...])
                if final:
                    xn = _rmsnorm(xn, fg_ref[...])
                xo_ref[g] = xn
                return carry

            lax.fori_loop(0, gb, one_seq, 0)

        pl.run_scoped(exact, *[pltpu.VMEM((HG_HEADS, tb, HG_DK), F32)] * 5)


def _hgrn_prompt_call(x, mods, norm_g, w_in, lb_raw, og, w_out, fg, decode, layer, mod_block):
    bsz, seq, _ = x.shape
    tb = HGRN_TB
    gb = HGRN_GB
    j = layer // 2
    final = layer == DEPTH - 1
    n_t = seq // tb
    qe, kr, v, el, states, prev_states = decode
    n_blocks = (bsz // gb) * n_t
    n_seq = el.shape[0]
    block_seqs = n_seq // n_blocks
    dec_rows = qe.shape[0] // n_blocks
    el_rows = max(block_seqs, SUBLANE)
    assert n_seq == n_blocks * block_seqs and el_rows % block_seqs == 0 and dec_rows % SUBLANE == 0
    dec_spec = pl.BlockSpec((dec_rows, HG_F), lambda b, t: (b * n_t + t, 0))
    st_spec = pl.BlockSpec((None, block_seqs, HG_HEADS, HG_DK, HG_DV), lambda b, t: (j, b * n_t + t, 0, 0, 0))
    in_specs = [
        pl.BlockSpec((gb, tb, D_MODEL), lambda b, t: (b, t, 0)),
        _resident((None, bsz, 3 * D_MODEL), (layer, mod_block, 0), 2),
        _resident((None, 1, D_MODEL), (layer, 0, 0), 2),
        _resident((None, D_MODEL, 2 * HG_F + 2 * HG_I), (j, 0, 0), 2),
        _resident((DEPTH, HG_F), (0, 0), 2),
        _resident((None, 1, HG_I), (j, 0, 0), 2),
        _resident((None, HG_I, D_MODEL), (j, 0, 0), 2),
        _resident((1, D_MODEL), (0, 0), 2),
        dec_spec, dec_spec, dec_spec,
        pl.BlockSpec((el_rows, HG_F), lambda b, t: ((b * n_t + t) * block_seqs // el_rows, 0)),
        st_spec,
    ]
    args = [x, mods, norm_g, w_in, lb_raw, og, w_out, fg, qe, kr, v, el, states]
    aliases = {}
    if prev_states is not None:
        in_specs.append(pl.BlockSpec(memory_space=pl.ANY))
        args.append(prev_states)
        aliases = {len(args) - 1: 3}
    return pl.pallas_call(
        functools.partial(_hgrn_prompt_kernel, tb=tb, gb=gb, layer=layer, final=final,
                          has_prev=prev_states is not None),
        grid=(bsz // gb, n_t),
        in_specs=in_specs,
        out_specs=[
            pl.BlockSpec((gb, tb, D_MODEL), lambda b, t: (b, t, 0)),
            pl.BlockSpec((gb, HG_HEADS, HG_DK, HG_DV), lambda b, t: (b, 0, 0, 0)),
            dec_spec,
            st_spec,
        ],
        out_shape=[
            jax.ShapeDtypeStruct((bsz, seq, D_MODEL), F32),
            jax.ShapeDtypeStruct((bsz, HG_HEADS, HG_DK, HG_DV), F32),
            jax.ShapeDtypeStruct(qe.shape, F32),
            jax.ShapeDtypeStruct(states.shape, F32),
        ],
        scratch_shapes=[pltpu.VMEM((gb, tb, HG_F), F32)] * 6
        + [pltpu.VMEM((gb, HG_HEADS, HG_DK, HG_DV), F32)],
        input_output_aliases=aliases,
        compiler_params=_params(2),
        name="hgrn_prompt",
    )(*args)


def _token_row_permutation(n_seq, n_tok, seq_major_out):
    n = n_seq * n_tok
    r = lax.broadcasted_iota(jnp.int32, (n, n), 0)
    c = lax.broadcasted_iota(jnp.int32, (n, n), 1)
    sm, tm = (r, c) if seq_major_out else (c, r)
    return jnp.where(tm == (sm % n_tok) * n_seq + sm // n_tok, 1.0, 0.0).astype(BF16)


def _permute_rows_f32(p, a):
    hi = a.astype(BF16)
    rest = a - hi.astype(F32)
    mid = rest.astype(BF16)
    lo = (rest - mid.astype(F32)).astype(BF16)
    return (_dot(p, hi) + _dot(p, mid)) + _dot(p, lo)


def _hgrn_sample_pre_kernel(x_ref, mod_ref, g_ref, win_ref, lbraw_ref,
                            qe_ref, kr_ref, v_ref, el_ref, oi_ref, z_ref, *, n_tok, layer):
    ns = x_ref.shape[0]
    h = _sample_hidden(x_ref, mod_ref, g_ref, n_tok)
    lb = _lower_bound(lbraw_ref[...], layer)
    qf = _silu(_dot(h, win_ref[:, 0:HG_F]))
    lf, kk = _gate_math(_dot(h, win_ref[:, HG_F:2 * HG_F]), lb)
    vv = _dot(h, win_ref[:, 2 * HG_F:2 * HG_F + HG_I])
    z = _dot(h, win_ref[:, 2 * HG_F + HG_I:])

    def tok(a, s):
        return a[s * ns:(s + 1) * ns]

    def decay(s, t):
        if s == t:
            return 1.0
        return jnp.exp2(functools.reduce(lambda a, b: a + b, [tok(lf, u) for u in range(s + 1, t + 1)]))

    last = n_tok - 1
    el_ref[...] = decay(-1, last)
    to_seq_major = _token_row_permutation(ns, n_tok, seq_major_out=True)
    qe = jnp.concatenate([tok(qf, s) * decay(-1, s) for s in range(n_tok)], axis=0)
    kr = jnp.concatenate([tok(kk, s) * decay(s, last) for s in range(n_tok)], axis=0)
    qe_ref[...] = _dot(to_seq_major, qe.astype(BF16)).astype(BF16)
    kr_ref[...] = _dot(to_seq_major, kr.astype(BF16)).astype(BF16)
    v_ref[...] = _dot(to_seq_major, vv.astype(BF16)).astype(BF16)
    for s in range(n_tok):
        z_ref[:, _tok(s, HG_I)] = tok(z, s)

    for t in range(n_tok):
        acc = jnp.zeros((ns, HG_I), F32)
        for s in range(t + 1):
            e = tok(qf, t) * tok(kk, s) * decay(s, t)
            vs = tok(vv, s)
            parts = []
            for hd in range(HG_HEADS):
                hs = slice(hd * HG_DK, (hd + 1) * HG_DK)
                parts.append(jnp.sum(e[:, hs], axis=-1, keepdims=True) * vs[:, hs])
            acc = acc + jnp.concatenate(parts, axis=1)
        oi_ref[:, _tok(t, HG_I)] = acc


def _hgrn_sample_pre_call(x, mods, norm_g, w_in, lb_raw, layer, n_tok):
    n_seq = x.shape[0]
    j = layer // 2
    act = jax.ShapeDtypeStruct((n_seq, n_tok * HG_F), F32)
    tok_rows = jax.ShapeDtypeStruct((n_seq * n_tok, HG_F), BF16)
    tok_spec = pl.BlockSpec((SAMPLE_SEQ_BLOCK * n_tok, HG_F), lambda i: (i, 0))
    return pl.pallas_call(
        functools.partial(_hgrn_sample_pre_kernel, n_tok=n_tok, layer=layer),
        grid=(n_seq // SAMPLE_SEQ_BLOCK,),
        in_specs=[
            _seq_rows(n_tok * D_MODEL),
            _seq_rows(3 * D_MODEL, layer),
            _resident((None, 1, D_MODEL), (layer, 0, 0), 1),
            _resident((None, D_MODEL, 2 * HG_F + 2 * HG_I), (j, 0, 0), 1),
            _resident((DEPTH, HG_F), (0, 0), 1),
        ],
        out_specs=[tok_spec] * 3 + [_seq_rows(HG_F)] + [_seq_rows(n_tok * HG_I)] * 2,
        out_shape=[tok_rows, tok_rows, tok_rows, jax.ShapeDtypeStruct((n_seq, HG_F), F32), act, act],
        compiler_params=_params(1),
        name="hgrn_sample_pre",
    )(x, mods, norm_g, w_in, lb_raw)


def _hgrn_sample_post_kernel(x_ref, mod_ref, oa_ref, ob_ref, z_ref, og_ref, wout_ref, fg_ref, xo_ref,
                             *, n_tok, final):
    ns = x_ref.shape[0]
    o_state = _permute_rows_f32(_token_row_permutation(ns, n_tok, seq_major_out=False), oa_ref[...])
    o = o_state + _stack_tokens(ob_ref, n_tok, HG_I)
    y = _head_norm_gate(o, _stack_tokens(z_ref, n_tok, HG_I), og_ref[...])
    out = _dot(y, wout_ref[...])
    gate = mod_ref[:, 2 * D_MODEL:]
    for s in range(n_tok):
        xn = x_ref[:, _tok(s, D_MODEL)] + gate * out[s * ns:(s + 1) * ns]
        if final:
            xn = _rmsnorm(xn, fg_ref[...])
        _write_tok(xo_ref, s, D_MODEL, xn)


def _hgrn_sample_post_call(x, mods, o_inter, o_intra, z, og, w_out, fg, layer, n_tok):
    n_seq = x.shape[0]
    j = layer // 2
    return pl.pallas_call(
        functools.partial(_hgrn_sample_post_kernel, n_tok=n_tok, final=layer == DEPTH - 1),
        grid=(n_seq // SAMPLE_SEQ_BLOCK,),
        in_specs=[
            _seq_rows(n_tok * D_MODEL),
            _seq_rows(3 * D_MODEL, layer),
            pl.BlockSpec((SAMPLE_SEQ_BLOCK * n_tok, HG_I), lambda i: (i, 0)),
            _seq_rows(n_tok * HG_I), _seq_rows(n_tok * HG_I),
            _resident((None, 1, HG_I), (j, 0, 0), 1),
            _resident((None, HG_I, D_MODEL), (j, 0, 0), 1),
            _resident((1, D_MODEL), (0, 0), 1),
        ],
        out_specs=(pl.BlockSpec((SAMPLE_SEQ_BLOCK, n_tok, D_MODEL), lambda i: (i, 0, 0)) if layer == DEPTH - 1
                   else _seq_rows(n_tok * D_MODEL)),
        out_shape=jax.ShapeDtypeStruct((n_seq, n_tok, D_MODEL) if layer == DEPTH - 1
                                       else (n_seq, n_tok * D_MODEL), F32),
        compiler_params=_params(1),
        name="hgrn_sample_post",
    )(x, mods, o_inter, o_intra, z, og, w_out, fg)


def kernel(x_prompt, x_sample, state_conv, state_hgrn, c_prompt, c_sample, norm_g, w_ada, b_ada, conv_w_in, conv_w, conv_w_out, hgrn_w_in, hgrn_lower_bounds, hgrn_onorm_g, hgrn_w_out, final_norm_g):
    n_p = x_prompt.shape[0]
    n_s, t_s, _ = x_sample.shape
    assert t_s >= CONV_WIDTH - 1 and n_s % SAMPLE_SEQ_BLOCK == 0 and n_s % n_p == 0

    mods = _ada_call(jnp.concatenate([c_sample, c_prompt], axis=0), w_ada, b_ada)
    mod_block_p = n_s // n_p

    n_conv, n_hgrn = conv_w_in.shape[0], hgrn_w_in.shape[0]
    assert n_conv == 2 and DEPTH == 4
    conv_w_in_b = [conv_w_in[:1].astype(BF16)]
    conv_w_out_b = [conv_w_out[:1].astype(BF16)]
    later_weights = [
        (hgrn_w_in.reshape(n_hgrn * D_MODEL, -1), 0, n_hgrn * D_MODEL),
        (hgrn_w_out.reshape(n_hgrn * HG_I, D_MODEL), 0, n_hgrn * HG_I),
        (conv_w_in.reshape(n_conv * D_MODEL, -1), D_MODEL, (n_conv - 1) * D_MODEL),
        (conv_w_out.reshape(n_conv * E_CONV, D_MODEL), E_CONV, (n_conv - 1) * E_CONV),
    ]
    norm_g3 = norm_g.reshape(DEPTH, 1, D_MODEL)
    og3 = hgrn_onorm_g.reshape(DEPTH // 2, 1, HG_I)
    fg = final_norm_g.reshape(1, D_MODEL)

    xp = x_prompt
    xs = x_sample
    conv_p, conv_s, hgrn_p = [], [], []
    hgrn_s = None
    for layer in range(DEPTH):
        if layer % 2 == 0:
            j = layer // 2
            xp, tail, *rounded = _conv_prompt_call(xp, mods, norm_g3, conv_w_in_b[j], conv_w, conv_w_out_b[j],
                                                   layer, mod_block_p, later_weights if layer == 0 else ())
            if layer == 0:
                hgrn_w_in_b = rounded[0].reshape(hgrn_w_in.shape)
                hgrn_w_out_b = rounded[1].reshape(hgrn_w_out.shape)
                conv_w_in_b.append(rounded[2].reshape((1,) + conv_w_in.shape[1:]))
                conv_w_out_b.append(rounded[3].reshape((1,) + conv_w_out.shape[1:]))
            conv_p.append(tail[:, SUBLANE - (CONV_WIDTH - 1):])
            xs, st = _conv_sample_call(xs, mods, norm_g3, conv_w_in_b[j], conv_w, conv_w_out_b[j], state_conv,
                                       layer, t_s)
            conv_s.append(st)
        else:
            qe, kr, v, el, o_intra, z = _hgrn_sample_pre_call(xs, mods, norm_g3, hgrn_w_in_b,
                                                              hgrn_lower_bounds, layer, t_s)
            decode = (qe, kr, v, el, state_hgrn, hgrn_s)
            xp, s_new, o_inter, hgrn_s = _hgrn_prompt_call(xp, mods, norm_g3, hgrn_w_in_b, hgrn_lower_bounds,
                                                           og3, hgrn_w_out_b, fg, decode, layer, mod_block_p)
            hgrn_p.append(s_new)
            xs = _hgrn_sample_post_call(xs, mods, o_inter, o_intra, z, og3, hgrn_w_out_b, fg, layer, t_s)
    return (xp, xs, jnp.stack(conv_p), jnp.stack(hgrn_p), jnp.stack(conv_s), hgrn_s)
```

```python
import functools

import jax
import jax.numpy as jnp
from jax import lax
from jax.experimental import pallas as pl
from jax.experimental.pallas import tpu as pltpu

F32 = jnp.float32
BF16 = jnp.bfloat16

D_MODEL = 1024
DEPTH = 4
CONV_WIDTH = 3
E_CONV = D_MODEL
HG_HEADS = 8
HG_DK = 128
HG_DV = 128
HG_F = HG_HEADS * HG_DK
HG_I = HG_HEADS * HG_DV
EPS = 1e-6

SUBLANE = 8
CHUNK = 128
SUB = CHUNK // 2
CONV_COL_TILE = 256
CONV_TB = 512
CONV_GB = 2
HGRN_TB = 256
HGRN_GB = 2
HGRN_PIPE_SKEW = 3
HGRN_SAFE_LOG2_SPAN = 200.0
SAMPLE_SEQ_BLOCK = 64
VMEM_LIMIT = 56 * 1024 * 1024


def _dot(a, b):
    return jnp.dot(a, b, preferred_element_type=F32)


def _dot_nt(a, b):
    return lax.dot_general(a, b, (((1,), (1,)), ((), ())), preferred_element_type=F32)


def _dot_tn(a, b):
    return lax.dot_general(a, b, (((0,), (0,)), ((), ())), preferred_element_type=F32)


def _silu(x):
    return x * jax.nn.sigmoid(x)


def _rmsnorm(x, g):
    ms = jnp.mean(x * x, axis=-1, keepdims=True)
    return x * lax.rsqrt(ms + EPS) * g


def _split_mod(mod):
    return mod[:, :D_MODEL], mod[:, D_MODEL:2 * D_MODEL], mod[:, 2 * D_MODEL:]


def _mod_norm(x, g, shift, scale):
    return (_rmsnorm(x, g) * (1.0 + scale) + shift).astype(BF16)


def _mod_norm_seq(x, g, shift, scale):
    ms = jnp.mean(x * x, axis=-1, keepdims=True)
    return (x * lax.rsqrt(ms + EPS) * (g * (1.0 + scale)) + shift).astype(BF16)


def _lower_bound(raw, layer):
    rows = [raw[i:i + 1, :] for i in range(DEPTH)]
    m = functools.reduce(jnp.maximum, rows)
    es = [jnp.exp(r - m) for r in rows]
    tot = functools.reduce(lambda a, b: a + b, es)
    acc = es[1]
    for i in range(2, layer + 1):
        acc = acc + es[i]
    return acc / tot


def _gate_math(fpre, lb):
    t = jnp.exp(-jnp.abs(fpre))
    r = 1.0 / (1.0 + t)
    pos = fpre >= 0.0
    sig = jnp.where(pos, 1.0, t) * r
    sig_neg = jnp.where(pos, t, 1.0) * r
    return jnp.log2(lb + (1.0 - lb) * sig), (1.0 - lb) * sig_neg


def _head_norm_gate(o, z, og):
    parts = []
    for hd in range(HG_HEADS):
        hs = slice(hd * HG_DV, (hd + 1) * HG_DV)
        oh = o[:, hs]
        ms = jnp.mean(oh * oh, axis=-1, keepdims=True)
        parts.append(oh * lax.rsqrt(ms + EPS))
    on = jnp.concatenate(parts, axis=1) * og
    return (on * _silu(z)).astype(BF16)


def _params(n_grid):
    return pltpu.CompilerParams(dimension_semantics=("arbitrary",) * n_grid, vmem_limit_bytes=VMEM_LIMIT)


def _resident(shape, index, n_grid):
    if n_grid == 1:
        return pl.BlockSpec(shape, lambda i: index, pipeline_mode=pl.Buffered(1))
    return pl.BlockSpec(shape, lambda b, t: index, pipeline_mode=pl.Buffered(1))


def _ada_kernel(c_ref, w_ref, b_ref, o_ref):
    s = _silu(c_ref[...]).astype(BF16)
    o_ref[...] = _dot(s, w_ref[...].astype(BF16)) + b_ref[...]


def _ada_call(c_all, w_ada, b_ada):
    rows = c_all.shape[0]
    n_col = 1
    col = 3 * D_MODEL // n_col
    return pl.pallas_call(
        _ada_kernel,
        grid=(DEPTH, n_col),
        in_specs=[
            pl.BlockSpec((rows, D_MODEL), lambda l, n: (0, 0)),
            pl.BlockSpec((None, D_MODEL, col), lambda l, n: (l, 0, n)),
            pl.BlockSpec((None, 1, col), lambda l, n: (l, 0, n)),
        ],
        out_specs=pl.BlockSpec((None, rows, col), lambda l, n: (l, 0, n)),
        out_shape=jax.ShapeDtypeStruct((DEPTH, rows, 3 * D_MODEL), F32),
        compiler_params=_params(2),
        name="ada_mod",
    )(c_all, w_ada, b_ada.reshape(DEPTH, 1, 3 * D_MODEL))


def _conv_columns(h, win_ref, cw_ref, y_scr, prev_fn, tail_fn):
    ct = CONV_COL_TILE
    for n in range(E_CONV // ct):
        def col(j):
            return slice(j * E_CONV + n * ct, j * E_CONV + (n + 1) * ct)
        cs = slice(n * ct, (n + 1) * ct)
        b_gate = _dot(h, win_ref[:, col(0)])
        c_gate = _dot(h, win_ref[:, col(1)])
        v = _dot(h, win_ref[:, col(2)])
        z = _dot(h, win_ref[:, col(3)])
        u = c_gate * v
        p1, p2 = prev_fn(cs, u)
        conv = cw_ref[0:1, cs] * p2 + cw_ref[1:2, cs] * p1 + cw_ref[2:3, cs] * u
        y_scr[:, cs] = (b_gate * conv * _silu(z)).astype(BF16)
        tail_fn(cs, u)


def _conv_prompt_kernel(*refs, tb, gb, n_cast):
    x_ref, mod_ref, g_ref, win_ref, cw_ref, wout_ref = refs[:6]
    cast_in = refs[6:6 + n_cast]
    xo_ref, st_ref = refs[6 + n_cast:8 + n_cast]
    cast_out = refs[8 + n_cast:8 + 2 * n_cast]
    y_scr, = refs[8 + 2 * n_cast:]
    b = pl.program_id(0)
    t = pl.program_id(1)
    ct = CONV_COL_TILE
    for src, dst in zip(cast_in, cast_out):
        dst[...] = src[...].astype(BF16)

    @pl.when(t == 0)
    def _():
        st_ref[...] = jnp.zeros_like(st_ref)

    seqs = range(gb)
    x = [x_ref[g] for g in seqs]
    mod = [_split_mod(mod_ref[pl.ds(b * gb + g, 1), :]) for g in seqs]
    h = [_mod_norm_seq(x[g], g_ref[...], mod[g][0], mod[g][1]) for g in seqs]
    row = lax.broadcasted_iota(jnp.int32, (tb, ct), 0)

    for g in seqs:
        def prev_fn(cs, u, g=g):
            c0 = st_ref[g, SUBLANE - 2:SUBLANE - 1, cs]
            c1 = st_ref[g, SUBLANE - 1:SUBLANE, cs]
            p1 = jnp.where(row == 0, c1, pltpu.roll(u, 1, 0))
            p2 = jnp.where(row == 0, c0, jnp.where(row == 1, c1, pltpu.roll(u, 2, 0)))
            return p1, p2

        def tail_fn(cs, u, g=g):
            st_ref[g, :, cs] = u[tb - SUBLANE:tb, :]

        _conv_columns(h[g], win_ref, cw_ref, y_scr.at[g], prev_fn, tail_fn)
    for g in seqs:
        xo_ref[g] = x[g] + mod[g][2] * _dot(y_scr[g], wout_ref[...])


def _conv_prompt_call(x, mods, norm_g, w_in, cw, w_out, layer, mod_block, casts=()):
    bsz, seq, _ = x.shape
    tb = CONV_TB
    gb = CONV_GB
    j = layer // 2
    n_t = seq // tb
    n_steps = (bsz // gb) * n_t
    cast_in_specs, cast_out_specs, cast_shapes = [], [], []
    for mat, first, rows in casts:
        blk = rows // n_steps
        assert rows % n_steps == 0 and blk % (2 * SUBLANE) == 0 and first % blk == 0
        cast_in_specs.append(pl.BlockSpec((blk, mat.shape[1]), lambda b, t, o=first // blk: (o + b * n_t + t, 0)))
        cast_out_specs.append(pl.BlockSpec((blk, mat.shape[1]), lambda b, t: (b * n_t + t, 0)))
        cast_shapes.append(jax.ShapeDtypeStruct((rows, mat.shape[1]), BF16))
    return pl.pallas_call(
        functools.partial(_conv_prompt_kernel, tb=tb, gb=gb, n_cast=len(casts)),
        grid=(bsz // gb, n_t),
        in_specs=[
            pl.BlockSpec((gb, tb, D_MODEL), lambda b, t: (b, t, 0)),
            _resident((None, bsz, 3 * D_MODEL), (layer, mod_block, 0), 2),
            _resident((None, 1, D_MODEL), (layer, 0, 0), 2),
            _resident((None, D_MODEL, 4 * E_CONV), (0, 0, 0), 2),
            _resident((None, CONV_WIDTH, E_CONV), (j, 0, 0), 2),
            _resident((None, E_CONV, D_MODEL), (0, 0, 0), 2),
        ] + cast_in_specs,
        out_specs=[
            pl.BlockSpec((gb, tb, D_MODEL), lambda b, t: (b, t, 0)),
            pl.BlockSpec((gb, SUBLANE, E_CONV), lambda b, t: (b, 0, 0)),
        ] + cast_out_specs,
        out_shape=[
            jax.ShapeDtypeStruct((bsz, seq, D_MODEL), F32),
            jax.ShapeDtypeStruct((bsz, SUBLANE, E_CONV), F32),
        ] + cast_shapes,
        scratch_shapes=[pltpu.VMEM((gb, tb, E_CONV), BF16)],
        compiler_params=_params(2),
        name="conv_prompt",
    )(x, mods, norm_g, w_in, cw, w_out, *[mat for mat, _, _ in casts])


def _tok(s, width):
    return slice(s * width, (s + 1) * width)


def _stack_tokens(ref, n_tok, width, cols=None):
    parts = []
    for s in range(n_tok):
        lo = s * width + (0 if cols is None else cols.start)
        hi = s * width + (width if cols is None else cols.stop)
        parts.append(ref[:, lo:hi])
    return jnp.concatenate(parts, axis=0)


def _read_tok(ref, s, width):
    return ref[:, s, :] if len(ref.shape) == 3 else ref[:, _tok(s, width)]


def _write_tok(ref, s, width, val):
    if len(ref.shape) == 3:
        ref[:, s, :] = val
    else:
        ref[:, _tok(s, width)] = val


def _sample_hidden(x_ref, mod_ref, g_ref, n_tok):
    shift, scale, _ = _split_mod(mod_ref[...])
    g = g_ref[...]
    return jnp.concatenate(
        [_mod_norm(_read_tok(x_ref, s, D_MODEL), g, shift, scale) for s in range(n_tok)], axis=0)


def _conv_sample_kernel(x_ref, mod_ref, g_ref, win_ref, cw_ref, wout_ref, st_ref,
                        xo_ref, so_ref, y_scr, *, n_tok):
    ns = x_ref.shape[0]
    h = _sample_hidden(x_ref, mod_ref, g_ref, n_tok)

    def prev_fn(cs, u):
        st0 = st_ref[:, 0, cs]
        st1 = st_ref[:, 1, cs]
        p1 = jnp.concatenate([st1, u[:(n_tok - 1) * ns]], axis=0)
        p2 = jnp.concatenate([st0, st1, u[:(n_tok - 2) * ns]], axis=0)
        return p1, p2

    def tail_fn(cs, u):
        so_ref[:, 0, cs] = u[(n_tok - 2) * ns:(n_tok - 1) * ns]
        so_ref[:, 1, cs] = u[(n_tok - 1) * ns:]

    _conv_columns(h, win_ref, cw_ref, y_scr, prev_fn, tail_fn)
    out = _dot(y_scr[...], wout_ref[...])
    gate = mod_ref[:, 2 * D_MODEL:]
    for s in range(n_tok):
        xo_ref[:, _tok(s, D_MODEL)] = _read_tok(x_ref, s, D_MODEL) + gate * out[s * ns:(s + 1) * ns]


def _seq_rows(width, layer=None):
    if layer is None:
        return pl.BlockSpec((SAMPLE_SEQ_BLOCK, width), lambda i: (i, 0))
    return pl.BlockSpec((None, SAMPLE_SEQ_BLOCK, width), lambda i: (layer, i, 0))


def _conv_sample_call(x, mods, norm_g, w_in, cw, w_out, state, layer, n_tok):
    n_seq = x.shape[0]
    j = layer // 2
    return pl.pallas_call(
        functools.partial(_conv_sample_kernel, n_tok=n_tok),
        grid=(n_seq // SAMPLE_SEQ_BLOCK,),
        in_specs=[
            _seq_rows(n_tok * D_MODEL) if x.ndim == 2 else
            pl.BlockSpec((SAMPLE_SEQ_BLOCK, n_tok, D_MODEL), lambda i: (i, 0, 0)),
            _seq_rows(3 * D_MODEL, layer),
            _resident((None, 1, D_MODEL), (layer, 0, 0), 1),
            _resident((None, D_MODEL, 4 * E_CONV), (0, 0, 0), 1),
            _resident((None, CONV_WIDTH, E_CONV), (j, 0, 0), 1),
            _resident((None, E_CONV, D_MODEL), (0, 0, 0), 1),
            pl.BlockSpec((None, SAMPLE_SEQ_BLOCK, CONV_WIDTH - 1, E_CONV), lambda i: (j, i, 0, 0)),
        ],
        out_specs=[_seq_rows(n_tok * D_MODEL),
                   pl.BlockSpec((SAMPLE_SEQ_BLOCK, CONV_WIDTH - 1, E_CONV), lambda i: (i, 0, 0))],
        out_shape=[
            jax.ShapeDtypeStruct((n_seq, n_tok * D_MODEL), F32),
            jax.ShapeDtypeStruct((n_seq, CONV_WIDTH - 1, E_CONV), F32),
        ],
        scratch_shapes=[pltpu.VMEM((n_tok * SAMPLE_SEQ_BLOCK, E_CONV), BF16)],
        compiler_params=_params(1),
        name="conv_sample",
    )(x, mods, norm_g, w_in, cw, w_out, state)


def _rows_from_blocks(vecs):
    return jnp.concatenate([jnp.broadcast_to(v, (SUB, v.shape[1])) for v in vecs], axis=0)


def _chunk_masks():
    r = lax.broadcasted_iota(jnp.int32, (CHUNK, CHUNK), 0)
    c = lax.broadcasted_iota(jnp.int32, (CHUNK, CHUNK), 1)
    causal = (r // SUB == c // SUB) & (c <= r)
    tri = jnp.where(causal, 1.0, 0.0).astype(BF16)
    first_cols = lax.broadcasted_iota(jnp.int32, (SUB, CHUNK), 1) < SUB
    return causal, first_cols, jnp.concatenate([tri, tri], axis=1)


def _chunk_decay(lf, tri2):
    hi = lf.astype(BF16)
    lo = (lf - hi.astype(F32)).astype(BF16)
    return _dot(tri2, jnp.concatenate([hi, lo], axis=0))


def _chunk_key_operands(kk, gl):
    t0 = gl[SUB - 1:SUB, :]
    t1 = gl[2 * SUB - 1:2 * SUB, :]
    k_mid = kk * jnp.exp2(_rows_from_blocks([0.5 * t0, 0.5 * t1]) - gl)
    k_state = k_mid * _rows_from_blocks([jnp.exp2(0.5 * t0 + t1), jnp.exp2(0.5 * t1)])
    in_span = (t0 >= -HGRN_SAFE_LOG2_SPAN) & (t1 >= -HGRN_SAFE_LOG2_SPAN)
    return k_mid.astype(BF16), k_state.astype(BF16), jnp.exp2(t0 + t1), jnp.where(in_span, 0.0, 1.0)


def _chunk_query_operands(qf, gl):
    t0 = gl[SUB - 1:SUB, :]
    t1 = gl[2 * SUB - 1:2 * SUB, :]
    q_mid = qf * jnp.exp2(gl - _rows_from_blocks([0.5 * t0, 0.5 * t1]))
    q_next = q_mid[SUB:] * jnp.exp2(0.5 * (t0 + t1))
    q_state = q_mid * _rows_from_blocks([jnp.exp2(0.5 * t0), jnp.exp2(0.5 * t1 + t0)])
    return jnp.concatenate([q_mid, q_next], axis=0).astype(BF16), q_state.astype(BF16)


def _chunk_finish(ast, qe, kr, vb, ptot, s_old, causal, first_cols):
    same = jnp.where(causal, ast[0:CHUNK], 0.0)
    a = jnp.concatenate([same[:SUB], jnp.where(first_cols, ast[CHUNK:], same[SUB:])], axis=0)
    o = _dot(jnp.concatenate([qe, a.astype(BF16)], axis=1),
             jnp.concatenate([s_old.astype(BF16), vb], axis=0))
    dec = jnp.broadcast_to(ptot, (HG_DK, HG_DK)).T
    return o, s_old * dec + _dot_tn(kr, vb)


def _exact_recurrence(q_ref, k_ref, f_ref, v_ref, o_ref, s0, n_rows):
    def col(tile, r):
        return jnp.broadcast_to(tile[r:r + 1, :], (HG_DK, HG_DK)).T

    def step(i, s):
        rows = pl.ds(pl.multiple_of(i * SUBLANE, SUBLANE), SUBLANE)
        q, k, f, v = q_ref[rows, :], k_ref[rows, :], f_ref[rows, :], v_ref[rows, :]
        out = []
        for r in range(SUBLANE):
            s = s * col(f, r) + col(k, r) * v[r:r + 1, :]
            out.append(jnp.sum(col(q, r) * s, axis=0, keepdims=True))
        o_ref[rows, :] = jnp.concatenate(out, axis=0)
        return s
    return lax.fori_loop(0, n_rows // SUBLANE, step, s0)


def _decode_state_update(p, el_row0, dq_ref, dk_ref, dv_ref, del_ref, ds_ref, do_ref, dso_ref, partial_o):
    nd = ds_ref.shape[0]
    n_tok = dq_ref.shape[0] // nd
    hd, sq = divmod(p, nd)
    hs = slice(hd * HG_DK, (hd + 1) * HG_DK)
    mine = lax.broadcasted_iota(jnp.int32, (dq_ref.shape[0], HG_DK), 0) // n_tok == sq
    s_old = ds_ref[sq, hd]
    o = _dot(dq_ref[:, hs], s_old.astype(BF16))
    partial_o[hd] = jnp.where(mine, o, partial_o[hd]) if hd in partial_o else jnp.where(mine, o, 0.0)
    k_mine = jnp.where(mine, dk_ref[:, hs], 0.0).astype(BF16)
    dec = jnp.broadcast_to(del_ref[pl.ds(el_row0 + sq, 1), :][:, hs], (HG_DK, HG_DK)).T
    dso_ref[sq, hd] = s_old * dec + _dot_tn(k_mine, dv_ref[:, hs])
    if sq == nd - 1:
        do_ref[:, hs] = partial_o.pop(hd)


def _hgrn_prompt_kernel(*refs, tb, gb, layer, final, has_prev):
    n_in = 13 + int(has_prev)
    x_ref, mod_ref, g_ref, win_ref, lbraw_ref, og_ref, wout_ref, fg_ref = refs[:8]
    decode_in = refs[8:13]
    xo_ref, s_ref, do_ref, dso_ref = refs[n_in:n_in + 4]
    q_scr, lf_scr, k_scr, v_scr, z_scr, o_scr, s0_scr = refs[n_in + 4:]
    b = pl.program_id(0)
    t = pl.program_id(1)

    @pl.when(t == 0)
    def _():
        s_ref[...] = jnp.zeros_like(s_ref)

    s0_scr[...] = s_ref[...]

    seqs = range(gb)
    lb = _lower_bound(lbraw_ref[...], layer)
    x = [x_ref[g] for g in seqs]
    mod = [_split_mod(mod_ref[pl.ds(b * gb + g, 1), :]) for g in seqs]
    h = [_mod_norm_seq(x[g], g_ref[...], mod[g][0], mod[g][1]) for g in seqs]
    for g in seqs:
        q_scr[g] = _silu(_dot(h[g], win_ref[:, 0:HG_F]))
    for g in seqs:
        logf, kk = _gate_math(_dot(h[g], win_ref[:, HG_F:2 * HG_F]), lb)
        lf_scr[g] = logf
        k_scr[g] = kk
    for g in seqs:
        v_scr[g] = _dot(h[g], win_ref[:, 2 * HG_F:2 * HG_F + HG_I])
    for g in seqs:
        z_scr[g] = _dot(h[g], win_ref[:, 2 * HG_F + HG_I:])

    causal, first_cols, tri2 = _chunk_masks()
    chains = [(c, hd, g) for c in range(tb // CHUNK) for hd in range(HG_HEADS) for g in seqs]
    ops, scores = {}, {}
    unsafe = jnp.zeros((1, HG_DK), F32)

    def where(n):
        c, hd, g = chains[n]
        return g, hd, slice(c * CHUNK, (c + 1) * CHUNK), slice(hd * HG_DK, (hd + 1) * HG_DK)

    block_seqs = decode_in[4].shape[0]
    n_decode = HG_HEADS * block_seqs
    blocks_per_el = decode_in[3].shape[0] // block_seqs
    el_row0 = ((b * pl.num_programs(1) + t) % blocks_per_el) * block_seqs
    partial_o = {}
    for step in range(max(len(chains) + 2 * HGRN_PIPE_SKEW, n_decode)):
        if step < n_decode:
            _decode_state_update(step, el_row0, *decode_in, do_ref, dso_ref, partial_o)
        n = step
        if n < len(chains):
            g, hd, rows, hs = where(n)
            gl = _chunk_decay(lf_scr[g, rows, hs], tri2)
            km, kr, ptot, bad = _chunk_key_operands(k_scr[g, rows, hs], gl)
            lhs, qe = _chunk_query_operands(q_scr[g, rows, hs], gl)
            ops[n] = (lhs, km, qe, kr, ptot)
            unsafe = jnp.maximum(unsafe, bad)
        n = step - HGRN_PIPE_SKEW
        if 0 <= n < len(chains):
            scores[n] = _dot_nt(ops[n][0], ops[n][1])
        n = step - 2 * HGRN_PIPE_SKEW
        if 0 <= n < len(chains):
            g, hd, rows, hs = where(n)
            _, _, qe, kr, ptot = ops.pop(n)
            vb = v_scr[g, rows, hs].astype(BF16)
            o, s_new = _chunk_finish(scores.pop(n), qe, kr, vb, ptot, s_ref[g, hd], causal, first_cols)
            o_scr[g, rows, hs] = o
            s_ref[g, hd] = s_new

    y = [_head_norm_gate(o_scr[g], z_scr[g], og_ref[...]) for g in seqs]
    for g in seqs:
        xn = x[g] + mod[g][2] * _dot(y[g], wout_ref[...])
        if final:
            xn = _rmsnorm(xn, fg_ref[...])
        xo_ref[g] = xn

    @pl.when(jnp.max(unsafe) > 0.0)
    def _():
        def exact(q_tmp, k_tmp, f_tmp, v_tmp, o_tmp):
            def one_seq(g, carry):
                shift, scale, gate = _split_mod(mod_ref[pl.ds(b * gb + g, 1), :])
                hg = _mod_norm_seq(x_ref[g], g_ref[...], shift, scale)
                qf = _silu(_dot(hg, win_ref[:, 0:HG_F]))
                logf, kk = _gate_math(_dot(hg, win_ref[:, HG_F:2 * HG_F]), lb)
                ff = jnp.exp2(logf)
                vv = _dot(hg, win_ref[:, 2 * HG_F:2 * HG_F + HG_I])
                for hd in range(HG_HEADS):
                    hs = slice(hd * HG_DK, (hd + 1) * HG_DK)
                    q_tmp[hd], k_tmp[hd], f_tmp[hd], v_tmp[hd] = qf[:, hs], kk[:, hs], ff[:, hs], vv[:, hs]

                def one_head(hd, c2):
                    s_ref[g, hd] = _exact_recurrence(q_tmp.at[hd], k_tmp.at[hd], f_tmp.at[hd], v_tmp.at[hd],
                                                     o_tmp.at[hd], s0_scr[g, hd], tb)
                    return c2

                lax.fori_loop(0, HG_HEADS, one_head, 0)
                o = jnp.concatenate([o_tmp[hd] for hd in range(HG_HEADS)], axis=1)
                y = _head_norm_gate(o, z_scr[g], og_ref[...])
                xn = x_ref[g] + gate * _dot(y, wout_ref[...])
                if final:
                    xn = _rmsnorm(xn, fg_ref[...])
                xo_ref[g] = xn
                return carry

            lax.fori_loop(0, gb, one_seq, 0)

        pl.run_scoped(exact, *[pltpu.VMEM((HG_HEADS, tb, HG_DK), F32)] * 5)


def _hgrn_prompt_call(x, mods, norm_g, w_in, lb_raw, og, w_out, fg, decode, layer, mod_block):
    bsz, seq, _ = x.shape
    tb = HGRN_TB
    gb = HGRN_GB
    j = layer // 2
    final = layer == DEPTH - 1
    n_t = seq // tb
    qe, kr, v, el, states, prev_states = decode
    n_blocks = (bsz // gb) * n_t
    n_seq = el.shape[0]
    block_seqs = n_seq // n_blocks
    dec_rows = qe.shape[0] // n_blocks
    el_rows = max(block_seqs, SUBLANE)
    assert n_seq == n_blocks * block_seqs and el_rows % block_seqs == 0 and dec_rows % SUBLANE == 0
    dec_spec = pl.BlockSpec((dec_rows, HG_F), lambda b, t: (b * n_t + t, 0))
    st_spec = pl.BlockSpec((None, block_seqs, HG_HEADS, HG_DK, HG_DV), lambda b, t: (j, b * n_t + t, 0, 0, 0))
    in_specs = [
        pl.BlockSpec((gb, tb, D_MODEL), lambda b, t: (b, t, 0)),
        _resident((None, bsz, 3 * D_MODEL), (layer, mod_block, 0), 2),
        _resident((None, 1, D_MODEL), (layer, 0, 0), 2),
        _resident((None, D_MODEL, 2 * HG_F + 2 * HG_I), (j, 0, 0), 2),
        _resident((DEPTH, HG_F), (0, 0), 2),
        _resident((None, 1, HG_I), (j, 0, 0), 2),
        _resident((None, HG_I, D_MODEL), (j, 0, 0), 2),
        _resident((1, D_MODEL), (0, 0), 2),
        dec_spec, dec_spec, dec_spec,
        pl.BlockSpec((el_rows, HG_F), lambda b, t: ((b * n_t + t) * block_seqs // el_rows, 0)),
        st_spec,
    ]
    args = [x, mods, norm_g, w_in, lb_raw, og, w_out, fg, qe, kr, v, el, states]
    aliases = {}
    if prev_states is not None:
        in_specs.append(pl.BlockSpec(memory_space=pl.ANY))
        args.append(prev_states)
        aliases = {len(args) - 1: 3}
    return pl.pallas_call(
        functools.partial(_hgrn_prompt_kernel, tb=tb, gb=gb, layer=layer, final=final,
                          has_prev=prev_states is not None),
        grid=(bsz // gb, n_t),
        in_specs=in_specs,
        out_specs=[
            pl.BlockSpec((gb, tb, D_MODEL), lambda b, t: (b, t, 0)),
            pl.BlockSpec((gb, HG_HEADS, HG_DK, HG_DV), lambda b, t: (b, 0, 0, 0)),
            dec_spec,
            st_spec,
        ],
        out_shape=[
            jax.ShapeDtypeStruct((bsz, seq, D_MODEL), F32),
            jax.ShapeDtypeStruct((bsz, HG_HEADS, HG_DK, HG_DV), F32),
            jax.ShapeDtypeStruct(qe.shape, F32),
            jax.ShapeDtypeStruct(states.shape, F32),
        ],
        scratch_shapes=[pltpu.VMEM((gb, tb, HG_F), F32)] * 6
        + [pltpu.VMEM((gb, HG_HEADS, HG_DK, HG_DV), F32)],
        input_output_aliases=aliases,
        compiler_params=_params(2),
        name="hgrn_prompt",
    )(*args)


def _token_row_permutation(n_seq, n_tok, seq_major_out):
    n = n_seq * n_tok
    r = lax.broadcasted_iota(jnp.int32, (n, n), 0)
    c = lax.broadcasted_iota(jnp.int32, (n, n), 1)
    sm, tm = (r, c) if seq_major_out else (c, r)
    return jnp.where(tm == (sm % n_tok) * n_seq + sm // n_tok, 1.0, 0.0).astype(BF16)


def _permute_rows_f32(p, a):
    hi = a.astype(BF16)
    rest = a - hi.astype(F32)
    mid = rest.astype(BF16)
    lo = (rest - mid.astype(F32)).astype(BF16)
    return (_dot(p, hi) + _dot(p, mid)) + _dot(p, lo)


def _hgrn_sample_pre_kernel(x_ref, mod_ref, g_ref, win_ref, lbraw_ref,
                            qe_ref, kr_ref, v_ref, el_ref, oi_ref, z_ref, *, n_tok, layer):
    ns = x_ref.shape[0]
    h = _sample_hidden(x_ref, mod_ref, g_ref, n_tok)
    lb = _lower_bound(lbraw_ref[...], layer)
    qf = _silu(_dot(h, win_ref[:, 0:HG_F]))
    lf, kk = _gate_math(_dot(h, win_ref[:, HG_F:2 * HG_F]), lb)
    vv = _dot(h, win_ref[:, 2 * HG_F:2 * HG_F + HG_I])
    z = _dot(h, win_ref[:, 2 * HG_F + HG_I:])

    def tok(a, s):
        return a[s * ns:(s + 1) * ns]

    def decay(s, t):
        if s == t:
            return 1.0
        return jnp.exp2(functools.reduce(lambda a, b: a + b, [tok(lf, u) for u in range(s + 1, t + 1)]))

    last = n_tok - 1
    el_ref[...] = decay(-1, last)
    to_seq_major = _token_row_permutation(ns, n_tok, seq_major_out=True)
    qe = jnp.concatenate([tok(qf, s) * decay(-1, s) for s in range(n_tok)], axis=0)
    kr = jnp.concatenate([tok(kk, s) * decay(s, last) for s in range(n_tok)], axis=0)
    qe_ref[...] = _dot(to_seq_major, qe.astype(BF16)).astype(BF16)
    kr_ref[...] = _dot(to_seq_major, kr.astype(BF16)).astype(BF16)
    v_ref[...] = _dot(to_seq_major, vv.astype(BF16)).astype(BF16)
    for s in range(n_tok):
        z_ref[:, _tok(s, HG_I)] = tok(z, s)

    for t in range(n_tok):
        acc = jnp.zeros((ns, HG_I), F32)
        for s in range(t + 1):
            e = tok(qf, t) * tok(kk, s) * decay(s, t)
            vs = tok(vv, s)
            parts = []
            for hd in range(HG_HEADS):
                hs = slice(hd * HG_DK, (hd + 1) * HG_DK)
                parts.append(jnp.sum(e[:, hs], axis=-1, keepdims=True) * vs[:, hs])
            acc = acc + jnp.concatenate(parts, axis=1)
        oi_ref[:, _tok(t, HG_I)] = acc


def _hgrn_sample_pre_call(x, mods, norm_g, w_in, lb_raw, layer, n_tok):
    n_seq = x.shape[0]
    j = layer // 2
    act = jax.ShapeDtypeStruct((n_seq, n_tok * HG_F), F32)
    tok_rows = jax.ShapeDtypeStruct((n_seq * n_tok, HG_F), BF16)
    tok_spec = pl.BlockSpec((SAMPLE_SEQ_BLOCK * n_tok, HG_F), lambda i: (i, 0))
    return pl.pallas_call(
        functools.partial(_hgrn_sample_pre_kernel, n_tok=n_tok, layer=layer),
        grid=(n_seq // SAMPLE_SEQ_BLOCK,),
        in_specs=[
            _seq_rows(n_tok * D_MODEL),
            _seq_rows(3 * D_MODEL, layer),
            _resident((None, 1, D_MODEL), (layer, 0, 0), 1),
            _resident((None, D_MODEL, 2 * HG_F + 2 * HG_I), (j, 0, 0), 1),
            _resident((DEPTH, HG_F), (0, 0), 1),
        ],
        out_specs=[tok_spec] * 3 + [_seq_rows(HG_F)] + [_seq_rows(n_tok * HG_I)] * 2,
        out_shape=[tok_rows, tok_rows, tok_rows, jax.ShapeDtypeStruct((n_seq, HG_F), F32), act, act],
        compiler_params=_params(1),
        name="hgrn_sample_pre",
    )(x, mods, norm_g, w_in, lb_raw)


def _hgrn_sample_post_kernel(x_ref, mod_ref, oa_ref, ob_ref, z_ref, og_ref, wout_ref, fg_ref, xo_ref,
                             *, n_tok, final):
    ns = x_ref.shape[0]
    o_state = _permute_rows_f32(_token_row_permutation(ns, n_tok, seq_major_out=False), oa_ref[...])
    o = o_state + _stack_tokens(ob_ref, n_tok, HG_I)
    y = _head_norm_gate(o, _stack_tokens(z_ref, n_tok, HG_I), og_ref[...])
    out = _dot(y, wout_ref[...])
    gate = mod_ref[:, 2 * D_MODEL:]
    for s in range(n_tok):
        xn = x_ref[:, _tok(s, D_MODEL)] + gate * out[s * ns:(s + 1) * ns]
        if final:
            xn = _rmsnorm(xn, fg_ref[...])
        _write_tok(xo_ref, s, D_MODEL, xn)


def _hgrn_sample_post_call(x, mods, o_inter, o_intra, z, og, w_out, fg, layer, n_tok):
    n_seq = x.shape[0]
    j = layer // 2
    return pl.pallas_call(
        functools.partial(_hgrn_sample_post_kernel, n_tok=n_tok, final=layer == DEPTH - 1),
        grid=(n_seq // SAMPLE_SEQ_BLOCK,),
        in_specs=[
            _seq_rows(n_tok * D_MODEL),
            _seq_rows(3 * D_MODEL, layer),
            pl.BlockSpec((SAMPLE_SEQ_BLOCK * n_tok, HG_I), lambda i: (i, 0)),
            _seq_rows(n_tok * HG_I), _seq_rows(n_tok * HG_I),
            _resident((None, 1, HG_I), (j, 0, 0), 1),
            _resident((None, HG_I, D_MODEL), (j, 0, 0), 1),
            _resident((1, D_MODEL), (0, 0), 1),
        ],
        out_specs=(pl.BlockSpec((SAMPLE_SEQ_BLOCK, n_tok, D_MODEL), lambda i: (i, 0, 0)) if layer == DEPTH - 1
                   else _seq_rows(n_tok * D_MODEL)),
        out_shape=jax.ShapeDtypeStruct((n_seq, n_tok, D_MODEL) if layer == DEPTH - 1
                                       else (n_seq, n_tok * D_MODEL), F32),
        compiler_params=_params(1),
        name="hgrn_sample_post",
    )(x, mods, o_inter, o_intra, z, og, w_out, fg)


def kernel(x_prompt, x_sample, state_conv, state_hgrn, c_prompt, c_sample, norm_g, w_ada, b_ada, conv_w_in, conv_w, conv_w_out, hgrn_w_in, hgrn_lower_bounds, hgrn_onorm_g, hgrn_w_out, final_norm_g):
    n_p = x_prompt.shape[0]
    n_s, t_s, _ = x_sample.shape
    assert t_s >= CONV_WIDTH - 1 and n_s % SAMPLE_SEQ_BLOCK == 0 and n_s % n_p == 0

    mods = _ada_call(jnp.concatenate([c_sample, c_prompt], axis=0), w_ada, b_ada)
    mod_block_p = n_s // n_p

    n_conv, n_hgrn = conv_w_in.shape[0], hgrn_w_in.shape[0]
    assert n_conv == 2 and DEPTH == 4
    conv_w_in_b = [conv_w_in[:1].astype(BF16)]
    conv_w_out_b = [conv_w_out[:1].astype(BF16)]
    later_weights = [
        (hgrn_w_in.reshape(n_hgrn * D_MODEL, -1), 0, n_hgrn * D_MODEL),
        (hgrn_w_out.reshape(n_hgrn * HG_I, D_MODEL), 0, n_hgrn * HG_I),
        (conv_w_in.reshape(n_conv * D_MODEL, -1), D_MODEL, (n_conv - 1) * D_MODEL),
        (conv_w_out.reshape(n_conv * E_CONV, D_MODEL), E_CONV, (n_conv - 1) * E_CONV),
    ]
    norm_g3 = norm_g.reshape(DEPTH, 1, D_MODEL)
    og3 = hgrn_onorm_g.reshape(DEPTH // 2, 1, HG_I)
    fg = final_norm_g.reshape(1, D_MODEL)

    xp = x_prompt
    xs = x_sample
    conv_p, conv_s, hgrn_p = [], [], []
    hgrn_s = None
    for layer in range(DEPTH):
        if layer % 2 == 0:
            j = layer // 2
            xp, tail, *rounded = _conv_prompt_call(xp, mods, norm_g3, conv_w_in_b[j], conv_w, conv_w_out_b[j],
                                                   layer, mod_block_p, later_weights if layer == 0 else ())
            if layer == 0:
                hgrn_w_in_b = rounded[0].reshape(hgrn_w_in.shape)
                hgrn_w_out_b = rounded[1].reshape(hgrn_w_out.shape)
                conv_w_in_b.append(rounded[2].reshape((1,) + conv_w_in.shape[1:]))
                conv_w_out_b.append(rounded[3].reshape((1,) + conv_w_out.shape[1:]))
            conv_p.append(tail[:, SUBLANE - (CONV_WIDTH - 1):])
            xs, st = _conv_sample_call(xs, mods, norm_g3, conv_w_in_b[j], conv_w, conv_w_out_b[j], state_conv,
                                       layer, t_s)
            conv_s.append(st)
        else:
            qe, kr, v, el, o_intra, z = _hgrn_sample_pre_call(xs, mods, norm_g3, hgrn_w_in_b,
                                                              hgrn_lower_bounds, layer, t_s)
            decode = (qe, kr, v, el, state_hgrn, hgrn_s)
            xp, s_new, o_inter, hgrn_s = _hgrn_prompt_call(xp, mods, norm_g3, hgrn_w_in_b, hgrn_lower_bounds,
                                                           og3, hgrn_w_out_b, fg, decode, layer, mod_block_p)
            hgrn_p.append(s_new)
            xs = _hgrn_sample_post_call(xs, mods, o_inter, o_intra, z, og3, hgrn_w_out_b, fg, layer, t_s)
    return (xp, xs, jnp.stack(conv_p), jnp.stack(hgrn_p), jnp.stack(conv_s), hgrn_s)
```

```python
import functools

import jax
import jax.numpy as jnp
from jax import lax
from jax.experimental import pallas as pl
from jax.experimental.pallas import tpu as pltpu

F32 = jnp.float32
BF16 = jnp.bfloat16

D_MODEL = 1024
DEPTH = 4
CONV_WIDTH = 3
E_CONV = D_MODEL
HG_HEADS = 8
HG_DK = 128
HG_DV = 128
HG_F = HG_HEADS * HG_DK
HG_I = HG_HEADS * HG_DV
EPS = 1e-6

SUBLANE = 8
CHUNK = 128
SUB = CHUNK // 2
CONV_COL_TILE = 256
CONV_TB = 512
CONV_GB = 2
HGRN_TB = 256
HGRN_GB = 2
HGRN_PIPE_SKEW = 3
HGRN_SAFE_LOG2_SPAN = 200.0
SAMPLE_SEQ_BLOCK = 64
V7X_VMEM_BYTES = 64 * 1024 * 1024
VMEM_LIMIT = V7X_VMEM_BYTES * 7 // 8


def _dot(a, b):
    return jnp.dot(a, b, preferred_element_type=F32)


def _dot_nt(a, b):
    return lax.dot_general(a, b, (((1,), (1,)), ((), ())), preferred_element_type=F32)


def _dot_tn(a, b):
    return lax.dot_general(a, b, (((0,), (0,)), ((), ())), preferred_element_type=F32)


def _silu(x):
    return x * jax.nn.sigmoid(x)


def _rmsnorm(x, g):
    ms = jnp.mean(x * x, axis=-1, keepdims=True)
    return x * lax.rsqrt(ms + EPS) * g


def _split_mod(mod):
    return mod[:, :D_MODEL], mod[:, D_MODEL:2 * D_MODEL], mod[:, 2 * D_MODEL:]


def _mod_norm(x, g, shift, scale):
    return (_rmsnorm(x, g) * (1.0 + scale) + shift).astype(BF16)


def _mod_norm_seq(x, g, shift, scale):
    ms = jnp.mean(x * x, axis=-1, keepdims=True)
    return (x * lax.rsqrt(ms + EPS) * (g * (1.0 + scale)) + shift).astype(BF16)


def _lower_bound(raw, layer):
    rows = [raw[i:i + 1, :] for i in range(DEPTH)]
    m = functools.reduce(jnp.maximum, rows)
    es = [jnp.exp(r - m) for r in rows]
    tot = functools.reduce(lambda a, b: a + b, es)
    acc = es[1]
    for i in range(2, layer + 1):
        acc = acc + es[i]
    return acc / tot


def _gate_math(fpre, lb):
    t = jnp.exp(-jnp.abs(fpre))
    r = 1.0 / (1.0 + t)
    pos = fpre >= 0.0
    sig = jnp.where(pos, 1.0, t) * r
    sig_neg = jnp.where(pos, t, 1.0) * r
    return jnp.log2(lb + (1.0 - lb) * sig), (1.0 - lb) * sig_neg


def _head_norm_gate(o, z, og):
    parts = []
    for hd in range(HG_HEADS):
        hs = slice(hd * HG_DV, (hd + 1) * HG_DV)
        oh = o[:, hs]
        ms = jnp.mean(oh * oh, axis=-1, keepdims=True)
        parts.append(oh * lax.rsqrt(ms + EPS))
    on = jnp.concatenate(parts, axis=1) * og
    return (on * _silu(z)).astype(BF16)


def _params(n_grid):
    return pltpu.CompilerParams(dimension_semantics=("arbitrary",) * n_grid, vmem_limit_bytes=VMEM_LIMIT)


def _resident(shape, index, n_grid):
    if n_grid == 1:
        return pl.BlockSpec(shape, lambda i: index, pipeline_mode=pl.Buffered(1))
    return pl.BlockSpec(shape, lambda b, t: index, pipeline_mode=pl.Buffered(1))


def _ada_kernel(c_ref, w_ref, b_ref, o_ref):
    s = _silu(c_ref[...]).astype(BF16)
    o_ref[...] = _dot(s, w_ref[...].astype(BF16)) + b_ref[...]


def _ada_call(c_all, w_ada, b_ada):
    rows = c_all.shape[0]
    n_col = 1
    col = 3 * D_MODEL // n_col
    return pl.pallas_call(
        _ada_kernel,
        grid=(DEPTH, n_col),
        in_specs=[
            pl.BlockSpec((rows, D_MODEL), lambda l, n: (0, 0)),
            pl.BlockSpec((None, D_MODEL, col), lambda l, n: (l, 0, n)),
            pl.BlockSpec((None, 1, col), lambda l, n: (l, 0, n)),
        ],
        out_specs=pl.BlockSpec((None, rows, col), lambda l, n: (l, 0, n)),
        out_shape=jax.ShapeDtypeStruct((DEPTH, rows, 3 * D_MODEL), F32),
        compiler_params=_params(2),
        name="ada_mod",
    )(c_all, w_ada, b_ada.reshape(DEPTH, 1, 3 * D_MODEL))


def _conv_columns(h, win_ref, cw_ref, y_scr, prev_fn, tail_fn):
    ct = CONV_COL_TILE
    for n in range(E_CONV // ct):
        def col(j):
            return slice(j * E_CONV + n * ct, j * E_CONV + (n + 1) * ct)
        cs = slice(n * ct, (n + 1) * ct)
        b_gate = _dot(h, win_ref[:, col(0)])
        c_gate = _dot(h, win_ref[:, col(1)])
        v = _dot(h, win_ref[:, col(2)])
        z = _dot(h, win_ref[:, col(3)])
        u = c_gate * v
        p1, p2 = prev_fn(cs, u)
        conv = cw_ref[0:1, cs] * p2 + cw_ref[1:2, cs] * p1 + cw_ref[2:3, cs] * u
        y_scr[:, cs] = (b_gate * conv * _silu(z)).astype(BF16)
        tail_fn(cs, u)


def _conv_prompt_kernel(*refs, tb, gb, n_cast):
    x_ref, mod_ref, g_ref, win_ref, cw_ref, wout_ref = refs[:6]
    cast_in = refs[6:6 + n_cast]
    xo_ref, st_ref = refs[6 + n_cast:8 + n_cast]
    cast_out = refs[8 + n_cast:8 + 2 * n_cast]
    y_scr, = refs[8 + 2 * n_cast:]
    b = pl.program_id(0)
    t = pl.program_id(1)
    ct = CONV_COL_TILE
    for src, dst in zip(cast_in, cast_out):
        dst[...] = src[...].astype(BF16)

    @pl.when(t == 0)
    def _():
        st_ref[...] = jnp.zeros_like(st_ref)

    seqs = range(gb)
    x = [x_ref[g] for g in seqs]
    mod = [_split_mod(mod_ref[pl.ds(b * gb + g, 1), :]) for g in seqs]
    h = [_mod_norm_seq(x[g], g_ref[...], mod[g][0], mod[g][1]) for g in seqs]
    row = lax.broadcasted_iota(jnp.int32, (tb, ct), 0)

    for g in seqs:
        def prev_fn(cs, u, g=g):
            c0 = st_ref[g, SUBLANE - 2:SUBLANE - 1, cs]
            c1 = st_ref[g, SUBLANE - 1:SUBLANE, cs]
            p1 = jnp.where(row == 0, c1, pltpu.roll(u, 1, 0))
            p2 = jnp.where(row == 0, c0, jnp.where(row == 1, c1, pltpu.roll(u, 2, 0)))
            return p1, p2

        def tail_fn(cs, u, g=g):
            st_ref[g, :, cs] = u[tb - SUBLANE:tb, :]

        _conv_columns(h[g], win_ref, cw_ref, y_scr.at[g], prev_fn, tail_fn)
    for g in seqs:
        xo_ref[g] = x[g] + mod[g][2] * _dot(y_scr[g], wout_ref[...])


def _conv_prompt_call(x, mods, norm_g, w_in, cw, w_out, layer, mod_block, casts=()):
    bsz, seq, _ = x.shape
    tb = CONV_TB
    gb = CONV_GB
    j = layer // 2
    n_t = seq // tb
    n_steps = (bsz // gb) * n_t
    cast_in_specs, cast_out_specs, cast_shapes = [], [], []
    for mat, first, rows in casts:
        blk = rows // n_steps
        assert rows % n_steps == 0 and blk % (2 * SUBLANE) == 0 and first % blk == 0
        cast_in_specs.append(pl.BlockSpec((blk, mat.shape[1]), lambda b, t, o=first // blk: (o + b * n_t + t, 0)))
        cast_out_specs.append(pl.BlockSpec((blk, mat.shape[1]), lambda b, t: (b * n_t + t, 0)))
        cast_shapes.append(jax.ShapeDtypeStruct((rows, mat.shape[1]), BF16))
    return pl.pallas_call(
        functools.partial(_conv_prompt_kernel, tb=tb, gb=gb, n_cast=len(casts)),
        grid=(bsz // gb, n_t),
        in_specs=[
            pl.BlockSpec((gb, tb, D_MODEL), lambda b, t: (b, t, 0)),
            _resident((None, bsz, 3 * D_MODEL), (layer, mod_block, 0), 2),
            _resident((None, 1, D_MODEL), (layer, 0, 0), 2),
            _resident((None, D_MODEL, 4 * E_CONV), (0, 0, 0), 2),
            _resident((None, CONV_WIDTH, E_CONV), (j, 0, 0), 2),
            _resident((None, E_CONV, D_MODEL), (0, 0, 0), 2),
        ] + cast_in_specs,
        out_specs=[
            pl.BlockSpec((gb, tb, D_MODEL), lambda b, t: (b, t, 0)),
            pl.BlockSpec((gb, SUBLANE, E_CONV), lambda b, t: (b, 0, 0)),
        ] + cast_out_specs,
        out_shape=[
            jax.ShapeDtypeStruct((bsz, seq, D_MODEL), F32),
            jax.ShapeDtypeStruct((bsz, SUBLANE, E_CONV), F32),
        ] + cast_shapes,
        scratch_shapes=[pltpu.VMEM((gb, tb, E_CONV), BF16)],
        compiler_params=_params(2),
        name="conv_prompt",
    )(x, mods, norm_g, w_in, cw, w_out, *[mat for mat, _, _ in casts])


def _tok(s, width):
    return slice(s * width, (s + 1) * width)


def _stack_tokens(ref, n_tok, width):
    return jnp.concatenate([ref[:, _tok(s, width)] for s in range(n_tok)], axis=0)


def _read_tok(ref, s, width):
    return ref[:, s, :] if len(ref.shape) == 3 else ref[:, _tok(s, width)]


def _write_tok(ref, s, width, val):
    if len(ref.shape) == 3:
        ref[:, s, :] = val
    else:
        ref[:, _tok(s, width)] = val


def _sample_hidden(x_ref, mod_ref, g_ref, n_tok):
    shift, scale, _ = _split_mod(mod_ref[...])
    g = g_ref[...]
    return jnp.concatenate(
        [_mod_norm(_read_tok(x_ref, s, D_MODEL), g, shift, scale) for s in range(n_tok)], axis=0)


def _conv_sample_kernel(x_ref, mod_ref, g_ref, win_ref, cw_ref, wout_ref, st_ref,
                        xo_ref, so_ref, y_scr, *, n_tok):
    ns = x_ref.shape[0]
    h = _sample_hidden(x_ref, mod_ref, g_ref, n_tok)

    def prev_fn(cs, u):
        st0 = st_ref[:, 0, cs]
        st1 = st_ref[:, 1, cs]
        p1 = jnp.concatenate([st1, u[:(n_tok - 1) * ns]], axis=0)
        p2 = jnp.concatenate([st0, st1, u[:(n_tok - 2) * ns]], axis=0)
        return p1, p2

    def tail_fn(cs, u):
        so_ref[:, 0, cs] = u[(n_tok - 2) * ns:(n_tok - 1) * ns]
        so_ref[:, 1, cs] = u[(n_tok - 1) * ns:]

    _conv_columns(h, win_ref, cw_ref, y_scr, prev_fn, tail_fn)
    out = _dot(y_scr[...], wout_ref[...])
    gate = mod_ref[:, 2 * D_MODEL:]
    for s in range(n_tok):
        xo_ref[:, _tok(s, D_MODEL)] = _read_tok(x_ref, s, D_MODEL) + gate * out[s * ns:(s + 1) * ns]


def _seq_rows(width, layer=None):
    if layer is None:
        return pl.BlockSpec((SAMPLE_SEQ_BLOCK, width), lambda i: (i, 0))
    return pl.BlockSpec((None, SAMPLE_SEQ_BLOCK, width), lambda i: (layer, i, 0))


def _conv_sample_call(x, mods, norm_g, w_in, cw, w_out, state, layer, n_tok):
    n_seq = x.shape[0]
    j = layer // 2
    return pl.pallas_call(
        functools.partial(_conv_sample_kernel, n_tok=n_tok),
        grid=(n_seq // SAMPLE_SEQ_BLOCK,),
        in_specs=[
            _seq_rows(n_tok * D_MODEL) if x.ndim == 2 else
            pl.BlockSpec((SAMPLE_SEQ_BLOCK, n_tok, D_MODEL), lambda i: (i, 0, 0)),
            _seq_rows(3 * D_MODEL, layer),
            _resident((None, 1, D_MODEL), (layer, 0, 0), 1),
            _resident((None, D_MODEL, 4 * E_CONV), (0, 0, 0), 1),
            _resident((None, CONV_WIDTH, E_CONV), (j, 0, 0), 1),
            _resident((None, E_CONV, D_MODEL), (0, 0, 0), 1),
            pl.BlockSpec((None, SAMPLE_SEQ_BLOCK, CONV_WIDTH - 1, E_CONV), lambda i: (j, i, 0, 0)),
        ],
        out_specs=[_seq_rows(n_tok * D_MODEL),
                   pl.BlockSpec((SAMPLE_SEQ_BLOCK, CONV_WIDTH - 1, E_CONV), lambda i: (i, 0, 0))],
        out_shape=[
            jax.ShapeDtypeStruct((n_seq, n_tok * D_MODEL), F32),
            jax.ShapeDtypeStruct((n_seq, CONV_WIDTH - 1, E_CONV), F32),
        ],
        scratch_shapes=[pltpu.VMEM((n_tok * SAMPLE_SEQ_BLOCK, E_CONV), BF16)],
        compiler_params=_params(1),
        name="conv_sample",
    )(x, mods, norm_g, w_in, cw, w_out, state)


def _rows_from_blocks(vecs):
    return jnp.concatenate([jnp.broadcast_to(v, (SUB, v.shape[1])) for v in vecs], axis=0)


def _chunk_masks():
    r = lax.broadcasted_iota(jnp.int32, (CHUNK, CHUNK), 0)
    c = lax.broadcasted_iota(jnp.int32, (CHUNK, CHUNK), 1)
    causal = (r // SUB == c // SUB) & (c <= r)
    tri = jnp.where(causal, 1.0, 0.0).astype(BF16)
    first_cols = lax.broadcasted_iota(jnp.int32, (SUB, CHUNK), 1) < SUB
    return causal, first_cols, jnp.concatenate([tri, tri], axis=1)


def _chunk_decay(lf, tri2):
    hi = lf.astype(BF16)
    lo = (lf - hi.astype(F32)).astype(BF16)
    return _dot(tri2, jnp.concatenate([hi, lo], axis=0))


def _chunk_key_operands(kk, gl):
    t0 = gl[SUB - 1:SUB, :]
    t1 = gl[2 * SUB - 1:2 * SUB, :]
    k_mid = kk * jnp.exp2(_rows_from_blocks([0.5 * t0, 0.5 * t1]) - gl)
    k_state = k_mid * _rows_from_blocks([jnp.exp2(0.5 * t0 + t1), jnp.exp2(0.5 * t1)])
    in_span = (t0 >= -HGRN_SAFE_LOG2_SPAN) & (t1 >= -HGRN_SAFE_LOG2_SPAN)
    return k_mid.astype(BF16), k_state.astype(BF16), jnp.exp2(t0 + t1), jnp.where(in_span, 0.0, 1.0)


def _chunk_query_operands(qf, gl):
    t0 = gl[SUB - 1:SUB, :]
    t1 = gl[2 * SUB - 1:2 * SUB, :]
    q_mid = qf * jnp.exp2(gl - _rows_from_blocks([0.5 * t0, 0.5 * t1]))
    q_next = q_mid[SUB:] * jnp.exp2(0.5 * (t0 + t1))
    q_state = q_mid * _rows_from_blocks([jnp.exp2(0.5 * t0), jnp.exp2(0.5 * t1 + t0)])
    return jnp.concatenate([q_mid, q_next], axis=0).astype(BF16), q_state.astype(BF16)


def _chunk_finish(ast, qe, kr, vb, ptot, s_old, causal, first_cols):
    same = jnp.where(causal, ast[0:CHUNK], 0.0)
    a = jnp.concatenate([same[:SUB], jnp.where(first_cols, ast[CHUNK:], same[SUB:])], axis=0)
    o = _dot(jnp.concatenate([qe, a.astype(BF16)], axis=1),
             jnp.concatenate([s_old.astype(BF16), vb], axis=0))
    dec = jnp.broadcast_to(ptot, (HG_DK, HG_DK)).T
    return o, s_old * dec + _dot_tn(kr, vb)


def _exact_recurrence(q_ref, k_ref, f_ref, v_ref, o_ref, s0, n_rows):
    def col(tile, r):
        return jnp.broadcast_to(tile[r:r + 1, :], (HG_DK, HG_DK)).T

    def step(i, s):
        rows = pl.ds(pl.multiple_of(i * SUBLANE, SUBLANE), SUBLANE)
        q, k, f, v = q_ref[rows, :], k_ref[rows, :], f_ref[rows, :], v_ref[rows, :]
        out = []
        for r in range(SUBLANE):
            s = s * col(f, r) + col(k, r) * v[r:r + 1, :]
            out.append(jnp.sum(col(q, r) * s, axis=0, keepdims=True))
        o_ref[rows, :] = jnp.concatenate(out, axis=0)
        return s
    return lax.fori_loop(0, n_rows // SUBLANE, step, s0)


def _decode_state_update(p, el_row0, dq_ref, dk_ref, dv_ref, del_ref, ds_ref, do_ref, dso_ref, partial_o):
    nd = ds_ref.shape[0]
    n_tok = dq_ref.shape[0] // nd
    hd, sq = divmod(p, nd)
    hs = slice(hd * HG_DK, (hd + 1) * HG_DK)
    mine = lax.broadcasted_iota(jnp.int32, (dq_ref.shape[0], HG_DK), 0) // n_tok == sq
    s_old = ds_ref[sq, hd]
    o = _dot(dq_ref[:, hs], s_old.astype(BF16))
    partial_o[hd] = jnp.where(mine, o, partial_o[hd]) if hd in partial_o else jnp.where(mine, o, 0.0)
    k_mine = jnp.where(mine, dk_ref[:, hs], 0.0).astype(BF16)
    dec = jnp.broadcast_to(del_ref[pl.ds(el_row0 + sq, 1), :][:, hs], (HG_DK, HG_DK)).T
    dso_ref[sq, hd] = s_old * dec + _dot_tn(k_mine, dv_ref[:, hs])
    if sq == nd - 1:
        do_ref[:, hs] = partial_o.pop(hd)


def _hgrn_prompt_kernel(*refs, tb, gb, layer, final, has_prev):
    n_in = 13 + int(has_prev)
    x_ref, mod_ref, g_ref, win_ref, lbraw_ref, og_ref, wout_ref, fg_ref = refs[:8]
    decode_in = refs[8:13]
    xo_ref, s_ref, do_ref, dso_ref = refs[n_in:n_in + 4]
    q_scr, lf_scr, k_scr, v_scr, z_scr, o_scr, s0_scr = refs[n_in + 4:]
    b = pl.program_id(0)
    t = pl.program_id(1)

    @pl.when(t == 0)
    def _():
        s_ref[...] = jnp.zeros_like(s_ref)

    s0_scr[...] = s_ref[...]

    seqs = range(gb)
    lb = _lower_bound(lbraw_ref[...], layer)
    x = [x_ref[g] for g in seqs]
    mod = [_split_mod(mod_ref[pl.ds(b * gb + g, 1), :]) for g in seqs]
    h = [_mod_norm_seq(x[g], g_ref[...], mod[g][0], mod[g][1]) for g in seqs]
    for g in seqs:
        q_scr[g] = _silu(_dot(h[g], win_ref[:, 0:HG_F]))
    for g in seqs:
        logf, kk = _gate_math(_dot(h[g], win_ref[:, HG_F:2 * HG_F]), lb)
        lf_scr[g] = logf
        k_scr[g] = kk
    for g in seqs:
        v_scr[g] = _dot(h[g], win_ref[:, 2 * HG_F:2 * HG_F + HG_I])
    for g in seqs:
        z_scr[g] = _dot(h[g], win_ref[:, 2 * HG_F + HG_I:])

    causal, first_cols, tri2 = _chunk_masks()
    chains = [(c, hd, g) for c in range(tb // CHUNK) for hd in range(HG_HEADS) for g in seqs]
    ops, scores = {}, {}
    unsafe = jnp.zeros((1, HG_DK), F32)

    def where(n):
        c, hd, g = chains[n]
        return g, hd, slice(c * CHUNK, (c + 1) * CHUNK), slice(hd * HG_DK, (hd + 1) * HG_DK)

    block_seqs = decode_in[4].shape[0]
    n_decode = HG_HEADS * block_seqs
    blocks_per_el = decode_in[3].shape[0] // block_seqs
    el_row0 = ((b * pl.num_programs(1) + t) % blocks_per_el) * block_seqs
    partial_o = {}
    for step in range(max(len(chains) + 2 * HGRN_PIPE_SKEW, n_decode)):
        if step < n_decode:
            _decode_state_update(step, el_row0, *decode_in, do_ref, dso_ref, partial_o)
        n = step
        if n < len(chains):
            g, hd, rows, hs = where(n)
            gl = _chunk_decay(lf_scr[g, rows, hs], tri2)
            km, kr, ptot, bad = _chunk_key_operands(k_scr[g, rows, hs], gl)
            lhs, qe = _chunk_query_operands(q_scr[g, rows, hs], gl)
            ops[n] = (lhs, km, qe, kr, ptot)
            unsafe = jnp.maximum(unsafe, bad)
        n = step - HGRN_PIPE_SKEW
        if 0 <= n < len(chains):
            scores[n] = _dot_nt(ops[n][0], ops[n][1])
        n = step - 2 * HGRN_PIPE_SKEW
        if 0 <= n < len(chains):
            g, hd, rows, hs = where(n)
            _, _, qe, kr, ptot = ops.pop(n)
            vb = v_scr[g, rows, hs].astype(BF16)
            o, s_new = _chunk_finish(scores.pop(n), qe, kr, vb, ptot, s_ref[g, hd], causal, first_cols)
            o_scr[g, rows, hs] = o
            s_ref[g, hd] = s_new

    y = [_head_norm_gate(o_scr[g], z_scr[g], og_ref[...]) for g in seqs]
    for g in seqs:
        xn = x[g] + mod[g][2] * _dot(y[g], wout_ref[...])
        if final:
            xn = _rmsnorm(xn, fg_ref[...])
        xo_ref[g] = xn

    @pl.when(jnp.max(unsafe) > 0.0)
    def _():
        def exact(q_tmp, k_tmp, f_tmp, v_tmp, o_tmp):
            def one_seq(g, carry):
                shift, scale, gate = _split_mod(mod_ref[pl.ds(b * gb + g, 1), :])
                hg = _mod_norm_seq(x_ref[g], g_ref[...], shift, scale)
                qf = _silu(_dot(hg, win_ref[:, 0:HG_F]))
                logf, kk = _gate_math(_dot(hg, win_ref[:, HG_F:2 * HG_F]), lb)
                ff = jnp.exp2(logf)
                vv = _dot(hg, win_ref[:, 2 * HG_F:2 * HG_F + HG_I])
                for hd in range(HG_HEADS):
                    hs = slice(hd * HG_DK, (hd + 1) * HG_DK)
                    q_tmp[hd], k_tmp[hd], f_tmp[hd], v_tmp[hd] = qf[:, hs], kk[:, hs], ff[:, hs], vv[:, hs]

                def one_head(hd, c2):
                    s_ref[g, hd] = _exact_recurrence(q_tmp.at[hd], k_tmp.at[hd], f_tmp.at[hd], v_tmp.at[hd],
                                                     o_tmp.at[hd], s0_scr[g, hd], tb)
                    return c2

                lax.fori_loop(0, HG_HEADS, one_head, 0)
                o = jnp.concatenate([o_tmp[hd] for hd in range(HG_HEADS)], axis=1)
                y = _head_norm_gate(o, z_scr[g], og_ref[...])
                xn = x_ref[g] + gate * _dot(y, wout_ref[...])
                if final:
                    xn = _rmsnorm(xn, fg_ref[...])
                xo_ref[g] = xn
                return carry

            lax.fori_loop(0, gb, one_seq, 0)

        pl.run_scoped(exact, *[pltpu.VMEM((HG_HEADS, tb, HG_DK), F32)] * 5)


def _hgrn_prompt_call(x, mods, norm_g, w_in, lb_raw, og, w_out, fg, decode, layer, mod_block):
    bsz, seq, _ = x.shape
    tb = HGRN_TB
    gb = HGRN_GB
    j = layer // 2
    final = layer == DEPTH - 1
    n_t = seq // tb
    qe, kr, v, el, states, prev_states = decode
    n_blocks = (bsz // gb) * n_t
    n_seq = el.shape[0]
    block_seqs = n_seq // n_blocks
    dec_rows = qe.shape[0] // n_blocks
    el_rows = max(block_seqs, SUBLANE)
    assert n_seq == n_blocks * block_seqs and el_rows % block_seqs == 0 and dec_rows % SUBLANE == 0
    dec_spec = pl.BlockSpec((dec_rows, HG_F), lambda b, t: (b * n_t + t, 0))
    st_spec = pl.BlockSpec((None, block_seqs, HG_HEADS, HG_DK, HG_DV), lambda b, t: (j, b * n_t + t, 0, 0, 0))
    in_specs = [
        pl.BlockSpec((gb, tb, D_MODEL), lambda b, t: (b, t, 0)),
        _resident((None, bsz, 3 * D_MODEL), (layer, mod_block, 0), 2),
        _resident((None, 1, D_MODEL), (layer, 0, 0), 2),
        _resident((None, D_MODEL, 2 * HG_F + 2 * HG_I), (j, 0, 0), 2),
        _resident((DEPTH, HG_F), (0, 0), 2),
        _resident((None, 1, HG_I), (j, 0, 0), 2),
        _resident((None, HG_I, D_MODEL), (j, 0, 0), 2),
        _resident((1, D_MODEL), (0, 0), 2),
        dec_spec, dec_spec, dec_spec,
        pl.BlockSpec((el_rows, HG_F), lambda b, t: ((b * n_t + t) * block_seqs // el_rows, 0)),
        st_spec,
    ]
    args = [x, mods, norm_g, w_in, lb_raw, og, w_out, fg, qe, kr, v, el, states]
    aliases = {}
    if prev_states is not None:
        in_specs.append(pl.BlockSpec(memory_space=pl.ANY))
        args.append(prev_states)
        aliases = {len(args) - 1: 3}
    return pl.pallas_call(
        functools.partial(_hgrn_prompt_kernel, tb=tb, gb=gb, layer=layer, final=final,
                          has_prev=prev_states is not None),
        grid=(bsz // gb, n_t),
        in_specs=in_specs,
        out_specs=[
            pl.BlockSpec((gb, tb, D_MODEL), lambda b, t: (b, t, 0)),
            pl.BlockSpec((gb, HG_HEADS, HG_DK, HG_DV), lambda b, t: (b, 0, 0, 0)),
            dec_spec,
            st_spec,
        ],
        out_shape=[
            jax.ShapeDtypeStruct((bsz, seq, D_MODEL), F32),
            jax.ShapeDtypeStruct((bsz, HG_HEADS, HG_DK, HG_DV), F32),
            jax.ShapeDtypeStruct(qe.shape, F32),
            jax.ShapeDtypeStruct(states.shape, F32),
        ],
        scratch_shapes=[pltpu.VMEM((gb, tb, HG_F), F32)] * 6
        + [pltpu.VMEM((gb, HG_HEADS, HG_DK, HG_DV), F32)],
        input_output_aliases=aliases,
        compiler_params=_params(2),
        name="hgrn_prompt",
    )(*args)


def _token_row_permutation(n_seq, n_tok, seq_major_out):
    n = n_seq * n_tok
    r = lax.broadcasted_iota(jnp.int32, (n, n), 0)
    c = lax.broadcasted_iota(jnp.int32, (n, n), 1)
    sm, tm = (r, c) if seq_major_out else (c, r)
    return jnp.where(tm == (sm % n_tok) * n_seq + sm // n_tok, 1.0, 0.0).astype(BF16)


def _permute_rows_f32(p, a):
    hi = a.astype(BF16)
    rest = a - hi.astype(F32)
    mid = rest.astype(BF16)
    lo = (rest - mid.astype(F32)).astype(BF16)
    return (_dot(p, hi) + _dot(p, mid)) + _dot(p, lo)


def _hgrn_sample_pre_kernel(x_ref, mod_ref, g_ref, win_ref, lbraw_ref,
                            qe_ref, kr_ref, v_ref, el_ref, oi_ref, z_ref, *, n_tok, layer):
    ns = x_ref.shape[0]
    h = _sample_hidden(x_ref, mod_ref, g_ref, n_tok)
    lb = _lower_bound(lbraw_ref[...], layer)
    qf = _silu(_dot(h, win_ref[:, 0:HG_F]))
    lf, kk = _gate_math(_dot(h, win_ref[:, HG_F:2 * HG_F]), lb)
    vv = _dot(h, win_ref[:, 2 * HG_F:2 * HG_F + HG_I])
    z = _dot(h, win_ref[:, 2 * HG_F + HG_I:])

    def tok(a, s):
        return a[s * ns:(s + 1) * ns]

    def decay(s, t):
        if s == t:
            return 1.0
        return jnp.exp2(functools.reduce(lambda a, b: a + b, [tok(lf, u) for u in range(s + 1, t + 1)]))

    last = n_tok - 1
    el_ref[...] = decay(-1, last)
    to_seq_major = _token_row_permutation(ns, n_tok, seq_major_out=True)
    qe = jnp.concatenate([tok(qf, s) * decay(-1, s) for s in range(n_tok)], axis=0)
    kr = jnp.concatenate([tok(kk, s) * decay(s, last) for s in range(n_tok)], axis=0)
    qe_ref[...] = _dot(to_seq_major, qe.astype(BF16)).astype(BF16)
    kr_ref[...] = _dot(to_seq_major, kr.astype(BF16)).astype(BF16)
    v_ref[...] = _dot(to_seq_major, vv.astype(BF16)).astype(BF16)
    for s in range(n_tok):
        z_ref[:, _tok(s, HG_I)] = tok(z, s)

    for t in range(n_tok):
        acc = jnp.zeros((ns, HG_I), F32)
        for s in range(t + 1):
            e = tok(qf, t) * tok(kk, s) * decay(s, t)
            vs = tok(vv, s)
            parts = []
            for hd in range(HG_HEADS):
                hs = slice(hd * HG_DK, (hd + 1) * HG_DK)
                parts.append(jnp.sum(e[:, hs], axis=-1, keepdims=True) * vs[:, hs])
            acc = acc + jnp.concatenate(parts, axis=1)
        oi_ref[:, _tok(t, HG_I)] = acc


def _hgrn_sample_pre_call(x, mods, norm_g, w_in, lb_raw, layer, n_tok):
    n_seq = x.shape[0]
    j = layer // 2
    act = jax.ShapeDtypeStruct((n_seq, n_tok * HG_F), F32)
    tok_rows = jax.ShapeDtypeStruct((n_seq * n_tok, HG_F), BF16)
    tok_spec = pl.BlockSpec((SAMPLE_SEQ_BLOCK * n_tok, HG_F), lambda i: (i, 0))
    return pl.pallas_call(
        functools.partial(_hgrn_sample_pre_kernel, n_tok=n_tok, layer=layer),
        grid=(n_seq // SAMPLE_SEQ_BLOCK,),
        in_specs=[
            _seq_rows(n_tok * D_MODEL),
            _seq_rows(3 * D_MODEL, layer),
            _resident((None, 1, D_MODEL), (layer, 0, 0), 1),
            _resident((None, D_MODEL, 2 * HG_F + 2 * HG_I), (j, 0, 0), 1),
            _resident((DEPTH, HG_F), (0, 0), 1),
        ],
        out_specs=[tok_spec] * 3 + [_seq_rows(HG_F)] + [_seq_rows(n_tok * HG_I)] * 2,
        out_shape=[tok_rows, tok_rows, tok_rows, jax.ShapeDtypeStruct((n_seq, HG_F), F32), act, act],
        compiler_params=_params(1),
        name="hgrn_sample_pre",
    )(x, mods, norm_g, w_in, lb_raw)


def _hgrn_sample_post_kernel(x_ref, mod_ref, oa_ref, ob_ref, z_ref, og_ref, wout_ref, fg_ref, xo_ref,
                             *, n_tok, final):
    ns = x_ref.shape[0]
    o_state = _permute_rows_f32(_token_row_permutation(ns, n_tok, seq_major_out=False), oa_ref[...])
    o = o_state + _stack_tokens(ob_ref, n_tok, HG_I)
    y = _head_norm_gate(o, _stack_tokens(z_ref, n_tok, HG_I), og_ref[...])
    out = _dot(y, wout_ref[...])
    gate = mod_ref[:, 2 * D_MODEL:]
    for s in range(n_tok):
        xn = x_ref[:, _tok(s, D_MODEL)] + gate * out[s * ns:(s + 1) * ns]
        if final:
            xn = _rmsnorm(xn, fg_ref[...])
        _write_tok(xo_ref, s, D_MODEL, xn)


def _hgrn_sample_post_call(x, mods, o_inter, o_intra, z, og, w_out, fg, layer, n_tok):
    n_seq = x.shape[0]
    j = layer // 2
    return pl.pallas_call(
        functools.partial(_hgrn_sample_post_kernel, n_tok=n_tok, final=layer == DEPTH - 1),
        grid=(n_seq // SAMPLE_SEQ_BLOCK,),
        in_specs=[
            _seq_rows(n_tok * D_MODEL),
            _seq_rows(3 * D_MODEL, layer),
            pl.BlockSpec((SAMPLE_SEQ_BLOCK * n_tok, HG_I), lambda i: (i, 0)),
            _seq_rows(n_tok * HG_I), _seq_rows(n_tok * HG_I),
            _resident((None, 1, HG_I), (j, 0, 0), 1),
            _resident((None, HG_I, D_MODEL), (j, 0, 0), 1),
            _resident((1, D_MODEL), (0, 0), 1),
        ],
        out_specs=(pl.BlockSpec((SAMPLE_SEQ_BLOCK, n_tok, D_MODEL), lambda i: (i, 0, 0)) if layer == DEPTH - 1
                   else _seq_rows(n_tok * D_MODEL)),
        out_shape=jax.ShapeDtypeStruct((n_seq, n_tok, D_MODEL) if layer == DEPTH - 1
                                       else (n_seq, n_tok * D_MODEL), F32),
        compiler_params=_params(1),
        name="hgrn_sample_post",
    )(x, mods, o_inter, o_intra, z, og, w_out, fg)


def kernel(x_prompt, x_sample, state_conv, state_hgrn, c_prompt, c_sample, norm_g, w_ada, b_ada, conv_w_in, conv_w, conv_w_out, hgrn_w_in, hgrn_lower_bounds, hgrn_onorm_g, hgrn_w_out, final_norm_g):
    n_p = x_prompt.shape[0]
    n_s, t_s, _ = x_sample.shape
    assert t_s >= CONV_WIDTH - 1 and n_s % SAMPLE_SEQ_BLOCK == 0 and n_s % n_p == 0

    mods = _ada_call(jnp.concatenate([c_sample, c_prompt], axis=0), w_ada, b_ada)
    mod_block_p = n_s // n_p

    n_conv, n_hgrn = conv_w_in.shape[0], hgrn_w_in.shape[0]
    assert n_conv == 2 and DEPTH == 4
    conv_w_in_b = [conv_w_in[:1].astype(BF16)]
    conv_w_out_b = [conv_w_out[:1].astype(BF16)]
    later_weights = [
        (hgrn_w_in.reshape(n_hgrn * D_MODEL, -1), 0, n_hgrn * D_MODEL),
        (hgrn_w_out.reshape(n_hgrn * HG_I, D_MODEL), 0, n_hgrn * HG_I),
        (conv_w_in.reshape(n_conv * D_MODEL, -1), D_MODEL, (n_conv - 1) * D_MODEL),
        (conv_w_out.reshape(n_conv * E_CONV, D_MODEL), E_CONV, (n_conv - 1) * E_CONV),
    ]
    norm_g3 = norm_g.reshape(DEPTH, 1, D_MODEL)
    og3 = hgrn_onorm_g.reshape(DEPTH // 2, 1, HG_I)
    fg = final_norm_g.reshape(1, D_MODEL)

    xp = x_prompt
    xs = x_sample
    conv_p, conv_s, hgrn_p = [], [], []
    hgrn_s = None
    for layer in range(DEPTH):
        if layer % 2 == 0:
            j = layer // 2
            xp, tail, *rounded = _conv_prompt_call(xp, mods, norm_g3, conv_w_in_b[j], conv_w, conv_w_out_b[j],
                                                   layer, mod_block_p, later_weights if layer == 0 else ())
            if layer == 0:
                hgrn_w_in_b = rounded[0].reshape(hgrn_w_in.shape)
                hgrn_w_out_b = rounded[1].reshape(hgrn_w_out.shape)
                conv_w_in_b.append(rounded[2].reshape((1,) + conv_w_in.shape[1:]))
                conv_w_out_b.append(rounded[3].reshape((1,) + conv_w_out.shape[1:]))
            conv_p.append(tail[:, SUBLANE - (CONV_WIDTH - 1):])
            xs, st = _conv_sample_call(xs, mods, norm_g3, conv_w_in_b[j], conv_w, conv_w_out_b[j], state_conv,
                                       layer, t_s)
            conv_s.append(st)
        else:
            qe, kr, v, el, o_intra, z = _hgrn_sample_pre_call(xs, mods, norm_g3, hgrn_w_in_b,
                                                              hgrn_lower_bounds, layer, t_s)
            decode = (qe, kr, v, el, state_hgrn, hgrn_s)
            xp, s_new, o_inter, hgrn_s = _hgrn_prompt_call(xp, mods, norm_g3, hgrn_w_in_b, hgrn_lower_bounds,
                                                           og3, hgrn_w_out_b, fg, decode, layer, mod_block_p)
            hgrn_p.append(s_new)
            xs = _hgrn_sample_post_call(xs, mods, o_inter, o_intra, z, og3, hgrn_w_out_b, fg, layer, t_s)
    return (xp, xs, jnp.stack(conv_p), jnp.stack(hgrn_p), jnp.stack(conv_s), hgrn_s)
```

```python
import functools

import jax
import jax.numpy as jnp
from jax import lax
from jax.experimental import pallas as pl
from jax.experimental.pallas import tpu as pltpu

F32 = jnp.float32
BF16 = jnp.bfloat16

D_MODEL = 1024
DEPTH = 4
CONV_WIDTH = 3
E_CONV = D_MODEL
HG_HEADS = 8
HG_DK = 128
HG_DV = 128
HG_F = HG_HEADS * HG_DK
HG_I = HG_HEADS * HG_DV
EPS = 1e-6

SUBLANE = 8
CHUNK = 128
SUB = CHUNK // 2
CONV_COL_TILE = 256
CONV_TB = 1024
CONV_GB = 1
HGRN_TB = 256
HGRN_GB = 2
HGRN_PIPE_SKEW = 3
HGRN_SAFE_LOG2_SPAN = 200.0
SAMPLE_SEQ_BLOCK = 64
V7X_VMEM_BYTES = 64 * 1024 * 1024
VMEM_LIMIT = V7X_VMEM_BYTES * 7 // 8


def _dot(a, b):
    return jnp.dot(a, b, preferred_element_type=F32)


def _dot_nt(a, b):
    return lax.dot_general(a, b, (((1,), (1,)), ((), ())), preferred_element_type=F32)


def _dot_tn(a, b):
    return lax.dot_general(a, b, (((0,), (0,)), ((), ())), preferred_element_type=F32)


def _silu(x):
    return x * jax.nn.sigmoid(x)


def _rmsnorm(x, g):
    ms = jnp.mean(x * x, axis=-1, keepdims=True)
    return x * lax.rsqrt(ms + EPS) * g


def _split_mod(mod):
    return mod[:, :D_MODEL], mod[:, D_MODEL:2 * D_MODEL], mod[:, 2 * D_MODEL:]


def _mod_norm(x, g, shift, scale):
    return (_rmsnorm(x, g) * (1.0 + scale) + shift).astype(BF16)


def _mod_norm_seq(x, g, shift, scale):
    ms = jnp.mean(x * x, axis=-1, keepdims=True)
    return (x * lax.rsqrt(ms + EPS) * (g * (1.0 + scale)) + shift).astype(BF16)


def _lower_bound(raw, layer):
    rows = [raw[i:i + 1, :] for i in range(DEPTH)]
    m = functools.reduce(jnp.maximum, rows)
    es = [jnp.exp(r - m) for r in rows]
    tot = functools.reduce(lambda a, b: a + b, es)
    acc = es[1]
    for i in range(2, layer + 1):
        acc = acc + es[i]
    return acc / tot


def _gate_math(fpre, lb):
    t = jnp.exp(-jnp.abs(fpre))
    r = 1.0 / (1.0 + t)
    pos = fpre >= 0.0
    sig = jnp.where(pos, 1.0, t) * r
    sig_neg = jnp.where(pos, t, 1.0) * r
    return jnp.log2(lb + (1.0 - lb) * sig), (1.0 - lb) * sig_neg


def _head_norm_gate(o, z, og):
    parts = []
    for hd in range(HG_HEADS):
        hs = slice(hd * HG_DV, (hd + 1) * HG_DV)
        oh = o[:, hs]
        ms = jnp.mean(oh * oh, axis=-1, keepdims=True)
        parts.append(oh * lax.rsqrt(ms + EPS))
    on = jnp.concatenate(parts, axis=1) * og
    return (on * _silu(z)).astype(BF16)


def _params(n_grid):
    return pltpu.CompilerParams(dimension_semantics=("arbitrary",) * n_grid, vmem_limit_bytes=VMEM_LIMIT)


def _resident(shape, index, n_grid):
    if n_grid == 1:
        return pl.BlockSpec(shape, lambda i: index, pipeline_mode=pl.Buffered(1))
    return pl.BlockSpec(shape, lambda b, t: index, pipeline_mode=pl.Buffered(1))


def _ada_kernel(c_ref, w_ref, b_ref, o_ref):
    s = _silu(c_ref[...]).astype(BF16)
    o_ref[...] = _dot(s, w_ref[...].astype(BF16)) + b_ref[...]


def _ada_call(c_all, w_ada, b_ada):
    rows = c_all.shape[0]
    n_col = 1
    col = 3 * D_MODEL // n_col
    return pl.pallas_call(
        _ada_kernel,
        grid=(DEPTH, n_col),
        in_specs=[
            pl.BlockSpec((rows, D_MODEL), lambda l, n: (0, 0)),
            pl.BlockSpec((None, D_MODEL, col), lambda l, n: (l, 0, n)),
            pl.BlockSpec((None, 1, col), lambda l, n: (l, 0, n)),
        ],
        out_specs=pl.BlockSpec((None, rows, col), lambda l, n: (l, 0, n)),
        out_shape=jax.ShapeDtypeStruct((DEPTH, rows, 3 * D_MODEL), F32),
        compiler_params=_params(2),
        name="ada_mod",
    )(c_all, w_ada, b_ada.reshape(DEPTH, 1, 3 * D_MODEL))


def _conv_columns(h, win_ref, cw_ref, y_scr, prev_fn, tail_fn):
    ct = CONV_COL_TILE
    for n in range(E_CONV // ct):
        def col(j):
            return slice(j * E_CONV + n * ct, j * E_CONV + (n + 1) * ct)
        cs = slice(n * ct, (n + 1) * ct)
        b_gate = _dot(h, win_ref[:, col(0)])
        c_gate = _dot(h, win_ref[:, col(1)])
        v = _dot(h, win_ref[:, col(2)])
        z = _dot(h, win_ref[:, col(3)])
        u = c_gate * v
        p1, p2 = prev_fn(cs, u)
        conv = cw_ref[0:1, cs] * p2 + cw_ref[1:2, cs] * p1 + cw_ref[2:3, cs] * u
        y_scr[:, cs] = (b_gate * conv * _silu(z)).astype(BF16)
        tail_fn(cs, u)


def _conv_prompt_kernel(*refs, tb, gb, n_cast):
    x_ref, mod_ref, g_ref, win_ref, cw_ref, wout_ref = refs[:6]
    cast_in = refs[6:6 + n_cast]
    xo_ref, st_ref = refs[6 + n_cast:8 + n_cast]
    cast_out = refs[8 + n_cast:8 + 2 * n_cast]
    y_scr, = refs[8 + 2 * n_cast:]
    b = pl.program_id(0)
    t = pl.program_id(1)
    ct = CONV_COL_TILE
    for src, dst in zip(cast_in, cast_out):
        dst[...] = src[...].astype(BF16)

    @pl.when(t == 0)
    def _():
        st_ref[...] = jnp.zeros_like(st_ref)

    seqs = range(gb)
    x = [x_ref[g] for g in seqs]
    mod = [_split_mod(mod_ref[pl.ds(b * gb + g, 1), :]) for g in seqs]
    h = [_mod_norm_seq(x[g], g_ref[...], mod[g][0], mod[g][1]) for g in seqs]
    row = lax.broadcasted_iota(jnp.int32, (tb, ct), 0)

    for g in seqs:
        def prev_fn(cs, u, g=g):
            c0 = st_ref[g, SUBLANE - 2:SUBLANE - 1, cs]
            c1 = st_ref[g, SUBLANE - 1:SUBLANE, cs]
            p1 = jnp.where(row == 0, c1, pltpu.roll(u, 1, 0))
            p2 = jnp.where(row == 0, c0, jnp.where(row == 1, c1, pltpu.roll(u, 2, 0)))
            return p1, p2

        def tail_fn(cs, u, g=g):
            st_ref[g, :, cs] = u[tb - SUBLANE:tb, :]

        _conv_columns(h[g], win_ref, cw_ref, y_scr.at[g], prev_fn, tail_fn)
    for g in seqs:
        xo_ref[g] = x[g] + mod[g][2] * _dot(y_scr[g], wout_ref[...])


def _conv_prompt_call(x, mods, norm_g, w_in, cw, w_out, layer, mod_block, casts=()):
    bsz, seq, _ = x.shape
    tb = CONV_TB
    gb = CONV_GB
    j = layer // 2
    n_t = seq // tb
    n_steps = (bsz // gb) * n_t
    cast_in_specs, cast_out_specs, cast_shapes = [], [], []
    for mat, first, rows in casts:
        blk = rows // n_steps
        assert rows % n_steps == 0 and blk % (2 * SUBLANE) == 0 and first % blk == 0
        cast_in_specs.append(pl.BlockSpec((blk, mat.shape[1]), lambda b, t, o=first // blk: (o + b * n_t + t, 0)))
        cast_out_specs.append(pl.BlockSpec((blk, mat.shape[1]), lambda b, t: (b * n_t + t, 0)))
        cast_shapes.append(jax.ShapeDtypeStruct((rows, mat.shape[1]), BF16))
    return pl.pallas_call(
        functools.partial(_conv_prompt_kernel, tb=tb, gb=gb, n_cast=len(casts)),
        grid=(bsz // gb, n_t),
        in_specs=[
            pl.BlockSpec((gb, tb, D_MODEL), lambda b, t: (b, t, 0)),
            _resident((None, bsz, 3 * D_MODEL), (layer, mod_block, 0), 2),
            _resident((None, 1, D_MODEL), (layer, 0, 0), 2),
            _resident((None, D_MODEL, 4 * E_CONV), (0, 0, 0), 2),
            _resident((None, CONV_WIDTH, E_CONV), (j, 0, 0), 2),
            _resident((None, E_CONV, D_MODEL), (0, 0, 0), 2),
        ] + cast_in_specs,
        out_specs=[
            pl.BlockSpec((gb, tb, D_MODEL), lambda b, t: (b, t, 0)),
            pl.BlockSpec((gb, SUBLANE, E_CONV), lambda b, t: (b, 0, 0)),
        ] + cast_out_specs,
        out_shape=[
            jax.ShapeDtypeStruct((bsz, seq, D_MODEL), F32),
            jax.ShapeDtypeStruct((bsz, SUBLANE, E_CONV), F32),
        ] + cast_shapes,
        scratch_shapes=[pltpu.VMEM((gb, tb, E_CONV), BF16)],
        compiler_params=_params(2),
        name="conv_prompt",
    )(x, mods, norm_g, w_in, cw, w_out, *[mat for mat, _, _ in casts])


def _tok(s, width):
    return slice(s * width, (s + 1) * width)


def _stack_tokens(ref, n_tok, width):
    return jnp.concatenate([ref[:, _tok(s, width)] for s in range(n_tok)], axis=0)


def _read_tok(ref, s, width):
    return ref[:, s, :] if len(ref.shape) == 3 else ref[:, _tok(s, width)]


def _write_tok(ref, s, width, val):
    if len(ref.shape) == 3:
        ref[:, s, :] = val
    else:
        ref[:, _tok(s, width)] = val


def _sample_hidden(x_ref, mod_ref, g_ref, n_tok):
    shift, scale, _ = _split_mod(mod_ref[...])
    g = g_ref[...]
    return jnp.concatenate(
        [_mod_norm(_read_tok(x_ref, s, D_MODEL), g, shift, scale) for s in range(n_tok)], axis=0)


def _conv_sample_kernel(x_ref, mod_ref, g_ref, win_ref, cw_ref, wout_ref, st_ref,
                        xo_ref, so_ref, y_scr, *, n_tok):
    ns = x_ref.shape[0]
    h = _sample_hidden(x_ref, mod_ref, g_ref, n_tok)

    def prev_fn(cs, u):
        st0 = st_ref[:, 0, cs]
        st1 = st_ref[:, 1, cs]
        p1 = jnp.concatenate([st1, u[:(n_tok - 1) * ns]], axis=0)
        p2 = jnp.concatenate([st0, st1, u[:(n_tok - 2) * ns]], axis=0)
        return p1, p2

    def tail_fn(cs, u):
        so_ref[:, 0, cs] = u[(n_tok - 2) * ns:(n_tok - 1) * ns]
        so_ref[:, 1, cs] = u[(n_tok - 1) * ns:]

    _conv_columns(h, win_ref, cw_ref, y_scr, prev_fn, tail_fn)
    out = _dot(y_scr[...], wout_ref[...])
    gate = mod_ref[:, 2 * D_MODEL:]
    for s in range(n_tok):
        xo_ref[:, _tok(s, D_MODEL)] = _read_tok(x_ref, s, D_MODEL) + gate * out[s * ns:(s + 1) * ns]


def _seq_rows(width, layer=None):
    if layer is None:
        return pl.BlockSpec((SAMPLE_SEQ_BLOCK, width), lambda i: (i, 0))
    return pl.BlockSpec((None, SAMPLE_SEQ_BLOCK, width), lambda i: (layer, i, 0))


def _conv_sample_call(x, mods, norm_g, w_in, cw, w_out, state, layer, n_tok):
    n_seq = x.shape[0]
    j = layer // 2
    return pl.pallas_call(
        functools.partial(_conv_sample_kernel, n_tok=n_tok),
        grid=(n_seq // SAMPLE_SEQ_BLOCK,),
        in_specs=[
            _seq_rows(n_tok * D_MODEL) if x.ndim == 2 else
            pl.BlockSpec((SAMPLE_SEQ_BLOCK, n_tok, D_MODEL), lambda i: (i, 0, 0)),
            _seq_rows(3 * D_MODEL, layer),
            _resident((None, 1, D_MODEL), (layer, 0, 0), 1),
            _resident((None, D_MODEL, 4 * E_CONV), (0, 0, 0), 1),
            _resident((None, CONV_WIDTH, E_CONV), (j, 0, 0), 1),
            _resident((None, E_CONV, D_MODEL), (0, 0, 0), 1),
            pl.BlockSpec((None, SAMPLE_SEQ_BLOCK, CONV_WIDTH - 1, E_CONV), lambda i: (j, i, 0, 0)),
        ],
        out_specs=[_seq_rows(n_tok * D_MODEL),
                   pl.BlockSpec((SAMPLE_SEQ_BLOCK, CONV_WIDTH - 1, E_CONV), lambda i: (i, 0, 0))],
        out_shape=[
            jax.ShapeDtypeStruct((n_seq, n_tok * D_MODEL), F32),
            jax.ShapeDtypeStruct((n_seq, CONV_WIDTH - 1, E_CONV), F32),
        ],
        scratch_shapes=[pltpu.VMEM((n_tok * SAMPLE_SEQ_BLOCK, E_CONV), BF16)],
        compiler_params=_params(1),
        name="conv_sample",
    )(x, mods, norm_g, w_in, cw, w_out, state)


def _rows_from_blocks(vecs):
    return jnp.concatenate([jnp.broadcast_to(v, (SUB, v.shape[1])) for v in vecs], axis=0)


def _chunk_masks():
    r = lax.broadcasted_iota(jnp.int32, (CHUNK, CHUNK), 0)
    c = lax.broadcasted_iota(jnp.int32, (CHUNK, CHUNK), 1)
    causal = (r // SUB == c // SUB) & (c <= r)
    tri = jnp.where(causal, 1.0, 0.0).astype(BF16)
    first_cols = lax.broadcasted_iota(jnp.int32, (SUB, CHUNK), 1) < SUB
    return causal, first_cols, jnp.concatenate([tri, tri], axis=1)


def _chunk_decay(lf, tri2):
    hi = lf.astype(BF16)
    lo = (lf - hi.astype(F32)).astype(BF16)
    return _dot(tri2, jnp.concatenate([hi, lo], axis=0))


def _chunk_key_operands(kk, gl):
    t0 = gl[SUB - 1:SUB, :]
    t1 = gl[2 * SUB - 1:2 * SUB, :]
    k_mid = kk * jnp.exp2(_rows_from_blocks([0.5 * t0, 0.5 * t1]) - gl)
    k_state = k_mid * _rows_from_blocks([jnp.exp2(0.5 * t0 + t1), jnp.exp2(0.5 * t1)])
    in_span = (t0 >= -HGRN_SAFE_LOG2_SPAN) & (t1 >= -HGRN_SAFE_LOG2_SPAN)
    return k_mid.astype(BF16), k_state.astype(BF16), jnp.exp2(t0 + t1), jnp.where(in_span, 0.0, 1.0)


def _chunk_query_operands(qf, gl):
    t0 = gl[SUB - 1:SUB, :]
    t1 = gl[2 * SUB - 1:2 * SUB, :]
    q_mid = qf * jnp.exp2(gl - _rows_from_blocks([0.5 * t0, 0.5 * t1]))
    q_next = q_mid[SUB:] * jnp.exp2(0.5 * (t0 + t1))
    q_state = q_mid * _rows_from_blocks([jnp.exp2(0.5 * t0), jnp.exp2(0.5 * t1 + t0)])
    return jnp.concatenate([q_mid, q_next], axis=0).astype(BF16), q_state.astype(BF16)


def _chunk_finish(ast, qe, kr, vb, ptot, s_old, causal, first_cols):
    same = jnp.where(causal, ast[0:CHUNK], 0.0)
    a = jnp.concatenate([same[:SUB], jnp.where(first_cols, ast[CHUNK:], same[SUB:])], axis=0)
    o = _dot(jnp.concatenate([qe, a.astype(BF16)], axis=1),
             jnp.concatenate([s_old.astype(BF16), vb], axis=0))
    dec = jnp.broadcast_to(ptot, (HG_DK, HG_DK)).T
    return o, s_old * dec + _dot_tn(kr, vb)


def _exact_recurrence(q_ref, k_ref, f_ref, v_ref, o_ref, s0, n_rows):
    def col(tile, r):
        return jnp.broadcast_to(tile[r:r + 1, :], (HG_DK, HG_DK)).T

    def step(i, s):
        rows = pl.ds(pl.multiple_of(i * SUBLANE, SUBLANE), SUBLANE)
        q, k, f, v = q_ref[rows, :], k_ref[rows, :], f_ref[rows, :], v_ref[rows, :]
        out = []
        for r in range(SUBLANE):
            s = s * col(f, r) + col(k, r) * v[r:r + 1, :]
            out.append(jnp.sum(col(q, r) * s, axis=0, keepdims=True))
        o_ref[rows, :] = jnp.concatenate(out, axis=0)
        return s
    return lax.fori_loop(0, n_rows // SUBLANE, step, s0)


def _decode_state_update(p, el_row0, dq_ref, dk_ref, dv_ref, del_ref, ds_ref, do_ref, dso_ref, partial_o):
    nd = ds_ref.shape[0]
    n_tok = dq_ref.shape[0] // nd
    hd, sq = divmod(p, nd)
    hs = slice(hd * HG_DK, (hd + 1) * HG_DK)
    mine = lax.broadcasted_iota(jnp.int32, (dq_ref.shape[0], HG_DK), 0) // n_tok == sq
    s_old = ds_ref[sq, hd]
    o = _dot(dq_ref[:, hs], s_old.astype(BF16))
    partial_o[hd] = jnp.where(mine, o, partial_o[hd]) if hd in partial_o else jnp.where(mine, o, 0.0)
    k_mine = jnp.where(mine, dk_ref[:, hs], 0.0).astype(BF16)
    dec = jnp.broadcast_to(del_ref[pl.ds(el_row0 + sq, 1), :][:, hs], (HG_DK, HG_DK)).T
    dso_ref[sq, hd] = s_old * dec + _dot_tn(k_mine, dv_ref[:, hs])
    if sq == nd - 1:
        do_ref[:, hs] = partial_o.pop(hd)


def _hgrn_prompt_kernel(*refs, tb, gb, layer, final, has_prev):
    n_in = 13 + int(has_prev)
    x_ref, mod_ref, g_ref, win_ref, lbraw_ref, og_ref, wout_ref, fg_ref = refs[:8]
    decode_in = refs[8:13]
    xo_ref, s_ref, do_ref, dso_ref = refs[n_in:n_in + 4]
    q_scr, lf_scr, k_scr, v_scr, z_scr, o_scr, s0_scr = refs[n_in + 4:]
    b = pl.program_id(0)
    t = pl.program_id(1)

    @pl.when(t == 0)
    def _():
        s_ref[...] = jnp.zeros_like(s_ref)

    s0_scr[...] = s_ref[...]

    seqs = range(gb)
    lb = _lower_bound(lbraw_ref[...], layer)
    x = [x_ref[g] for g in seqs]
    mod = [_split_mod(mod_ref[pl.ds(b * gb + g, 1), :]) for g in seqs]
    h = [_mod_norm_seq(x[g], g_ref[...], mod[g][0], mod[g][1]) for g in seqs]
    for g in seqs:
        q_scr[g] = _silu(_dot(h[g], win_ref[:, 0:HG_F]))
    for g in seqs:
        logf, kk = _gate_math(_dot(h[g], win_ref[:, HG_F:2 * HG_F]), lb)
        lf_scr[g] = logf
        k_scr[g] = kk
    for g in seqs:
        v_scr[g] = _dot(h[g], win_ref[:, 2 * HG_F:2 * HG_F + HG_I])
    for g in seqs:
        z_scr[g] = _dot(h[g], win_ref[:, 2 * HG_F + HG_I:])

    causal, first_cols, tri2 = _chunk_masks()
    chains = [(c, hd, g) for c in range(tb // CHUNK) for hd in range(HG_HEADS) for g in seqs]
    ops, scores = {}, {}
    unsafe = jnp.zeros((1, HG_DK), F32)

    def where(n):
        c, hd, g = chains[n]
        return g, hd, slice(c * CHUNK, (c + 1) * CHUNK), slice(hd * HG_DK, (hd + 1) * HG_DK)

    block_seqs = decode_in[4].shape[0]
    n_decode = HG_HEADS * block_seqs
    blocks_per_el = decode_in[3].shape[0] // block_seqs
    el_row0 = ((b * pl.num_programs(1) + t) % blocks_per_el) * block_seqs
    partial_o = {}
    for step in range(max(len(chains) + 2 * HGRN_PIPE_SKEW, n_decode)):
        if step < n_decode:
            _decode_state_update(step, el_row0, *decode_in, do_ref, dso_ref, partial_o)
        n = step
        if n < len(chains):
            g, hd, rows, hs = where(n)
            gl = _chunk_decay(lf_scr[g, rows, hs], tri2)
            km, kr, ptot, bad = _chunk_key_operands(k_scr[g, rows, hs], gl)
            lhs, qe = _chunk_query_operands(q_scr[g, rows, hs], gl)
            ops[n] = (lhs, km, qe, kr, ptot)
            unsafe = jnp.maximum(unsafe, bad)
        n = step - HGRN_PIPE_SKEW
        if 0 <= n < len(chains):
            scores[n] = _dot_nt(ops[n][0], ops[n][1])
        n = step - 2 * HGRN_PIPE_SKEW
        if 0 <= n < len(chains):
            g, hd, rows, hs = where(n)
            _, _, qe, kr, ptot = ops.pop(n)
            vb = v_scr[g, rows, hs].astype(BF16)
            o, s_new = _chunk_finish(scores.pop(n), qe, kr, vb, ptot, s_ref[g, hd], causal, first_cols)
            o_scr[g, rows, hs] = o
            s_ref[g, hd] = s_new

    y = [_head_norm_gate(o_scr[g], z_scr[g], og_ref[...]) for g in seqs]
    for g in seqs:
        xn = x[g] + mod[g][2] * _dot(y[g], wout_ref[...])
        if final:
            xn = _rmsnorm(xn, fg_ref[...])
        xo_ref[g] = xn

    @pl.when(jnp.max(unsafe) > 0.0)
    def _():
        def exact(q_tmp, k_tmp, f_tmp, v_tmp, o_tmp):
            def one_seq(g, carry):
                shift, scale, gate = _split_mod(mod_ref[pl.ds(b * gb + g, 1), :])
                hg = _mod_norm_seq(x_ref[g], g_ref[...], shift, scale)
                qf = _silu(_dot(hg, win_ref[:, 0:HG_F]))
                logf, kk = _gate_math(_dot(hg, win_ref[:, HG_F:2 * HG_F]), lb)
                ff = jnp.exp2(logf)
                vv = _dot(hg, win_ref[:, 2 * HG_F:2 * HG_F + HG_I])
                for hd in range(HG_HEADS):
                    hs = slice(hd * HG_DK, (hd + 1) * HG_DK)
                    q_tmp[hd], k_tmp[hd], f_tmp[hd], v_tmp[hd] = qf[:, hs], kk[:, hs], ff[:, hs], vv[:, hs]

                def one_head(hd, c2):
                    s_ref[g, hd] = _exact_recurrence(q_tmp.at[hd], k_tmp.at[hd], f_tmp.at[hd], v_tmp.at[hd],
                                                     o_tmp.at[hd], s0_scr[g, hd], tb)
                    return c2

                lax.fori_loop(0, HG_HEADS, one_head, 0)
                o = jnp.concatenate([o_tmp[hd] for hd in range(HG_HEADS)], axis=1)
                y = _head_norm_gate(o, z_scr[g], og_ref[...])
                xn = x_ref[g] + gate * _dot(y, wout_ref[...])
                if final:
                    xn = _rmsnorm(xn, fg_ref[...])
                xo_ref[g] = xn
                return carry

            lax.fori_loop(0, gb, one_seq, 0)

        pl.run_scoped(exact, *[pltpu.VMEM((HG_HEADS, tb, HG_DK), F32)] * 5)


def _hgrn_prompt_call(x, mods, norm_g, w_in, lb_raw, og, w_out, fg, decode, layer, mod_block):
    bsz, seq, _ = x.shape
    tb = HGRN_TB
    gb = HGRN_GB
    j = layer // 2
    final = layer == DEPTH - 1
    n_t = seq // tb
    qe, kr, v, el, states, prev_states = decode
    n_blocks = (bsz // gb) * n_t
    n_seq = el.shape[0]
    block_seqs = n_seq // n_blocks
    dec_rows = qe.shape[0] // n_blocks
    el_rows = max(block_seqs, SUBLANE)
    assert n_seq == n_blocks * block_seqs and el_rows % block_seqs == 0 and dec_rows % SUBLANE == 0
    dec_spec = pl.BlockSpec((dec_rows, HG_F), lambda b, t: (b * n_t + t, 0))
    st_spec = pl.BlockSpec((None, block_seqs, HG_HEADS, HG_DK, HG_DV), lambda b, t: (j, b * n_t + t, 0, 0, 0))
    in_specs = [
        pl.BlockSpec((gb, tb, D_MODEL), lambda b, t: (b, t, 0)),
        _resident((None, bsz, 3 * D_MODEL), (layer, mod_block, 0), 2),
        _resident((None, 1, D_MODEL), (layer, 0, 0), 2),
        _resident((None, D_MODEL, 2 * HG_F + 2 * HG_I), (j, 0, 0), 2),
        _resident((DEPTH, HG_F), (0, 0), 2),
        _resident((None, 1, HG_I), (j, 0, 0), 2),
        _resident((None, HG_I, D_MODEL), (j, 0, 0), 2),
        _resident((1, D_MODEL), (0, 0), 2),
        dec_spec, dec_spec, dec_spec,
        pl.BlockSpec((el_rows, HG_F), lambda b, t: ((b * n_t + t) * block_seqs // el_rows, 0)),
        st_spec,
    ]
    args = [x, mods, norm_g, w_in, lb_raw, og, w_out, fg, qe, kr, v, el, states]
    aliases = {}
    if prev_states is not None:
        in_specs.append(pl.BlockSpec(memory_space=pl.ANY))
        args.append(prev_states)
        aliases = {len(args) - 1: 3}
    return pl.pallas_call(
        functools.partial(_hgrn_prompt_kernel, tb=tb, gb=gb, layer=layer, final=final,
                          has_prev=prev_states is not None),
        grid=(bsz // gb, n_t),
        in_specs=in_specs,
        out_specs=[
            pl.BlockSpec((gb, tb, D_MODEL), lambda b, t: (b, t, 0)),
            pl.BlockSpec((gb, HG_HEADS, HG_DK, HG_DV), lambda b, t: (b, 0, 0, 0)),
            dec_spec,
            st_spec,
        ],
        out_shape=[
            jax.ShapeDtypeStruct((bsz, seq, D_MODEL), F32),
            jax.ShapeDtypeStruct((bsz, HG_HEADS, HG_DK, HG_DV), F32),
            jax.ShapeDtypeStruct(qe.shape, F32),
            jax.ShapeDtypeStruct(states.shape, F32),
        ],
        scratch_shapes=[pltpu.VMEM((gb, tb, HG_F), F32)] * 6
        + [pltpu.VMEM((gb, HG_HEADS, HG_DK, HG_DV), F32)],
        input_output_aliases=aliases,
        compiler_params=_params(2),
        name="hgrn_prompt",
    )(*args)


def _token_row_permutation(n_seq, n_tok, seq_major_out):
    n = n_seq * n_tok
    r = lax.broadcasted_iota(jnp.int32, (n, n), 0)
    c = lax.broadcasted_iota(jnp.int32, (n, n), 1)
    sm, tm = (r, c) if seq_major_out else (c, r)
    return jnp.where(tm == (sm % n_tok) * n_seq + sm // n_tok, 1.0, 0.0).astype(BF16)


def _permute_rows_f32(p, a):
    hi = a.astype(BF16)
    rest = a - hi.astype(F32)
    mid = rest.astype(BF16)
    lo = (rest - mid.astype(F32)).astype(BF16)
    return (_dot(p, hi) + _dot(p, mid)) + _dot(p, lo)


def _hgrn_sample_pre_kernel(x_ref, mod_ref, g_ref, win_ref, lbraw_ref,
                            qe_ref, kr_ref, v_ref, el_ref, oi_ref, z_ref, *, n_tok, layer):
    ns = x_ref.shape[0]
    h = _sample_hidden(x_ref, mod_ref, g_ref, n_tok)
    lb = _lower_bound(lbraw_ref[...], layer)
    qf = _silu(_dot(h, win_ref[:, 0:HG_F]))
    lf, kk = _gate_math(_dot(h, win_ref[:, HG_F:2 * HG_F]), lb)
    vv = _dot(h, win_ref[:, 2 * HG_F:2 * HG_F + HG_I])
    z = _dot(h, win_ref[:, 2 * HG_F + HG_I:])

    def tok(a, s):
        return a[s * ns:(s + 1) * ns]

    def decay(s, t):
        if s == t:
            return 1.0
        return jnp.exp2(functools.reduce(lambda a, b: a + b, [tok(lf, u) for u in range(s + 1, t + 1)]))

    last = n_tok - 1
    el_ref[...] = decay(-1, last)
    to_seq_major = _token_row_permutation(ns, n_tok, seq_major_out=True)
    qe = jnp.concatenate([tok(qf, s) * decay(-1, s) for s in range(n_tok)], axis=0)
    kr = jnp.concatenate([tok(kk, s) * decay(s, last) for s in range(n_tok)], axis=0)
    qe_ref[...] = _dot(to_seq_major, qe.astype(BF16)).astype(BF16)
    kr_ref[...] = _dot(to_seq_major, kr.astype(BF16)).astype(BF16)
    v_ref[...] = _dot(to_seq_major, vv.astype(BF16)).astype(BF16)
    for s in range(n_tok):
        z_ref[:, _tok(s, HG_I)] = tok(z, s)

    for t in range(n_tok):
        acc = jnp.zeros((ns, HG_I), F32)
        for s in range(t + 1):
            e = tok(qf, t) * tok(kk, s) * decay(s, t)
            vs = tok(vv, s)
            parts = []
            for hd in range(HG_HEADS):
                hs = slice(hd * HG_DK, (hd + 1) * HG_DK)
                parts.append(jnp.sum(e[:, hs], axis=-1, keepdims=True) * vs[:, hs])
            acc = acc + jnp.concatenate(parts, axis=1)
        oi_ref[:, _tok(t, HG_I)] = acc


def _hgrn_sample_pre_call(x, mods, norm_g, w_in, lb_raw, layer, n_tok):
    n_seq = x.shape[0]
    j = layer // 2
    act = jax.ShapeDtypeStruct((n_seq, n_tok * HG_F), F32)
    tok_rows = jax.ShapeDtypeStruct((n_seq * n_tok, HG_F), BF16)
    tok_spec = pl.BlockSpec((SAMPLE_SEQ_BLOCK * n_tok, HG_F), lambda i: (i, 0))
    return pl.pallas_call(
        functools.partial(_hgrn_sample_pre_kernel, n_tok=n_tok, layer=layer),
        grid=(n_seq // SAMPLE_SEQ_BLOCK,),
        in_specs=[
            _seq_rows(n_tok * D_MODEL),
            _seq_rows(3 * D_MODEL, layer),
            _resident((None, 1, D_MODEL), (layer, 0, 0), 1),
            _resident((None, D_MODEL, 2 * HG_F + 2 * HG_I), (j, 0, 0), 1),
            _resident((DEPTH, HG_F), (0, 0), 1),
        ],
        out_specs=[tok_spec] * 3 + [_seq_rows(HG_F)] + [_seq_rows(n_tok * HG_I)] * 2,
        out_shape=[tok_rows, tok_rows, tok_rows, jax.ShapeDtypeStruct((n_seq, HG_F), F32), act, act],
        compiler_params=_params(1),
        name="hgrn_sample_pre",
    )(x, mods, norm_g, w_in, lb_raw)


def _hgrn_sample_post_kernel(x_ref, mod_ref, oa_ref, ob_ref, z_ref, og_ref, wout_ref, fg_ref, xo_ref,
                             *, n_tok, final):
    ns = x_ref.shape[0]
    o_state = _permute_rows_f32(_token_row_permutation(ns, n_tok, seq_major_out=False), oa_ref[...])
    o = o_state + _stack_tokens(ob_ref, n_tok, HG_I)
    y = _head_norm_gate(o, _stack_tokens(z_ref, n_tok, HG_I), og_ref[...])
    out = _dot(y, wout_ref[...])
    gate = mod_ref[:, 2 * D_MODEL:]
    for s in range(n_tok):
        xn = x_ref[:, _tok(s, D_MODEL)] + gate * out[s * ns:(s + 1) * ns]
        if final:
            xn = _rmsnorm(xn, fg_ref[...])
        _write_tok(xo_ref, s, D_MODEL, xn)


def _hgrn_sample_post_call(x, mods, o_inter, o_intra, z, og, w_out, fg, layer, n_tok):
    n_seq = x.shape[0]
    j = layer // 2
    return pl.pallas_call(
        functools.partial(_hgrn_sample_post_kernel, n_tok=n_tok, final=layer == DEPTH - 1),
        grid=(n_seq // SAMPLE_SEQ_BLOCK,),
        in_specs=[
            _seq_rows(n_tok * D_MODEL),
            _seq_rows(3 * D_MODEL, layer),
            pl.BlockSpec((SAMPLE_SEQ_BLOCK * n_tok, HG_I), lambda i: (i, 0)),
            _seq_rows(n_tok * HG_I), _seq_rows(n_tok * HG_I),
            _resident((None, 1, HG_I), (j, 0, 0), 1),
            _resident((None, HG_I, D_MODEL), (j, 0, 0), 1),
            _resident((1, D_MODEL), (0, 0), 1),
        ],
        out_specs=(pl.BlockSpec((SAMPLE_SEQ_BLOCK, n_tok, D_MODEL), lambda i: (i, 0, 0)) if layer == DEPTH - 1
                   else _seq_rows(n_tok * D_MODEL)),
        out_shape=jax.ShapeDtypeStruct((n_seq, n_tok, D_MODEL) if layer == DEPTH - 1
                                       else (n_seq, n_tok * D_MODEL), F32),
        compiler_params=_params(1),
        name="hgrn_sample_post",
    )(x, mods, o_inter, o_intra, z, og, w_out, fg)


def kernel(x_prompt, x_sample, state_conv, state_hgrn, c_prompt, c_sample, norm_g, w_ada, b_ada, conv_w_in, conv_w, conv_w_out, hgrn_w_in, hgrn_lower_bounds, hgrn_onorm_g, hgrn_w_out, final_norm_g):
    n_p = x_prompt.shape[0]
    n_s, t_s, _ = x_sample.shape
    assert t_s >= CONV_WIDTH - 1 and n_s % SAMPLE_SEQ_BLOCK == 0 and n_s % n_p == 0

    mods = _ada_call(jnp.concatenate([c_sample, c_prompt], axis=0), w_ada, b_ada)
    mod_block_p = n_s // n_p

    n_conv, n_hgrn = conv_w_in.shape[0], hgrn_w_in.shape[0]
    assert n_conv == 2 and DEPTH == 4
    conv_w_in_b = [conv_w_in[:1].astype(BF16)]
    conv_w_out_b = [conv_w_out[:1].astype(BF16)]
    later_weights = [
        (hgrn_w_in.reshape(n_hgrn * D_MODEL, -1), 0, n_hgrn * D_MODEL),
        (hgrn_w_out.reshape(n_hgrn * HG_I, D_MODEL), 0, n_hgrn * HG_I),
        (conv_w_in.reshape(n_conv * D_MODEL, -1), D_MODEL, (n_conv - 1) * D_MODEL),
        (conv_w_out.reshape(n_conv * E_CONV, D_MODEL), E_CONV, (n_conv - 1) * E_CONV),
    ]
    norm_g3 = norm_g.reshape(DEPTH, 1, D_MODEL)
    og3 = hgrn_onorm_g.reshape(DEPTH // 2, 1, HG_I)
    fg = final_norm_g.reshape(1, D_MODEL)

    xp = x_prompt
    xs = x_sample
    conv_p, conv_s, hgrn_p = [], [], []
    hgrn_s = None
    for layer in range(DEPTH):
        if layer % 2 == 0:
            j = layer // 2
            xp, tail, *rounded = _conv_prompt_call(xp, mods, norm_g3, conv_w_in_b[j], conv_w, conv_w_out_b[j],
                                                   layer, mod_block_p, later_weights if layer == 0 else ())
            if layer == 0:
                hgrn_w_in_b = rounded[0].reshape(hgrn_w_in.shape)
                hgrn_w_out_b = rounded[1].reshape(hgrn_w_out.shape)
                conv_w_in_b.append(rounded[2].reshape((1,) + conv_w_in.shape[1:]))
                conv_w_out_b.append(rounded[3].reshape((1,) + conv_w_out.shape[1:]))
            conv_p.append(tail[:, SUBLANE - (CONV_WIDTH - 1):])
            xs, st = _conv_sample_call(xs, mods, norm_g3, conv_w_in_b[j], conv_w, conv_w_out_b[j], state_conv,
                                       layer, t_s)
            conv_s.append(st)
        else:
            qe, kr, v, el, o_intra, z = _hgrn_sample_pre_call(xs, mods, norm_g3, hgrn_w_in_b,
                                                              hgrn_lower_bounds, layer, t_s)
            decode = (qe, kr, v, el, state_hgrn, hgrn_s)
            xp, s_new, o_inter, hgrn_s = _hgrn_prompt_call(xp, mods, norm_g3, hgrn_w_in_b, hgrn_lower_bounds,
                                                           og3, hgrn_w_out_b, fg, decode, layer, mod_block_p)
            hgrn_p.append(s_new)
            xs = _hgrn_sample_post_call(xs, mods, o_inter, o_intra, z, og3, hgrn_w_out_b, fg, layer, t_s)
    return (xp, xs, jnp.stack(conv_p), jnp.stack(hgrn_p), jnp.stack(conv_s), hgrn_s)
```
